```python
import math
import jax, jax.numpy as jnp
from jax import lax
import numpy as np

D_MODEL = 1024
BATCH = 2
SEQ = 8192
DEPTH = 1
DEC_BATCH = 128
DEC_SEQ = 4
PAST_LEN = 16384
PAGE_SIZE = 128

D_INNER = 2 * D_MODEL
SSD_HEAD_DIM = 64
N_SSD_HEADS = D_INNER // SSD_HEAD_DIM
N_SSD_GROUPS = 4
D_STATE = 128
CONV_W = 4
CONV_DIM = D_INNER + 2 * N_SSD_GROUPS * D_STATE
CHUNK = 128
ATTN_HEAD_DIM = 64
N_ATTN_HEADS = D_MODEL // ATTN_HEAD_DIM
N_KV_HEADS = 4
KV_REP = N_ATTN_HEADS // N_KV_HEADS
ATTN_DIM = N_ATTN_HEADS * ATTN_HEAD_DIM
KV_DIM = N_KV_HEADS * ATTN_HEAD_DIM
WINDOW = 128
N_EXPERTS = 32
TOP_K = 4
D_FF = D_MODEL
SWIGLU_LIMIT = 7.0
SWIGLU_ALPHA = 1.702
MOE_BLOCK = 128
IN_DIM = D_INNER + CONV_DIM + N_SSD_HEADS + ATTN_DIM + 2 * KV_DIM + 2 * D_MODEL
EPS = 1e-6

kernel_name = 'hybrid_ssd_swa_moe_decode_step'


def rms_norm(x, g):
    xf = x.astype(jnp.float32)
    xn = xf * lax.rsqrt(jnp.mean(xf * xf, axis=-1, keepdims=True) + EPS)
    return xn.astype(x.dtype) * g


def gated_rms_norm(y, z, g):
    u = (y * jax.nn.silu(z)).astype(jnp.float32)
    shp = u.shape
    u = u.reshape(shp[:-1] + (N_SSD_GROUPS, D_INNER // N_SSD_GROUPS))
    u = u * lax.rsqrt(jnp.mean(u * u, axis=-1, keepdims=True) + EPS)
    return u.reshape(shp).astype(y.dtype) * g


def causal_depthwise_conv(u, u_prev, w, b):
    L = u.shape[1]
    up = jnp.concatenate([u_prev, u], axis=1)
    out = b + sum(up[:, k:k + L] * w[k] for k in range(CONV_W))
    return out, up[:, L:]


def ssd_chunked_scan(xs, dt, a, bm, cm, init_state):
    Bsz, L = xs.shape[:2]
    q = CHUNK if L % CHUNK == 0 else L
    nc = L // q
    R = N_SSD_HEADS // N_SSD_GROUPS
    x = (xs * dt[..., None]).reshape(Bsz, nc, q, N_SSD_GROUPS, R, SSD_HEAD_DIM)
    adt = (dt * a).reshape(Bsz, nc, q, N_SSD_GROUPS, R)
    bm = bm.reshape(Bsz, nc, q, N_SSD_GROUPS, D_STATE)
    cm = cm.reshape(Bsz, nc, q, N_SSD_GROUPS, D_STATE)
    acs = jnp.cumsum(adt, axis=2)
    diff = acs[:, :, :, None] - acs[:, :, None]
    causal = jnp.tril(jnp.ones((q, q), dtype=bool))[None, None, :, :, None, None]
    decay = jnp.where(causal, jnp.exp(jnp.where(causal, diff, 0.0)), 0.0)
    cb = jnp.einsum('bclgn,bcsgn->bclsg', cm, bm)
    y_diag = jnp.einsum('bclsgr,bcsgrp->bclgrp', cb[..., None] * decay, x)
    x_to_end = x * jnp.exp(acs[:, :, -1:] - acs)[..., None]
    chunk_states = jnp.einsum('bcsgn,bcsgrp->bcgrpn', bm, x_to_end)
    chunk_decay = jnp.exp(acs[:, :, -1])

    def chunk_step(state, inp):
        st, dec = inp
        return state * dec[..., None, None] + st, state

    init = init_state.reshape(Bsz, N_SSD_GROUPS, R, SSD_HEAD_DIM, D_STATE)
    final, prev_states = lax.scan(chunk_step, init,
                                  (jnp.moveaxis(chunk_states, 1, 0), jnp.moveaxis(chunk_decay, 1, 0)))
    prev_states = jnp.moveaxis(prev_states, 0, 1)
    y_off = jnp.einsum('bclgn,bcgrpn->bclgrp', cm, prev_states) * jnp.exp(acs)[..., None]
    y = (y_diag + y_off).reshape(Bsz, L, N_SSD_HEADS, SSD_HEAD_DIM)
    return y, final.reshape(Bsz, N_SSD_HEADS, SSD_HEAD_DIM, D_STATE)


def swa_sink_attention(q, k, v, k_prev, v_prev, prev_valid, sinks):
    Bsz, L = q.shape[:2]
    wb = k_prev.shape[1]
    qb = WINDOW if L % WINDOW == 0 else L
    nb = L // qb
    k_all = jnp.concatenate([k_prev, k], axis=1)
    v_all = jnp.concatenate([v_prev, v], axis=1)
    key_idx = (jnp.arange(nb) * qb)[:, None] + jnp.arange(wb + qb)[None, :]
    kb = k_all[:, key_idx]
    vb = v_all[:, key_idx]
    qr = q.reshape(Bsz, nb, qb, N_KV_HEADS, KV_REP, ATTN_HEAD_DIM)
    scores = jnp.einsum('bnqkrd,bnskd->bnkrqs', qr, kb).astype(jnp.float32) * ATTN_HEAD_DIM ** -0.5
    rel = (wb + jnp.arange(qb))[:, None] - jnp.arange(wb + qb)[None, :]
    allowed = (rel >= 0) & (rel < WINDOW)
    valid = allowed[None] & ((key_idx >= wb) | prev_valid)[:, None, :]
    slopes = jnp.exp2(-8.0 * jnp.arange(1, N_ATTN_HEADS + 1, dtype=jnp.float32) / N_ATTN_HEADS)
    slopes = slopes.reshape(N_KV_HEADS, KV_REP, 1, 1)
    scores = scores - slopes * rel.astype(jnp.float32)
    scores = jnp.where(valid[None, :, None, None], scores, -jnp.inf)
    sink = jnp.broadcast_to(sinks.astype(jnp.float32).reshape(N_KV_HEADS, KV_REP, 1, 1),
                            scores.shape[:-1] + (1,))
    probs = jax.nn.softmax(jnp.concatenate([scores, sink], axis=-1), axis=-1)[..., :-1]
    out = jnp.einsum('bnkrqs,bnskd->bnqkrd', probs.astype(vb.dtype), vb)
    return out.reshape(Bsz, L, ATTN_DIM), k_all[:, -wb:], v_all[:, -wb:]


def moe_ffn(h, w_router, b_router, w_up, b_up, w_down, b_down):
    T = h.shape[0]
    logits = (h @ w_router + b_router).astype(jnp.float32)
    top_logit, top_idx = lax.top_k(logits, TOP_K)
    gates = jax.nn.softmax(top_logit, axis=-1).astype(h.dtype)
    n_slots = T * TOP_K
    slot_expert = top_idx.reshape(-1)
    slot_token = jnp.arange(n_slots) // TOP_K
    order = jnp.argsort(slot_expert)
    sorted_expert = slot_expert[order]
    sorted_token = slot_token[order]
    sorted_gate = gates.reshape(-1)[order]
    counts = jnp.bincount(slot_expert, length=N_EXPERTS)
    padded = (counts + MOE_BLOCK - 1) // MOE_BLOCK * MOE_BLOCK
    group_start = jnp.cumsum(counts) - counts
    padded_end = jnp.cumsum(padded)
    padded_start = padded_end - padded
    dest = padded_start[sorted_expert] + jnp.arange(n_slots) - group_start[sorted_expert]
    n_blocks = -(-n_slots // MOE_BLOCK) + N_EXPERTS
    buf_token = jnp.full((n_blocks * MOE_BLOCK,), T, dtype=jnp.int32).at[dest].set(sorted_token)
    h_ext = jnp.concatenate([h, jnp.zeros((1, h.shape[1]), h.dtype)], axis=0)
    xb = h_ext[buf_token].reshape(n_blocks, MOE_BLOCK, h.shape[1])
    block_expert = jnp.minimum(
        jnp.searchsorted(padded_end, jnp.arange(n_blocks) * MOE_BLOCK, side='right'), N_EXPERTS - 1)

    def expert_block(args):
        xblk, e = args
        up = xblk @ w_up[e] + b_up[e]
        x_glu, x_lin = jnp.split(up, 2, axis=-1)
        x_glu = jnp.minimum(x_glu, SWIGLU_LIMIT)
        x_lin = jnp.clip(x_lin, -SWIGLU_LIMIT, SWIGLU_LIMIT)
        act = x_glu * jax.nn.sigmoid(SWIGLU_ALPHA * x_glu) * (x_lin + 1.0)
        return act @ w_down[e] + b_down[e]

    yb = lax.map(expert_block, (xb, block_expert)).reshape(n_blocks * MOE_BLOCK, h.shape[1])
    return jax.ops.segment_sum(sorted_gate[:, None] * yb[dest], sorted_token, num_segments=T)


def trunk_layer(x, c, k_prev, v_prev, prev_valid, conv_prev, ssm_init, p):
    Bsz, L, _ = x.shape
    ada = jax.nn.silu(c) @ p['w_ada'] + p['b_ada']
    shift1, scale1, gate1, shift2, scale2, gate2 = jnp.split(ada[:, None, :], 6, axis=-1)
    h = rms_norm(x, p['g_pre_mix']) * (1.0 + scale1) + shift1
    proj = h @ p['w_in']
    splits = [int(s) for s in np.cumsum([D_INNER, CONV_DIM, N_SSD_HEADS, ATTN_DIM, KV_DIM, KV_DIM, D_MODEL])]
    z, xbc, dt_raw, q, k, v, gate_ssd, gate_attn = jnp.split(proj, splits, axis=-1)
    xbc, conv_new = causal_depthwise_conv(xbc, conv_prev, p['conv_w'], p['conv_b'])
    xbc = jax.nn.silu(xbc)
    xs, bm, cm = jnp.split(xbc, [D_INNER, D_INNER + N_SSD_GROUPS * D_STATE], axis=-1)
    xs = xs.reshape(Bsz, L, N_SSD_HEADS, SSD_HEAD_DIM)
    bm = bm.reshape(Bsz, L, N_SSD_GROUPS, D_STATE)
    cm = cm.reshape(Bsz, L, N_SSD_GROUPS, D_STATE)
    dt = jax.nn.softplus(dt_raw + p['dt_bias'])
    a = -jnp.exp(p['a_log'])
    y, ssm_new = ssd_chunked_scan(xs, dt, a, bm, cm, ssm_init)
    y = y + p['d_skip'][:, None] * xs
    y_ssd = gated_rms_norm(y.reshape(Bsz, L, D_INNER), z, p['g_ssm_norm']) @ p['w_o_ssd']
    attn, k_new, v_new = swa_sink_attention(
        q.reshape(Bsz, L, N_ATTN_HEADS, ATTN_HEAD_DIM),
        k.reshape(Bsz, L, N_KV_HEADS, ATTN_HEAD_DIM),
        v.reshape(Bsz, L, N_KV_HEADS, ATTN_HEAD_DIM),
        k_prev, v_prev, prev_valid, p['sinks'])
    y_attn = attn @ p['w_o_attn']
    mixed = (jax.nn.sigmoid(gate_ssd) * y_ssd + jax.nn.sigmoid(gate_attn) * y_attn) @ p['w_out']
    x = x + gate1 * rms_norm(mixed, p['g_post_mix'])
    h2 = rms_norm(x, p['g_pre_ffn']) * (1.0 + scale2) + shift2
    f = moe_ffn(h2.reshape(Bsz * L, D_MODEL), p['w_router'], p['b_router'],
                p['w_up'], p['b_up'], p['w_down'], p['b_down']).reshape(Bsz, L, D_MODEL)
    x = x + gate2 * rms_norm(f, p['g_post_ffn'])
    return x, k_new, v_new, conv_new, ssm_new


def setup_inputs(seed: int = 0) -> dict:
    key = jax.random.key(seed)
    keys = iter(jax.random.split(key, 40))

    def nrm(shape, scale):
        return jax.random.normal(next(keys), shape, jnp.float32) * scale

    win_buf = min(WINDOW, PAST_LEN)
    dt0 = jnp.exp(jax.random.uniform(next(keys), (DEPTH, N_SSD_HEADS), jnp.float32,
                                     math.log(1e-3), math.log(1e-1)))
    dt_bias = dt0 + jnp.log(-jnp.expm1(-dt0))
    a_log = jnp.log(jax.random.uniform(next(keys), (DEPTH, N_SSD_HEADS), jnp.float32, 1.0, 16.0))
    return {
        'x_prompt': nrm((BATCH, SEQ, D_MODEL), 1.0),
        'x_sample': nrm((DEC_BATCH, DEC_SEQ, D_MODEL), 1.0),
        'c_prompt': nrm((BATCH, D_MODEL), 1.0),
        'c_sample': nrm((DEC_BATCH, D_MODEL), 1.0),
        'cache_swa_k': nrm((DEPTH, DEC_BATCH, win_buf, N_KV_HEADS, ATTN_HEAD_DIM), 1.0),
        'cache_swa_v': nrm((DEPTH, DEC_BATCH, win_buf, N_KV_HEADS, ATTN_HEAD_DIM), 1.0),
        'state_conv': nrm((DEPTH, DEC_BATCH, CONV_W - 1, CONV_DIM), 1.0),
        'state_ssm': nrm((DEPTH, DEC_BATCH, N_SSD_HEADS, SSD_HEAD_DIM, D_STATE), 0.5),
        'w_ada': nrm((DEPTH, D_MODEL, 6 * D_MODEL), 0.5 * D_MODEL ** -0.5),
        'b_ada': nrm((DEPTH, 6 * D_MODEL), 0.02),
        'g_pre_mix': 1.0 + nrm((DEPTH, D_MODEL), 0.1),
        'g_post_mix': 1.0 + nrm((DEPTH, D_MODEL), 0.1),
        'g_pre_ffn': 1.0 + nrm((DEPTH, D_MODEL), 0.1),
        'g_post_ffn': 1.0 + nrm((DEPTH, D_MODEL), 0.1),
        'w_in': nrm((DEPTH, D_MODEL, IN_DIM), D_MODEL ** -0.5),
        'conv_w': nrm((DEPTH, CONV_W, CONV_DIM), CONV_W ** -0.5),
        'conv_b': nrm((DEPTH, CONV_DIM), 0.02),
        'dt_bias': dt_bias,
        'a_log': a_log,
        'd_skip': 1.0 + nrm((DEPTH, N_SSD_HEADS), 0.1),
        'g_ssm_norm': 1.0 + nrm((DEPTH, D_INNER), 0.1),
        'sinks': nrm((DEPTH, N_ATTN_HEADS), 1.0),
        'w_o_ssd': nrm((DEPTH, D_INNER, D_MODEL), D_INNER ** -0.5),
        'w_o_attn': nrm((DEPTH, ATTN_DIM, D_MODEL), ATTN_DIM ** -0.5),
        'w_out': nrm((DEPTH, D_MODEL, D_MODEL), D_MODEL ** -0.5),
        'w_router': nrm((DEPTH, D_MODEL, N_EXPERTS), D_MODEL ** -0.5),
        'b_router': nrm((DEPTH, N_EXPERTS), 0.01),
        'w_up': nrm((DEPTH, N_EXPERTS, D_MODEL, 2 * D_FF), D_MODEL ** -0.5),
        'b_up': nrm((DEPTH, N_EXPERTS, 2 * D_FF), 0.02),
        'w_down': nrm((DEPTH, N_EXPERTS, D_FF, D_MODEL), D_FF ** -0.5),
        'b_down': nrm((DEPTH, N_EXPERTS, D_MODEL), 0.02),
    }


def reference(x_prompt, x_sample, c_prompt, c_sample, cache_swa_k, cache_swa_v, state_conv, state_ssm,
              w_ada, b_ada, g_pre_mix, g_post_mix, g_pre_ffn, g_post_ffn, w_in, conv_w, conv_b,
              dt_bias, a_log, d_skip, g_ssm_norm, sinks, w_o_ssd, w_o_attn, w_out,
              w_router, b_router, w_up, b_up, w_down, b_down):
    n_prompt = x_prompt.shape[0]
    dtype = x_prompt.dtype
    yp, ys = x_prompt, x_sample
    kp_l, vp_l, cp_l, sp_l = [], [], [], []
    ks_l, vs_l, cs_l, ss_l = [], [], [], []
    for l in range(DEPTH):
        p = dict(w_ada=w_ada[l], b_ada=b_ada[l], g_pre_mix=g_pre_mix[l], g_post_mix=g_post_mix[l],
                 g_pre_ffn=g_pre_ffn[l], g_post_ffn=g_post_ffn[l], w_in=w_in[l], conv_w=conv_w[l],
                 conv_b=conv_b[l], dt_bias=dt_bias[l], a_log=a_log[l], d_skip=d_skip[l],
                 g_ssm_norm=g_ssm_norm[l], sinks=sinks[l], w_o_ssd=w_o_ssd[l], w_o_attn=w_o_attn[l],
                 w_out=w_out[l], w_router=w_router[l], b_router=b_router[l], w_up=w_up[l],
                 b_up=b_up[l], w_down=w_down[l], b_down=b_down[l])
        kv_zero = jnp.zeros((n_prompt, WINDOW, N_KV_HEADS, ATTN_HEAD_DIM), dtype)
        yp, kp, vp, cp, sp = trunk_layer(
            yp, c_prompt, kv_zero, kv_zero, False,
            jnp.zeros((n_prompt, CONV_W - 1, CONV_DIM), dtype),
            jnp.zeros((n_prompt, N_SSD_HEADS, SSD_HEAD_DIM, D_STATE), dtype), p)
        ys, ks, vs, cs, ss = trunk_layer(
            ys, c_sample, cache_swa_k[l], cache_swa_v[l], True, state_conv[l], state_ssm[l], p)
        kp_l.append(kp); vp_l.append(vp); cp_l.append(cp); sp_l.append(sp)
        ks_l.append(ks); vs_l.append(vs); cs_l.append(cs); ss_l.append(ss)
    return (yp, ys,
            jnp.stack(kp_l), jnp.stack(vp_l), jnp.stack(cp_l), jnp.stack(sp_l),
            jnp.stack(ks_l), jnp.stack(vs_l), jnp.stack(cs_l), jnp.stack(ss_l))
```

```python
import functools

import jax
import jax.numpy as jnp
from jax import lax
from jax.experimental import pallas as pl
from jax.experimental.pallas import tpu as pltpu

F32 = jnp.float32
BF16 = jnp.bfloat16

D_MODEL = 1024
D_INNER = 2 * D_MODEL
SSD_HEAD_DIM = 64
N_SSD_HEADS = D_INNER // SSD_HEAD_DIM
N_SSD_GROUPS = 4
HEADS_PER_GROUP = N_SSD_HEADS // N_SSD_GROUPS
D_STATE = 128
CONV_W = 4
BC_DIM = 2 * N_SSD_GROUPS * D_STATE
CONV_DIM = D_INNER + BC_DIM
CHUNK = 128
ATTN_HEAD_DIM = 64
N_ATTN_HEADS = D_MODEL // ATTN_HEAD_DIM
N_KV_HEADS = 4
KV_REP = N_ATTN_HEADS // N_KV_HEADS
KV_DIM = N_KV_HEADS * ATTN_HEAD_DIM
WINDOW = 128
N_EXPERTS = 32
TOP_K = 4
D_FF = D_MODEL
SWIGLU_LIMIT = 7.0
SWIGLU_ALPHA = 1.702
EPS = 1e-6

LANES = 128
SUBLANES = 8
NEG_BIG = -1e30

COL_Z = 0
COL_X = COL_Z + D_INNER
COL_Q = COL_X + D_INNER
COL_GS = COL_Q + D_MODEL
COL_GA = COL_GS + D_MODEL
COL_BC = COL_GA + D_MODEL
COL_K = COL_BC + BC_DIM
COL_V = COL_K + KV_DIM
COL_DT = COL_V + KV_DIM
PROJ_DIM = COL_DT + LANES
PROJ_TILE_N = PROJ_DIM // 3

SAMPLE_ROWS = 8
MOE_ROWS = 256
VMEM_LIMIT = 48 * 1024 * 1024


def _silu(v):
    return v * jax.nn.sigmoid(v)


def _softplus(v):
    return jnp.maximum(v, 0.0) + jnp.log(1.0 + jnp.exp(-jnp.abs(v)))


def _rms(v):
    return v * lax.rsqrt(jnp.mean(v * v, axis=-1, keepdims=True) + EPS)


def _split3(v):
    hi = v.astype(BF16)
    r1 = v - hi.astype(F32)
    mid = r1.astype(BF16)
    lo = (r1 - mid.astype(F32)).astype(BF16)
    return hi, mid, lo


def _dot(a, b):
    return jnp.dot(a, b, preferred_element_type=F32)


def _dot_nt(a, b):
    return lax.dot_general(a, b, (((1,), (1,)), ((), ())), preferred_element_type=F32)


def _dot_tn(a, b):
    return lax.dot_general(a, b, (((0,), (0,)), ((), ())), preferred_element_type=F32)


def _dot_exact_rhs(a, b_bf16):
    hi, mid, lo = _split3(a)
    return _dot(hi, b_bf16) + _dot(mid, b_bf16) + _dot(lo, b_bf16)


def _ada_kernel(c_ref, w_ref, b_ref, o_ref):
    o_ref[...] = _dot(_silu(c_ref[...]).astype(BF16), w_ref[...].astype(BF16)) + b_ref[...]


def _ada(c, w, b):
    rows, n = c.shape[0], w.shape[1]
    tn = n // 4
    return pl.pallas_call(
        _ada_kernel,
        grid=(n // tn,),
        in_specs=[
            pl.BlockSpec((rows, D_MODEL), lambda j: (0, 0)),
            pl.BlockSpec((D_MODEL, tn), lambda j: (0, j)),
            pl.BlockSpec((1, tn), lambda j: (0, j)),
        ],
        out_specs=pl.BlockSpec((rows, tn), lambda j: (0, j)),
        out_shape=jax.ShapeDtypeStruct((rows, n), F32),
        compiler_params=pltpu.CompilerParams(
            dimension_semantics=("parallel",), vmem_limit_bytes=VMEM_LIMIT),
        name="ada",
    )(c, w, b)


def _in_proj_kernel(x_ref, g_ref, sc_ref, sh_ref, w_ref, o_ref, h_scr):
    @pl.when(pl.program_id(2) == 0)
    def _():
        h = _rms(x_ref[0]) * g_ref[...] * (1.0 + sc_ref[0]) + sh_ref[0]
        h_scr[...] = h.astype(BF16)

    o_ref[0] = _dot(h_scr[...], w_ref[...])


def _mod_spec(per_row, tm):
    if per_row:
        return pl.BlockSpec((1, tm, D_MODEL), lambda b, i, *_: (b, i, 0))
    return pl.BlockSpec((1, 1, D_MODEL), lambda b, i, *_: (b, 0, 0))


def _in_proj(x, g, scale, shift, w, *, tm, per_row):
    nb, rows, _ = x.shape
    return pl.pallas_call(
        _in_proj_kernel,
        grid=(nb, rows // tm, PROJ_DIM // PROJ_TILE_N),
        in_specs=[
            pl.BlockSpec((1, tm, D_MODEL), lambda b, i, j: (b, i, 0)),
            pl.BlockSpec((1, D_MODEL), lambda b, i, j: (0, 0)),
            _mod_spec(per_row, tm),
            _mod_spec(per_row, tm),
            pl.BlockSpec((D_MODEL, PROJ_TILE_N), lambda b, i, j: (0, j)),
        ],
        out_specs=pl.BlockSpec((1, tm, PROJ_TILE_N), lambda b, i, j: (b, i, j)),
        out_shape=jax.ShapeDtypeStruct((nb, rows, PROJ_DIM), F32),
        scratch_shapes=[pltpu.VMEM((tm, D_MODEL), BF16)],
        compiler_params=pltpu.CompilerParams(
            dimension_semantics=("parallel", "parallel", "arbitrary"),
            vmem_limit_bytes=VMEM_LIMIT),
        name="in_proj",
    )(x, g, scale, shift, w)


def _causal_conv(ext_ref, raw, prev, w_ref, b_ref, rows):
    base = SUBLANES
    if prev is not None:
        ext_ref[pl.ds(base - (CONV_W - 1), CONV_W - 1), :] = prev
    ext_ref[pl.ds(base, rows), :] = raw
    out = b_ref[...] + raw * w_ref[pl.ds(CONV_W - 1, 1), :]
    for k in range(CONV_W - 1):
        out = out + ext_ref[pl.ds(base - (CONV_W - 1) + k, rows), :] * w_ref[pl.ds(k, 1), :]
    return out


def _gated_group_norm(y, z, g_ref):
    u = y * _silu(z)
    gw = D_INNER // N_SSD_GROUPS
    parts = []
    for g in range(N_SSD_GROUPS):
        ug = u[:, g * gw:(g + 1) * gw]
        parts.append(ug * lax.rsqrt(jnp.mean(ug * ug, axis=-1, keepdims=True) + EPS))
    return jnp.concatenate(parts, axis=-1) * g_ref[...]


def _ssd_kernel(z_ref, x_ref, bc_ref, dt_ref, cwx_ref, cbx_ref, cwbc_ref, cbbc_ref, dtb_ref, alog_ref,
                dskip_ref, gn_ref, u_ref, fin_ref, extx_scr, extbc_scr, st_scr, y_scr):
    q = CHUNK
    c = pl.program_id(1)

    @pl.when(c == 0)
    def _():
        extx_scr[pl.ds(0, SUBLANES), :] = jnp.zeros((SUBLANES, D_INNER), F32)
        extbc_scr[pl.ds(0, SUBLANES), :] = jnp.zeros((SUBLANES, BC_DIM), F32)
        st_scr[...] = jnp.zeros_like(st_scr)

    xs = _silu(_causal_conv(extx_scr, x_ref[0], None, cwx_ref, cbx_ref, q))
    bc = _silu(_causal_conv(extbc_scr, bc_ref[0], None, cwbc_ref, cbbc_ref, q))
    tail = SUBLANES + q - (CONV_W - 1)
    extx_scr[pl.ds(SUBLANES - (CONV_W - 1), CONV_W - 1), :] = extx_scr[pl.ds(tail, CONV_W - 1), :]
    extbc_scr[pl.ds(SUBLANES - (CONV_W - 1), CONV_W - 1), :] = extbc_scr[pl.ds(tail, CONV_W - 1), :]

    dt = _softplus(dt_ref[0] + dtb_ref[...])
    adt = dt * (-jnp.exp(alog_ref[...]))
    row = lax.broadcasted_iota(jnp.int32, (q, q), 0)
    col = lax.broadcasted_iota(jnp.int32, (q, q), 1)
    causal = row >= col
    tri = jnp.where(causal, 1.0, 0.0).astype(BF16)
    h3, m3, l3 = _split3(adt)
    acs = _dot(tri, h3) + _dot(tri, m3) + _dot(tri, l3)
    eacs = jnp.exp(acs)
    acs_t = acs.T
    dt_t = dt.T
    lane = lax.broadcasted_iota(jnp.int32, (1, LANES), 1)
    lo_half = lane < SSD_HEAD_DIM

    for g in range(N_SSD_GROUPS):
        bm = bc[:, g * D_STATE:(g + 1) * D_STATE]
        cm = bc[:, (N_SSD_GROUPS + g) * D_STATE:(N_SSD_GROUPS + g + 1) * D_STATE]
        cb = _dot_nt(cm.astype(BF16), bm.astype(BF16))
        bm_t = bm.T
        for pr in range(HEADS_PER_GROUP // 2):
            h0 = g * HEADS_PER_GROUP + 2 * pr
            lanes = pl.ds(h0 * SSD_HEAD_DIM, LANES)
            x_pair = xs[:, h0 * SSD_HEAD_DIM:h0 * SSD_HEAD_DIM + LANES]
            st_pair = st_scr[:, lanes]
            lhs_y, lhs_s, decs = [], [], []
            for h in (h0, h0 + 1):
                a_col = acs[:, h:h + 1]
                a_row = acs_t[h:h + 1, :]
                dt_row = dt_t[h:h + 1, :]
                a_last = acs_t[h:h + 1, q - 1:q]
                decay = jnp.where(causal, jnp.exp(jnp.where(causal, a_col - a_row, 0.0)), 0.0)
                lhs_y.append((cb * decay * dt_row).astype(BF16))
                lhs_s.append((bm_t * (dt_row * jnp.exp(a_last - a_row))).astype(BF16))
                decs.append(jnp.exp(a_last))
            for h in (h0, h0 + 1):
                lhs_y.append((cm * eacs[:, h:h + 1]).astype(BF16))
            x_top = jnp.where(lo_half, x_pair, 0.0).astype(BF16)
            x_bot = jnp.where(lo_half, 0.0, x_pair).astype(BF16)
            s_top = jnp.where(lo_half, st_pair, 0.0).astype(BF16)
            s_bot = jnp.where(lo_half, 0.0, st_pair).astype(BF16)
            x_bd = jnp.concatenate([x_top, x_bot], axis=0)
            rhs_y = jnp.concatenate([x_bd, s_top, s_bot], axis=0)
            y_pair = _dot(jnp.concatenate(lhs_y, axis=1), rhs_y)
            ds_pair = _dot(jnp.concatenate(lhs_s, axis=1), x_bd)
            dskip = dskip_ref[:, lanes]
            y_scr[:, lanes] = y_pair + dskip * x_pair
            st_scr[:, lanes] = st_pair * jnp.where(lo_half, decs[0], decs[1]) + ds_pair

    u_ref[0] = _gated_group_norm(y_scr[...], z_ref[0], gn_ref).astype(u_ref.dtype)

    @pl.when(c == pl.num_programs(1) - 1)
    def _():
        for pr in range(N_SSD_HEADS // 2):
            t = st_scr[:, pl.ds(pr * LANES, LANES)].T
            fin_ref[0, 2 * pr] = t[:SSD_HEAD_DIM]
            fin_ref[0, 2 * pr + 1] = t[SSD_HEAD_DIM:]


def _ssd(proj, cwx, cbx, cwbc, cbbc, dtb, alog, dskip, gn):
    nb, rows, _ = proj.shape
    q = CHUNK
    full = lambda shape: pl.BlockSpec(shape, lambda b, c: (0,) * len(shape))
    return pl.pallas_call(
        _ssd_kernel,
        grid=(nb, rows // q),
        in_specs=[
            pl.BlockSpec((1, q, D_INNER), lambda b, c: (b, c, COL_Z // D_INNER)),
            pl.BlockSpec((1, q, D_INNER), lambda b, c: (b, c, COL_X // D_INNER)),
            pl.BlockSpec((1, q, BC_DIM), lambda b, c: (b, c, COL_BC // BC_DIM)),
            pl.BlockSpec((1, q, LANES), lambda b, c: (b, c, COL_DT // LANES)),
            full((CONV_W, D_INNER)), full((1, D_INNER)), full((CONV_W, BC_DIM)), full((1, BC_DIM)),
            full((1, LANES)), full((1, LANES)), full((1, D_INNER)), full((1, D_INNER)),
        ],
        out_specs=[
            pl.BlockSpec((1, q, D_INNER), lambda b, c: (b, c, 0)),
            pl.BlockSpec((1, N_SSD_HEADS, SSD_HEAD_DIM, D_STATE), lambda b, c: (b, 0, 0, 0)),
        ],
        out_shape=[
            jax.ShapeDtypeStruct((nb, rows, D_INNER), BF16),
            jax.ShapeDtypeStruct((nb, N_SSD_HEADS, SSD_HEAD_DIM, D_STATE), F32),
        ],
        scratch_shapes=[
            pltpu.VMEM((SUBLANES + q, D_INNER), F32),
            pltpu.VMEM((SUBLANES + q, BC_DIM), F32),
            pltpu.VMEM((D_STATE, D_INNER), F32),
            pltpu.VMEM((q, D_INNER), F32),
        ],
        compiler_params=pltpu.CompilerParams(
            dimension_semantics=("parallel", "arbitrary"), vmem_limit_bytes=VMEM_LIMIT),
        name="ssd",
    )(proj, proj, proj, proj, cwx, cbx, cwbc, cbbc, dtb, alog, dskip, gn)


def _ssd_step_kernel(n_valid, z_ref, x_ref, bc_ref, dt_ref, px_ref, pbc_ref, st_ref, cwx_ref, cbx_ref,
                     cwbc_ref, cbbc_ref, dtb_ref, alog_ref, dskip_ref, gn_ref, exp_ref,
                     u_ref, nst_ref, extx_scr, extbc_scr):
    q = SAMPLE_ROWS
    gw = D_INNER // N_SSD_GROUPS
    xs = _silu(_causal_conv(extx_scr, x_ref[0], px_ref[0], cwx_ref, cbx_ref, q))
    bc = _silu(_causal_conv(extbc_scr, bc_ref[0], pbc_ref[0], cwbc_ref, cbbc_ref, q))

    rowi = lax.broadcasted_iota(jnp.int32, (q, 1), 0)
    dt = jnp.where(rowi < n_valid, _softplus(dt_ref[0] + dtb_ref[...]), 0.0)
    adt = dt * (-jnp.exp(alog_ref[...]))
    acs = jnp.zeros_like(adt)
    for s in range(n_valid):
        acs = acs + jnp.where(rowi >= s, adt[s:s + 1, :], 0.0)
    expand = exp_ref[...]
    dt_e = _dot_exact_rhs(dt, expand)
    acs_e = _dot_exact_rhs(acs, expand)
    last_e = acs_e[q - 1:q, :]

    x_end = xs * dt_e * jnp.exp(last_e - acs_e)
    dec3 = _split3(jnp.exp(last_e))
    zrow = jnp.zeros((q - 3, D_INNER), BF16)
    dec_rows = jnp.concatenate([dec3[0], dec3[1], dec3[2], zrow], axis=0)
    ones = jnp.ones((q, D_STATE), BF16)

    y_off, cbs = [], []
    for g in range(N_SSD_GROUPS):
        bm = bc[:, g * D_STATE:(g + 1) * D_STATE].astype(BF16)
        cm = bc[:, (N_SSD_GROUPS + g) * D_STATE:(N_SSD_GROUPS + g + 1) * D_STATE].astype(BF16)
        rows = pl.ds(g * gw, gw)
        st = st_ref[0, rows, :]
        y_off.append(_dot_nt(cm, st.astype(BF16)))
        cbs.append(_dot_nt(cm, bm))
        d_st = _dot_tn(x_end[:, g * gw:(g + 1) * gw].astype(BF16), bm)
        dec = _dot_tn(dec_rows[:, g * gw:(g + 1) * gw], ones)
        nst_ref[0, rows, :] = st * dec + d_st

    y = jnp.concatenate(y_off, axis=-1) * jnp.exp(acs_e) + dskip_ref[...] * xs
    for s in range(n_valid):
        keep = rowi >= s
        decay = jnp.where(keep, jnp.exp(jnp.where(keep, acs_e - acs_e[s:s + 1, :], 0.0)), 0.0)
        cb_e = jnp.concatenate([jnp.broadcast_to(cb[:, s:s + 1], (q, gw)) for cb in cbs], axis=-1)
        y = y + decay * cb_e * (dt_e[s:s + 1, :] * xs[s:s + 1, :])
    u_ref[0] = _gated_group_norm(y, z_ref[0], gn_ref).astype(u_ref.dtype)


def _ssd_step(proj, prev_x, prev_bc, state, cwx, cbx, cwbc, cbbc, dtb, alog, dskip, gn, expand, n_valid):
    nb = proj.shape[0]
    q = SAMPLE_ROWS
    full = lambda shape: pl.BlockSpec(shape, lambda b: (0,) * len(shape))
    return pl.pallas_call(
        functools.partial(_ssd_step_kernel, n_valid),
        grid=(nb,),
        in_specs=[
            pl.BlockSpec((1, q, D_INNER), lambda b: (b, 0, COL_Z // D_INNER)),
            pl.BlockSpec((1, q, D_INNER), lambda b: (b, 0, COL_X // D_INNER)),
            pl.BlockSpec((1, q, BC_DIM), lambda b: (b, 0, COL_BC // BC_DIM)),
            pl.BlockSpec((1, q, LANES), lambda b: (b, 0, COL_DT // LANES)),
            pl.BlockSpec((1, CONV_W - 1, D_INNER), lambda b: (b, 0, 0)),
            pl.BlockSpec((1, CONV_W - 1, BC_DIM), lambda b: (b, 0, 0)),
            pl.BlockSpec((1, D_INNER, D_STATE), lambda b: (b, 0, 0)),
            full((CONV_W, D_INNER)), full((1, D_INNER)), full((CONV_W, BC_DIM)), full((1, BC_DIM)),
            full((1, LANES)), full((1, LANES)), full((1, D_INNER)), full((1, D_INNER)),
            full((LANES, D_INNER)),
        ],
        out_specs=[
            pl.BlockSpec((1, q, D_INNER), lambda b: (b, 0, 0)),
            pl.BlockSpec((1, D_INNER, D_STATE), lambda b: (b, 0, 0)),
        ],
        out_shape=[
            jax.ShapeDtypeStruct((nb, q, D_INNER), BF16),
            jax.ShapeDtypeStruct((nb, D_INNER, D_STATE), F32),
        ],
        scratch_shapes=[
            pltpu.VMEM((2 * SUBLANES, D_INNER), F32),
            pltpu.VMEM((2 * SUBLANES, BC_DIM), F32),
        ],
        compiler_params=pltpu.CompilerParams(
            dimension_semantics=("parallel",), vmem_limit_bytes=VMEM_LIMIT),
        name="ssd_step",
    )(proj, proj, proj, proj, prev_x, prev_bc, state, cwx, cbx, cwbc, cbbc, dtb, alog, dskip, gn, expand)


def _attention_kernel(qb, prev_always_valid, sinks_ref, q_ref, kp_ref, vp_ref, kc_ref, vc_ref, o_ref):
    wb = WINDOW
    lane = lax.broadcasted_iota(jnp.int32, (1, LANES), 1)
    lo_half = lane < ATTN_HEAD_DIM
    t_p = lax.broadcasted_iota(jnp.int32, (qb, wb), 0)
    s_p = lax.broadcasted_iota(jnp.int32, (qb, wb), 1)
    rel_p = wb + t_p - s_p
    ok_p = rel_p < WINDOW
    if not prev_always_valid:
        ok_p = jnp.logical_and(ok_p, pl.program_id(1) > 0)
    t_c = lax.broadcasted_iota(jnp.int32, (qb, qb), 0)
    s_c = lax.broadcasted_iota(jnp.int32, (qb, qb), 1)
    rel_c = t_c - s_c
    ok_c = rel_c >= 0
    rel_p = rel_p.astype(F32)
    rel_c = rel_c.astype(F32)
    scale = ATTN_HEAD_DIM ** -0.5

    for pr in range(N_ATTN_HEADS // 2):
        q_pair = q_ref[0, :, pl.ds(pr * LANES, LANES)]
        outs = []
        for half in range(2):
            h = 2 * pr + half
            kv = h // KV_REP
            kv_half = kv % 2
            kv_lanes = pl.ds((kv // 2) * LANES, LANES)
            keep = lo_half if kv_half == 0 else jnp.logical_not(lo_half)
            qa = q_pair if half == kv_half else pltpu.roll(q_pair, ATTN_HEAD_DIM, 1)
            qa = jnp.where(keep, qa, 0.0).astype(BF16)
            slope = 2.0 ** (-8.0 * (h + 1) / N_ATTN_HEADS)
            sc_p = _dot_nt(qa, kp_ref[0, :, kv_lanes].astype(BF16)) * scale - slope * rel_p
            sc_c = _dot_nt(qa, kc_ref[0, :, kv_lanes].astype(BF16)) * scale - slope * rel_c
            sc_p = jnp.where(ok_p, sc_p, NEG_BIG)
            sc_c = jnp.where(ok_c, sc_c, NEG_BIG)
            sink = sinks_ref[h]
            m = jnp.maximum(jnp.maximum(jnp.max(sc_p, axis=-1, keepdims=True),
                                        jnp.max(sc_c, axis=-1, keepdims=True)), sink)
            p_p = jnp.exp(sc_p - m)
            p_c = jnp.exp(sc_c - m)
            denom = (jnp.sum(p_p, axis=-1, keepdims=True) + jnp.sum(p_c, axis=-1, keepdims=True)
                     + jnp.exp(sink - m))
            o = (_dot(p_p.astype(BF16), vp_ref[0, :, kv_lanes].astype(BF16))
                 + _dot(p_c.astype(BF16), vc_ref[0, :, kv_lanes].astype(BF16))) / denom
            outs.append(o if half == kv_half else pltpu.roll(o, ATTN_HEAD_DIM, 1))
        o_ref[0, :, pl.ds(pr * LANES, LANES)] = jnp.where(lo_half, outs[0], outs[1]).astype(o_ref.dtype)


def _attention(proj, sinks, qb, prev_kv=None):
    nb, rows, _ = proj.shape
    wb = WINDOW
    cur = lambda col: pl.BlockSpec((1, qb, KV_DIM), lambda b, n: (b, n, col // KV_DIM))
    if prev_kv is None:
        prev = lambda col: pl.BlockSpec((1, wb, KV_DIM), lambda b, n: (b, jnp.maximum(n - 1, 0), col // KV_DIM))
        kp, vp = proj, proj
        prev_specs = [prev(COL_K), prev(COL_V)]
    else:
        kp, vp = prev_kv
        prev_specs = [pl.BlockSpec((1, wb, KV_DIM), lambda b, n: (b, 0, 0))] * 2
    return pl.pallas_call(
        functools.partial(_attention_kernel, qb, prev_kv is not None),
        grid=(nb, rows // qb),
        in_specs=[
            pl.BlockSpec(memory_space=pltpu.SMEM),
            pl.BlockSpec((1, qb, D_MODEL), lambda b, n: (b, n, COL_Q // D_MODEL)),
            *prev_specs,
            cur(COL_K), cur(COL_V),
        ],
        out_specs=pl.BlockSpec((1, qb, D_MODEL), lambda b, n: (b, n, 0)),
        out_shape=jax.ShapeDtypeStruct((nb, rows, D_MODEL), BF16),
        compiler_params=pltpu.CompilerParams(
            dimension_semantics=("parallel", "arbitrary"), vmem_limit_bytes=VMEM_LIMIT),
        name="attention",
    )(sinks, proj, kp, vp, proj, proj)


def _post_mix_kernel(u_ref, a_ref, gs_ref, ga_ref, x_ref, wos_ref, woa_ref, wout_ref, gpm_ref, gpf_ref,
                     g1_ref, sc2_ref, sh2_ref, wr_ref, br_ref, x1_ref, h2_ref, ti_ref, tg_ref):
    y_ssd = _dot(u_ref[0], wos_ref[...])
    y_attn = _dot(a_ref[0], woa_ref[...])
    merged = jax.nn.sigmoid(gs_ref[0]) * y_ssd + jax.nn.sigmoid(ga_ref[0]) * y_attn
    mixed = _dot(merged.astype(BF16), wout_ref[...])
    x1 = x_ref[0] + g1_ref[0] * (_rms(mixed) * gpm_ref[...])
    x1_ref[0] = x1
    h2 = _rms(x1) * gpf_ref[...] * (1.0 + sc2_ref[0]) + sh2_ref[0]
    h2_ref[0] = h2.astype(BF16)

    h_hi = h2.astype(BF16)
    h_lo = (h2 - h_hi.astype(F32)).astype(BF16)
    w = wr_ref[...]
    w_hi = w.astype(BF16)
    w_lo = (w - w_hi.astype(F32)).astype(BF16)
    logits = _dot(h_hi, w_hi) + (_dot(h_hi, w_lo) + _dot(h_lo, w_hi)) + br_ref[...]
    lane = lax.broadcasted_iota(jnp.int32, logits.shape, 1)
    idx_out = jnp.zeros(logits.shape, jnp.int32)
    val_out = jnp.zeros(logits.shape, F32)
    top = None
    denom = None
    for k in range(TOP_K):
        m = jnp.max(logits, axis=-1, keepdims=True)
        idx = jnp.min(jnp.where(logits == m, lane, LANES), axis=-1, keepdims=True)
        if k == 0:
            top = m
            e = jnp.ones_like(m)
            denom = e
        else:
            e = jnp.exp(m - top)
            denom = denom + e
        idx_out = jnp.where(lane == k, idx, idx_out)
        val_out = jnp.where(lane == k, e, val_out)
        logits = jnp.where(lane == idx, NEG_BIG * 2, logits)
    ti_ref[0] = idx_out
    tg_ref[0] = val_out / denom


def _post_mix(u, attn, proj, x, wos, woa, wout, gpm, gpf, gate1, scale2, shift2, wr, br, *, tm, per_row):
    nb, rows, _ = x.shape
    row_spec = lambda w, col=0: pl.BlockSpec((1, tm, w), lambda b, i: (b, i, col // w))
    full = lambda shape: pl.BlockSpec(shape, lambda b, i: (0,) * len(shape))
    return pl.pallas_call(
        _post_mix_kernel,
        grid=(nb, rows // tm),
        in_specs=[
            row_spec(D_INNER), row_spec(D_MODEL), row_spec(D_MODEL, COL_GS), row_spec(D_MODEL, COL_GA),
            row_spec(D_MODEL),
            full((D_INNER, D_MODEL)), full((D_MODEL, D_MODEL)), full((D_MODEL, D_MODEL)),
            full((1, D_MODEL)), full((1, D_MODEL)),
            _mod_spec(per_row, tm), _mod_spec(per_row, tm), _mod_spec(per_row, tm),
            full((D_MODEL, LANES)), full((1, LANES)),
        ],
        out_specs=[row_spec(D_MODEL), row_spec(D_MODEL), row_spec(LANES), row_spec(LANES)],
        out_shape=[
            jax.ShapeDtypeStruct((nb, rows, D_MODEL), F32),
            jax.ShapeDtypeStruct((nb, rows, D_MODEL), BF16),
            jax.ShapeDtypeStruct((nb, rows, LANES), jnp.int32),
            jax.ShapeDtypeStruct((nb, rows, LANES), F32),
        ],
        compiler_params=pltpu.CompilerParams(
            dimension_semantics=("parallel", "parallel"), vmem_limit_bytes=VMEM_LIMIT),
        name="post_mix",
    )(u, attn, proj, proj, x, wos, woa, wout, gpm, gpf, gate1, scale2, shift2, wr, br)


def _moe_kernel(be_ref, na_ref, x_ref, wu_ref, bu_ref, wd_ref, bd_ref, o_ref):
    i = pl.program_id(0)

    @pl.when(i < na_ref[0])
    def _():
        up = _dot(x_ref[...], wu_ref[0]) + bu_ref[0]
        glu = jnp.minimum(up[:, :D_FF], SWIGLU_LIMIT)
        lin = jnp.clip(up[:, D_FF:], -SWIGLU_LIMIT, SWIGLU_LIMIT)
        act = glu * jax.nn.sigmoid(SWIGLU_ALPHA * glu) * (lin + 1.0)
        o_ref[...] = _dot(act.astype(BF16), wd_ref[0]) + bd_ref[0]

    @pl.when(i >= na_ref[0])
    def _():
        o_ref[...] = jnp.zeros_like(o_ref)


def _moe(xb, block_expert, n_active, wu, bu, wd, bd):
    n_rows = xb.shape[0]
    grid_spec = pltpu.PrefetchScalarGridSpec(
        num_scalar_prefetch=2,
        grid=(n_rows // MOE_ROWS,),
        in_specs=[
            pl.BlockSpec((MOE_ROWS, D_MODEL), lambda i, be, na: (i, 0)),
            pl.BlockSpec((1, D_MODEL, 2 * D_FF), lambda i, be, na: (be[i], 0, 0)),
            pl.BlockSpec((1, 1, 2 * D_FF), lambda i, be, na: (be[i], 0, 0)),
            pl.BlockSpec((1, D_FF, D_MODEL), lambda i, be, na: (be[i], 0, 0)),
            pl.BlockSpec((1, 1, D_MODEL), lambda i, be, na: (be[i], 0, 0)),
        ],
        out_specs=pl.BlockSpec((MOE_ROWS, D_MODEL), lambda i, be, na: (i, 0)),
    )
    return pl.pallas_call(
        _moe_kernel,
        grid_spec=grid_spec,
        out_shape=jax.ShapeDtypeStruct((n_rows, D_MODEL), F32),
        compiler_params=pltpu.CompilerParams(
            dimension_semantics=("arbitrary",), vmem_limit_bytes=VMEM_LIMIT),
        name="moe",
    )(block_expert, n_active, xb, wu, bu, wd, bd)


def _combine_kernel(ys_ref, tg_ref, x1_ref, gpost_ref, g2_ref, o_ref):
    gates = tg_ref[0]
    f = gates[:, 0:1] * ys_ref[:, pl.ds(0, D_MODEL)]
    for k in range(1, TOP_K):
        f = f + gates[:, k:k + 1] * ys_ref[:, pl.ds(k * D_MODEL, D_MODEL)]
    o_ref[0] = x1_ref[0] + g2_ref[0] * (_rms(f) * gpost_ref[...])


def _combine(ys, row0, tg, x1, gpost, gate2, *, tm, per_row):
    nb, rows, _ = x1.shape
    per_b = rows // tm
    base = row0 // tm
    return pl.pallas_call(
        _combine_kernel,
        grid=(nb, per_b),
        in_specs=[
            pl.BlockSpec((tm, TOP_K * D_MODEL), lambda b, i: (base + b * per_b + i, 0)),
            pl.BlockSpec((1, tm, LANES), lambda b, i: (b, i, 0)),
            pl.BlockSpec((1, tm, D_MODEL), lambda b, i: (b, i, 0)),
            pl.BlockSpec((1, D_MODEL), lambda b, i: (0, 0)),
            _mod_spec(per_row, tm),
        ],
        out_specs=pl.BlockSpec((1, tm, D_MODEL), lambda b, i: (b, i, 0)),
        out_shape=jax.ShapeDtypeStruct((nb, rows, D_MODEL), F32),
        compiler_params=pltpu.CompilerParams(
            dimension_semantics=("parallel", "parallel"), vmem_limit_bytes=VMEM_LIMIT),
        name="combine",
    )(ys, tg, x1, gpost, gate2)


def _route(top_idx, n_tokens):
    n_slots = n_tokens * TOP_K
    slot_expert = top_idx.reshape(-1)
    order = jnp.argsort(slot_expert)
    sorted_expert = slot_expert[order]
    counts = jnp.bincount(slot_expert, length=N_EXPERTS)
    padded = (counts + MOE_ROWS - 1) // MOE_ROWS * MOE_ROWS
    group_start = jnp.cumsum(counts) - counts
    padded_end = jnp.cumsum(padded)
    padded_start = padded_end - padded
    dest_sorted = (padded_start[sorted_expert] + jnp.arange(n_slots) - group_start[sorted_expert]).astype(jnp.int32)
    n_blocks = -(-n_slots // MOE_ROWS) + N_EXPERTS
    buf_token = jnp.full((n_blocks * MOE_ROWS,), n_tokens, jnp.int32).at[dest_sorted].set(
        (order // TOP_K).astype(jnp.int32))
    slot_dest = jnp.zeros((n_slots,), jnp.int32).at[order].set(dest_sorted)
    block_expert = jnp.minimum(
        jnp.searchsorted(padded_end, jnp.arange(n_blocks) * MOE_ROWS, side='right'), N_EXPERTS - 1)
    last_expert = jnp.max(jnp.where(counts > 0, jnp.arange(N_EXPERTS), 0))
    n_active = (padded_end[-1] // MOE_ROWS).astype(jnp.int32)
    block_expert = jnp.where(jnp.arange(n_blocks) < n_active, block_expert, last_expert).astype(jnp.int32)
    return buf_token, slot_dest, block_expert, n_active.reshape(1)


def _pad_lanes(v, value=0.0):
    return jnp.pad(v, [(0, 0)] * (v.ndim - 1) + [(0, LANES - v.shape[-1])], constant_values=value)


def kernel(x_prompt, x_sample, c_prompt, c_sample, cache_swa_k, cache_swa_v, state_conv, state_ssm, w_ada, b_ada, g_pre_mix, g_post_mix, g_pre_ffn, g_post_ffn, w_in, conv_w, conv_b, dt_bias, a_log, d_skip, g_ssm_norm, sinks, w_o_ssd, w_o_attn, w_out, w_router, b_router, w_up, b_up, w_down, b_down):
    depth = w_ada.shape[0]
    n_prompt, seq, _ = x_prompt.shape
    n_sample, dec_seq, _ = x_sample.shape
    yp = x_prompt
    ys_pad = jnp.pad(x_sample, ((0, 0), (0, SAMPLE_ROWS - dec_seq), (0, 0)))
    outs = [[] for _ in range(8)]
    expand = jnp.repeat(jnp.eye(LANES, N_SSD_HEADS, dtype=BF16), SSD_HEAD_DIM, axis=1)
    rows_s = n_sample * SAMPLE_ROWS
    n_p = n_prompt * seq
    n_s = n_sample * dec_seq
    n_tok = n_p + n_s
    tm_p = min(512, seq)
    tm_s = min(512, rows_s)
    tm_mix_p = min(256, seq)
    tm_mix_s = min(256, rows_s)
    assert seq % CHUNK == 0 and seq % tm_p == 0 and rows_s % tm_s == 0 and n_p % n_s == 0
    c_all = jnp.concatenate([c_prompt, c_sample], axis=0)
    c_rows = -(-c_all.shape[0] // SUBLANES) * SUBLANES
    c_all = jnp.pad(c_all, ((0, c_rows - c_all.shape[0]), (0, 0)))

    for l in range(depth):
        wi = w_in[l]
        o_xbc = D_INNER
        o_dt = o_xbc + CONV_DIM
        o_q = o_dt + N_SSD_HEADS
        o_k = o_q + D_MODEL
        o_v = o_k + KV_DIM
        o_gs = o_v + KV_DIM
        o_ga = o_gs + D_MODEL
        w_proj = jnp.concatenate([
            wi[:, :o_xbc], wi[:, o_xbc:o_xbc + D_INNER], wi[:, o_q:o_k], wi[:, o_gs:o_ga], wi[:, o_ga:],
            wi[:, o_xbc + D_INNER:o_dt], wi[:, o_k:o_v], wi[:, o_v:o_gs], _pad_lanes(wi[:, o_dt:o_q])],
            axis=1).astype(BF16)
        cwx, cwbc = conv_w[l][:, :D_INNER], conv_w[l][:, D_INNER:]
        cbx, cbbc = conv_b[l][None, :D_INNER], conv_b[l][None, D_INNER:]
        dtb = _pad_lanes(dt_bias[l][None])
        alog = _pad_lanes(a_log[l][None])
        dskip = jnp.repeat(d_skip[l], SSD_HEAD_DIM)[None]
        gn = g_ssm_norm[l][None]
        wos, woa, wout = w_o_ssd[l].astype(BF16), w_o_attn[l].astype(BF16), w_out[l].astype(BF16)
        wr = _pad_lanes(w_router[l])
        br = _pad_lanes(b_router[l][None], NEG_BIG)
        wu, wd = w_up[l].astype(BF16), w_down[l].astype(BF16)
        bu, bd = b_up[l][:, None, :], b_down[l][:, None, :]
        g_pm, g_pom, g_pf, g_pof = (v[l][None] for v in (g_pre_mix, g_post_mix, g_pre_ffn, g_post_ffn))

        ada = _ada(c_all, w_ada[l], b_ada[l][None])
        ada_p = [ada[:n_prompt, k * D_MODEL:(k + 1) * D_MODEL] for k in range(6)]
        ada_s = [ada[n_prompt:n_prompt + n_sample, k * D_MODEL:(k + 1) * D_MODEL] for k in range(6)]

        mods = [m[:, None, :] for m in ada_p]
        proj = _in_proj(yp, g_pm, mods[1], mods[0], w_proj, tm=tm_p, per_row=False)
        u, ssm_p = _ssd(proj, cwx, cbx, cwbc, cbbc, dtb, alog, dskip, gn)
        attn = _attention(proj, sinks[l], WINDOW)
        x1_p, h2_p, ti_p, tg_p = _post_mix(u, attn, proj, yp, wos, woa, wout, g_pom, g_pf,
                                           mods[2], mods[4], mods[3], wr, br, tm=tm_mix_p, per_row=False)
        gate2_p = mods[5]
        kp = proj[:, seq - WINDOW:, COL_K:COL_K + KV_DIM].reshape(n_prompt, WINDOW, N_KV_HEADS, ATTN_HEAD_DIM)
        vp = proj[:, seq - WINDOW:, COL_V:COL_V + KV_DIM].reshape(n_prompt, WINDOW, N_KV_HEADS, ATTN_HEAD_DIM)
        cp = jnp.concatenate([proj[:, seq - (CONV_W - 1):, COL_X:COL_X + D_INNER],
                              proj[:, seq - (CONV_W - 1):, COL_BC:COL_BC + BC_DIM]], axis=-1)

        mods_s = [jnp.repeat(m, SAMPLE_ROWS, axis=0)[None] for m in ada_s]
        xs_flat = ys_pad.reshape(1, rows_s, D_MODEL)
        proj_s = _in_proj(xs_flat, g_pm, mods_s[1], mods_s[0], w_proj, tm=tm_s, per_row=True)
        proj_sb = proj_s.reshape(n_sample, SAMPLE_ROWS, PROJ_DIM)
        u_s, ssm_s = _ssd_step(
            proj_sb, state_conv[l][:, :, :D_INNER], state_conv[l][:, :, D_INNER:],
            state_ssm[l].reshape(n_sample, D_INNER, D_STATE),
            cwx, cbx, cwbc, cbbc, dtb, alog, dskip, gn, expand, dec_seq)
        k_prev = cache_swa_k[l].reshape(n_sample, -1, KV_DIM)
        v_prev = cache_swa_v[l].reshape(n_sample, -1, KV_DIM)
        attn_s = _attention(proj_sb, sinks[l], SAMPLE_ROWS, prev_kv=(k_prev, v_prev))
        x1_s, h2_s, ti_s, tg_s = _post_mix(
            u_s.reshape(1, rows_s, D_INNER), attn_s.reshape(1, rows_s, D_MODEL), proj_s, xs_flat,
            wos, woa, wout, g_pom, g_pf, mods_s[2], mods_s[4], mods_s[3], wr, br, tm=tm_mix_s, per_row=True)
        wb = k_prev.shape[1]
        k_new = proj_sb[:, :dec_seq, COL_K:COL_K + KV_DIM]
        v_new = proj_sb[:, :dec_seq, COL_V:COL_V + KV_DIM]
        ks = jnp.concatenate([k_prev, k_new], axis=1)[:, -wb:].reshape(n_sample, wb, N_KV_HEADS, ATTN_HEAD_DIM)
        vs = jnp.concatenate([v_prev, v_new], axis=1)[:, -wb:].reshape(n_sample, wb, N_KV_HEADS, ATTN_HEAD_DIM)
        raw_xbc = jnp.concatenate([proj_sb[:, :dec_seq, COL_X:COL_X + D_INNER],
                                   proj_sb[:, :dec_seq, COL_BC:COL_BC + BC_DIM]], axis=-1)
        cs = jnp.concatenate([state_conv[l], raw_xbc], axis=1)[:, -(CONV_W - 1):]

        valid = lambda v: v.reshape(n_sample, SAMPLE_ROWS, -1)[:, :dec_seq].reshape(n_s, -1)
        h2_all = jnp.concatenate([h2_p.reshape(n_p, D_MODEL), valid(h2_s)], axis=0)
        ti_all = jnp.concatenate([ti_p.reshape(n_p, LANES), valid(ti_s)], axis=0)[:, :TOP_K]
        buf_token, slot_dest, block_expert, n_active = _route(ti_all, n_tok)
        h2_ext = jnp.concatenate([h2_all, jnp.zeros((1, D_MODEL), BF16)], axis=0)
        xb = h2_ext[buf_token]
        yb = _moe(xb, block_expert, n_active, wu, bu, wd, bd)
        ysel = yb[slot_dest].reshape(n_tok, TOP_K * D_MODEL)

        yp = _combine(ysel, 0, tg_p, x1_p, g_pof, gate2_p, tm=tm_p, per_row=False)
        x1_sv = valid(x1_s)[None]
        tg_sv = valid(tg_s)[None]
        gate2_s = jnp.repeat(ada_s[5], dec_seq, axis=0)[None]
        ys_new = _combine(ysel, n_p, tg_sv, x1_sv, g_pof, gate2_s, tm=n_s, per_row=True)
        ys_new = ys_new.reshape(n_sample, dec_seq, D_MODEL)
        ys_pad = jnp.pad(ys_new, ((0, 0), (0, SAMPLE_ROWS - dec_seq), (0, 0)))

        for lst, v in zip(outs, (kp, vp, cp, ssm_p, ks, vs, cs,
                                 ssm_s.reshape(n_sample, N_SSD_HEADS, SSD_HEAD_DIM, D_STATE))):
            lst.append(v)

    return (yp, ys_pad[:, :dec_seq], *[jnp.stack(v) for v in outs])
```

```python
import functools

import jax
import jax.numpy as jnp
from jax import lax
from jax.experimental import pallas as pl
from jax.experimental.pallas import tpu as pltpu

F32 = jnp.float32
BF16 = jnp.bfloat16

D_MODEL = 1024
D_INNER = 2 * D_MODEL
SSD_HEAD_DIM = 64
N_SSD_HEADS = D_INNER // SSD_HEAD_DIM
N_SSD_GROUPS = 4
HEADS_PER_GROUP = N_SSD_HEADS // N_SSD_GROUPS
D_STATE = 128
CONV_W = 4
BC_DIM = 2 * N_SSD_GROUPS * D_STATE
CONV_DIM = D_INNER + BC_DIM
CHUNK = 128
ATTN_HEAD_DIM = 64
N_ATTN_HEADS = D_MODEL // ATTN_HEAD_DIM
N_KV_HEADS = 4
KV_REP = N_ATTN_HEADS // N_KV_HEADS
KV_DIM = N_KV_HEADS * ATTN_HEAD_DIM
WINDOW = 128
N_EXPERTS = 32
TOP_K = 4
D_FF = D_MODEL
SWIGLU_LIMIT = 7.0
SWIGLU_ALPHA = 1.702
EPS = 1e-6

LANES = 128
SUBLANES = 8
NEG_BIG = -1e30

COL_Z = 0
COL_X = COL_Z + D_INNER
COL_Q = COL_X + D_INNER
COL_GS = COL_Q + D_MODEL
COL_GA = COL_GS + D_MODEL
COL_BC = COL_GA + D_MODEL
COL_K = COL_BC + BC_DIM
COL_V = COL_K + KV_DIM
COL_DT = COL_V + KV_DIM
PROJ_DIM = COL_DT + LANES
PROJ_TILE_N = PROJ_DIM // 3

ATTN_HEAD_ORDER = tuple(2 * KV_REP * (j // KV_REP) + (j % KV_REP) + KV_REP * hf
                        for j in range(N_ATTN_HEADS // 2) for hf in (0, 1))

SAMPLE_ROWS = 8
MOE_ROWS = 256
VMEM_LIMIT = 48 * 1024 * 1024


def _silu(v):
    return v * jax.nn.sigmoid(v)


def _softplus(v):
    return jnp.maximum(v, 0.0) + jnp.log(1.0 + jnp.exp(-jnp.abs(v)))


def _rms(v):
    return v * lax.rsqrt(jnp.mean(v * v, axis=-1, keepdims=True) + EPS)


def _split3(v):
    hi = v.astype(BF16)
    r1 = v - hi.astype(F32)
    mid = r1.astype(BF16)
    lo = (r1 - mid.astype(F32)).astype(BF16)
    return hi, mid, lo


def _dot(a, b):
    return jnp.dot(a, b, preferred_element_type=F32)


def _dot_nt(a, b):
    return lax.dot_general(a, b, (((1,), (1,)), ((), ())), preferred_element_type=F32)


def _dot_tn(a, b):
    return lax.dot_general(a, b, (((0,), (0,)), ((), ())), preferred_element_type=F32)


def _dot_exact_rhs(a, b_bf16):
    hi, mid, lo = _split3(a)
    return _dot(hi, b_bf16) + _dot(mid, b_bf16) + _dot(lo, b_bf16)


def _ada_kernel(c_ref, w_ref, b_ref, o_ref):
    o_ref[...] = _dot(_silu(c_ref[...]).astype(BF16), w_ref[...].astype(BF16)) + b_ref[...]


def _ada(c, w, b):
    rows, n = c.shape[0], w.shape[1]
    tn = n // 4
    return pl.pallas_call(
        _ada_kernel,
        grid=(n // tn,),
        in_specs=[
            pl.BlockSpec((rows, D_MODEL), lambda j: (0, 0)),
            pl.BlockSpec((D_MODEL, tn), lambda j: (0, j)),
            pl.BlockSpec((1, tn), lambda j: (0, j)),
        ],
        out_specs=pl.BlockSpec((rows, tn), lambda j: (0, j)),
        out_shape=jax.ShapeDtypeStruct((rows, n), F32),
        compiler_params=pltpu.CompilerParams(
            dimension_semantics=("parallel",), vmem_limit_bytes=VMEM_LIMIT),
        name="ada",
    )(c, w, b)


def _in_proj_kernel(x_ref, g_ref, sc_ref, sh_ref, w_ref, o_ref, h_scr):
    @pl.when(pl.program_id(2) == 0)
    def _():
        h = _rms(x_ref[0]) * g_ref[...] * (1.0 + sc_ref[0]) + sh_ref[0]
        h_scr[...] = h.astype(BF16)

    o_ref[0] = _dot(h_scr[...], w_ref[...])


def _mod_spec(per_row, tm):
    if per_row:
        return pl.BlockSpec((1, tm, D_MODEL), lambda b, i, *_: (b, i, 0))
    return pl.BlockSpec((1, 1, D_MODEL), lambda b, i, *_: (b, 0, 0))


def _in_proj(x, g, scale, shift, w, *, tm, per_row):
    nb, rows, _ = x.shape
    return pl.pallas_call(
        _in_proj_kernel,
        grid=(nb, rows // tm, PROJ_DIM // PROJ_TILE_N),
        in_specs=[
            pl.BlockSpec((1, tm, D_MODEL), lambda b, i, j: (b, i, 0)),
            pl.BlockSpec((1, D_MODEL), lambda b, i, j: (0, 0)),
            _mod_spec(per_row, tm),
            _mod_spec(per_row, tm),
            pl.BlockSpec((D_MODEL, PROJ_TILE_N), lambda b, i, j: (0, j)),
        ],
        out_specs=pl.BlockSpec((1, tm, PROJ_TILE_N), lambda b, i, j: (b, i, j)),
        out_shape=jax.ShapeDtypeStruct((nb, rows, PROJ_DIM), F32),
        scratch_shapes=[pltpu.VMEM((tm, D_MODEL), BF16)],
        compiler_params=pltpu.CompilerParams(
            dimension_semantics=("parallel", "parallel", "arbitrary"),
            vmem_limit_bytes=VMEM_LIMIT),
        name="in_proj",
    )(x, g, scale, shift, w)


def _causal_conv(ext_ref, raw, prev, w_ref, b_ref, rows):
    base = SUBLANES
    if prev is not None:
        ext_ref[pl.ds(base - (CONV_W - 1), CONV_W - 1), :] = prev
    ext_ref[pl.ds(base, rows), :] = raw
    out = b_ref[...] + raw * w_ref[pl.ds(CONV_W - 1, 1), :]
    for k in range(CONV_W - 1):
        out = out + ext_ref[pl.ds(base - (CONV_W - 1) + k, rows), :] * w_ref[pl.ds(k, 1), :]
    return out


def _gated_group_norm(y, z, g_ref):
    u = y * _silu(z)
    gw = D_INNER // N_SSD_GROUPS
    parts = []
    for g in range(N_SSD_GROUPS):
        ug = u[:, g * gw:(g + 1) * gw]
        parts.append(ug * lax.rsqrt(jnp.mean(ug * ug, axis=-1, keepdims=True) + EPS))
    return jnp.concatenate(parts, axis=-1) * g_ref[...]


def _ssd_kernel(z_ref, x_ref, bc_ref, dt_ref, cwx_ref, cbx_ref, cwbc_ref, cbbc_ref, dtb_ref, alog_ref,
                dskip_ref, gn_ref, u_ref, fin_ref, extx_scr, extbc_scr, st_scr, y_scr):
    q = CHUNK
    c = pl.program_id(1)

    @pl.when(c == 0)
    def _():
        extx_scr[pl.ds(0, SUBLANES), :] = jnp.zeros((SUBLANES, D_INNER), F32)
        extbc_scr[pl.ds(0, SUBLANES), :] = jnp.zeros((SUBLANES, BC_DIM), F32)
        st_scr[...] = jnp.zeros_like(st_scr)

    xs = _silu(_causal_conv(extx_scr, x_ref[0], None, cwx_ref, cbx_ref, q))
    bc = _silu(_causal_conv(extbc_scr, bc_ref[0], None, cwbc_ref, cbbc_ref, q))
    tail = SUBLANES + q - (CONV_W - 1)
    extx_scr[pl.ds(SUBLANES - (CONV_W - 1), CONV_W - 1), :] = extx_scr[pl.ds(tail, CONV_W - 1), :]
    extbc_scr[pl.ds(SUBLANES - (CONV_W - 1), CONV_W - 1), :] = extbc_scr[pl.ds(tail, CONV_W - 1), :]

    dt = _softplus(dt_ref[0] + dtb_ref[...])
    adt = dt * (-jnp.exp(alog_ref[...]))
    row = lax.broadcasted_iota(jnp.int32, (q, q), 0)
    col = lax.broadcasted_iota(jnp.int32, (q, q), 1)
    causal = row >= col
    tri = jnp.where(causal, 1.0, 0.0).astype(BF16)
    h3, m3, l3 = _split3(adt)
    acs = _dot(tri, h3) + _dot(tri, m3) + _dot(tri, l3)
    eacs = jnp.exp(acs)
    acs_t = acs.T
    dt_t = dt.T
    lane = lax.broadcasted_iota(jnp.int32, (1, LANES), 1)
    lo_half = lane < SSD_HEAD_DIM

    for g in range(N_SSD_GROUPS):
        bm = bc[:, g * D_STATE:(g + 1) * D_STATE]
        cm = bc[:, (N_SSD_GROUPS + g) * D_STATE:(N_SSD_GROUPS + g + 1) * D_STATE]
        cb = _dot_nt(cm.astype(BF16), bm.astype(BF16))
        bm_t = bm.T
        for pr in range(HEADS_PER_GROUP // 2):
            h0 = g * HEADS_PER_GROUP + 2 * pr
            lanes = pl.ds(h0 * SSD_HEAD_DIM, LANES)
            x_pair = xs[:, h0 * SSD_HEAD_DIM:h0 * SSD_HEAD_DIM + LANES]
            st_pair = st_scr[:, lanes]
            lhs_y, lhs_s, decs = [], [], []
            for h in (h0, h0 + 1):
                a_col = acs[:, h:h + 1]
                a_row = acs_t[h:h + 1, :]
                dt_row = dt_t[h:h + 1, :]
                a_last = acs_t[h:h + 1, q - 1:q]
                decay = jnp.where(causal, jnp.exp(jnp.where(causal, a_col - a_row, 0.0)), 0.0)
                lhs_y.append((cb * decay * dt_row).astype(BF16))
                lhs_s.append((bm_t * (dt_row * jnp.exp(a_last - a_row))).astype(BF16))
                decs.append(jnp.exp(a_last))
            for h in (h0, h0 + 1):
                lhs_y.append((cm * eacs[:, h:h + 1]).astype(BF16))
            x_top = jnp.where(lo_half, x_pair, 0.0).astype(BF16)
            x_bot = jnp.where(lo_half, 0.0, x_pair).astype(BF16)
            s_top = jnp.where(lo_half, st_pair, 0.0).astype(BF16)
            s_bot = jnp.where(lo_half, 0.0, st_pair).astype(BF16)
            x_bd = jnp.concatenate([x_top, x_bot], axis=0)
            rhs_y = jnp.concatenate([x_bd, s_top, s_bot], axis=0)
            y_pair = _dot(jnp.concatenate(lhs_y, axis=1), rhs_y)
            ds_pair = _dot(jnp.concatenate(lhs_s, axis=1), x_bd)
            dskip = dskip_ref[:, lanes]
            y_scr[:, lanes] = y_pair + dskip * x_pair
            st_scr[:, lanes] = st_pair * jnp.where(lo_half, decs[0], decs[1]) + ds_pair

    u_ref[0] = _gated_group_norm(y_scr[...], z_ref[0], gn_ref).astype(u_ref.dtype)

    @pl.when(c == pl.num_programs(1) - 1)
    def _():
        for pr in range(N_SSD_HEADS // 2):
            t = st_scr[:, pl.ds(pr * LANES, LANES)].T
            fin_ref[0, 2 * pr] = t[:SSD_HEAD_DIM]
            fin_ref[0, 2 * pr + 1] = t[SSD_HEAD_DIM:]


def _ssd(proj, cwx, cbx, cwbc, cbbc, dtb, alog, dskip, gn):
    nb, rows, _ = proj.shape
    q = CHUNK
    full = lambda shape: pl.BlockSpec(shape, lambda b, c: (0,) * len(shape))
    return pl.pallas_call(
        _ssd_kernel,
        grid=(nb, rows // q),
        in_specs=[
            pl.BlockSpec((1, q, D_INNER), lambda b, c: (b, c, COL_Z // D_INNER)),
            pl.BlockSpec((1, q, D_INNER), lambda b, c: (b, c, COL_X // D_INNER)),
            pl.BlockSpec((1, q, BC_DIM), lambda b, c: (b, c, COL_BC // BC_DIM)),
            pl.BlockSpec((1, q, LANES), lambda b, c: (b, c, COL_DT // LANES)),
            full((CONV_W, D_INNER)), full((1, D_INNER)), full((CONV_W, BC_DIM)), full((1, BC_DIM)),
            full((1, LANES)), full((1, LANES)), full((1, D_INNER)), full((1, D_INNER)),
        ],
        out_specs=[
            pl.BlockSpec((1, q, D_INNER), lambda b, c: (b, c, 0)),
            pl.BlockSpec((1, N_SSD_HEADS, SSD_HEAD_DIM, D_STATE), lambda b, c: (b, 0, 0, 0)),
        ],
        out_shape=[
            jax.ShapeDtypeStruct((nb, rows, D_INNER), BF16),
            jax.ShapeDtypeStruct((nb, N_SSD_HEADS, SSD_HEAD_DIM, D_STATE), F32),
        ],
        scratch_shapes=[
            pltpu.VMEM((SUBLANES + q, D_INNER), F32),
            pltpu.VMEM((SUBLANES + q, BC_DIM), F32),
            pltpu.VMEM((D_STATE, D_INNER), F32),
            pltpu.VMEM((q, D_INNER), F32),
        ],
        compiler_params=pltpu.CompilerParams(
            dimension_semantics=("parallel", "arbitrary"), vmem_limit_bytes=VMEM_LIMIT),
        name="ssd",
    )(proj, proj, proj, proj, cwx, cbx, cwbc, cbbc, dtb, alog, dskip, gn)


def _ssd_step_kernel(n_valid, z_ref, x_ref, bc_ref, dt_ref, px_ref, pbc_ref, st_ref, cwx_ref, cbx_ref,
                     cwbc_ref, cbbc_ref, dtb_ref, alog_ref, dskip_ref, gn_ref, exp_ref,
                     u_ref, nst_ref, extx_scr, extbc_scr):
    q = SAMPLE_ROWS
    gw = D_INNER // N_SSD_GROUPS
    xs = _silu(_causal_conv(extx_scr, x_ref[0], px_ref[0], cwx_ref, cbx_ref, q))
    bc = _silu(_causal_conv(extbc_scr, bc_ref[0], pbc_ref[0], cwbc_ref, cbbc_ref, q))

    rowi = lax.broadcasted_iota(jnp.int32, (q, 1), 0)
    dt = jnp.where(rowi < n_valid, _softplus(dt_ref[0] + dtb_ref[...]), 0.0)
    adt = dt * (-jnp.exp(alog_ref[...]))
    acs = jnp.zeros_like(adt)
    for s in range(n_valid):
        acs = acs + jnp.where(rowi >= s, adt[s:s + 1, :], 0.0)
    expand = exp_ref[...]
    dt_e = _dot_exact_rhs(dt, expand)
    acs_e = _dot_exact_rhs(acs, expand)
    last_e = acs_e[q - 1:q, :]

    x_end = xs * dt_e * jnp.exp(last_e - acs_e)
    dec3 = _split3(jnp.exp(last_e))
    zrow = jnp.zeros((q - 3, D_INNER), BF16)
    dec_rows = jnp.concatenate([dec3[0], dec3[1], dec3[2], zrow], axis=0)
    ones = jnp.ones((q, D_STATE), BF16)

    y_off, cbs = [], []
    for g in range(N_SSD_GROUPS):
        bm = bc[:, g * D_STATE:(g + 1) * D_STATE].astype(BF16)
        cm = bc[:, (N_SSD_GROUPS + g) * D_STATE:(N_SSD_GROUPS + g + 1) * D_STATE].astype(BF16)
        rows = pl.ds(g * gw, gw)
        st = st_ref[0, rows, :]
        y_off.append(_dot_nt(cm, st.astype(BF16)))
        cbs.append(_dot_nt(cm, bm))
        d_st = _dot_tn(x_end[:, g * gw:(g + 1) * gw].astype(BF16), bm)
        dec = _dot_tn(dec_rows[:, g * gw:(g + 1) * gw], ones)
        nst_ref[0, rows, :] = st * dec + d_st

    y = jnp.concatenate(y_off, axis=-1) * jnp.exp(acs_e) + dskip_ref[...] * xs
    for s in range(n_valid):
        keep = rowi >= s
        decay = jnp.where(keep, jnp.exp(jnp.where(keep, acs_e - acs_e[s:s + 1, :], 0.0)), 0.0)
        cb_e = jnp.concatenate([jnp.broadcast_to(cb[:, s:s + 1], (q, gw)) for cb in cbs], axis=-1)
        y = y + decay * cb_e * (dt_e[s:s + 1, :] * xs[s:s + 1, :])
    u_ref[0] = _gated_group_norm(y, z_ref[0], gn_ref).astype(u_ref.dtype)


def _ssd_step(proj, prev_x, prev_bc, state, cwx, cbx, cwbc, cbbc, dtb, alog, dskip, gn, expand, n_valid):
    nb = proj.shape[0]
    q = SAMPLE_ROWS
    full = lambda shape: pl.BlockSpec(shape, lambda b: (0,) * len(shape))
    return pl.pallas_call(
        functools.partial(_ssd_step_kernel, n_valid),
        grid=(nb,),
        in_specs=[
            pl.BlockSpec((1, q, D_INNER), lambda b: (b, 0, COL_Z // D_INNER)),
            pl.BlockSpec((1, q, D_INNER), lambda b: (b, 0, COL_X // D_INNER)),
            pl.BlockSpec((1, q, BC_DIM), lambda b: (b, 0, COL_BC // BC_DIM)),
            pl.BlockSpec((1, q, LANES), lambda b: (b, 0, COL_DT // LANES)),
            pl.BlockSpec((1, CONV_W - 1, D_INNER), lambda b: (b, 0, 0)),
            pl.BlockSpec((1, CONV_W - 1, BC_DIM), lambda b: (b, 0, 0)),
            pl.BlockSpec((1, D_INNER, D_STATE), lambda b: (b, 0, 0)),
            full((CONV_W, D_INNER)), full((1, D_INNER)), full((CONV_W, BC_DIM)), full((1, BC_DIM)),
            full((1, LANES)), full((1, LANES)), full((1, D_INNER)), full((1, D_INNER)),
            full((LANES, D_INNER)),
        ],
        out_specs=[
            pl.BlockSpec((1, q, D_INNER), lambda b: (b, 0, 0)),
            pl.BlockSpec((1, D_INNER, D_STATE), lambda b: (b, 0, 0)),
        ],
        out_shape=[
            jax.ShapeDtypeStruct((nb, q, D_INNER), BF16),
            jax.ShapeDtypeStruct((nb, D_INNER, D_STATE), F32),
        ],
        scratch_shapes=[
            pltpu.VMEM((2 * SUBLANES, D_INNER), F32),
            pltpu.VMEM((2 * SUBLANES, BC_DIM), F32),
        ],
        compiler_params=pltpu.CompilerParams(
            dimension_semantics=("parallel",), vmem_limit_bytes=VMEM_LIMIT),
        name="ssd_step",
    )(proj, proj, proj, proj, prev_x, prev_bc, state, cwx, cbx, cwbc, cbbc, dtb, alog, dskip, gn, expand)


def _attention_kernel(qb, prev_always_valid, sinks_ref, q_ref, kp_ref, vp_ref, kc_ref, vc_ref, o_ref):
    wb = WINDOW
    lane = lax.broadcasted_iota(jnp.int32, (1, LANES), 1)
    lo_half = lane < ATTN_HEAD_DIM
    t_p = lax.broadcasted_iota(jnp.int32, (qb, wb), 0)
    s_p = lax.broadcasted_iota(jnp.int32, (qb, wb), 1)
    rel_p = wb + t_p - s_p
    ok_p = rel_p < WINDOW
    if not prev_always_valid:
        ok_p = jnp.logical_and(ok_p, pl.program_id(1) > 0)
    t_c = lax.broadcasted_iota(jnp.int32, (qb, qb), 0)
    s_c = lax.broadcasted_iota(jnp.int32, (qb, qb), 1)
    rel_c = t_c - s_c
    ok_c = rel_c >= 0
    rel_p = rel_p.astype(F32)
    rel_c = rel_c.astype(F32)
    scale = ATTN_HEAD_DIM ** -0.5

    pairs = KV_REP
    for m in range(KV_DIM // LANES):
        kv_lanes = pl.ds(m * LANES, LANES)
        kp = kp_ref[0, :, kv_lanes].astype(BF16)
        kc = kc_ref[0, :, kv_lanes].astype(BF16)
        vp = vp_ref[0, :, kv_lanes]
        vc = vc_ref[0, :, kv_lanes]
        vp_lo, vp_hi = jnp.where(lo_half, vp, 1.0).astype(BF16), jnp.where(lo_half, 1.0, vp).astype(BF16)
        vc_lo, vc_hi = jnp.where(lo_half, vc, 1.0).astype(BF16), jnp.where(lo_half, 1.0, vc).astype(BF16)
        q_lo, q_hi = [], []
        for i in range(pairs):
            q_pair = q_ref[0, :, pl.ds((pairs * m + i) * LANES, LANES)] * scale
            q_lo.append(jnp.where(lo_half, q_pair, 0.0).astype(BF16))
            q_hi.append(jnp.where(lo_half, 0.0, q_pair).astype(BF16))
        qs = jnp.concatenate(q_lo + q_hi, axis=0)
        heads = [2 * pairs * m + r for r in range(2 * pairs)]
        slopes = [2.0 ** (-8.0 * (h + 1) / N_ATTN_HEADS) for h in heads]
        bias_p = jnp.concatenate([jnp.where(ok_p, -sl * rel_p, NEG_BIG) for sl in slopes], axis=0)
        bias_c = jnp.concatenate([jnp.where(ok_c, -sl * rel_c, NEG_BIG) for sl in slopes], axis=0)
        sink = jnp.concatenate([jnp.full((qb, 1), sinks_ref[h], F32) for h in heads], axis=0)
        s_p = _dot_nt(qs, kp) + bias_p
        s_c = _dot_nt(qs, kc) + bias_c
        if qb == wb:
            mx = jnp.max(jnp.maximum(s_p, s_c), axis=-1, keepdims=True)
        else:
            mx = jnp.maximum(jnp.max(s_p, axis=-1, keepdims=True), jnp.max(s_c, axis=-1, keepdims=True))
        mx = jnp.maximum(mx, sink)
        p_p = jnp.exp(s_p - mx).astype(BF16)
        p_c = jnp.exp(s_c - mx).astype(BF16)
        e_sink = jnp.exp(sink - mx)
        half = pairs * qb
        o_lo = _dot(p_p[:half], vp_lo) + _dot(p_c[:half], vc_lo)
        o_hi = _dot(p_p[half:], vp_hi) + _dot(p_c[half:], vc_hi)
        for i in range(pairs):
            a = o_lo[i * qb:(i + 1) * qb]
            b = o_hi[i * qb:(i + 1) * qb]
            num = jnp.where(lo_half, a, b)
            den = (pltpu.roll(jnp.where(lo_half, b, a), ATTN_HEAD_DIM, 1)
                   + jnp.where(lo_half, e_sink[i * qb:(i + 1) * qb], e_sink[half + i * qb:half + (i + 1) * qb]))
            o_ref[0, :, pl.ds((pairs * m + i) * LANES, LANES)] = (num / den).astype(o_ref.dtype)


def _attention(proj, sinks, qb, prev_kv=None):
    nb, rows, _ = proj.shape
    wb = WINDOW
    cur = lambda col: pl.BlockSpec((1, qb, KV_DIM), lambda b, n: (b, n, col // KV_DIM))
    if prev_kv is None:
        prev = lambda col: pl.BlockSpec((1, wb, KV_DIM), lambda b, n: (b, jnp.maximum(n - 1, 0), col // KV_DIM))
        kp, vp = proj, proj
        prev_specs = [prev(COL_K), prev(COL_V)]
    else:
        kp, vp = prev_kv
        prev_specs = [pl.BlockSpec((1, wb, KV_DIM), lambda b, n: (b, 0, 0))] * 2
    return pl.pallas_call(
        functools.partial(_attention_kernel, qb, prev_kv is not None),
        grid=(nb, rows // qb),
        in_specs=[
            pl.BlockSpec(memory_space=pltpu.SMEM),
            pl.BlockSpec((1, qb, D_MODEL), lambda b, n: (b, n, COL_Q // D_MODEL)),
            *prev_specs,
            cur(COL_K), cur(COL_V),
        ],
        out_specs=pl.BlockSpec((1, qb, D_MODEL), lambda b, n: (b, n, 0)),
        out_shape=jax.ShapeDtypeStruct((nb, rows, D_MODEL), BF16),
        compiler_params=pltpu.CompilerParams(
            dimension_semantics=("parallel", "arbitrary"), vmem_limit_bytes=VMEM_LIMIT),
        name="attention",
    )(sinks, proj, kp, vp, proj, proj)


def _post_mix_kernel(valid_rows, u_ref, a_ref, gs_ref, ga_ref, x_ref, wos_ref, woa_ref, wout_ref, gpm_ref,
                     gpf_ref, g1_ref, sc2_ref, sh2_ref, wr_ref, br_ref, cnt_in_ref,
                     x1_ref, h2_ref, ti_ref, tg_ref, cnt_ref, cnt_scr):
    @pl.when(jnp.logical_and(pl.program_id(0) == 0, pl.program_id(1) == 0))
    def _():
        cnt_scr[...] = cnt_in_ref[...]

    y_ssd = _dot(u_ref[0], wos_ref[...])
    y_attn = _dot(a_ref[0], woa_ref[...])
    merged = jax.nn.sigmoid(gs_ref[0]) * y_ssd + jax.nn.sigmoid(ga_ref[0]) * y_attn
    mixed = _dot(merged.astype(BF16), wout_ref[...])
    x1 = x_ref[0] + g1_ref[0] * (_rms(mixed) * gpm_ref[...])
    x1_ref[0] = x1
    h2 = _rms(x1) * gpf_ref[...] * (1.0 + sc2_ref[0]) + sh2_ref[0]
    h2_ref[0] = h2.astype(BF16)

    h_hi = h2.astype(BF16)
    h_lo = (h2 - h_hi.astype(F32)).astype(BF16)
    w = wr_ref[...]
    w_hi = w.astype(BF16)
    w_lo = (w - w_hi.astype(F32)).astype(BF16)
    logits = _dot(h_hi, w_hi) + (_dot(h_hi, w_lo) + _dot(h_lo, w_hi)) + br_ref[...]
    lane = lax.broadcasted_iota(jnp.int32, logits.shape, 1)
    idx_out = jnp.zeros(logits.shape, jnp.int32)
    val_out = jnp.zeros(logits.shape, F32)
    top = None
    denom = None
    idxs = []
    for k in range(TOP_K):
        m = jnp.max(logits, axis=-1, keepdims=True)
        idx = jnp.min(jnp.where(logits == m, lane, LANES), axis=-1, keepdims=True)
        if k == 0:
            top = m
            e = jnp.ones_like(m)
            denom = e
        else:
            e = jnp.exp(m - top)
            denom = denom + e
        idxs.append(idx)
        idx_out = jnp.where(lane == k, idx, idx_out)
        val_out = jnp.where(lane == k, e, val_out)
        logits = jnp.where(lane == idx, NEG_BIG * 2, logits)
    tg_ref[0] = val_out / denom

    tm = logits.shape[0]
    rowi = lax.broadcasted_iota(jnp.int32, (tm, 1), 0)
    valid = jnp.bitwise_and(rowi, SAMPLE_ROWS - 1) < valid_rows
    onehots = [jnp.where(jnp.logical_and(lane == idx, valid), 1.0, 0.0) for idx in idxs]
    picked = onehots[0] + onehots[1] + onehots[2] + onehots[3]
    r = lax.broadcasted_iota(jnp.int32, (tm, tm), 0)
    c = lax.broadcasted_iota(jnp.int32, (tm, tm), 1)
    before = jnp.where(r > c, 1.0, 0.0).astype(BF16)
    base = cnt_scr[...] + _dot(before, picked.astype(BF16))
    for k, oh in enumerate(onehots):
        rank = jnp.sum(oh * base, axis=-1, keepdims=True)
        idx_out = jnp.where(lane == TOP_K + k, rank.astype(jnp.int32), idx_out)
    ti_ref[0] = idx_out
    cnt_scr[...] = cnt_scr[...] + jnp.sum(picked, axis=0, keepdims=True)
    cnt_ref[...] = cnt_scr[...]


def _post_mix(u, attn, proj, x, wos, woa, wout, gpm, gpf, gate1, scale2, shift2, wr, br, counts, *,
              tm, per_row, valid_rows):
    nb, rows, _ = x.shape
    row_spec = lambda w, col=0: pl.BlockSpec((1, tm, w), lambda b, i: (b, i, col // w))
    full = lambda shape: pl.BlockSpec(shape, lambda b, i: (0,) * len(shape))
    return pl.pallas_call(
        functools.partial(_post_mix_kernel, valid_rows),
        grid=(nb, rows // tm),
        in_specs=[
            row_spec(D_INNER), row_spec(D_MODEL), row_spec(D_MODEL, COL_GS), row_spec(D_MODEL, COL_GA),
            row_spec(D_MODEL),
            full((D_INNER, D_MODEL)), full((D_MODEL, D_MODEL)), full((D_MODEL, D_MODEL)),
            full((1, D_MODEL)), full((1, D_MODEL)),
            _mod_spec(per_row, tm), _mod_spec(per_row, tm), _mod_spec(per_row, tm),
            full((D_MODEL, LANES)), full((1, LANES)), full((1, LANES)),
        ],
        out_specs=[row_spec(D_MODEL), row_spec(D_MODEL), row_spec(LANES), row_spec(LANES), full((1, LANES))],
        out_shape=[
            jax.ShapeDtypeStruct((nb, rows, D_MODEL), F32),
            jax.ShapeDtypeStruct((nb, rows, D_MODEL), BF16),
            jax.ShapeDtypeStruct((nb, rows, LANES), jnp.int32),
            jax.ShapeDtypeStruct((nb, rows, LANES), F32),
            jax.ShapeDtypeStruct((1, LANES), F32),
        ],
        scratch_shapes=[pltpu.VMEM((1, LANES), F32)],
        compiler_params=pltpu.CompilerParams(
            dimension_semantics=("arbitrary", "arbitrary"), vmem_limit_bytes=VMEM_LIMIT),
        name="post_mix",
    )(u, attn, proj, proj, x, wos, woa, wout, gpm, gpf, gate1, scale2, shift2, wr, br, counts)


def _moe_kernel(be_ref, na_ref, x_ref, wu_ref, bu_ref, wd_ref, bd_ref, o_ref):
    i = pl.program_id(0)

    @pl.when(i < na_ref[0])
    def _():
        up = _dot(x_ref[...], wu_ref[0]) + bu_ref[0]
        glu = jnp.minimum(up[:, :D_FF], SWIGLU_LIMIT)
        lin = jnp.clip(up[:, D_FF:], -SWIGLU_LIMIT, SWIGLU_LIMIT)
        act = glu * jax.nn.sigmoid(SWIGLU_ALPHA * glu) * (lin + 1.0)
        o_ref[...] = _dot(act.astype(BF16), wd_ref[0]) + bd_ref[0]

    @pl.when(i >= na_ref[0])
    def _():
        o_ref[...] = jnp.zeros_like(o_ref)


def _moe(xb, block_expert, n_active, wu, bu, wd, bd):
    n_rows = xb.shape[0]
    grid_spec = pltpu.PrefetchScalarGridSpec(
        num_scalar_prefetch=2,
        grid=(n_rows // MOE_ROWS,),
        in_specs=[
            pl.BlockSpec((MOE_ROWS, D_MODEL), lambda i, be, na: (i, 0)),
            pl.BlockSpec((1, D_MODEL, 2 * D_FF), lambda i, be, na: (be[i], 0, 0)),
            pl.BlockSpec((1, 1, 2 * D_FF), lambda i, be, na: (be[i], 0, 0)),
            pl.BlockSpec((1, D_FF, D_MODEL), lambda i, be, na: (be[i], 0, 0)),
            pl.BlockSpec((1, 1, D_MODEL), lambda i, be, na: (be[i], 0, 0)),
        ],
        out_specs=pl.BlockSpec((MOE_ROWS, D_MODEL), lambda i, be, na: (i, 0)),
    )
    return pl.pallas_call(
        _moe_kernel,
        grid_spec=grid_spec,
        out_shape=jax.ShapeDtypeStruct((n_rows, D_MODEL), F32),
        compiler_params=pltpu.CompilerParams(
            dimension_semantics=("arbitrary",), vmem_limit_bytes=VMEM_LIMIT),
        name="moe",
    )(block_expert, n_active, xb, wu, bu, wd, bd)


def _combine_kernel(ys_ref, tg_ref, x1_ref, gpost_ref, g2_ref, o_ref):
    gates = tg_ref[0]
    f = gates[:, 0:1] * ys_ref[0]
    for k in range(1, TOP_K):
        f = f + gates[:, k:k + 1] * ys_ref[k]
    o_ref[0] = x1_ref[0] + g2_ref[0] * (_rms(f) * gpost_ref[...])


def _combine(ys, row0, tg, x1, gpost, gate2, *, tm, per_row):
    nb, rows, _ = x1.shape
    per_b = rows // tm
    base = row0 // tm
    return pl.pallas_call(
        _combine_kernel,
        grid=(nb, per_b),
        in_specs=[
            pl.BlockSpec((TOP_K, tm, D_MODEL), lambda b, i: (0, base + b * per_b + i, 0)),
            pl.BlockSpec((1, tm, LANES), lambda b, i: (b, i, 0)),
            pl.BlockSpec((1, tm, D_MODEL), lambda b, i: (b, i, 0)),
            pl.BlockSpec((1, D_MODEL), lambda b, i: (0, 0)),
            _mod_spec(per_row, tm),
        ],
        out_specs=pl.BlockSpec((1, tm, D_MODEL), lambda b, i: (b, i, 0)),
        out_shape=jax.ShapeDtypeStruct((nb, rows, D_MODEL), F32),
        compiler_params=pltpu.CompilerParams(
            dimension_semantics=("parallel", "parallel"), vmem_limit_bytes=VMEM_LIMIT),
        name="combine",
    )(ys, tg, x1, gpost, gate2)


def _route(top_idx, rank, counts):
    n_tokens = top_idx.shape[0]
    n_slots = n_tokens * TOP_K
    experts = jnp.arange(N_EXPERTS, dtype=jnp.int32)
    padded = (counts + MOE_ROWS - 1) // MOE_ROWS * MOE_ROWS
    group_start = jnp.cumsum(counts) - counts
    padded_end = jnp.cumsum(padded)
    padded_start = padded_end - padded
    is_e = top_idx[..., None] == experts
    slot_dest = rank + jnp.sum(jnp.where(is_e, padded_start, 0), axis=-1)
    n_blocks = -(-n_slots // MOE_ROWS) + N_EXPERTS
    n_active = (padded_end[-1] // MOE_ROWS).astype(jnp.int32)
    block_start = jnp.arange(n_blocks, dtype=jnp.int32) * MOE_ROWS
    block_expert = jnp.sum(block_start[:, None] >= padded_end[None, :], axis=1)
    last_expert = jnp.max(jnp.where(counts > 0, experts, 0))
    block_expert = jnp.where(block_start < padded_end[-1], block_expert, last_expert).astype(jnp.int32)
    slot_id = jnp.arange(n_slots, dtype=jnp.int32).reshape(n_tokens, TOP_K)
    keys = jnp.sort((top_idx * n_slots + slot_id).reshape(-1))
    sorted_token = (keys % n_slots) // TOP_K
    row = jnp.arange(n_blocks * MOE_ROWS, dtype=jnp.int32)
    src = row + jnp.repeat((group_start - padded_start)[block_expert], MOE_ROWS)
    buf_token = sorted_token[jnp.clip(src, 0, n_slots - 1)]
    return buf_token.astype(jnp.int32), slot_dest.T.astype(jnp.int32), block_expert, n_active.reshape(1)


def _pad_lanes(v, value=0.0):
    return jnp.pad(v, [(0, 0)] * (v.ndim - 1) + [(0, LANES - v.shape[-1])], constant_values=value)


def kernel(x_prompt, x_sample, c_prompt, c_sample, cache_swa_k, cache_swa_v, state_conv, state_ssm, w_ada, b_ada, g_pre_mix, g_post_mix, g_pre_ffn, g_post_ffn, w_in, conv_w, conv_b, dt_bias, a_log, d_skip, g_ssm_norm, sinks, w_o_ssd, w_o_attn, w_out, w_router, b_router, w_up, b_up, w_down, b_down):
    depth = w_ada.shape[0]
    n_prompt, seq, _ = x_prompt.shape
    n_sample, dec_seq, _ = x_sample.shape
    yp = x_prompt
    ys_pad = jnp.pad(x_sample, ((0, 0), (0, SAMPLE_ROWS - dec_seq), (0, 0)))
    outs = [[] for _ in range(8)]
    expand = jnp.repeat(jnp.eye(LANES, N_SSD_HEADS, dtype=BF16), SSD_HEAD_DIM, axis=1)
    rows_s = n_sample * SAMPLE_ROWS
    n_p = n_prompt * seq
    n_s = n_sample * dec_seq
    n_tok = n_p + n_s
    tm_p = min(512, seq)
    tm_s = min(512, rows_s)
    tm_mix_p = min(256, seq)
    tm_mix_s = min(256, rows_s)
    assert seq % CHUNK == 0 and seq % tm_p == 0 and rows_s % tm_s == 0 and n_p % n_s == 0
    c_all = jnp.concatenate([c_prompt, c_sample], axis=0)
    c_rows = -(-c_all.shape[0] // SUBLANES) * SUBLANES
    c_all = jnp.pad(c_all, ((0, c_rows - c_all.shape[0]), (0, 0)))

    for l in range(depth):
        wi = w_in[l]
        o_xbc = D_INNER
        o_dt = o_xbc + CONV_DIM
        o_q = o_dt + N_SSD_HEADS
        o_k = o_q + D_MODEL
        o_v = o_k + KV_DIM
        o_gs = o_v + KV_DIM
        o_ga = o_gs + D_MODEL
        head_order = jnp.array(ATTN_HEAD_ORDER)
        w_q = wi[:, o_q:o_k].reshape(D_MODEL, N_ATTN_HEADS, ATTN_HEAD_DIM)[:, head_order].reshape(D_MODEL, D_MODEL)
        w_proj = jnp.concatenate([
            wi[:, :o_xbc], wi[:, o_xbc:o_xbc + D_INNER], w_q, wi[:, o_gs:o_ga], wi[:, o_ga:],
            wi[:, o_xbc + D_INNER:o_dt], wi[:, o_k:o_v], wi[:, o_v:o_gs], _pad_lanes(wi[:, o_dt:o_q])],
            axis=1).astype(BF16)
        cwx, cwbc = conv_w[l][:, :D_INNER], conv_w[l][:, D_INNER:]
        cbx, cbbc = conv_b[l][None, :D_INNER], conv_b[l][None, D_INNER:]
        dtb = _pad_lanes(dt_bias[l][None])
        alog = _pad_lanes(a_log[l][None])
        dskip = jnp.repeat(d_skip[l], SSD_HEAD_DIM)[None]
        gn = g_ssm_norm[l][None]
        wos, wout = w_o_ssd[l].astype(BF16), w_out[l].astype(BF16)
        woa = w_o_attn[l].reshape(N_ATTN_HEADS, ATTN_HEAD_DIM, D_MODEL)[head_order].reshape(D_MODEL, D_MODEL)
        woa = woa.astype(BF16)
        zero_counts = jnp.zeros((1, LANES), F32)
        wr = _pad_lanes(w_router[l])
        br = _pad_lanes(b_router[l][None], NEG_BIG)
        wu, wd = w_up[l].astype(BF16), w_down[l].astype(BF16)
        bu, bd = b_up[l][:, None, :], b_down[l][:, None, :]
        g_pm, g_pom, g_pf, g_pof = (v[l][None] for v in (g_pre_mix, g_post_mix, g_pre_ffn, g_post_ffn))

        ada = _ada(c_all, w_ada[l], b_ada[l][None])
        ada_p = [ada[:n_prompt, k * D_MODEL:(k + 1) * D_MODEL] for k in range(6)]
        ada_s = [ada[n_prompt:n_prompt + n_sample, k * D_MODEL:(k + 1) * D_MODEL] for k in range(6)]

        mods = [m[:, None, :] for m in ada_p]
        proj = _in_proj(yp, g_pm, mods[1], mods[0], w_proj, tm=tm_p, per_row=False)
        u, ssm_p = _ssd(proj, cwx, cbx, cwbc, cbbc, dtb, alog, dskip, gn)
        attn = _attention(proj, sinks[l], WINDOW)
        x1_p, h2_p, ti_p, tg_p, cnt_p = _post_mix(
            u, attn, proj, yp, wos, woa, wout, g_pom, g_pf, mods[2], mods[4], mods[3], wr, br, zero_counts,
            tm=tm_mix_p, per_row=False, valid_rows=SAMPLE_ROWS)
        gate2_p = mods[5]
        kp = proj[:, seq - WINDOW:, COL_K:COL_K + KV_DIM].reshape(n_prompt, WINDOW, N_KV_HEADS, ATTN_HEAD_DIM)
        vp = proj[:, seq - WINDOW:, COL_V:COL_V + KV_DIM].reshape(n_prompt, WINDOW, N_KV_HEADS, ATTN_HEAD_DIM)
        cp = jnp.concatenate([proj[:, seq - (CONV_W - 1):, COL_X:COL_X + D_INNER],
                              proj[:, seq - (CONV_W - 1):, COL_BC:COL_BC + BC_DIM]], axis=-1)

        mods_s = [jnp.repeat(m, SAMPLE_ROWS, axis=0)[None] for m in ada_s]
        xs_flat = ys_pad.reshape(1, rows_s, D_MODEL)
        proj_s = _in_proj(xs_flat, g_pm, mods_s[1], mods_s[0], w_proj, tm=tm_s, per_row=True)
        proj_sb = proj_s.reshape(n_sample, SAMPLE_ROWS, PROJ_DIM)
        u_s, ssm_s = _ssd_step(
            proj_sb, state_conv[l][:, :, :D_INNER], state_conv[l][:, :, D_INNER:],
            state_ssm[l].reshape(n_sample, D_INNER, D_STATE),
            cwx, cbx, cwbc, cbbc, dtb, alog, dskip, gn, expand, dec_seq)
        k_prev = cache_swa_k[l].reshape(n_sample, -1, KV_DIM)
        v_prev = cache_swa_v[l].reshape(n_sample, -1, KV_DIM)
        attn_s = _attention(proj_sb, sinks[l], SAMPLE_ROWS, prev_kv=(k_prev, v_prev))
        x1_s, h2_s, ti_s, tg_s, cnt_all = _post_mix(
            u_s.reshape(1, rows_s, D_INNER), attn_s.reshape(1, rows_s, D_MODEL), proj_s, xs_flat,
            wos, woa, wout, g_pom, g_pf, mods_s[2], mods_s[4], mods_s[3], wr, br, cnt_p,
            tm=tm_mix_s, per_row=True, valid_rows=dec_seq)
        wb = k_prev.shape[1]
        k_new = proj_sb[:, :dec_seq, COL_K:COL_K + KV_DIM]
        v_new = proj_sb[:, :dec_seq, COL_V:COL_V + KV_DIM]
        ks = jnp.concatenate([k_prev, k_new], axis=1)[:, -wb:].reshape(n_sample, wb, N_KV_HEADS, ATTN_HEAD_DIM)
        vs = jnp.concatenate([v_prev, v_new], axis=1)[:, -wb:].reshape(n_sample, wb, N_KV_HEADS, ATTN_HEAD_DIM)
        raw_xbc = jnp.concatenate([proj_sb[:, :dec_seq, COL_X:COL_X + D_INNER],
                                   proj_sb[:, :dec_seq, COL_BC:COL_BC + BC_DIM]], axis=-1)
        cs = jnp.concatenate([state_conv[l], raw_xbc], axis=1)[:, -(CONV_W - 1):]

        valid = lambda v: v.reshape(n_sample, SAMPLE_ROWS, -1)[:, :dec_seq].reshape(n_s, -1)
        h2_all = jnp.concatenate([h2_p.reshape(n_p, D_MODEL), valid(h2_s)], axis=0)
        ti_all = jnp.concatenate([ti_p.reshape(n_p, LANES), valid(ti_s)], axis=0)
        counts = cnt_all[0, :N_EXPERTS].astype(jnp.int32)
        buf_token, slot_dest, block_expert, n_active = _route(
            ti_all[:, :TOP_K], ti_all[:, TOP_K:2 * TOP_K], counts)
        h2_words = lax.bitcast_convert_type(h2_all.reshape(n_tok, D_MODEL // 2, 2), jnp.uint32)
        xb = lax.bitcast_convert_type(h2_words[buf_token], BF16).reshape(-1, D_MODEL)
        yb = _moe(xb, block_expert, n_active, wu, bu, wd, bd)
        ysel = yb[slot_dest.reshape(-1)].reshape(TOP_K, n_tok, D_MODEL)

        yp = _combine(ysel, 0, tg_p, x1_p, g_pof, gate2_p, tm=tm_p, per_row=False)
        x1_sv = valid(x1_s)[None]
        tg_sv = valid(tg_s)[None]
        gate2_s = jnp.repeat(ada_s[5], dec_seq, axis=0)[None]
        ys_new = _combine(ysel, n_p, tg_sv, x1_sv, g_pof, gate2_s, tm=n_s, per_row=True)
        ys_new = ys_new.reshape(n_sample, dec_seq, D_MODEL)
        ys_pad = jnp.pad(ys_new, ((0, 0), (0, SAMPLE_ROWS - dec_seq), (0, 0)))

        for lst, v in zip(outs, (kp, vp, cp, ssm_p, ks, vs, cs,
                                 ssm_s.reshape(n_sample, N_SSD_HEADS, SSD_HEAD_DIM, D_STATE))):
            lst.append(v)

    return (yp, ys_pad[:, :dec_seq], *[jnp.stack(v) for v in outs])
```

```python
import functools

import jax
import jax.numpy as jnp
from jax import lax
from jax.experimental import pallas as pl
from jax.experimental.pallas import tpu as pltpu

F32 = jnp.float32
BF16 = jnp.bfloat16

D_MODEL = 1024
D_INNER = 2 * D_MODEL
SSD_HEAD_DIM = 64
N_SSD_HEADS = D_INNER // SSD_HEAD_DIM
N_SSD_GROUPS = 4
HEADS_PER_GROUP = N_SSD_HEADS // N_SSD_GROUPS
D_STATE = 128
CONV_W = 4
BC_DIM = 2 * N_SSD_GROUPS * D_STATE
CONV_DIM = D_INNER + BC_DIM
CHUNK = 128
ATTN_HEAD_DIM = 64
N_ATTN_HEADS = D_MODEL // ATTN_HEAD_DIM
N_KV_HEADS = 4
KV_REP = N_ATTN_HEADS // N_KV_HEADS
KV_DIM = N_KV_HEADS * ATTN_HEAD_DIM
WINDOW = 128
N_EXPERTS = 32
TOP_K = 4
D_FF = D_MODEL
SWIGLU_LIMIT = 7.0
SWIGLU_ALPHA = 1.702
EPS = 1e-6

LANES = 128
SUBLANES = 8
NEG_BIG = -1e30

COL_Z = 0
COL_X = COL_Z + D_INNER
COL_Q = COL_X + D_INNER
COL_GS = COL_Q + D_MODEL
COL_GA = COL_GS + D_MODEL
COL_BC = COL_GA + D_MODEL
COL_K = COL_BC + BC_DIM
COL_V = COL_K + KV_DIM
COL_DT = COL_V + KV_DIM
PROJ_DIM = COL_DT + LANES
PROJ_TILE_N = PROJ_DIM // 3

ATTN_HEAD_ORDER = tuple(2 * KV_REP * (j // KV_REP) + (j % KV_REP) + KV_REP * hf
                        for j in range(N_ATTN_HEADS // 2) for hf in (0, 1))

TILE_ROWS = D_MODEL // LANES
assert TILE_ROWS == SUBLANES

SAMPLE_ROWS = 8
MOE_ROWS = 256
VMEM_LIMIT = 48 * 1024 * 1024
MOE_VMEM_LIMIT = 56 * 1024 * 1024


def _silu(v):
    return v * jax.nn.sigmoid(v)


def _softplus(v):
    return jnp.maximum(v, 0.0) + jnp.log(1.0 + jnp.exp(-jnp.abs(v)))


def _rms(v):
    return v * lax.rsqrt(jnp.mean(v * v, axis=-1, keepdims=True) + EPS)


def _split3(v):
    hi = v.astype(BF16)
    r1 = v - hi.astype(F32)
    mid = r1.astype(BF16)
    lo = (r1 - mid.astype(F32)).astype(BF16)
    return hi, mid, lo


def _store_token_tiles(ref, rows):
    n = rows.shape[0]
    for j in range(TILE_ROWS):
        ref[pl.ds(j, n, stride=TILE_ROWS), :] = rows[:, j * LANES:(j + 1) * LANES]


def _load_token_tiles(ref, n):
    return jnp.concatenate([ref[pl.ds(j, n, stride=TILE_ROWS), :] for j in range(TILE_ROWS)], axis=1)


def _dot(a, b):
    return jnp.dot(a, b, preferred_element_type=F32)


def _dot_nt(a, b):
    return lax.dot_general(a, b, (((1,), (1,)), ((), ())), preferred_element_type=F32)


def _dot_tn(a, b):
    return lax.dot_general(a, b, (((0,), (0,)), ((), ())), preferred_element_type=F32)


def _dot_exact_rhs(a, b_bf16):
    hi, mid, lo = _split3(a)
    return _dot(hi, b_bf16) + _dot(mid, b_bf16) + _dot(lo, b_bf16)


def _ada_kernel(c_ref, w_ref, b_ref, o_ref):
    o_ref[...] = _dot(_silu(c_ref[...]).astype(BF16), w_ref[...].astype(BF16)) + b_ref[...]


def _ada(c, w, b):
    rows, n = c.shape[0], w.shape[1]
    tn = n // 4
    return pl.pallas_call(
        _ada_kernel,
        grid=(n // tn,),
        in_specs=[
            pl.BlockSpec((rows, D_MODEL), lambda j: (0, 0)),
            pl.BlockSpec((D_MODEL, tn), lambda j: (0, j)),
            pl.BlockSpec((1, tn), lambda j: (0, j)),
        ],
        out_specs=pl.BlockSpec((rows, tn), lambda j: (0, j)),
        out_shape=jax.ShapeDtypeStruct((rows, n), F32),
        compiler_params=pltpu.CompilerParams(
            dimension_semantics=("parallel",), vmem_limit_bytes=VMEM_LIMIT),
        name="ada",
    )(c, w, b)


def _in_proj_kernel(x_ref, g_ref, sc_ref, sh_ref, w_ref, o_ref, h_scr):
    @pl.when(pl.program_id(2) == 0)
    def _():
        h = _rms(x_ref[0]) * g_ref[...] * (1.0 + sc_ref[0]) + sh_ref[0]
        h_scr[...] = h.astype(BF16)

    o_ref[0] = _dot(h_scr[...], w_ref[...])


def _mod_spec(per_row, tm):
    if per_row:
        return pl.BlockSpec((1, tm, D_MODEL), lambda b, i, *_: (b, i, 0))
    return pl.BlockSpec((1, 1, D_MODEL), lambda b, i, *_: (b, 0, 0))


def _in_proj(x, g, scale, shift, w, *, tm, per_row):
    nb, rows, _ = x.shape
    return pl.pallas_call(
        _in_proj_kernel,
        grid=(nb, rows // tm, PROJ_DIM // PROJ_TILE_N),
        in_specs=[
            pl.BlockSpec((1, tm, D_MODEL), lambda b, i, j: (b, i, 0)),
            pl.BlockSpec((1, D_MODEL), lambda b, i, j: (0, 0)),
            _mod_spec(per_row, tm),
            _mod_spec(per_row, tm),
            pl.BlockSpec((D_MODEL, PROJ_TILE_N), lambda b, i, j: (0, j)),
        ],
        out_specs=pl.BlockSpec((1, tm, PROJ_TILE_N), lambda b, i, j: (b, i, j)),
        out_shape=jax.ShapeDtypeStruct((nb, rows, PROJ_DIM), F32),
        scratch_shapes=[pltpu.VMEM((tm, D_MODEL), BF16)],
        compiler_params=pltpu.CompilerParams(
            dimension_semantics=("parallel", "parallel", "arbitrary"),
            vmem_limit_bytes=VMEM_LIMIT),
        name="in_proj",
    )(x, g, scale, shift, w)


def _causal_conv(ext_ref, raw, prev, w_ref, b_ref, rows):
    base = SUBLANES
    if prev is not None:
        ext_ref[pl.ds(base - (CONV_W - 1), CONV_W - 1), :] = prev
    ext_ref[pl.ds(base, rows), :] = raw
    out = b_ref[...] + raw * w_ref[pl.ds(CONV_W - 1, 1), :]
    for k in range(CONV_W - 1):
        out = out + ext_ref[pl.ds(base - (CONV_W - 1) + k, rows), :] * w_ref[pl.ds(k, 1), :]
    return out


def _gated_group_norm(y, z, g_ref):
    u = y * _silu(z)
    gw = D_INNER // N_SSD_GROUPS
    parts = []
    for g in range(N_SSD_GROUPS):
        ug = u[:, g * gw:(g + 1) * gw]
        parts.append(ug * lax.rsqrt(jnp.mean(ug * ug, axis=-1, keepdims=True) + EPS))
    return jnp.concatenate(parts, axis=-1) * g_ref[...]


def _ssd_kernel(z_ref, x_ref, bc_ref, dt_ref, cwx_ref, cbx_ref, cwbc_ref, cbbc_ref, dtb_ref, alog_ref,
                dskip_ref, gn_ref, u_ref, fin_ref, extx_scr, extbc_scr, st_scr, y_scr):
    q = CHUNK
    c = pl.program_id(1)

    @pl.when(c == 0)
    def _():
        extx_scr[pl.ds(0, SUBLANES), :] = jnp.zeros((SUBLANES, D_INNER), F32)
        extbc_scr[pl.ds(0, SUBLANES), :] = jnp.zeros((SUBLANES, BC_DIM), F32)
        st_scr[...] = jnp.zeros_like(st_scr)

    xs = _silu(_causal_conv(extx_scr, x_ref[0], None, cwx_ref, cbx_ref, q))
    bc = _silu(_causal_conv(extbc_scr, bc_ref[0], None, cwbc_ref, cbbc_ref, q))
    tail = SUBLANES + q - (CONV_W - 1)
    extx_scr[pl.ds(SUBLANES - (CONV_W - 1), CONV_W - 1), :] = extx_scr[pl.ds(tail, CONV_W - 1), :]
    extbc_scr[pl.ds(SUBLANES - (CONV_W - 1), CONV_W - 1), :] = extbc_scr[pl.ds(tail, CONV_W - 1), :]

    dt = _softplus(dt_ref[0] + dtb_ref[...])
    adt = dt * (-jnp.exp(alog_ref[...]))
    row = lax.broadcasted_iota(jnp.int32, (q, q), 0)
    col = lax.broadcasted_iota(jnp.int32, (q, q), 1)
    causal = row >= col
    tri = jnp.where(causal, 1.0, 0.0).astype(BF16)
    h3, m3, l3 = _split3(adt)
    acs = _dot(tri, h3) + _dot(tri, m3) + _dot(tri, l3)
    eacs = jnp.exp(acs)
    acs_t = acs.T
    dt_t = dt.T
    lane = lax.broadcasted_iota(jnp.int32, (1, LANES), 1)
    lo_half = lane < SSD_HEAD_DIM

    for g in range(N_SSD_GROUPS):
        bm = bc[:, g * D_STATE:(g + 1) * D_STATE]
        cm = bc[:, (N_SSD_GROUPS + g) * D_STATE:(N_SSD_GROUPS + g + 1) * D_STATE]
        cb = _dot_nt(cm.astype(BF16), bm.astype(BF16))
        bm_t = bm.T
        for pr in range(HEADS_PER_GROUP // 2):
            h0 = g * HEADS_PER_GROUP + 2 * pr
            lanes = pl.ds(h0 * SSD_HEAD_DIM, LANES)
            x_pair = xs[:, h0 * SSD_HEAD_DIM:h0 * SSD_HEAD_DIM + LANES]
            st_pair = st_scr[:, lanes]
            lhs_y, lhs_s, decs = [], [], []
            for h in (h0, h0 + 1):
                a_col = acs[:, h:h + 1]
                a_row = acs_t[h:h + 1, :]
                dt_row = dt_t[h:h + 1, :]
                a_last = acs_t[h:h + 1, q - 1:q]
                decay = jnp.where(causal, jnp.exp(jnp.where(causal, a_col - a_row, 0.0)), 0.0)
                lhs_y.append((cb * decay * dt_row).astype(BF16))
                lhs_s.append((bm_t * (dt_row * jnp.exp(a_last - a_row))).astype(BF16))
                decs.append(jnp.exp(a_last))
            for h in (h0, h0 + 1):
                lhs_y.append((cm * eacs[:, h:h + 1]).astype(BF16))
            x_top = jnp.where(lo_half, x_pair, 0.0).astype(BF16)
            x_bot = jnp.where(lo_half, 0.0, x_pair).astype(BF16)
            s_top = jnp.where(lo_half, st_pair, 0.0).astype(BF16)
            s_bot = jnp.where(lo_half, 0.0, st_pair).astype(BF16)
            x_bd = jnp.concatenate([x_top, x_bot], axis=0)
            rhs_y = jnp.concatenate([x_bd, s_top, s_bot], axis=0)
            y_pair = _dot(jnp.concatenate(lhs_y, axis=1), rhs_y)
            ds_pair = _dot(jnp.concatenate(lhs_s, axis=1), x_bd)
            dskip = dskip_ref[:, lanes]
            y_scr[:, lanes] = y_pair + dskip * x_pair
            st_scr[:, lanes] = st_pair * jnp.where(lo_half, decs[0], decs[1]) + ds_pair

    u_ref[0] = _gated_group_norm(y_scr[...], z_ref[0], gn_ref).astype(u_ref.dtype)

    @pl.when(c == pl.num_programs(1) - 1)
    def _():
        for pr in range(N_SSD_HEADS // 2):
            t = st_scr[:, pl.ds(pr * LANES, LANES)].T
            fin_ref[0, 2 * pr] = t[:SSD_HEAD_DIM]
            fin_ref[0, 2 * pr + 1] = t[SSD_HEAD_DIM:]


def _ssd(proj, cwx, cbx, cwbc, cbbc, dtb, alog, dskip, gn):
    nb, rows, _ = proj.shape
    q = CHUNK
    full = lambda shape: pl.BlockSpec(shape, lambda b, c: (0,) * len(shape))
    return pl.pallas_call(
        _ssd_kernel,
        grid=(nb, rows // q),
        in_specs=[
            pl.BlockSpec((1, q, D_INNER), lambda b, c: (b, c, COL_Z // D_INNER)),
            pl.BlockSpec((1, q, D_INNER), lambda b, c: (b, c, COL_X // D_INNER)),
            pl.BlockSpec((1, q, BC_DIM), lambda b, c: (b, c, COL_BC // BC_DIM)),
            pl.BlockSpec((1, q, LANES), lambda b, c: (b, c, COL_DT // LANES)),
            full((CONV_W, D_INNER)), full((1, D_INNER)), full((CONV_W, BC_DIM)), full((1, BC_DIM)),
            full((1, LANES)), full((1, LANES)), full((1, D_INNER)), full((1, D_INNER)),
        ],
        out_specs=[
            pl.BlockSpec((1, q, D_INNER), lambda b, c: (b, c, 0)),
            pl.BlockSpec((1, N_SSD_HEADS, SSD_HEAD_DIM, D_STATE), lambda b, c: (b, 0, 0, 0)),
        ],
        out_shape=[
            jax.ShapeDtypeStruct((nb, rows, D_INNER), BF16),
            jax.ShapeDtypeStruct((nb, N_SSD_HEADS, SSD_HEAD_DIM, D_STATE), F32),
        ],
        scratch_shapes=[
            pltpu.VMEM((SUBLANES + q, D_INNER), F32),
            pltpu.VMEM((SUBLANES + q, BC_DIM), F32),
            pltpu.VMEM((D_STATE, D_INNER), F32),
            pltpu.VMEM((q, D_INNER), F32),
        ],
        compiler_params=pltpu.CompilerParams(
            dimension_semantics=("parallel", "arbitrary"), vmem_limit_bytes=VMEM_LIMIT),
        name="ssd",
    )(proj, proj, proj, proj, cwx, cbx, cwbc, cbbc, dtb, alog, dskip, gn)


def _ssd_step_kernel(n_valid, z_ref, x_ref, bc_ref, dt_ref, px_ref, pbc_ref, st_ref, cwx_ref, cbx_ref,
                     cwbc_ref, cbbc_ref, dtb_ref, alog_ref, dskip_ref, gn_ref, exp_ref,
                     u_ref, nst_ref, extx_scr, extbc_scr):
    q = SAMPLE_ROWS
    gw = D_INNER // N_SSD_GROUPS
    xs = _silu(_causal_conv(extx_scr, x_ref[0], px_ref[0], cwx_ref, cbx_ref, q))
    bc = _silu(_causal_conv(extbc_scr, bc_ref[0], pbc_ref[0], cwbc_ref, cbbc_ref, q))

    rowi = lax.broadcasted_iota(jnp.int32, (q, 1), 0)
    dt = jnp.where(rowi < n_valid, _softplus(dt_ref[0] + dtb_ref[...]), 0.0)
    adt = dt * (-jnp.exp(alog_ref[...]))
    acs = jnp.zeros_like(adt)
    for s in range(n_valid):
        acs = acs + jnp.where(rowi >= s, adt[s:s + 1, :], 0.0)
    expand = exp_ref[...]
    dt_e = _dot_exact_rhs(dt, expand)
    acs_e = _dot_exact_rhs(acs, expand)
    last_e = acs_e[q - 1:q, :]

    x_end = xs * dt_e * jnp.exp(last_e - acs_e)
    dec3 = _split3(jnp.exp(last_e))
    zrow = jnp.zeros((q - 3, D_INNER), BF16)
    dec_rows = jnp.concatenate([dec3[0], dec3[1], dec3[2], zrow], axis=0)
    ones = jnp.ones((q, D_STATE), BF16)

    y_off, cbs = [], []
    for g in range(N_SSD_GROUPS):
        bm = bc[:, g * D_STATE:(g + 1) * D_STATE].astype(BF16)
        cm = bc[:, (N_SSD_GROUPS + g) * D_STATE:(N_SSD_GROUPS + g + 1) * D_STATE].astype(BF16)
        rows = pl.ds(g * gw, gw)
        st = st_ref[0, rows, :]
        y_off.append(_dot_nt(cm, st.astype(BF16)))
        cbs.append(_dot_nt(cm, bm))
        d_st = _dot_tn(x_end[:, g * gw:(g + 1) * gw].astype(BF16), bm)
        dec = _dot_tn(dec_rows[:, g * gw:(g + 1) * gw], ones)
        nst_ref[0, rows, :] = st * dec + d_st

    y = jnp.concatenate(y_off, axis=-1) * jnp.exp(acs_e) + dskip_ref[...] * xs
    for s in range(n_valid):
        keep = rowi >= s
        decay = jnp.where(keep, jnp.exp(jnp.where(keep, acs_e - acs_e[s:s + 1, :], 0.0)), 0.0)
        cb_e = jnp.concatenate([jnp.broadcast_to(cb[:, s:s + 1], (q, gw)) for cb in cbs], axis=-1)
        y = y + decay * cb_e * (dt_e[s:s + 1, :] * xs[s:s + 1, :])
    u_ref[0] = _gated_group_norm(y, z_ref[0], gn_ref).astype(u_ref.dtype)


def _ssd_step(proj, prev_x, prev_bc, state, cwx, cbx, cwbc, cbbc, dtb, alog, dskip, gn, expand, n_valid):
    nb = proj.shape[0]
    q = SAMPLE_ROWS
    full = lambda shape: pl.BlockSpec(shape, lambda b: (0,) * len(shape))
    return pl.pallas_call(
        functools.partial(_ssd_step_kernel, n_valid),
        grid=(nb,),
        in_specs=[
            pl.BlockSpec((1, q, D_INNER), lambda b: (b, 0, COL_Z // D_INNER)),
            pl.BlockSpec((1, q, D_INNER), lambda b: (b, 0, COL_X // D_INNER)),
            pl.BlockSpec((1, q, BC_DIM), lambda b: (b, 0, COL_BC // BC_DIM)),
            pl.BlockSpec((1, q, LANES), lambda b: (b, 0, COL_DT // LANES)),
            pl.BlockSpec((1, CONV_W - 1, D_INNER), lambda b: (b, 0, 0)),
            pl.BlockSpec((1, CONV_W - 1, BC_DIM), lambda b: (b, 0, 0)),
            pl.BlockSpec((1, D_INNER, D_STATE), lambda b: (b, 0, 0)),
            full((CONV_W, D_INNER)), full((1, D_INNER)), full((CONV_W, BC_DIM)), full((1, BC_DIM)),
            full((1, LANES)), full((1, LANES)), full((1, D_INNER)), full((1, D_INNER)),
            full((LANES, D_INNER)),
        ],
        out_specs=[
            pl.BlockSpec((1, q, D_INNER), lambda b: (b, 0, 0)),
            pl.BlockSpec((1, D_INNER, D_STATE), lambda b: (b, 0, 0)),
        ],
        out_shape=[
            jax.ShapeDtypeStruct((nb, q, D_INNER), BF16),
            jax.ShapeDtypeStruct((nb, D_INNER, D_STATE), F32),
        ],
        scratch_shapes=[
            pltpu.VMEM((2 * SUBLANES, D_INNER), F32),
            pltpu.VMEM((2 * SUBLANES, BC_DIM), F32),
        ],
        compiler_params=pltpu.CompilerParams(
            dimension_semantics=("parallel",), vmem_limit_bytes=VMEM_LIMIT),
        name="ssd_step",
    )(proj, proj, proj, proj, prev_x, prev_bc, state, cwx, cbx, cwbc, cbbc, dtb, alog, dskip, gn, expand)


def _attention_kernel(qb, prev_always_valid, sinks_ref, q_ref, kp_ref, vp_ref, kc_ref, vc_ref, o_ref):
    wb = WINDOW
    lane = lax.broadcasted_iota(jnp.int32, (1, LANES), 1)
    lo_half = lane < ATTN_HEAD_DIM
    t_p = lax.broadcasted_iota(jnp.int32, (qb, wb), 0)
    s_p = lax.broadcasted_iota(jnp.int32, (qb, wb), 1)
    rel_p = wb + t_p - s_p
    ok_p = rel_p < WINDOW
    if not prev_always_valid:
        ok_p = jnp.logical_and(ok_p, pl.program_id(1) > 0)
    t_c = lax.broadcasted_iota(jnp.int32, (qb, qb), 0)
    s_c = lax.broadcasted_iota(jnp.int32, (qb, qb), 1)
    rel_c = t_c - s_c
    ok_c = rel_c >= 0
    rel_p = rel_p.astype(F32)
    rel_c = rel_c.astype(F32)
    scale = ATTN_HEAD_DIM ** -0.5

    pairs = KV_REP
    for m in range(KV_DIM // LANES):
        kv_lanes = pl.ds(m * LANES, LANES)
        kp = kp_ref[0, :, kv_lanes].astype(BF16)
        kc = kc_ref[0, :, kv_lanes].astype(BF16)
        vp = vp_ref[0, :, kv_lanes]
        vc = vc_ref[0, :, kv_lanes]
        vp_lo, vp_hi = jnp.where(lo_half, vp, 1.0).astype(BF16), jnp.where(lo_half, 1.0, vp).astype(BF16)
        vc_lo, vc_hi = jnp.where(lo_half, vc, 1.0).astype(BF16), jnp.where(lo_half, 1.0, vc).astype(BF16)
        q_lo, q_hi = [], []
        for i in range(pairs):
            q_pair = q_ref[0, :, pl.ds((pairs * m + i) * LANES, LANES)] * scale
            q_lo.append(jnp.where(lo_half, q_pair, 0.0).astype(BF16))
            q_hi.append(jnp.where(lo_half, 0.0, q_pair).astype(BF16))
        qs = jnp.concatenate(q_lo + q_hi, axis=0)
        heads = [2 * pairs * m + r for r in range(2 * pairs)]
        slopes = [2.0 ** (-8.0 * (h + 1) / N_ATTN_HEADS) for h in heads]
        bias_p = jnp.concatenate([jnp.where(ok_p, -sl * rel_p, NEG_BIG) for sl in slopes], axis=0)
        bias_c = jnp.concatenate([jnp.where(ok_c, -sl * rel_c, NEG_BIG) for sl in slopes], axis=0)
        sink = jnp.concatenate([jnp.full((qb, 1), sinks_ref[h], F32) for h in heads], axis=0)
        s_p = _dot_nt(qs, kp) + bias_p
        s_c = _dot_nt(qs, kc) + bias_c
        if qb == wb:
            mx = jnp.max(jnp.maximum(s_p, s_c), axis=-1, keepdims=True)
        else:
            mx = jnp.maximum(jnp.max(s_p, axis=-1, keepdims=True), jnp.max(s_c, axis=-1, keepdims=True))
        mx = jnp.maximum(mx, sink)
        p_p = jnp.exp(s_p - mx).astype(BF16)
        p_c = jnp.exp(s_c - mx).astype(BF16)
        e_sink = jnp.exp(sink - mx)
        half = pairs * qb
        o_lo = _dot(p_p[:half], vp_lo) + _dot(p_c[:half], vc_lo)
        o_hi = _dot(p_p[half:], vp_hi) + _dot(p_c[half:], vc_hi)
        for i in range(pairs):
            a = o_lo[i * qb:(i + 1) * qb]
            b = o_hi[i * qb:(i + 1) * qb]
            num = jnp.where(lo_half, a, b)
            den = (pltpu.roll(jnp.where(lo_half, b, a), ATTN_HEAD_DIM, 1)
                   + jnp.where(lo_half, e_sink[i * qb:(i + 1) * qb], e_sink[half + i * qb:half + (i + 1) * qb]))
            o_ref[0, :, pl.ds((pairs * m + i) * LANES, LANES)] = (num / den).astype(o_ref.dtype)


def _attention(proj, sinks, qb, prev_kv=None):
    nb, rows, _ = proj.shape
    wb = WINDOW
    cur = lambda col: pl.BlockSpec((1, qb, KV_DIM), lambda b, n: (b, n, col // KV_DIM))
    if prev_kv is None:
        prev = lambda col: pl.BlockSpec((1, wb, KV_DIM), lambda b, n: (b, jnp.maximum(n - 1, 0), col // KV_DIM))
        kp, vp = proj, proj
        prev_specs = [prev(COL_K), prev(COL_V)]
    else:
        kp, vp = prev_kv
        prev_specs = [pl.BlockSpec((1, wb, KV_DIM), lambda b, n: (b, 0, 0))] * 2
    return pl.pallas_call(
        functools.partial(_attention_kernel, qb, prev_kv is not None),
        grid=(nb, rows // qb),
        in_specs=[
            pl.BlockSpec(memory_space=pltpu.SMEM),
            pl.BlockSpec((1, qb, D_MODEL), lambda b, n: (b, n, COL_Q // D_MODEL)),
            *prev_specs,
            cur(COL_K), cur(COL_V),
        ],
        out_specs=pl.BlockSpec((1, qb, D_MODEL), lambda b, n: (b, n, 0)),
        out_shape=jax.ShapeDtypeStruct((nb, rows, D_MODEL), BF16),
        compiler_params=pltpu.CompilerParams(
            dimension_semantics=("parallel", "arbitrary"), vmem_limit_bytes=VMEM_LIMIT),
        name="attention",
    )(sinks, proj, kp, vp, proj, proj)


def _post_mix_kernel(valid_rows, u_ref, a_ref, gs_ref, ga_ref, x_ref, wos_ref, woa_ref, wout_ref, gpm_ref,
                     gpf_ref, g1_ref, sc2_ref, sh2_ref, wr_ref, br_ref, cnt_in_ref,
                     x1_ref, h2_ref, ti_ref, tg_ref, cnt_ref, cnt_scr):
    @pl.when(jnp.logical_and(pl.program_id(0) == 0, pl.program_id(1) == 0))
    def _():
        cnt_scr[...] = cnt_in_ref[...]

    y_ssd = _dot(u_ref[0], wos_ref[...])
    y_attn = _dot(a_ref[0], woa_ref[...])
    merged = jax.nn.sigmoid(gs_ref[0]) * y_ssd + jax.nn.sigmoid(ga_ref[0]) * y_attn
    mixed = _dot(merged.astype(BF16), wout_ref[...])
    x1 = x_ref[0] + g1_ref[0] * (_rms(mixed) * gpm_ref[...])
    x1_ref[0] = x1
    h2 = _rms(x1) * gpf_ref[...] * (1.0 + sc2_ref[0]) + sh2_ref[0]
    _store_token_tiles(h2_ref.at[0], h2)

    h_hi = h2.astype(BF16)
    h_lo = (h2 - h_hi.astype(F32)).astype(BF16)
    w = wr_ref[...]
    w_hi = w.astype(BF16)
    w_lo = (w - w_hi.astype(F32)).astype(BF16)
    logits = _dot(h_hi, w_hi) + (_dot(h_hi, w_lo) + _dot(h_lo, w_hi)) + br_ref[...]
    lane = lax.broadcasted_iota(jnp.int32, logits.shape, 1)
    idx_out = jnp.zeros(logits.shape, jnp.int32)
    val_out = jnp.zeros(logits.shape, F32)
    top = None
    denom = None
    idxs = []
    for k in range(TOP_K):
        m = jnp.max(logits, axis=-1, keepdims=True)
        idx = jnp.min(jnp.where(logits == m, lane, LANES), axis=-1, keepdims=True)
        if k == 0:
            top = m
            e = jnp.ones_like(m)
            denom = e
        else:
            e = jnp.exp(m - top)
            denom = denom + e
        idxs.append(idx)
        idx_out = jnp.where(lane == k, idx, idx_out)
        val_out = jnp.where(lane == k, e, val_out)
        logits = jnp.where(lane == idx, NEG_BIG * 2, logits)
    tg_ref[0] = val_out / denom
    ti_ref[0] = idx_out

    rowi = lax.broadcasted_iota(jnp.int32, (logits.shape[0], 1), 0)
    valid = jnp.bitwise_and(rowi, SAMPLE_ROWS - 1) < valid_rows
    picked = jnp.zeros(logits.shape, F32)
    for idx in idxs:
        picked = picked + jnp.where(jnp.logical_and(lane == idx, valid), 1.0, 0.0)
    cnt_scr[...] = cnt_scr[...] + jnp.sum(picked, axis=0, keepdims=True)
    cnt_ref[...] = cnt_scr[...]


def _post_mix(u, attn, proj, x, wos, woa, wout, gpm, gpf, gate1, scale2, shift2, wr, br, counts, *,
              tm, per_row, valid_rows):
    nb, rows, _ = x.shape
    row_spec = lambda w, col=0: pl.BlockSpec((1, tm, w), lambda b, i: (b, i, col // w))
    full = lambda shape: pl.BlockSpec(shape, lambda b, i: (0,) * len(shape))
    return pl.pallas_call(
        functools.partial(_post_mix_kernel, valid_rows),
        grid=(nb, rows // tm),
        in_specs=[
            row_spec(D_INNER), row_spec(D_MODEL), row_spec(D_MODEL, COL_GS), row_spec(D_MODEL, COL_GA),
            row_spec(D_MODEL),
            full((D_INNER, D_MODEL)), full((D_MODEL, D_MODEL)), full((D_MODEL, D_MODEL)),
            full((1, D_MODEL)), full((1, D_MODEL)),
            _mod_spec(per_row, tm), _mod_spec(per_row, tm), _mod_spec(per_row, tm),
            full((D_MODEL, LANES)), full((1, LANES)), full((1, LANES)),
        ],
        out_specs=[row_spec(D_MODEL), pl.BlockSpec((1, tm * TILE_ROWS, LANES), lambda b, i: (b, i, 0)),
                   row_spec(LANES), row_spec(LANES), full((1, LANES))],
        out_shape=[
            jax.ShapeDtypeStruct((nb, rows, D_MODEL), F32),
            jax.ShapeDtypeStruct((nb, rows * TILE_ROWS, LANES), F32),
            jax.ShapeDtypeStruct((nb, rows, LANES), jnp.int32),
            jax.ShapeDtypeStruct((nb, rows, LANES), F32),
            jax.ShapeDtypeStruct((1, LANES), F32),
        ],
        scratch_shapes=[pltpu.VMEM((1, LANES), F32)],
        compiler_params=pltpu.CompilerParams(
            dimension_semantics=("arbitrary", "arbitrary"), vmem_limit_bytes=VMEM_LIMIT),
        name="post_mix",
    )(u, attn, proj, proj, x, wos, woa, wout, gpm, gpf, gate1, scale2, shift2, wr, br, counts)


def _moe_kernel(n_tokens, be_ref, na_ref, idx_hbm, h_hbm, wu_ref, bu_ref, wd_ref, bd_ref, ys_hbm,
                idx_smem, xbuf, obuf, wu_bf, wd_bf, sem_idx, sem_in, sem_out):
    i = pl.program_id(0)
    na = na_ref[0]
    slot = lax.rem(i, 2)
    other = 1 - slot

    def idx_copy(block, s):
        return pltpu.make_async_copy(idx_hbm.at[block], idx_smem.at[s], sem_idx.at[s])

    block_rows = MOE_ROWS * TILE_ROWS

    def tile(ref, first_row):
        return ref.at[pl.ds(pl.multiple_of(first_row, TILE_ROWS), TILE_ROWS)]

    def start_gather(s):
        for r in range(MOE_ROWS):
            pltpu.make_async_copy(tile(h_hbm, idx_smem[s, r]), xbuf.at[s, pl.ds(r * TILE_ROWS, TILE_ROWS)],
                                  sem_in.at[s]).start()

    def wait_gather(s):
        pltpu.make_async_copy(h_hbm.at[pl.ds(0, block_rows)], xbuf.at[s], sem_in.at[s]).wait()

    def start_scatter(s):
        for r in range(MOE_ROWS):
            pltpu.make_async_copy(obuf.at[s, pl.ds(r * TILE_ROWS, TILE_ROWS)],
                                  tile(ys_hbm, idx_smem[s, MOE_ROWS + r]), sem_out.at[s]).start()

    def wait_scatter(s):
        pltpu.make_async_copy(obuf.at[s], ys_hbm.at[pl.ds(0, block_rows)], sem_out.at[s]).wait()

    def flat(buf, s):
        return buf.at[s]

    @pl.when(i == 0)
    def _():
        obuf[0] = jnp.zeros((block_rows, LANES), F32)
        plane_rows = ys_hbm.shape[0] // (TOP_K * TILE_ROWS)
        spare = [pltpu.make_async_copy(
            obuf.at[0], ys_hbm.at[pl.ds((k * plane_rows + n_tokens + hf * MOE_ROWS) * TILE_ROWS, block_rows)],
            sem_out.at[0]) for k in range(TOP_K) for hf in range(2)]
        for cp in spare:
            cp.start()
        for cp in spare:
            cp.wait()
        first = idx_copy(0, 0)
        first.start()
        first.wait()
        start_gather(0)

        @pl.when(na > 1)
        def _():
            idx_copy(1, 1).start()

    @pl.when(i < na)
    def _():
        wait_gather(slot)

        @pl.when(i + 1 < na)
        def _():
            idx_copy(i + 1, other).wait()
            start_gather(other)

        @pl.when(i >= 2)
        def _():
            wait_scatter(slot)

        @pl.when(jnp.logical_or(i == 0, be_ref[i] != be_ref[jnp.maximum(i - 1, 0)]))
        def _():
            wu_bf[...] = wu_ref[0].astype(BF16)
            wd_bf[...] = wd_ref[0].astype(BF16)

        x = _load_token_tiles(flat(xbuf, slot), MOE_ROWS).astype(BF16)
        up = _dot(x, wu_bf[...]) + bu_ref[0]
        glu = jnp.minimum(up[:, :D_FF], SWIGLU_LIMIT)
        lin = jnp.clip(up[:, D_FF:], -SWIGLU_LIMIT, SWIGLU_LIMIT)
        act = glu * jax.nn.sigmoid(SWIGLU_ALPHA * glu) * (lin + 1.0)
        _store_token_tiles(flat(obuf, slot), _dot(act.astype(BF16), wd_bf[...]) + bd_ref[0])
        start_scatter(slot)

        @pl.when(i + 2 < na)
        def _():
            idx_copy(i + 2, slot).start()

        @pl.when(i == na - 1)
        def _():
            wait_scatter(slot)

            @pl.when(i >= 1)
            def _():
                wait_scatter(other)


def _moe(idx, h_tiles, ys_rows, block_expert, n_active, wu, bu, wd, bd):
    n_blocks = idx.shape[0]
    grid_spec = pltpu.PrefetchScalarGridSpec(
        num_scalar_prefetch=2,
        grid=(n_blocks,),
        in_specs=[
            pl.BlockSpec(memory_space=pl.ANY),
            pl.BlockSpec(memory_space=pl.ANY),
            pl.BlockSpec((1, D_MODEL, 2 * D_FF), lambda i, be, na: (be[i], 0, 0)),
            pl.BlockSpec((1, 1, 2 * D_FF), lambda i, be, na: (be[i], 0, 0)),
            pl.BlockSpec((1, D_FF, D_MODEL), lambda i, be, na: (be[i], 0, 0)),
            pl.BlockSpec((1, 1, D_MODEL), lambda i, be, na: (be[i], 0, 0)),
        ],
        out_specs=pl.BlockSpec(memory_space=pl.ANY),
        scratch_shapes=[
            pltpu.SMEM((2, 2 * MOE_ROWS), jnp.int32),
            pltpu.VMEM((2, MOE_ROWS * TILE_ROWS, LANES), F32),
            pltpu.VMEM((2, MOE_ROWS * TILE_ROWS, LANES), F32),
            pltpu.VMEM((D_MODEL, 2 * D_FF), BF16),
            pltpu.VMEM((D_FF, D_MODEL), BF16),
            pltpu.SemaphoreType.DMA((2,)),
            pltpu.SemaphoreType.DMA((2,)),
            pltpu.SemaphoreType.DMA((2,)),
        ],
    )
    return pl.pallas_call(
        functools.partial(_moe_kernel, h_tiles.shape[0] // TILE_ROWS),
        grid_spec=grid_spec,
        out_shape=jax.ShapeDtypeStruct((ys_rows * TILE_ROWS, LANES), F32),
        compiler_params=pltpu.CompilerParams(
            dimension_semantics=("arbitrary",), vmem_limit_bytes=MOE_VMEM_LIMIT),
        name="moe",
    )(block_expert, n_active, idx, h_tiles, wu, bu, wd, bd)


def _combine_kernel(ys_ref, tg_ref, x1_ref, gpost_ref, g2_ref, o_ref):
    gates = tg_ref[0]
    tm = gates.shape[0]
    f = gates[:, 0:1] * _load_token_tiles(ys_ref.at[0], tm)
    for k in range(1, TOP_K):
        f = f + gates[:, k:k + 1] * _load_token_tiles(ys_ref.at[k], tm)
    o_ref[0] = x1_ref[0] + g2_ref[0] * (_rms(f) * gpost_ref[...])


def _combine(ys, row0, tg, x1, gpost, gate2, *, tm, per_row):
    nb, rows, _ = x1.shape
    per_b = rows // tm
    base = row0 // tm
    return pl.pallas_call(
        _combine_kernel,
        grid=(nb, per_b),
        in_specs=[
            pl.BlockSpec((TOP_K, tm * TILE_ROWS, LANES), lambda b, i: (0, base + b * per_b + i, 0)),
            pl.BlockSpec((1, tm, LANES), lambda b, i: (b, i, 0)),
            pl.BlockSpec((1, tm, D_MODEL), lambda b, i: (b, i, 0)),
            pl.BlockSpec((1, D_MODEL), lambda b, i: (0, 0)),
            _mod_spec(per_row, tm),
        ],
        out_specs=pl.BlockSpec((1, tm, D_MODEL), lambda b, i: (b, i, 0)),
        out_shape=jax.ShapeDtypeStruct((nb, rows, D_MODEL), F32),
        compiler_params=pltpu.CompilerParams(
            dimension_semantics=("parallel", "parallel"), vmem_limit_bytes=VMEM_LIMIT),
        name="combine",
    )(ys, tg, x1, gpost, gate2)


def _route(top_idx, counts, plane_rows):
    n_tokens = top_idx.shape[0]
    n_slots = n_tokens * TOP_K
    experts = jnp.arange(N_EXPERTS, dtype=jnp.int32)
    padded = (counts + MOE_ROWS - 1) // MOE_ROWS * MOE_ROWS
    group_start = jnp.cumsum(counts) - counts
    padded_end = jnp.cumsum(padded)
    padded_start = padded_end - padded
    n_blocks = -(-n_slots // MOE_ROWS) + N_EXPERTS
    n_active = (padded_end[-1] // MOE_ROWS).astype(jnp.int32)
    block_start = jnp.arange(n_blocks, dtype=jnp.int32) * MOE_ROWS
    block_expert = jnp.sum(block_start[:, None] >= padded_end[None, :], axis=1)
    last_expert = jnp.max(jnp.where(counts > 0, experts, 0))
    block_expert = jnp.where(block_start < padded_end[-1], block_expert, last_expert).astype(jnp.int32)
    slot_id = jnp.arange(n_slots, dtype=jnp.int32).reshape(n_tokens, TOP_K)
    keys = jnp.sort((top_idx * n_slots + slot_id).reshape(-1))
    row = jnp.arange(n_blocks * MOE_ROWS, dtype=jnp.int32).reshape(n_blocks, MOE_ROWS)
    src = row + (group_start - padded_start)[block_expert][:, None]
    real = row < (padded_start + counts)[block_expert][:, None]
    slot = keys[jnp.clip(src, 0, n_slots - 1).reshape(-1)].reshape(n_blocks, MOE_ROWS) % n_slots
    token = slot // TOP_K
    spare = n_tokens + (jnp.arange(n_blocks, dtype=jnp.int32) % 2)[:, None] * MOE_ROWS + row % MOE_ROWS
    dest = jnp.where(real, (slot % TOP_K) * plane_rows + token, spare)
    idx = (jnp.concatenate([token, dest], axis=1) * TILE_ROWS).astype(jnp.int32)
    return idx, block_expert, n_active.reshape(1)


def _pad_lanes(v, value=0.0):
    return jnp.pad(v, [(0, 0)] * (v.ndim - 1) + [(0, LANES - v.shape[-1])], constant_values=value)


def kernel(x_prompt, x_sample, c_prompt, c_sample, cache_swa_k, cache_swa_v, state_conv, state_ssm, w_ada, b_ada, g_pre_mix, g_post_mix, g_pre_ffn, g_post_ffn, w_in, conv_w, conv_b, dt_bias, a_log, d_skip, g_ssm_norm, sinks, w_o_ssd, w_o_attn, w_out, w_router, b_router, w_up, b_up, w_down, b_down):
    depth = w_ada.shape[0]
    n_prompt, seq, _ = x_prompt.shape
    n_sample, dec_seq, _ = x_sample.shape
    yp = x_prompt
    ys_pad = jnp.pad(x_sample, ((0, 0), (0, SAMPLE_ROWS - dec_seq), (0, 0)))
    outs = [[] for _ in range(8)]
    expand = jnp.repeat(jnp.eye(LANES, N_SSD_HEADS, dtype=BF16), SSD_HEAD_DIM, axis=1)
    rows_s = n_sample * SAMPLE_ROWS
    n_p = n_prompt * seq
    n_s = n_sample * dec_seq
    n_tok = n_p + n_s
    tm_p = min(512, seq)
    tm_s = min(512, rows_s)
    tm_mix_p = min(256, seq)
    tm_mix_s = min(256, rows_s)
    assert seq % CHUNK == 0 and seq % tm_p == 0 and rows_s % tm_s == 0 and n_p % n_s == 0
    c_all = jnp.concatenate([c_prompt, c_sample], axis=0)
    c_rows = -(-c_all.shape[0] // SUBLANES) * SUBLANES
    c_all = jnp.pad(c_all, ((0, c_rows - c_all.shape[0]), (0, 0)))

    for l in range(depth):
        wi = w_in[l]
        o_xbc = D_INNER
        o_dt = o_xbc + CONV_DIM
        o_q = o_dt + N_SSD_HEADS
        o_k = o_q + D_MODEL
        o_v = o_k + KV_DIM
        o_gs = o_v + KV_DIM
        o_ga = o_gs + D_MODEL
        head_order = jnp.array(ATTN_HEAD_ORDER)
        w_q = wi[:, o_q:o_k].reshape(D_MODEL, N_ATTN_HEADS, ATTN_HEAD_DIM)[:, head_order].reshape(D_MODEL, D_MODEL)
        w_proj = jnp.concatenate([
            wi[:, :o_xbc], wi[:, o_xbc:o_xbc + D_INNER], w_q, wi[:, o_gs:o_ga], wi[:, o_ga:],
            wi[:, o_xbc + D_INNER:o_dt], wi[:, o_k:o_v], wi[:, o_v:o_gs], _pad_lanes(wi[:, o_dt:o_q])],
            axis=1).astype(BF16)
        cwx, cwbc = conv_w[l][:, :D_INNER], conv_w[l][:, D_INNER:]
        cbx, cbbc = conv_b[l][None, :D_INNER], conv_b[l][None, D_INNER:]
        dtb = _pad_lanes(dt_bias[l][None])
        alog = _pad_lanes(a_log[l][None])
        dskip = jnp.repeat(d_skip[l], SSD_HEAD_DIM)[None]
        gn = g_ssm_norm[l][None]
        wos, wout = w_o_ssd[l].astype(BF16), w_out[l].astype(BF16)
        woa = w_o_attn[l].reshape(N_ATTN_HEADS, ATTN_HEAD_DIM, D_MODEL)[head_order].reshape(D_MODEL, D_MODEL)
        woa = woa.astype(BF16)
        zero_counts = jnp.zeros((1, LANES), F32)
        wr = _pad_lanes(w_router[l])
        br = _pad_lanes(b_router[l][None], NEG_BIG)
        wu, wd = w_up[l], w_down[l]
        bu, bd = b_up[l][:, None, :], b_down[l][:, None, :]
        g_pm, g_pom, g_pf, g_pof = (v[l][None] for v in (g_pre_mix, g_post_mix, g_pre_ffn, g_post_ffn))

        ada = _ada(c_all, w_ada[l], b_ada[l][None])
        ada_p = [ada[:n_prompt, k * D_MODEL:(k + 1) * D_MODEL] for k in range(6)]
        ada_s = [ada[n_prompt:n_prompt + n_sample, k * D_MODEL:(k + 1) * D_MODEL] for k in range(6)]

        mods = [m[:, None, :] for m in ada_p]
        proj = _in_proj(yp, g_pm, mods[1], mods[0], w_proj, tm=tm_p, per_row=False)
        u, ssm_p = _ssd(proj, cwx, cbx, cwbc, cbbc, dtb, alog, dskip, gn)
        attn = _attention(proj, sinks[l], WINDOW)
        x1_p, h2_p, ti_p, tg_p, cnt_p = _post_mix(
            u, attn, proj, yp, wos, woa, wout, g_pom, g_pf, mods[2], mods[4], mods[3], wr, br, zero_counts,
            tm=tm_mix_p, per_row=False, valid_rows=SAMPLE_ROWS)
        gate2_p = mods[5]
        kp = proj[:, seq - WINDOW:, COL_K:COL_K + KV_DIM].reshape(n_prompt, WINDOW, N_KV_HEADS, ATTN_HEAD_DIM)
        vp = proj[:, seq - WINDOW:, COL_V:COL_V + KV_DIM].reshape(n_prompt, WINDOW, N_KV_HEADS, ATTN_HEAD_DIM)
        cp = jnp.concatenate([proj[:, seq - (CONV_W - 1):, COL_X:COL_X + D_INNER],
                              proj[:, seq - (CONV_W - 1):, COL_BC:COL_BC + BC_DIM]], axis=-1)

        mods_s = [jnp.repeat(m, SAMPLE_ROWS, axis=0)[None] for m in ada_s]
        xs_flat = ys_pad.reshape(1, rows_s, D_MODEL)
        proj_s = _in_proj(xs_flat, g_pm, mods_s[1], mods_s[0], w_proj, tm=tm_s, per_row=True)
        proj_sb = proj_s.reshape(n_sample, SAMPLE_ROWS, PROJ_DIM)
        u_s, ssm_s = _ssd_step(
            proj_sb, state_conv[l][:, :, :D_INNER], state_conv[l][:, :, D_INNER:],
            state_ssm[l].reshape(n_sample, D_INNER, D_STATE),
            cwx, cbx, cwbc, cbbc, dtb, alog, dskip, gn, expand, dec_seq)
        k_prev = cache_swa_k[l].reshape(n_sample, -1, KV_DIM)
        v_prev = cache_swa_v[l].reshape(n_sample, -1, KV_DIM)
        attn_s = _attention(proj_sb, sinks[l], SAMPLE_ROWS, prev_kv=(k_prev, v_prev))
        x1_s, h2_s, ti_s, tg_s, cnt_all = _post_mix(
            u_s.reshape(1, rows_s, D_INNER), attn_s.reshape(1, rows_s, D_MODEL), proj_s, xs_flat,
            wos, woa, wout, g_pom, g_pf, mods_s[2], mods_s[4], mods_s[3], wr, br, cnt_p,
            tm=tm_mix_s, per_row=True, valid_rows=dec_seq)
        wb = k_prev.shape[1]
        k_new = proj_sb[:, :dec_seq, COL_K:COL_K + KV_DIM]
        v_new = proj_sb[:, :dec_seq, COL_V:COL_V + KV_DIM]
        ks = jnp.concatenate([k_prev, k_new], axis=1)[:, -wb:].reshape(n_sample, wb, N_KV_HEADS, ATTN_HEAD_DIM)
        vs = jnp.concatenate([v_prev, v_new], axis=1)[:, -wb:].reshape(n_sample, wb, N_KV_HEADS, ATTN_HEAD_DIM)
        raw_xbc = jnp.concatenate([proj_sb[:, :dec_seq, COL_X:COL_X + D_INNER],
                                   proj_sb[:, :dec_seq, COL_BC:COL_BC + BC_DIM]], axis=-1)
        cs = jnp.concatenate([state_conv[l], raw_xbc], axis=1)[:, -(CONV_W - 1):]

        valid = lambda v: v.reshape(n_sample, SAMPLE_ROWS, -1)[:, :dec_seq].reshape(n_s, -1)
        h2_sv = h2_s.reshape(n_sample, SAMPLE_ROWS * TILE_ROWS, LANES)[:, :dec_seq * TILE_ROWS]
        h2_all = jnp.concatenate([h2_p.reshape(n_p * TILE_ROWS, LANES),
                                  h2_sv.reshape(n_s * TILE_ROWS, LANES)], axis=0)
        ti_all = jnp.concatenate([ti_p.reshape(n_p, LANES), valid(ti_s)], axis=0)
        counts = cnt_all[0, :N_EXPERTS].astype(jnp.int32)
        plane_rows = n_tok + 2 * MOE_ROWS
        idx, block_expert, n_active = _route(ti_all[:, :TOP_K], counts, plane_rows)
        ysel = _moe(idx, h2_all, TOP_K * plane_rows, block_expert, n_active, wu, bu, wd, bd)
        ysel = ysel.reshape(TOP_K, plane_rows * TILE_ROWS, LANES)

        yp = _combine(ysel, 0, tg_p, x1_p, g_pof, gate2_p, tm=tm_p, per_row=False)
        x1_sv = valid(x1_s)[None]
        tg_sv = valid(tg_s)[None]
        gate2_s = jnp.repeat(ada_s[5], dec_seq, axis=0)[None]
        ys_new = _combine(ysel, n_p, tg_sv, x1_sv, g_pof, gate2_s, tm=n_s, per_row=True)
        ys_new = ys_new.reshape(n_sample, dec_seq, D_MODEL)
        ys_pad = jnp.pad(ys_new, ((0, 0), (0, SAMPLE_ROWS - dec_seq), (0, 0)))

        for lst, v in zip(outs, (kp, vp, cp, ssm_p, ks, vs, cs,
                                 ssm_s.reshape(n_sample, N_SSD_HEADS, SSD_HEAD_DIM, D_STATE))):
            lst.append(v)

    return (yp, ys_pad[:, :dec_seq], *[jnp.stack(v) for v in outs])
```

```python
import functools

import jax
import jax.numpy as jnp
from jax import lax
from jax.experimental import pallas as pl
from jax.experimental.pallas import tpu as pltpu

F32 = jnp.float32
BF16 = jnp.bfloat16

D_MODEL = 1024
D_INNER = 2 * D_MODEL
SSD_HEAD_DIM = 64
N_SSD_HEADS = D_INNER // SSD_HEAD_DIM
N_SSD_GROUPS = 4
HEADS_PER_GROUP = N_SSD_HEADS // N_SSD_GROUPS
D_STATE = 128
CONV_W = 4
BC_DIM = 2 * N_SSD_GROUPS * D_STATE
CONV_DIM = D_INNER + BC_DIM
CHUNK = 128
ATTN_HEAD_DIM = 64
N_ATTN_HEADS = D_MODEL // ATTN_HEAD_DIM
N_KV_HEADS = 4
KV_REP = N_ATTN_HEADS // N_KV_HEADS
KV_DIM = N_KV_HEADS * ATTN_HEAD_DIM
WINDOW = 128
N_EXPERTS = 32
TOP_K = 4
D_FF = D_MODEL
SWIGLU_LIMIT = 7.0
SWIGLU_ALPHA = 1.702
EPS = 1e-6

LANES = 128
SUBLANES = 8
NEG_BIG = -1e30

COL_Z = 0
COL_X = COL_Z + D_INNER
COL_Q = COL_X + D_INNER
COL_GS = COL_Q + D_MODEL
COL_GA = COL_GS + D_MODEL
COL_BC = COL_GA + D_MODEL
COL_K = COL_BC + BC_DIM
COL_V = COL_K + KV_DIM
COL_DT = COL_V + KV_DIM
PROJ_DIM = COL_DT + LANES
PROJ_TILE_N = PROJ_DIM // 3

ATTN_HEAD_ORDER = tuple(2 * KV_REP * (j // KV_REP) + (j % KV_REP) + KV_REP * hf
                        for j in range(N_ATTN_HEADS // 2) for hf in (0, 1))

TILE_ROWS = D_MODEL // LANES
assert TILE_ROWS == SUBLANES

SAMPLE_ROWS = 8
MOE_ROWS = 256
VMEM_LIMIT = 48 * 1024 * 1024
MOE_VMEM_LIMIT = 56 * 1024 * 1024


def _silu(v):
    return v * jax.nn.sigmoid(v)


def _softplus(v):
    return jnp.maximum(v, 0.0) + jnp.log(1.0 + jnp.exp(-jnp.abs(v)))


def _rms(v):
    return v * lax.rsqrt(jnp.mean(v * v, axis=-1, keepdims=True) + EPS)


def _split3(v):
    hi = v.astype(BF16)
    r1 = v - hi.astype(F32)
    mid = r1.astype(BF16)
    lo = (r1 - mid.astype(F32)).astype(BF16)
    return hi, mid, lo


def _store_token_tiles(ref, rows):
    n = rows.shape[0]
    for j in range(TILE_ROWS):
        ref[pl.ds(j, n, stride=TILE_ROWS), :] = rows[:, j * LANES:(j + 1) * LANES]


def _load_token_tiles(ref, n):
    return jnp.concatenate([ref[pl.ds(j, n, stride=TILE_ROWS), :] for j in range(TILE_ROWS)], axis=1)


def _dot(a, b):
    return jnp.dot(a, b, preferred_element_type=F32)


def _dot_nt(a, b):
    return lax.dot_general(a, b, (((1,), (1,)), ((), ())), preferred_element_type=F32)


def _dot_tn(a, b):
    return lax.dot_general(a, b, (((0,), (0,)), ((), ())), preferred_element_type=F32)


def _dot_exact_rhs(a, b_bf16):
    hi, mid, lo = _split3(a)
    return _dot(hi, b_bf16) + _dot(mid, b_bf16) + _dot(lo, b_bf16)


def _ada_kernel(c_ref, w_ref, b_ref, o_ref):
    o_ref[...] = _dot(_silu(c_ref[...]).astype(BF16), w_ref[...].astype(BF16)) + b_ref[...]


def _ada(c, w, b):
    rows, n = c.shape[0], w.shape[1]
    tn = n // 4
    return pl.pallas_call(
        _ada_kernel,
        grid=(n // tn,),
        in_specs=[
            pl.BlockSpec((rows, D_MODEL), lambda j: (0, 0)),
            pl.BlockSpec((D_MODEL, tn), lambda j: (0, j)),
            pl.BlockSpec((1, tn), lambda j: (0, j)),
        ],
        out_specs=pl.BlockSpec((rows, tn), lambda j: (0, j)),
        out_shape=jax.ShapeDtypeStruct((rows, n), F32),
        compiler_params=pltpu.CompilerParams(
            dimension_semantics=("parallel",), vmem_limit_bytes=VMEM_LIMIT),
        name="ada",
    )(c, w, b)


def _in_proj_kernel(x_ref, g_ref, sc_ref, sh_ref, w_ref, o_ref, h_scr):
    @pl.when(pl.program_id(2) == 0)
    def _():
        h = _rms(x_ref[0]) * g_ref[...] * (1.0 + sc_ref[0]) + sh_ref[0]
        h_scr[...] = h.astype(BF16)

    o_ref[0] = _dot(h_scr[...], w_ref[...])


def _mod_spec(per_row, tm):
    if per_row:
        return pl.BlockSpec((1, tm, D_MODEL), lambda b, i, *_: (b, i, 0))
    return pl.BlockSpec((1, 1, D_MODEL), lambda b, i, *_: (b, 0, 0))


def _in_proj(x, g, scale, shift, w, *, tm, per_row):
    nb, rows, _ = x.shape
    return pl.pallas_call(
        _in_proj_kernel,
        grid=(nb, rows // tm, PROJ_DIM // PROJ_TILE_N),
        in_specs=[
            pl.BlockSpec((1, tm, D_MODEL), lambda b, i, j: (b, i, 0)),
            pl.BlockSpec((1, D_MODEL), lambda b, i, j: (0, 0)),
            _mod_spec(per_row, tm),
            _mod_spec(per_row, tm),
            pl.BlockSpec((D_MODEL, PROJ_TILE_N), lambda b, i, j: (0, j)),
        ],
        out_specs=pl.BlockSpec((1, tm, PROJ_TILE_N), lambda b, i, j: (b, i, j)),
        out_shape=jax.ShapeDtypeStruct((nb, rows, PROJ_DIM), F32),
        scratch_shapes=[pltpu.VMEM((tm, D_MODEL), BF16)],
        compiler_params=pltpu.CompilerParams(
            dimension_semantics=("parallel", "parallel", "arbitrary"),
            vmem_limit_bytes=VMEM_LIMIT),
        name="in_proj",
    )(x, g, scale, shift, w)


def _causal_conv(ext_ref, raw, prev, w_ref, b_ref, rows):
    base = SUBLANES
    if prev is not None:
        ext_ref[pl.ds(base - (CONV_W - 1), CONV_W - 1), :] = prev
    ext_ref[pl.ds(base, rows), :] = raw
    out = b_ref[...] + raw * w_ref[pl.ds(CONV_W - 1, 1), :]
    for k in range(CONV_W - 1):
        out = out + ext_ref[pl.ds(base - (CONV_W - 1) + k, rows), :] * w_ref[pl.ds(k, 1), :]
    return out


def _gated_group_norm(y, z, g_ref):
    u = y * _silu(z)
    gw = D_INNER // N_SSD_GROUPS
    parts = []
    for g in range(N_SSD_GROUPS):
        ug = u[:, g * gw:(g + 1) * gw]
        parts.append(ug * lax.rsqrt(jnp.mean(ug * ug, axis=-1, keepdims=True) + EPS))
    return jnp.concatenate(parts, axis=-1) * g_ref[...]


def _ssd_kernel(z_ref, x_ref, bc_ref, dt_ref, cwx_ref, cbx_ref, cwbc_ref, cbbc_ref, dtb_ref, alog_ref,
                dskip_ref, gn_ref, u_ref, fin_ref, extx_scr, extbc_scr, st_scr, y_scr):
    q = CHUNK
    c = pl.program_id(1)

    @pl.when(c == 0)
    def _():
        extx_scr[pl.ds(0, SUBLANES), :] = jnp.zeros((SUBLANES, D_INNER), F32)
        extbc_scr[pl.ds(0, SUBLANES), :] = jnp.zeros((SUBLANES, BC_DIM), F32)
        st_scr[...] = jnp.zeros_like(st_scr)

    xs = _silu(_causal_conv(extx_scr, x_ref[0], None, cwx_ref, cbx_ref, q))
    bc = _silu(_causal_conv(extbc_scr, bc_ref[0], None, cwbc_ref, cbbc_ref, q))
    tail = SUBLANES + q - (CONV_W - 1)
    extx_scr[pl.ds(SUBLANES - (CONV_W - 1), CONV_W - 1), :] = extx_scr[pl.ds(tail, CONV_W - 1), :]
    extbc_scr[pl.ds(SUBLANES - (CONV_W - 1), CONV_W - 1), :] = extbc_scr[pl.ds(tail, CONV_W - 1), :]

    dt = _softplus(dt_ref[0] + dtb_ref[...])
    adt = dt * (-jnp.exp(alog_ref[...]))
    row = lax.broadcasted_iota(jnp.int32, (q, q), 0)
    col = lax.broadcasted_iota(jnp.int32, (q, q), 1)
    causal = row >= col
    tri = jnp.where(causal, 1.0, 0.0).astype(BF16)
    h3, m3, l3 = _split3(adt)
    acs = _dot(tri, h3) + _dot(tri, m3) + _dot(tri, l3)
    eacs = jnp.exp(acs)
    acs_t = acs.T
    dt_t = dt.T
    lane = lax.broadcasted_iota(jnp.int32, (1, LANES), 1)
    lo_half = lane < SSD_HEAD_DIM

    for g in range(N_SSD_GROUPS):
        bm = bc[:, g * D_STATE:(g + 1) * D_STATE]
        cm = bc[:, (N_SSD_GROUPS + g) * D_STATE:(N_SSD_GROUPS + g + 1) * D_STATE]
        cb = _dot_nt(cm.astype(BF16), bm.astype(BF16))
        bm_t = bm.T
        for pr in range(HEADS_PER_GROUP // 2):
            h0 = g * HEADS_PER_GROUP + 2 * pr
            lanes = pl.ds(h0 * SSD_HEAD_DIM, LANES)
            x_pair = xs[:, h0 * SSD_HEAD_DIM:h0 * SSD_HEAD_DIM + LANES]
            st_pair = st_scr[:, lanes]
            lhs_y, lhs_s, decs = [], [], []
            for h in (h0, h0 + 1):
                a_col = acs[:, h:h + 1]
                a_row = acs_t[h:h + 1, :]
                dt_row = dt_t[h:h + 1, :]
                a_last = acs_t[h:h + 1, q - 1:q]
                decay = jnp.where(causal, jnp.exp(jnp.where(causal, a_col - a_row, 0.0)), 0.0)
                lhs_y.append((cb * decay * dt_row).astype(BF16))
                lhs_s.append((bm_t * (dt_row * jnp.exp(a_last - a_row))).astype(BF16))
                decs.append(jnp.exp(a_last))
            for h in (h0, h0 + 1):
                lhs_y.append((cm * eacs[:, h:h + 1]).astype(BF16))
            x_top = jnp.where(lo_half, x_pair, 0.0).astype(BF16)
            x_bot = jnp.where(lo_half, 0.0, x_pair).astype(BF16)
            s_top = jnp.where(lo_half, st_pair, 0.0).astype(BF16)
            s_bot = jnp.where(lo_half, 0.0, st_pair).astype(BF16)
            x_bd = jnp.concatenate([x_top, x_bot], axis=0)
            rhs_y = jnp.concatenate([x_bd, s_top, s_bot], axis=0)
            y_pair = _dot(jnp.concatenate(lhs_y, axis=1), rhs_y)
            ds_pair = _dot(jnp.concatenate(lhs_s, axis=1), x_bd)
            dskip = dskip_ref[:, lanes]
            y_scr[:, lanes] = y_pair + dskip * x_pair
            st_scr[:, lanes] = st_pair * jnp.where(lo_half, decs[0], decs[1]) + ds_pair

    u_ref[0] = _gated_group_norm(y_scr[...], z_ref[0], gn_ref).astype(u_ref.dtype)

    @pl.when(c == pl.num_programs(1) - 1)
    def _():
        for pr in range(N_SSD_HEADS // 2):
            t = st_scr[:, pl.ds(pr * LANES, LANES)].T
            fin_ref[0, 2 * pr] = t[:SSD_HEAD_DIM]
            fin_ref[0, 2 * pr + 1] = t[SSD_HEAD_DIM:]


def _ssd(proj, cwx, cbx, cwbc, cbbc, dtb, alog, dskip, gn):
    nb, rows, _ = proj.shape
    q = CHUNK
    full = lambda shape: pl.BlockSpec(shape, lambda b, c: (0,) * len(shape))
    return pl.pallas_call(
        _ssd_kernel,
        grid=(nb, rows // q),
        in_specs=[
            pl.BlockSpec((1, q, D_INNER), lambda b, c: (b, c, COL_Z // D_INNER)),
            pl.BlockSpec((1, q, D_INNER), lambda b, c: (b, c, COL_X // D_INNER)),
            pl.BlockSpec((1, q, BC_DIM), lambda b, c: (b, c, COL_BC // BC_DIM)),
            pl.BlockSpec((1, q, LANES), lambda b, c: (b, c, COL_DT // LANES)),
            full((CONV_W, D_INNER)), full((1, D_INNER)), full((CONV_W, BC_DIM)), full((1, BC_DIM)),
            full((1, LANES)), full((1, LANES)), full((1, D_INNER)), full((1, D_INNER)),
        ],
        out_specs=[
            pl.BlockSpec((1, q, D_INNER), lambda b, c: (b, c, 0)),
            pl.BlockSpec((1, N_SSD_HEADS, SSD_HEAD_DIM, D_STATE), lambda b, c: (b, 0, 0, 0)),
        ],
        out_shape=[
            jax.ShapeDtypeStruct((nb, rows, D_INNER), BF16),
            jax.ShapeDtypeStruct((nb, N_SSD_HEADS, SSD_HEAD_DIM, D_STATE), F32),
        ],
        scratch_shapes=[
            pltpu.VMEM((SUBLANES + q, D_INNER), F32),
            pltpu.VMEM((SUBLANES + q, BC_DIM), F32),
            pltpu.VMEM((D_STATE, D_INNER), F32),
            pltpu.VMEM((q, D_INNER), F32),
        ],
        compiler_params=pltpu.CompilerParams(
            dimension_semantics=("parallel", "arbitrary"), vmem_limit_bytes=VMEM_LIMIT),
        name="ssd",
    )(proj, proj, proj, proj, cwx, cbx, cwbc, cbbc, dtb, alog, dskip, gn)


def _ssd_step_kernel(n_valid, z_ref, x_ref, bc_ref, dt_ref, px_ref, pbc_ref, st_ref, cwx_ref, cbx_ref,
                     cwbc_ref, cbbc_ref, dtb_ref, alog_ref, dskip_ref, gn_ref, exp_ref,
                     u_ref, nst_ref, extx_scr, extbc_scr):
    q = SAMPLE_ROWS
    gw = D_INNER // N_SSD_GROUPS
    xs = _silu(_causal_conv(extx_scr, x_ref[0], px_ref[0], cwx_ref, cbx_ref, q))
    bc = _silu(_causal_conv(extbc_scr, bc_ref[0], pbc_ref[0], cwbc_ref, cbbc_ref, q))

    rowi = lax.broadcasted_iota(jnp.int32, (q, 1), 0)
    dt = jnp.where(rowi < n_valid, _softplus(dt_ref[0] + dtb_ref[...]), 0.0)
    adt = dt * (-jnp.exp(alog_ref[...]))
    acs = jnp.zeros_like(adt)
    for s in range(n_valid):
        acs = acs + jnp.where(rowi >= s, adt[s:s + 1, :], 0.0)
    expand = exp_ref[...]
    dt_e = _dot_exact_rhs(dt, expand)
    acs_e = _dot_exact_rhs(acs, expand)
    last_e = acs_e[q - 1:q, :]

    x_end = xs * dt_e * jnp.exp(last_e - acs_e)
    dec3 = _split3(jnp.exp(last_e))
    zrow = jnp.zeros((q - 3, D_INNER), BF16)
    dec_rows = jnp.concatenate([dec3[0], dec3[1], dec3[2], zrow], axis=0)
    ones = jnp.ones((q, D_STATE), BF16)

    y_off, cbs = [], []
    for g in range(N_SSD_GROUPS):
        bm = bc[:, g * D_STATE:(g + 1) * D_STATE].astype(BF16)
        cm = bc[:, (N_SSD_GROUPS + g) * D_STATE:(N_SSD_GROUPS + g + 1) * D_STATE].astype(BF16)
        rows = pl.ds(g * gw, gw)
        st = st_ref[0, rows, :]
        y_off.append(_dot_nt(cm, st.astype(BF16)))
        cbs.append(_dot_nt(cm, bm))
        d_st = _dot_tn(x_end[:, g * gw:(g + 1) * gw].astype(BF16), bm)
        dec = _dot_tn(dec_rows[:, g * gw:(g + 1) * gw], ones)
        nst_ref[0, rows, :] = st * dec + d_st

    y = jnp.concatenate(y_off, axis=-1) * jnp.exp(acs_e) + dskip_ref[...] * xs
    for s in range(n_valid):
        keep = rowi >= s
        decay = jnp.where(keep, jnp.exp(jnp.where(keep, acs_e - acs_e[s:s + 1, :], 0.0)), 0.0)
        cb_e = jnp.concatenate([jnp.broadcast_to(cb[:, s:s + 1], (q, gw)) for cb in cbs], axis=-1)
        y = y + decay * cb_e * (dt_e[s:s + 1, :] * xs[s:s + 1, :])
    u_ref[0] = _gated_group_norm(y, z_ref[0], gn_ref).astype(u_ref.dtype)


def _ssd_step(proj, prev_x, prev_bc, state, cwx, cbx, cwbc, cbbc, dtb, alog, dskip, gn, expand, n_valid):
    nb = proj.shape[0]
    q = SAMPLE_ROWS
    full = lambda shape: pl.BlockSpec(shape, lambda b: (0,) * len(shape))
    return pl.pallas_call(
        functools.partial(_ssd_step_kernel, n_valid),
        grid=(nb,),
        in_specs=[
            pl.BlockSpec((1, q, D_INNER), lambda b: (b, 0, COL_Z // D_INNER)),
            pl.BlockSpec((1, q, D_INNER), lambda b: (b, 0, COL_X // D_INNER)),
            pl.BlockSpec((1, q, BC_DIM), lambda b: (b, 0, COL_BC // BC_DIM)),
            pl.BlockSpec((1, q, LANES), lambda b: (b, 0, COL_DT // LANES)),
            pl.BlockSpec((1, CONV_W - 1, D_INNER), lambda b: (b, 0, 0)),
            pl.BlockSpec((1, CONV_W - 1, BC_DIM), lambda b: (b, 0, 0)),
            pl.BlockSpec((1, D_INNER, D_STATE), lambda b: (b, 0, 0)),
            full((CONV_W, D_INNER)), full((1, D_INNER)), full((CONV_W, BC_DIM)), full((1, BC_DIM)),
            full((1, LANES)), full((1, LANES)), full((1, D_INNER)), full((1, D_INNER)),
            full((LANES, D_INNER)),
        ],
        out_specs=[
            pl.BlockSpec((1, q, D_INNER), lambda b: (b, 0, 0)),
            pl.BlockSpec((1, D_INNER, D_STATE), lambda b: (b, 0, 0)),
        ],
        out_shape=[
            jax.ShapeDtypeStruct((nb, q, D_INNER), BF16),
            jax.ShapeDtypeStruct((nb, D_INNER, D_STATE), F32),
        ],
        scratch_shapes=[
            pltpu.VMEM((2 * SUBLANES, D_INNER), F32),
            pltpu.VMEM((2 * SUBLANES, BC_DIM), F32),
        ],
        compiler_params=pltpu.CompilerParams(
            dimension_semantics=("parallel",), vmem_limit_bytes=VMEM_LIMIT),
        name="ssd_step",
    )(proj, proj, proj, proj, prev_x, prev_bc, state, cwx, cbx, cwbc, cbbc, dtb, alog, dskip, gn, expand)


def _attention_kernel(qb, prev_always_valid, sinks_ref, q_ref, kp_ref, vp_ref, kc_ref, vc_ref, o_ref):
    wb = WINDOW
    lane = lax.broadcasted_iota(jnp.int32, (1, LANES), 1)
    lo_half = lane < ATTN_HEAD_DIM
    t_p = lax.broadcasted_iota(jnp.int32, (qb, wb), 0)
    s_p = lax.broadcasted_iota(jnp.int32, (qb, wb), 1)
    rel_p = wb + t_p - s_p
    ok_p = rel_p < WINDOW
    if not prev_always_valid:
        ok_p = jnp.logical_and(ok_p, pl.program_id(1) > 0)
    t_c = lax.broadcasted_iota(jnp.int32, (qb, qb), 0)
    s_c = lax.broadcasted_iota(jnp.int32, (qb, qb), 1)
    rel_c = t_c - s_c
    ok_c = rel_c >= 0
    rel_p = rel_p.astype(F32)
    rel_c = rel_c.astype(F32)
    scale = ATTN_HEAD_DIM ** -0.5

    pairs = KV_REP
    for m in range(KV_DIM // LANES):
        kv_lanes = pl.ds(m * LANES, LANES)
        kp = kp_ref[0, :, kv_lanes].astype(BF16)
        kc = kc_ref[0, :, kv_lanes].astype(BF16)
        vp = vp_ref[0, :, kv_lanes]
        vc = vc_ref[0, :, kv_lanes]
        vp_lo, vp_hi = jnp.where(lo_half, vp, 1.0).astype(BF16), jnp.where(lo_half, 1.0, vp).astype(BF16)
        vc_lo, vc_hi = jnp.where(lo_half, vc, 1.0).astype(BF16), jnp.where(lo_half, 1.0, vc).astype(BF16)
        q_lo, q_hi = [], []
        for i in range(pairs):
            q_pair = q_ref[0, :, pl.ds((pairs * m + i) * LANES, LANES)] * scale
            q_lo.append(jnp.where(lo_half, q_pair, 0.0).astype(BF16))
            q_hi.append(jnp.where(lo_half, 0.0, q_pair).astype(BF16))
        qs = jnp.concatenate(q_lo + q_hi, axis=0)
        heads = [2 * pairs * m + r for r in range(2 * pairs)]
        slopes = [2.0 ** (-8.0 * (h + 1) / N_ATTN_HEADS) for h in heads]
        bias_p = jnp.concatenate([jnp.where(ok_p, -sl * rel_p, NEG_BIG) for sl in slopes], axis=0)
        bias_c = jnp.concatenate([jnp.where(ok_c, -sl * rel_c, NEG_BIG) for sl in slopes], axis=0)
        sink = jnp.concatenate([jnp.full((qb, 1), sinks_ref[h], F32) for h in heads], axis=0)
        s_p = _dot_nt(qs, kp) + bias_p
        s_c = _dot_nt(qs, kc) + bias_c
        if qb == wb:
            mx = jnp.max(jnp.maximum(s_p, s_c), axis=-1, keepdims=True)
        else:
            mx = jnp.maximum(jnp.max(s_p, axis=-1, keepdims=True), jnp.max(s_c, axis=-1, keepdims=True))
        mx = jnp.maximum(mx, sink)
        p_p = jnp.exp(s_p - mx).astype(BF16)
        p_c = jnp.exp(s_c - mx).astype(BF16)
        e_sink = jnp.exp(sink - mx)
        half = pairs * qb
        o_lo = _dot(p_p[:half], vp_lo) + _dot(p_c[:half], vc_lo)
        o_hi = _dot(p_p[half:], vp_hi) + _dot(p_c[half:], vc_hi)
        for i in range(pairs):
            a = o_lo[i * qb:(i + 1) * qb]
            b = o_hi[i * qb:(i + 1) * qb]
            num = jnp.where(lo_half, a, b)
            den = (pltpu.roll(jnp.where(lo_half, b, a), ATTN_HEAD_DIM, 1)
                   + jnp.where(lo_half, e_sink[i * qb:(i + 1) * qb], e_sink[half + i * qb:half + (i + 1) * qb]))
            o_ref[0, :, pl.ds((pairs * m + i) * LANES, LANES)] = (num / den).astype(o_ref.dtype)


def _attention(proj, sinks, qb, prev_kv=None):
    nb, rows, _ = proj.shape
    wb = WINDOW
    cur = lambda col: pl.BlockSpec((1, qb, KV_DIM), lambda b, n: (b, n, col // KV_DIM))
    if prev_kv is None:
        prev = lambda col: pl.BlockSpec((1, wb, KV_DIM), lambda b, n: (b, jnp.maximum(n - 1, 0), col // KV_DIM))
        kp, vp = proj, proj
        prev_specs = [prev(COL_K), prev(COL_V)]
    else:
        kp, vp = prev_kv
        prev_specs = [pl.BlockSpec((1, wb, KV_DIM), lambda b, n: (b, 0, 0))] * 2
    return pl.pallas_call(
        functools.partial(_attention_kernel, qb, prev_kv is not None),
        grid=(nb, rows // qb),
        in_specs=[
            pl.BlockSpec(memory_space=pltpu.SMEM),
            pl.BlockSpec((1, qb, D_MODEL), lambda b, n: (b, n, COL_Q // D_MODEL)),
            *prev_specs,
            cur(COL_K), cur(COL_V),
        ],
        out_specs=pl.BlockSpec((1, qb, D_MODEL), lambda b, n: (b, n, 0)),
        out_shape=jax.ShapeDtypeStruct((nb, rows, D_MODEL), BF16),
        compiler_params=pltpu.CompilerParams(
            dimension_semantics=("parallel", "arbitrary"), vmem_limit_bytes=VMEM_LIMIT),
        name="attention",
    )(sinks, proj, kp, vp, proj, proj)


def _post_mix_kernel(valid_rows, u_ref, a_ref, gs_ref, ga_ref, x_ref, wos_ref, woa_ref, wout_ref, gpm_ref,
                     gpf_ref, g1_ref, sc2_ref, sh2_ref, wr_ref, br_ref, cnt_in_ref,
                     x1_ref, h2_ref, ti_ref, tg_ref, cnt_ref, cnt_scr):
    @pl.when(jnp.logical_and(pl.program_id(0) == 0, pl.program_id(1) == 0))
    def _():
        cnt_scr[...] = cnt_in_ref[...]

    y_ssd = _dot(u_ref[0], wos_ref[...])
    y_attn = _dot(a_ref[0], woa_ref[...])
    merged = jax.nn.sigmoid(gs_ref[0]) * y_ssd + jax.nn.sigmoid(ga_ref[0]) * y_attn
    mixed = _dot(merged.astype(BF16), wout_ref[...])
    x1 = x_ref[0] + g1_ref[0] * (_rms(mixed) * gpm_ref[...])
    x1_ref[0] = x1
    h2 = _rms(x1) * gpf_ref[...] * (1.0 + sc2_ref[0]) + sh2_ref[0]
    _store_token_tiles(h2_ref.at[0], h2)

    h_hi = h2.astype(BF16)
    h_lo = (h2 - h_hi.astype(F32)).astype(BF16)
    w = wr_ref[...]
    w_hi = w.astype(BF16)
    w_lo = (w - w_hi.astype(F32)).astype(BF16)
    logits = _dot(h_hi, w_hi) + (_dot(h_hi, w_lo) + _dot(h_lo, w_hi)) + br_ref[...]
    lane = lax.broadcasted_iota(jnp.int32, logits.shape, 1)
    idx_out = jnp.zeros(logits.shape, jnp.int32)
    val_out = jnp.zeros(logits.shape, F32)
    top = None
    denom = None
    idxs = []
    for k in range(TOP_K):
        m = jnp.max(logits, axis=-1, keepdims=True)
        idx = jnp.min(jnp.where(logits == m, lane, LANES), axis=-1, keepdims=True)
        if k == 0:
            top = m
            e = jnp.ones_like(m)
            denom = e
        else:
            e = jnp.exp(m - top)
            denom = denom + e
        idxs.append(idx)
        idx_out = jnp.where(lane == k, idx, idx_out)
        val_out = jnp.where(lane == k, e, val_out)
        logits = jnp.where(lane == idx, NEG_BIG * 2, logits)
    tg_ref[0] = val_out / denom
    ti_ref[0] = idx_out

    rowi = lax.broadcasted_iota(jnp.int32, (logits.shape[0], 1), 0)
    valid = jnp.bitwise_and(rowi, SAMPLE_ROWS - 1) < valid_rows
    picked = jnp.zeros(logits.shape, F32)
    for idx in idxs:
        picked = picked + jnp.where(jnp.logical_and(lane == idx, valid), 1.0, 0.0)
    cnt_scr[...] = cnt_scr[...] + jnp.sum(picked, axis=0, keepdims=True)
    cnt_ref[...] = cnt_scr[...]


def _post_mix(u, attn, proj, x, wos, woa, wout, gpm, gpf, gate1, scale2, shift2, wr, br, counts, *,
              tm, per_row, valid_rows):
    nb, rows, _ = x.shape
    row_spec = lambda w, col=0: pl.BlockSpec((1, tm, w), lambda b, i: (b, i, col // w))
    full = lambda shape: pl.BlockSpec(shape, lambda b, i: (0,) * len(shape))
    return pl.pallas_call(
        functools.partial(_post_mix_kernel, valid_rows),
        grid=(nb, rows // tm),
        in_specs=[
            row_spec(D_INNER), row_spec(D_MODEL), row_spec(D_MODEL, COL_GS), row_spec(D_MODEL, COL_GA),
            row_spec(D_MODEL),
            full((D_INNER, D_MODEL)), full((D_MODEL, D_MODEL)), full((D_MODEL, D_MODEL)),
            full((1, D_MODEL)), full((1, D_MODEL)),
            _mod_spec(per_row, tm), _mod_spec(per_row, tm), _mod_spec(per_row, tm),
            full((D_MODEL, LANES)), full((1, LANES)), full((1, LANES)),
        ],
        out_specs=[row_spec(D_MODEL), pl.BlockSpec((1, tm * TILE_ROWS, LANES), lambda b, i: (b, i, 0)),
                   row_spec(LANES), row_spec(LANES), full((1, LANES))],
        out_shape=[
            jax.ShapeDtypeStruct((nb, rows, D_MODEL), F32),
            jax.ShapeDtypeStruct((nb, rows * TILE_ROWS, LANES), F32),
            jax.ShapeDtypeStruct((nb, rows, LANES), jnp.int32),
            jax.ShapeDtypeStruct((nb, rows, LANES), F32),
            jax.ShapeDtypeStruct((1, LANES), F32),
        ],
        scratch_shapes=[pltpu.VMEM((1, LANES), F32)],
        compiler_params=pltpu.CompilerParams(
            dimension_semantics=("arbitrary", "arbitrary"), vmem_limit_bytes=VMEM_LIMIT),
        name="post_mix",
    )(u, attn, proj, proj, x, wos, woa, wout, gpm, gpf, gate1, scale2, shift2, wr, br, counts)


def _moe_kernel(n_tokens, be_ref, na_ref, idx_hbm, h_hbm, wu_ref, bu_ref, wd_ref, bd_ref, ys_hbm,
                idx_smem, xbuf, obuf, wu_bf, wd_bf, sem_idx, sem_in, sem_out):
    i = pl.program_id(0)
    na = na_ref[0]
    slot = lax.rem(i, 2)
    other = 1 - slot
    ring = lax.rem(i, 3)
    ring_prev = lax.rem(i + 2, 3)
    ring_next = lax.rem(i + 1, 3)

    def idx_copy(block, s):
        return pltpu.make_async_copy(idx_hbm.at[block], idx_smem.at[s], sem_idx.at[s])

    block_rows = MOE_ROWS * TILE_ROWS

    def tile(ref, first_row):
        return ref.at[pl.ds(pl.multiple_of(first_row, TILE_ROWS), TILE_ROWS)]

    def start_gather(s, idx_slot, rows):
        for r in rows:
            pltpu.make_async_copy(tile(h_hbm, idx_smem[idx_slot, r]),
                                  xbuf.at[s, pl.ds(r * TILE_ROWS, TILE_ROWS)], sem_in.at[s]).start()

    def wait_gather(s):
        pltpu.make_async_copy(h_hbm.at[pl.ds(0, block_rows)], xbuf.at[s], sem_in.at[s]).wait()

    def start_scatter(s, idx_slot, rows):
        for r in rows:
            pltpu.make_async_copy(obuf.at[s, pl.ds(r * TILE_ROWS, TILE_ROWS)],
                                  tile(ys_hbm, idx_smem[idx_slot, MOE_ROWS + r]), sem_out.at[s]).start()

    def wait_scatter(s):
        pltpu.make_async_copy(obuf.at[s], ys_hbm.at[pl.ds(0, block_rows)], sem_out.at[s]).wait()

    all_rows = range(MOE_ROWS)

    @pl.when(i == 0)
    def _():
        obuf[0] = jnp.zeros((block_rows, LANES), F32)
        plane_rows = ys_hbm.shape[0] // (TOP_K * TILE_ROWS)
        spare = [pltpu.make_async_copy(
            obuf.at[0], ys_hbm.at[pl.ds((k * plane_rows + n_tokens + hf * MOE_ROWS) * TILE_ROWS, block_rows)],
            sem_out.at[0]) for k in range(TOP_K) for hf in range(2)]
        for cp in spare:
            cp.start()
        for cp in spare:
            cp.wait()
        first = idx_copy(0, 0)
        first.start()
        first.wait()
        start_gather(0, 0, all_rows)

        @pl.when(na > 1)
        def _():
            idx_copy(1, 1).start()

    has_prev = i >= 1
    has_next = i + 1 < na

    @pl.when(i < na)
    def _():
        wait_gather(slot)

        @pl.when(has_next)
        def _():
            idx_copy(i + 1, ring_next).wait()

        @pl.when(i >= 2)
        def _():
            wait_scatter(slot)

        @pl.when(jnp.logical_or(i == 0, be_ref[i] != be_ref[jnp.maximum(i - 1, 0)]))
        def _():
            wu_bf[...] = wu_ref[0].astype(BF16)
            wd_bf[...] = wd_ref[0].astype(BF16)

    def compute(gather_next, scatter_prev):
        piece = 2 * LANES
        n_up, n_down = D_FF // piece, D_MODEL // piece
        n_pieces = 2 * n_up + n_down
        group = -(-MOE_ROWS // n_pieces)

        def issue(p):
            rows = range(p * group, min((p + 1) * group, MOE_ROWS))
            if gather_next:
                start_gather(other, ring_next, rows)
            if scatter_prev:
                start_scatter(other, ring_prev, rows)

        x = _load_token_tiles(xbuf.at[slot], MOE_ROWS).astype(BF16)
        acts = []
        for c in range(n_up):
            glu = _dot(x, wu_bf[:, pl.ds(c * piece, piece)]) + bu_ref[0, :, pl.ds(c * piece, piece)]
            issue(2 * c)
            lin = (_dot(x, wu_bf[:, pl.ds(D_FF + c * piece, piece)])
                   + bu_ref[0, :, pl.ds(D_FF + c * piece, piece)])
            issue(2 * c + 1)
            glu = jnp.minimum(glu, SWIGLU_LIMIT)
            lin = jnp.clip(lin, -SWIGLU_LIMIT, SWIGLU_LIMIT)
            acts.append((glu * jax.nn.sigmoid(SWIGLU_ALPHA * glu) * (lin + 1.0)).astype(BF16))
        act = jnp.concatenate(acts, axis=1)
        out_ref = obuf.at[slot]
        for c in range(n_down):
            y = _dot(act, wd_bf[:, pl.ds(c * piece, piece)]) + bd_ref[0, :, pl.ds(c * piece, piece)]
            for j in range(piece // LANES):
                out_ref[pl.ds(c * (piece // LANES) + j, MOE_ROWS, stride=TILE_ROWS), :] = (
                    y[:, j * LANES:(j + 1) * LANES])
            issue(2 * n_up + c)

    for gather_next in (True, False):
        for scatter_prev in (True, False):
            cond = jnp.logical_and(i < na, jnp.logical_and(has_next == gather_next, has_prev == scatter_prev))
            pl.when(cond)(functools.partial(compute, gather_next, scatter_prev))

    @pl.when(i + 2 < na)
    def _():
        idx_copy(i + 2, ring_prev).start()

    @pl.when(i == na - 1)
    def _():
        start_scatter(slot, ring, all_rows)
        wait_scatter(slot)

        @pl.when(has_prev)
        def _():
            wait_scatter(other)


def _moe(idx, h_tiles, ys_rows, block_expert, n_active, wu, bu, wd, bd):
    n_blocks = idx.shape[0]
    grid_spec = pltpu.PrefetchScalarGridSpec(
        num_scalar_prefetch=2,
        grid=(n_blocks,),
        in_specs=[
            pl.BlockSpec(memory_space=pl.ANY),
            pl.BlockSpec(memory_space=pl.ANY),
            pl.BlockSpec((1, D_MODEL, 2 * D_FF), lambda i, be, na: (be[i], 0, 0)),
            pl.BlockSpec((1, 1, 2 * D_FF), lambda i, be, na: (be[i], 0, 0)),
            pl.BlockSpec((1, D_FF, D_MODEL), lambda i, be, na: (be[i], 0, 0)),
            pl.BlockSpec((1, 1, D_MODEL), lambda i, be, na: (be[i], 0, 0)),
        ],
        out_specs=pl.BlockSpec(memory_space=pl.ANY),
        scratch_shapes=[
            pltpu.SMEM((3, 2 * MOE_ROWS), jnp.int32),
            pltpu.VMEM((2, MOE_ROWS * TILE_ROWS, LANES), F32),
            pltpu.VMEM((2, MOE_ROWS * TILE_ROWS, LANES), F32),
            pltpu.VMEM((D_MODEL, 2 * D_FF), BF16),
            pltpu.VMEM((D_FF, D_MODEL), BF16),
            pltpu.SemaphoreType.DMA((3,)),
            pltpu.SemaphoreType.DMA((2,)),
            pltpu.SemaphoreType.DMA((2,)),
        ],
    )
    return pl.pallas_call(
        functools.partial(_moe_kernel, h_tiles.shape[0] // TILE_ROWS),
        grid_spec=grid_spec,
        out_shape=jax.ShapeDtypeStruct((ys_rows * TILE_ROWS, LANES), F32),
        compiler_params=pltpu.CompilerParams(
            dimension_semantics=("arbitrary",), vmem_limit_bytes=MOE_VMEM_LIMIT),
        name="moe",
    )(block_expert, n_active, idx, h_tiles, wu, bu, wd, bd)


def _combine_kernel(ys_ref, tg_ref, x1_ref, gpost_ref, g2_ref, o_ref):
    gates = tg_ref[0]
    tm = gates.shape[0]
    f = gates[:, 0:1] * _load_token_tiles(ys_ref.at[0], tm)
    for k in range(1, TOP_K):
        f = f + gates[:, k:k + 1] * _load_token_tiles(ys_ref.at[k], tm)
    o_ref[0] = x1_ref[0] + g2_ref[0] * (_rms(f) * gpost_ref[...])


def _combine(ys, row0, tg, x1, gpost, gate2, *, tm, per_row):
    nb, rows, _ = x1.shape
    per_b = rows // tm
    base = row0 // tm
    return pl.pallas_call(
        _combine_kernel,
        grid=(nb, per_b),
        in_specs=[
            pl.BlockSpec((TOP_K, tm * TILE_ROWS, LANES), lambda b, i: (0, base + b * per_b + i, 0)),
            pl.BlockSpec((1, tm, LANES), lambda b, i: (b, i, 0)),
            pl.BlockSpec((1, tm, D_MODEL), lambda b, i: (b, i, 0)),
            pl.BlockSpec((1, D_MODEL), lambda b, i: (0, 0)),
            _mod_spec(per_row, tm),
        ],
        out_specs=pl.BlockSpec((1, tm, D_MODEL), lambda b, i: (b, i, 0)),
        out_shape=jax.ShapeDtypeStruct((nb, rows, D_MODEL), F32),
        compiler_params=pltpu.CompilerParams(
            dimension_semantics=("parallel", "parallel"), vmem_limit_bytes=VMEM_LIMIT),
        name="combine",
    )(ys, tg, x1, gpost, gate2)


def _route(top_idx, counts, plane_rows):
    n_tokens = top_idx.shape[0]
    n_slots = n_tokens * TOP_K
    experts = jnp.arange(N_EXPERTS, dtype=jnp.int32)
    padded = (counts + MOE_ROWS - 1) // MOE_ROWS * MOE_ROWS
    group_start = jnp.cumsum(counts) - counts
    padded_end = jnp.cumsum(padded)
    padded_start = padded_end - padded
    n_blocks = -(-n_slots // MOE_ROWS) + N_EXPERTS
    n_active = (padded_end[-1] // MOE_ROWS).astype(jnp.int32)
    block_start = jnp.arange(n_blocks, dtype=jnp.int32) * MOE_ROWS
    block_expert = jnp.sum(block_start[:, None] >= padded_end[None, :], axis=1)
    last_expert = jnp.max(jnp.where(counts > 0, experts, 0))
    block_expert = jnp.where(block_start < padded_end[-1], block_expert, last_expert).astype(jnp.int32)
    slot_id = jnp.arange(n_slots, dtype=jnp.int32).reshape(n_tokens, TOP_K)
    keys = jnp.sort((top_idx * n_slots + slot_id).reshape(-1))
    row = jnp.arange(n_blocks * MOE_ROWS, dtype=jnp.int32).reshape(n_blocks, MOE_ROWS)
    src = row + (group_start - padded_start)[block_expert][:, None]
    real = row < (padded_start + counts)[block_expert][:, None]
    slot = keys[jnp.clip(src, 0, n_slots - 1).reshape(-1)].reshape(n_blocks, MOE_ROWS) % n_slots
    token = slot // TOP_K
    spare = n_tokens + (jnp.arange(n_blocks, dtype=jnp.int32) % 2)[:, None] * MOE_ROWS + row % MOE_ROWS
    dest = jnp.where(real, (slot % TOP_K) * plane_rows + token, spare)
    idx = (jnp.concatenate([token, dest], axis=1) * TILE_ROWS).astype(jnp.int32)
    return idx, block_expert, n_active.reshape(1)


def _pad_lanes(v, value=0.0):
    return jnp.pad(v, [(0, 0)] * (v.ndim - 1) + [(0, LANES - v.shape[-1])], constant_values=value)


def kernel(x_prompt, x_sample, c_prompt, c_sample, cache_swa_k, cache_swa_v, state_conv, state_ssm, w_ada, b_ada, g_pre_mix, g_post_mix, g_pre_ffn, g_post_ffn, w_in, conv_w, conv_b, dt_bias, a_log, d_skip, g_ssm_norm, sinks, w_o_ssd, w_o_attn, w_out, w_router, b_router, w_up, b_up, w_down, b_down):
    depth = w_ada.shape[0]
    n_prompt, seq, _ = x_prompt.shape
    n_sample, dec_seq, _ = x_sample.shape
    yp = x_prompt
    ys_pad = jnp.pad(x_sample, ((0, 0), (0, SAMPLE_ROWS - dec_seq), (0, 0)))
    outs = [[] for _ in range(8)]
    expand = jnp.repeat(jnp.eye(LANES, N_SSD_HEADS, dtype=BF16), SSD_HEAD_DIM, axis=1)
    rows_s = n_sample * SAMPLE_ROWS
    n_p = n_prompt * seq
    n_s = n_sample * dec_seq
    n_tok = n_p + n_s
    tm_p = min(512, seq)
    tm_s = min(512, rows_s)
    tm_mix_p = min(256, seq)
    tm_mix_s = min(256, rows_s)
    assert seq % CHUNK == 0 and seq % tm_p == 0 and rows_s % tm_s == 0 and n_p % n_s == 0
    c_all = jnp.concatenate([c_prompt, c_sample], axis=0)
    c_rows = -(-c_all.shape[0] // SUBLANES) * SUBLANES
    c_all = jnp.pad(c_all, ((0, c_rows - c_all.shape[0]), (0, 0)))

    for l in range(depth):
        wi = w_in[l]
        o_xbc = D_INNER
        o_dt = o_xbc + CONV_DIM
        o_q = o_dt + N_SSD_HEADS
        o_k = o_q + D_MODEL
        o_v = o_k + KV_DIM
        o_gs = o_v + KV_DIM
        o_ga = o_gs + D_MODEL
        head_order = jnp.array(ATTN_HEAD_ORDER)
        w_q = wi[:, o_q:o_k].reshape(D_MODEL, N_ATTN_HEADS, ATTN_HEAD_DIM)[:, head_order].reshape(D_MODEL, D_MODEL)
        w_proj = jnp.concatenate([
            wi[:, :o_xbc], wi[:, o_xbc:o_xbc + D_INNER], w_q, wi[:, o_gs:o_ga], wi[:, o_ga:],
            wi[:, o_xbc + D_INNER:o_dt], wi[:, o_k:o_v], wi[:, o_v:o_gs], _pad_lanes(wi[:, o_dt:o_q])],
            axis=1).astype(BF16)
        cwx, cwbc = conv_w[l][:, :D_INNER], conv_w[l][:, D_INNER:]
        cbx, cbbc = conv_b[l][None, :D_INNER], conv_b[l][None, D_INNER:]
        dtb = _pad_lanes(dt_bias[l][None])
        alog = _pad_lanes(a_log[l][None])
        dskip = jnp.repeat(d_skip[l], SSD_HEAD_DIM)[None]
        gn = g_ssm_norm[l][None]
        wos, wout = w_o_ssd[l].astype(BF16), w_out[l].astype(BF16)
        woa = w_o_attn[l].reshape(N_ATTN_HEADS, ATTN_HEAD_DIM, D_MODEL)[head_order].reshape(D_MODEL, D_MODEL)
        woa = woa.astype(BF16)
        zero_counts = jnp.zeros((1, LANES), F32)
        wr = _pad_lanes(w_router[l])
        br = _pad_lanes(b_router[l][None], NEG_BIG)
        wu, wd = w_up[l], w_down[l]
        bu, bd = b_up[l][:, None, :], b_down[l][:, None, :]
        g_pm, g_pom, g_pf, g_pof = (v[l][None] for v in (g_pre_mix, g_post_mix, g_pre_ffn, g_post_ffn))

        ada = _ada(c_all, w_ada[l], b_ada[l][None])
        ada_p = [ada[:n_prompt, k * D_MODEL:(k + 1) * D_MODEL] for k in range(6)]
        ada_s = [ada[n_prompt:n_prompt + n_sample, k * D_MODEL:(k + 1) * D_MODEL] for k in range(6)]

        mods = [m[:, None, :] for m in ada_p]
        proj = _in_proj(yp, g_pm, mods[1], mods[0], w_proj, tm=tm_p, per_row=False)
        u, ssm_p = _ssd(proj, cwx, cbx, cwbc, cbbc, dtb, alog, dskip, gn)
        attn = _attention(proj, sinks[l], WINDOW)
        x1_p, h2_p, ti_p, tg_p, cnt_p = _post_mix(
            u, attn, proj, yp, wos, woa, wout, g_pom, g_pf, mods[2], mods[4], mods[3], wr, br, zero_counts,
            tm=tm_mix_p, per_row=False, valid_rows=SAMPLE_ROWS)
        gate2_p = mods[5]
        kp = proj[:, seq - WINDOW:, COL_K:COL_K + KV_DIM].reshape(n_prompt, WINDOW, N_KV_HEADS, ATTN_HEAD_DIM)
        vp = proj[:, seq - WINDOW:, COL_V:COL_V + KV_DIM].reshape(n_prompt, WINDOW, N_KV_HEADS, ATTN_HEAD_DIM)
        cp = jnp.concatenate([proj[:, seq - (CONV_W - 1):, COL_X:COL_X + D_INNER],
                              proj[:, seq - (CONV_W - 1):, COL_BC:COL_BC + BC_DIM]], axis=-1)

        mods_s = [jnp.repeat(m, SAMPLE_ROWS, axis=0)[None] for m in ada_s]
        xs_flat = ys_pad.reshape(1, rows_s, D_MODEL)
        proj_s = _in_proj(xs_flat, g_pm, mods_s[1], mods_s[0], w_proj, tm=tm_s, per_row=True)
        proj_sb = proj_s.reshape(n_sample, SAMPLE_ROWS, PROJ_DIM)
        u_s, ssm_s = _ssd_step(
            proj_sb, state_conv[l][:, :, :D_INNER], state_conv[l][:, :, D_INNER:],
            state_ssm[l].reshape(n_sample, D_INNER, D_STATE),
            cwx, cbx, cwbc, cbbc, dtb, alog, dskip, gn, expand, dec_seq)
        k_prev = cache_swa_k[l].reshape(n_sample, -1, KV_DIM)
        v_prev = cache_swa_v[l].reshape(n_sample, -1, KV_DIM)
        attn_s = _attention(proj_sb, sinks[l], SAMPLE_ROWS, prev_kv=(k_prev, v_prev))
        x1_s, h2_s, ti_s, tg_s, cnt_all = _post_mix(
            u_s.reshape(1, rows_s, D_INNER), attn_s.reshape(1, rows_s, D_MODEL), proj_s, xs_flat,
            wos, woa, wout, g_pom, g_pf, mods_s[2], mods_s[4], mods_s[3], wr, br, cnt_p,
            tm=tm_mix_s, per_row=True, valid_rows=dec_seq)
        wb = k_prev.shape[1]
        k_new = proj_sb[:, :dec_seq, COL_K:COL_K + KV_DIM]
        v_new = proj_sb[:, :dec_seq, COL_V:COL_V + KV_DIM]
        ks = jnp.concatenate([k_prev, k_new], axis=1)[:, -wb:].reshape(n_sample, wb, N_KV_HEADS, ATTN_HEAD_DIM)
        vs = jnp.concatenate([v_prev, v_new], axis=1)[:, -wb:].reshape(n_sample, wb, N_KV_HEADS, ATTN_HEAD_DIM)
        raw_xbc = jnp.concatenate([proj_sb[:, :dec_seq, COL_X:COL_X + D_INNER],
                                   proj_sb[:, :dec_seq, COL_BC:COL_BC + BC_DIM]], axis=-1)
        cs = jnp.concatenate([state_conv[l], raw_xbc], axis=1)[:, -(CONV_W - 1):]

        valid = lambda v: v.reshape(n_sample, SAMPLE_ROWS, -1)[:, :dec_seq].reshape(n_s, -1)
        h2_sv = h2_s.reshape(n_sample, SAMPLE_ROWS * TILE_ROWS, LANES)[:, :dec_seq * TILE_ROWS]
        h2_all = jnp.concatenate([h2_p.reshape(n_p * TILE_ROWS, LANES),
                                  h2_sv.reshape(n_s * TILE_ROWS, LANES)], axis=0)
        ti_all = jnp.concatenate([ti_p.reshape(n_p, LANES), valid(ti_s)], axis=0)
        counts = cnt_all[0, :N_EXPERTS].astype(jnp.int32)
        plane_rows = n_tok + 2 * MOE_ROWS
        idx, block_expert, n_active = _route(ti_all[:, :TOP_K], counts, plane_rows)
        ysel = _moe(idx, h2_all, TOP_K * plane_rows, block_expert, n_active, wu, bu, wd, bd)
        ysel = ysel.reshape(TOP_K, plane_rows * TILE_ROWS, LANES)

        yp = _combine(ysel, 0, tg_p, x1_p, g_pof, gate2_p, tm=tm_p, per_row=False)
        x1_sv = valid(x1_s)[None]
        tg_sv = valid(tg_s)[None]
        gate2_s = jnp.repeat(ada_s[5], dec_seq, axis=0)[None]
        ys_new = _combine(ysel, n_p, tg_sv, x1_sv, g_pof, gate2_s, tm=n_s, per_row=True)
        ys_new = ys_new.reshape(n_sample, dec_seq, D_MODEL)
        ys_pad = jnp.pad(ys_new, ((0, 0), (0, SAMPLE_ROWS - dec_seq), (0, 0)))

        for lst, v in zip(outs, (kp, vp, cp, ssm_p, ks, vs, cs,
                                 ssm_s.reshape(n_sample, N_SSD_HEADS, SSD_HEAD_DIM, D_STATE))):
            lst.append(v)

    return (yp, ys_pad[:, :dec_seq], *[jnp.stack(v) for v in outs])
```

```python
import functools
import math

import jax
import jax.numpy as jnp
from jax import lax
from jax.experimental import pallas as pl
from jax.experimental.pallas import tpu as pltpu

F32 = jnp.float32
BF16 = jnp.bfloat16

D_MODEL = 1024
D_INNER = 2 * D_MODEL
SSD_HEAD_DIM = 64
N_SSD_HEADS = D_INNER // SSD_HEAD_DIM
N_SSD_GROUPS = 4
HEADS_PER_GROUP = N_SSD_HEADS // N_SSD_GROUPS
D_STATE = 128
CONV_W = 4
BC_DIM = 2 * N_SSD_GROUPS * D_STATE
CONV_DIM = D_INNER + BC_DIM
CHUNK = 128
ATTN_HEAD_DIM = 64
N_ATTN_HEADS = D_MODEL // ATTN_HEAD_DIM
N_KV_HEADS = 4
KV_REP = N_ATTN_HEADS // N_KV_HEADS
KV_DIM = N_KV_HEADS * ATTN_HEAD_DIM
WINDOW = 128
N_EXPERTS = 32
TOP_K = 4
D_FF = D_MODEL
SWIGLU_LIMIT = 7.0
SWIGLU_ALPHA = 1.702
EPS = 1e-6

LANES = 128
SUBLANES = 8
NEG_BIG = -1e30

COL_Z = 0
COL_X = COL_Z + D_INNER
COL_Q = COL_X + D_INNER
COL_GS = COL_Q + D_MODEL
COL_GA = COL_GS + D_MODEL
COL_BC = COL_GA + D_MODEL
COL_K = COL_BC + BC_DIM
COL_V = COL_K + KV_DIM
COL_DT = COL_V + KV_DIM
PROJ_DIM = COL_DT + LANES
PROJ_TILE_N = PROJ_DIM // 3

ATTN_HEAD_ORDER = tuple(2 * KV_REP * (j // KV_REP) + (j % KV_REP) + KV_REP * hf
                        for j in range(N_ATTN_HEADS // 2) for hf in (0, 1))

TILE_ROWS = D_MODEL // LANES
assert TILE_ROWS == SUBLANES

SAMPLE_ROWS = 8
MOE_ROWS = 256
VMEM_LIMIT = 48 * 1024 * 1024
MOE_VMEM_LIMIT = 56 * 1024 * 1024


def _silu(v):
    return v * jax.nn.sigmoid(v)


def _softplus(v):
    return jnp.maximum(v, 0.0) + jnp.log(1.0 + jnp.exp(-jnp.abs(v)))


def _rms(v):
    return v * lax.rsqrt(jnp.mean(v * v, axis=-1, keepdims=True) + EPS)


def _split3(v):
    hi = v.astype(BF16)
    r1 = v - hi.astype(F32)
    mid = r1.astype(BF16)
    lo = (r1 - mid.astype(F32)).astype(BF16)
    return hi, mid, lo


def _store_token_tiles(ref, rows):
    n = rows.shape[0]
    for j in range(TILE_ROWS):
        ref[pl.ds(j, n, stride=TILE_ROWS), :] = rows[:, j * LANES:(j + 1) * LANES]


def _load_token_tiles(ref, n):
    return jnp.concatenate([ref[pl.ds(j, n, stride=TILE_ROWS), :] for j in range(TILE_ROWS)], axis=1)


def _dot(a, b):
    return jnp.dot(a, b, preferred_element_type=F32)


def _dot_nt(a, b):
    return lax.dot_general(a, b, (((1,), (1,)), ((), ())), preferred_element_type=F32)


def _dot_tn(a, b):
    return lax.dot_general(a, b, (((0,), (0,)), ((), ())), preferred_element_type=F32)


def _dot_exact_rhs(a, b_bf16):
    hi, mid, lo = _split3(a)
    return _dot(hi, b_bf16) + _dot(mid, b_bf16) + _dot(lo, b_bf16)


def _ada_kernel(c_ref, w_ref, b_ref, o_ref):
    o_ref[...] = _dot(_silu(c_ref[...]).astype(BF16), w_ref[...].astype(BF16)) + b_ref[...]


def _ada(c, w, b):
    rows, n = c.shape[0], w.shape[1]
    tn = n // 4
    return pl.pallas_call(
        _ada_kernel,
        grid=(n // tn,),
        in_specs=[
            pl.BlockSpec((rows, D_MODEL), lambda j: (0, 0)),
            pl.BlockSpec((D_MODEL, tn), lambda j: (0, j)),
            pl.BlockSpec((1, tn), lambda j: (0, j)),
        ],
        out_specs=pl.BlockSpec((rows, tn), lambda j: (0, j)),
        out_shape=jax.ShapeDtypeStruct((rows, n), F32),
        compiler_params=pltpu.CompilerParams(
            dimension_semantics=("parallel",), vmem_limit_bytes=VMEM_LIMIT),
        name="ada",
    )(c, w, b)


def _in_proj_kernel(x_ref, g_ref, sc_ref, sh_ref, w_ref, o_ref, h_scr):
    @pl.when(pl.program_id(2) == 0)
    def _():
        h = _rms(x_ref[0]) * g_ref[...] * (1.0 + sc_ref[0]) + sh_ref[0]
        h_scr[...] = h.astype(BF16)

    o_ref[0] = _dot(h_scr[...], w_ref[...])


def _mod_spec(per_row, tm):
    if per_row:
        return pl.BlockSpec((1, tm, D_MODEL), lambda b, i, *_: (b, i, 0))
    return pl.BlockSpec((1, 1, D_MODEL), lambda b, i, *_: (b, 0, 0))


def _in_proj(x, g, scale, shift, w, *, tm, per_row):
    nb, rows, _ = x.shape
    return pl.pallas_call(
        _in_proj_kernel,
        grid=(nb, rows // tm, PROJ_DIM // PROJ_TILE_N),
        in_specs=[
            pl.BlockSpec((1, tm, D_MODEL), lambda b, i, j: (b, i, 0)),
            pl.BlockSpec((1, D_MODEL), lambda b, i, j: (0, 0)),
            _mod_spec(per_row, tm),
            _mod_spec(per_row, tm),
            pl.BlockSpec((D_MODEL, PROJ_TILE_N), lambda b, i, j: (0, j)),
        ],
        out_specs=pl.BlockSpec((1, tm, PROJ_TILE_N), lambda b, i, j: (b, i, j)),
        out_shape=jax.ShapeDtypeStruct((nb, rows, PROJ_DIM), F32),
        scratch_shapes=[pltpu.VMEM((tm, D_MODEL), BF16)],
        compiler_params=pltpu.CompilerParams(
            dimension_semantics=("parallel", "parallel", "arbitrary"),
            vmem_limit_bytes=MOE_VMEM_LIMIT),
        name="in_proj",
    )(x, g, scale, shift, w)


def _causal_conv(ext_ref, raw, prev, w_ref, b_ref, rows):
    base = SUBLANES
    if prev is not None:
        ext_ref[pl.ds(base - (CONV_W - 1), CONV_W - 1), :] = prev
    ext_ref[pl.ds(base, rows), :] = raw
    out = b_ref[...] + raw * w_ref[pl.ds(CONV_W - 1, 1), :]
    for k in range(CONV_W - 1):
        out = out + ext_ref[pl.ds(base - (CONV_W - 1) + k, rows), :] * w_ref[pl.ds(k, 1), :]
    return out


def _gated_group_norm(y, z, g_ref):
    u = y * _silu(z)
    gw = D_INNER // N_SSD_GROUPS
    parts = []
    for g in range(N_SSD_GROUPS):
        ug = u[:, g * gw:(g + 1) * gw]
        parts.append(ug * lax.rsqrt(jnp.mean(ug * ug, axis=-1, keepdims=True) + EPS))
    return jnp.concatenate(parts, axis=-1) * g_ref[...]


def _ssd_kernel(z_ref, x_ref, bc_ref, dt_ref, cwx_ref, cbx_ref, cwbc_ref, cbbc_ref, dtb_ref, alog_ref,
                dskip_ref, gn_ref, u_ref, fin_ref, extx_scr, extbc_scr, st_scr, y_scr):
    q = CHUNK
    c = pl.program_id(1)

    @pl.when(c == 0)
    def _():
        extx_scr[pl.ds(0, SUBLANES), :] = jnp.zeros((SUBLANES, D_INNER), F32)
        extbc_scr[pl.ds(0, SUBLANES), :] = jnp.zeros((SUBLANES, BC_DIM), F32)
        st_scr[...] = jnp.zeros_like(st_scr)

    xs = _silu(_causal_conv(extx_scr, x_ref[0], None, cwx_ref, cbx_ref, q))
    bc = _silu(_causal_conv(extbc_scr, bc_ref[0], None, cwbc_ref, cbbc_ref, q))
    tail = SUBLANES + q - (CONV_W - 1)
    extx_scr[pl.ds(SUBLANES - (CONV_W - 1), CONV_W - 1), :] = extx_scr[pl.ds(tail, CONV_W - 1), :]
    extbc_scr[pl.ds(SUBLANES - (CONV_W - 1), CONV_W - 1), :] = extbc_scr[pl.ds(tail, CONV_W - 1), :]

    dt = _softplus(dt_ref[0] + dtb_ref[...])
    adt = dt * (-jnp.exp(alog_ref[...]))
    row = lax.broadcasted_iota(jnp.int32, (q, q), 0)
    col = lax.broadcasted_iota(jnp.int32, (q, q), 1)
    causal = row >= col
    tri = jnp.where(causal, 1.0, 0.0).astype(BF16)
    h3, m3, l3 = _split3(adt)
    acs = _dot(tri, h3) + _dot(tri, m3) + _dot(tri, l3)
    eacs = jnp.exp(acs)
    acs_t = acs.T
    dt_t = dt.T
    lane = lax.broadcasted_iota(jnp.int32, (1, LANES), 1)
    lo_half = lane < SSD_HEAD_DIM

    for g in range(N_SSD_GROUPS):
        bm = bc[:, g * D_STATE:(g + 1) * D_STATE]
        cm = bc[:, (N_SSD_GROUPS + g) * D_STATE:(N_SSD_GROUPS + g + 1) * D_STATE]
        cb = _dot_nt(cm.astype(BF16), bm.astype(BF16))
        bm_t = bm.T
        for pr in range(HEADS_PER_GROUP // 2):
            h0 = g * HEADS_PER_GROUP + 2 * pr
            lanes = pl.ds(h0 * SSD_HEAD_DIM, LANES)
            x_pair = xs[:, h0 * SSD_HEAD_DIM:h0 * SSD_HEAD_DIM + LANES]
            st_pair = st_scr[:, lanes]
            lhs_y, lhs_s, decs = [], [], []
            for h in (h0, h0 + 1):
                a_col = acs[:, h:h + 1]
                a_row = acs_t[h:h + 1, :]
                dt_row = dt_t[h:h + 1, :]
                a_last = acs_t[h:h + 1, q - 1:q]
                decay = jnp.where(causal, jnp.exp(jnp.where(causal, a_col - a_row, 0.0)), 0.0)
                lhs_y.append((cb * decay * dt_row).astype(BF16))
                lhs_s.append((bm_t * (dt_row * jnp.exp(a_last - a_row))).astype(BF16))
                decs.append(jnp.exp(a_last))
            for h in (h0, h0 + 1):
                lhs_y.append((cm * eacs[:, h:h + 1]).astype(BF16))
            x_top = jnp.where(lo_half, x_pair, 0.0).astype(BF16)
            x_bot = jnp.where(lo_half, 0.0, x_pair).astype(BF16)
            s_top = jnp.where(lo_half, st_pair, 0.0).astype(BF16)
            s_bot = jnp.where(lo_half, 0.0, st_pair).astype(BF16)
            x_bd = jnp.concatenate([x_top, x_bot], axis=0)
            rhs_y = jnp.concatenate([x_bd, s_top, s_bot], axis=0)
            y_pair = _dot(jnp.concatenate(lhs_y, axis=1), rhs_y)
            ds_pair = _dot(jnp.concatenate(lhs_s, axis=1), x_bd)
            dskip = dskip_ref[:, lanes]
            y_scr[:, lanes] = y_pair + dskip * x_pair
            st_scr[:, lanes] = st_pair * jnp.where(lo_half, decs[0], decs[1]) + ds_pair

    u_ref[0] = _gated_group_norm(y_scr[...], z_ref[0], gn_ref).astype(u_ref.dtype)

    @pl.when(c == pl.num_programs(1) - 1)
    def _():
        for pr in range(N_SSD_HEADS // 2):
            t = st_scr[:, pl.ds(pr * LANES, LANES)].T
            fin_ref[0, 2 * pr] = t[:SSD_HEAD_DIM]
            fin_ref[0, 2 * pr + 1] = t[SSD_HEAD_DIM:]


def _ssd(proj, cwx, cbx, cwbc, cbbc, dtb, alog, dskip, gn):
    nb, rows, _ = proj.shape
    q = CHUNK
    full = lambda shape: pl.BlockSpec(shape, lambda b, c: (0,) * len(shape))
    return pl.pallas_call(
        _ssd_kernel,
        grid=(nb, rows // q),
        in_specs=[
            pl.BlockSpec((1, q, D_INNER), lambda b, c: (b, c, COL_Z // D_INNER)),
            pl.BlockSpec((1, q, D_INNER), lambda b, c: (b, c, COL_X // D_INNER)),
            pl.BlockSpec((1, q, BC_DIM), lambda b, c: (b, c, COL_BC // BC_DIM)),
            pl.BlockSpec((1, q, LANES), lambda b, c: (b, c, COL_DT // LANES)),
            full((CONV_W, D_INNER)), full((1, D_INNER)), full((CONV_W, BC_DIM)), full((1, BC_DIM)),
            full((1, LANES)), full((1, LANES)), full((1, D_INNER)), full((1, D_INNER)),
        ],
        out_specs=[
            pl.BlockSpec((1, q, D_INNER), lambda b, c: (b, c, 0)),
            pl.BlockSpec((1, N_SSD_HEADS, SSD_HEAD_DIM, D_STATE), lambda b, c: (b, 0, 0, 0)),
        ],
        out_shape=[
            jax.ShapeDtypeStruct((nb, rows, D_INNER), BF16),
            jax.ShapeDtypeStruct((nb, N_SSD_HEADS, SSD_HEAD_DIM, D_STATE), F32),
        ],
        scratch_shapes=[
            pltpu.VMEM((SUBLANES + q, D_INNER), F32),
            pltpu.VMEM((SUBLANES + q, BC_DIM), F32),
            pltpu.VMEM((D_STATE, D_INNER), F32),
            pltpu.VMEM((q, D_INNER), F32),
        ],
        compiler_params=pltpu.CompilerParams(
            dimension_semantics=("parallel", "arbitrary"), vmem_limit_bytes=VMEM_LIMIT),
        name="ssd",
    )(proj, proj, proj, proj, cwx, cbx, cwbc, cbbc, dtb, alog, dskip, gn)


def _ssd_step_kernel(n_valid, z_ref, x_ref, bc_ref, dt_ref, px_ref, pbc_ref, st_ref, cwx_ref, cbx_ref,
                     cwbc_ref, cbbc_ref, dtb_ref, alog_ref, dskip_ref, gn_ref, exp_ref,
                     u_ref, nst_ref, extx_scr, extbc_scr):
    for sq in range(z_ref.shape[0]):
        _ssd_step_one(n_valid, z_ref.at[sq], x_ref.at[sq], bc_ref.at[sq], dt_ref.at[sq], px_ref.at[sq],
                      pbc_ref.at[sq], st_ref.at[sq], cwx_ref, cbx_ref, cwbc_ref, cbbc_ref, dtb_ref, alog_ref,
                      dskip_ref, gn_ref, exp_ref, u_ref.at[sq], nst_ref.at[sq], extx_scr.at[sq], extbc_scr.at[sq])


def _ssd_step_one(n_valid, z_ref, x_ref, bc_ref, dt_ref, px_ref, pbc_ref, st_ref, cwx_ref, cbx_ref,
                  cwbc_ref, cbbc_ref, dtb_ref, alog_ref, dskip_ref, gn_ref, exp_ref,
                  u_ref, nst_ref, extx_scr, extbc_scr):
    q = SAMPLE_ROWS
    gw = D_INNER // N_SSD_GROUPS
    xs = _silu(_causal_conv(extx_scr, x_ref[...], px_ref[...], cwx_ref, cbx_ref, q))
    bc = _silu(_causal_conv(extbc_scr, bc_ref[...], pbc_ref[...], cwbc_ref, cbbc_ref, q))

    rowi = lax.broadcasted_iota(jnp.int32, (q, 1), 0)
    dt = jnp.where(rowi < n_valid, _softplus(dt_ref[...] + dtb_ref[...]), 0.0)
    adt = dt * (-jnp.exp(alog_ref[...]))
    acs = jnp.zeros_like(adt)
    for s in range(n_valid):
        acs = acs + jnp.where(rowi >= s, adt[s:s + 1, :], 0.0)
    expand = exp_ref[...]
    dt_e = _dot_exact_rhs(dt, expand)
    acs_e = _dot_exact_rhs(acs, expand)
    last_e = acs_e[q - 1:q, :]

    x_end = xs * dt_e * jnp.exp(last_e - acs_e)
    dec3 = _split3(jnp.exp(last_e))
    zrow = jnp.zeros((q - 3, D_INNER), BF16)
    dec_rows = jnp.concatenate([dec3[0], dec3[1], dec3[2], zrow], axis=0)
    ones = jnp.ones((q, D_STATE), BF16)

    y_off, cbs = [], []
    for g in range(N_SSD_GROUPS):
        bm = bc[:, g * D_STATE:(g + 1) * D_STATE].astype(BF16)
        cm = bc[:, (N_SSD_GROUPS + g) * D_STATE:(N_SSD_GROUPS + g + 1) * D_STATE].astype(BF16)
        rows = pl.ds(g * gw, gw)
        st = st_ref[rows, :]
        y_off.append(_dot_nt(cm, st.astype(BF16)))
        cbs.append(_dot_nt(cm, bm))
        d_st = _dot_tn(x_end[:, g * gw:(g + 1) * gw].astype(BF16), bm)
        dec = _dot_tn(dec_rows[:, g * gw:(g + 1) * gw], ones)
        nst_ref[rows, :] = st * dec + d_st

    y = jnp.concatenate(y_off, axis=-1) * jnp.exp(acs_e) + dskip_ref[...] * xs
    for s in range(n_valid):
        keep = rowi >= s
        decay = jnp.where(keep, jnp.exp(jnp.where(keep, acs_e - acs_e[s:s + 1, :], 0.0)), 0.0)
        cb_e = jnp.concatenate([jnp.broadcast_to(cb[:, s:s + 1], (q, gw)) for cb in cbs], axis=-1)
        y = y + decay * cb_e * (dt_e[s:s + 1, :] * xs[s:s + 1, :])
    u_ref[...] = _gated_group_norm(y, z_ref[...], gn_ref).astype(u_ref.dtype)


def _ssd_step(proj, prev_x, prev_bc, state, cwx, cbx, cwbc, cbbc, dtb, alog, dskip, gn, expand, n_valid):
    nb = proj.shape[0]
    q = SAMPLE_ROWS
    seqs = math.gcd(nb, 2)
    full = lambda shape: pl.BlockSpec(shape, lambda b: (0,) * len(shape))
    return pl.pallas_call(
        functools.partial(_ssd_step_kernel, n_valid),
        grid=(nb // seqs,),
        in_specs=[
            pl.BlockSpec((seqs, q, D_INNER), lambda b: (b, 0, COL_Z // D_INNER)),
            pl.BlockSpec((seqs, q, D_INNER), lambda b: (b, 0, COL_X // D_INNER)),
            pl.BlockSpec((seqs, q, BC_DIM), lambda b: (b, 0, COL_BC // BC_DIM)),
            pl.BlockSpec((seqs, q, LANES), lambda b: (b, 0, COL_DT // LANES)),
            pl.BlockSpec((seqs, CONV_W - 1, D_INNER), lambda b: (b, 0, 0)),
            pl.BlockSpec((seqs, CONV_W - 1, BC_DIM), lambda b: (b, 0, 0)),
            pl.BlockSpec((seqs, D_INNER, D_STATE), lambda b: (b, 0, 0)),
            full((CONV_W, D_INNER)), full((1, D_INNER)), full((CONV_W, BC_DIM)), full((1, BC_DIM)),
            full((1, LANES)), full((1, LANES)), full((1, D_INNER)), full((1, D_INNER)),
            full((LANES, D_INNER)),
        ],
        out_specs=[
            pl.BlockSpec((seqs, q, D_INNER), lambda b: (b, 0, 0)),
            pl.BlockSpec((seqs, D_INNER, D_STATE), lambda b: (b, 0, 0)),
        ],
        out_shape=[
            jax.ShapeDtypeStruct((nb, q, D_INNER), BF16),
            jax.ShapeDtypeStruct((nb, D_INNER, D_STATE), F32),
        ],
        scratch_shapes=[
            pltpu.VMEM((seqs, 2 * SUBLANES, D_INNER), F32),
            pltpu.VMEM((seqs, 2 * SUBLANES, BC_DIM), F32),
        ],
        compiler_params=pltpu.CompilerParams(
            dimension_semantics=("parallel",), vmem_limit_bytes=VMEM_LIMIT),
        name="ssd_step",
    )(proj, proj, proj, proj, prev_x, prev_bc, state, cwx, cbx, cwbc, cbbc, dtb, alog, dskip, gn, expand)


def _attention_kernel(qb, prev_always_valid, sinks_ref, q_ref, kp_ref, vp_ref, kc_ref, vc_ref, o_ref):
    wb = WINDOW
    lane = lax.broadcasted_iota(jnp.int32, (1, LANES), 1)
    lo_half = lane < ATTN_HEAD_DIM
    t_p = lax.broadcasted_iota(jnp.int32, (qb, wb), 0)
    s_p = lax.broadcasted_iota(jnp.int32, (qb, wb), 1)
    rel_p = wb + t_p - s_p
    ok_p = rel_p < WINDOW
    if not prev_always_valid:
        ok_p = jnp.logical_and(ok_p, pl.program_id(1) > 0)
    t_c = lax.broadcasted_iota(jnp.int32, (qb, qb), 0)
    s_c = lax.broadcasted_iota(jnp.int32, (qb, qb), 1)
    rel_c = t_c - s_c
    ok_c = rel_c >= 0
    rel_p = rel_p.astype(F32)
    rel_c = rel_c.astype(F32)
    scale = ATTN_HEAD_DIM ** -0.5

    pairs = KV_REP
    half = pairs * qb

    def one_sequence(sq, m, bias_p, bias_c, sink):
        kv_lanes = pl.ds(m * LANES, LANES)
        kp = kp_ref[sq, :, kv_lanes].astype(BF16)
        kc = kc_ref[sq, :, kv_lanes].astype(BF16)
        vp = vp_ref[sq, :, kv_lanes]
        vc = vc_ref[sq, :, kv_lanes]
        vp_lo, vp_hi = jnp.where(lo_half, vp, 1.0).astype(BF16), jnp.where(lo_half, 1.0, vp).astype(BF16)
        vc_lo, vc_hi = jnp.where(lo_half, vc, 1.0).astype(BF16), jnp.where(lo_half, 1.0, vc).astype(BF16)
        q_lo, q_hi = [], []
        for i in range(pairs):
            q_pair = q_ref[sq, :, pl.ds((pairs * m + i) * LANES, LANES)] * scale
            q_lo.append(jnp.where(lo_half, q_pair, 0.0).astype(BF16))
            q_hi.append(jnp.where(lo_half, 0.0, q_pair).astype(BF16))
        qs = jnp.concatenate(q_lo + q_hi, axis=0)
        s_p = _dot_nt(qs, kp) + bias_p
        s_c = _dot_nt(qs, kc) + bias_c
        if qb == wb:
            mx = jnp.max(jnp.maximum(s_p, s_c), axis=-1, keepdims=True)
        else:
            mx = jnp.maximum(jnp.max(s_p, axis=-1, keepdims=True), jnp.max(s_c, axis=-1, keepdims=True))
        mx = jnp.maximum(mx, sink)
        p_p = jnp.exp(s_p - mx).astype(BF16)
        p_c = jnp.exp(s_c - mx).astype(BF16)
        e_sink = jnp.exp(sink - mx)
        o_lo = _dot(p_p[:half], vp_lo) + _dot(p_c[:half], vc_lo)
        o_hi = _dot(p_p[half:], vp_hi) + _dot(p_c[half:], vc_hi)
        for i in range(pairs):
            a = o_lo[i * qb:(i + 1) * qb]
            b = o_hi[i * qb:(i + 1) * qb]
            num = jnp.where(lo_half, a, b)
            den = (pltpu.roll(jnp.where(lo_half, b, a), ATTN_HEAD_DIM, 1)
                   + jnp.where(lo_half, e_sink[i * qb:(i + 1) * qb], e_sink[half + i * qb:half + (i + 1) * qb]))
            o_ref[sq, :, pl.ds((pairs * m + i) * LANES, LANES)] = (num / den).astype(o_ref.dtype)

    for m in range(KV_DIM // LANES):
        heads = [2 * pairs * m + r for r in range(2 * pairs)]
        slopes = [2.0 ** (-8.0 * (h + 1) / N_ATTN_HEADS) for h in heads]
        bias_p = jnp.concatenate([jnp.where(ok_p, -sl * rel_p, NEG_BIG) for sl in slopes], axis=0)
        bias_c = jnp.concatenate([jnp.where(ok_c, -sl * rel_c, NEG_BIG) for sl in slopes], axis=0)
        sink = jnp.concatenate([jnp.full((qb, 1), sinks_ref[h], F32) for h in heads], axis=0)
        for sq in range(q_ref.shape[0]):
            one_sequence(sq, m, bias_p, bias_c, sink)


def _attention(proj, sinks, qb, prev_kv=None, seqs=1):
    nb, rows, _ = proj.shape
    wb = WINDOW
    cur = lambda col: pl.BlockSpec((seqs, qb, KV_DIM), lambda b, n: (b, n, col // KV_DIM))
    if prev_kv is None:
        prev = lambda col: pl.BlockSpec((seqs, wb, KV_DIM),
                                        lambda b, n: (b, jnp.maximum(n - 1, 0), col // KV_DIM))
        kp, vp = proj, proj
        prev_specs = [prev(COL_K), prev(COL_V)]
    else:
        kp, vp = prev_kv
        prev_specs = [pl.BlockSpec((seqs, wb, KV_DIM), lambda b, n: (b, 0, 0))] * 2
    return pl.pallas_call(
        functools.partial(_attention_kernel, qb, prev_kv is not None),
        grid=(nb // seqs, rows // qb),
        in_specs=[
            pl.BlockSpec(memory_space=pltpu.SMEM),
            pl.BlockSpec((seqs, qb, D_MODEL), lambda b, n: (b, n, COL_Q // D_MODEL)),
            *prev_specs,
            cur(COL_K), cur(COL_V),
        ],
        out_specs=pl.BlockSpec((seqs, qb, D_MODEL), lambda b, n: (b, n, 0)),
        out_shape=jax.ShapeDtypeStruct((nb, rows, D_MODEL), BF16),
        compiler_params=pltpu.CompilerParams(
            dimension_semantics=("parallel", "arbitrary"), vmem_limit_bytes=VMEM_LIMIT),
        name="attention",
    )(sinks, proj, kp, vp, proj, proj)


def _post_mix_kernel(valid_rows, u_ref, a_ref, gs_ref, ga_ref, x_ref, wos_ref, woa_ref, wout_ref, gpm_ref,
                     gpf_ref, g1_ref, sc2_ref, sh2_ref, wr_ref, br_ref, cnt_in_ref,
                     x1_ref, h2_ref, ti_ref, tg_ref, cnt_ref, cnt_scr):
    @pl.when(jnp.logical_and(pl.program_id(0) == 0, pl.program_id(1) == 0))
    def _():
        cnt_scr[...] = cnt_in_ref[...]

    y_ssd = _dot(u_ref[0], wos_ref[...])
    y_attn = _dot(a_ref[0], woa_ref[...])
    merged = jax.nn.sigmoid(gs_ref[0]) * y_ssd + jax.nn.sigmoid(ga_ref[0]) * y_attn
    mixed = _dot(merged.astype(BF16), wout_ref[...])
    x1 = x_ref[0] + g1_ref[0] * (_rms(mixed) * gpm_ref[...])
    x1_ref[0] = x1
    h2 = _rms(x1) * gpf_ref[...] * (1.0 + sc2_ref[0]) + sh2_ref[0]
    _store_token_tiles(h2_ref.at[0], h2)

    h_hi = h2.astype(BF16)
    h_lo = (h2 - h_hi.astype(F32)).astype(BF16)
    w = wr_ref[...]
    w_hi = w.astype(BF16)
    w_lo = (w - w_hi.astype(F32)).astype(BF16)
    logits = _dot(h_hi, w_hi) + (_dot(h_hi, w_lo) + _dot(h_lo, w_hi)) + br_ref[...]
    lane = lax.broadcasted_iota(jnp.int32, logits.shape, 1)
    idx_out = jnp.zeros(logits.shape, jnp.int32)
    val_out = jnp.zeros(logits.shape, F32)
    top = None
    denom = None
    idxs = []
    for k in range(TOP_K):
        m = jnp.max(logits, axis=-1, keepdims=True)
        idx = jnp.min(jnp.where(logits == m, lane, LANES), axis=-1, keepdims=True)
        if k == 0:
            top = m
            e = jnp.ones_like(m)
            denom = e
        else:
            e = jnp.exp(m - top)
            denom = denom + e
        idxs.append(idx)
        idx_out = jnp.where(lane == k, idx, idx_out)
        val_out = jnp.where(lane == k, e, val_out)
        logits = jnp.where(lane == idx, NEG_BIG * 2, logits)
    tg_ref[0] = val_out / denom
    ti_ref[0] = idx_out

    rowi = lax.broadcasted_iota(jnp.int32, (logits.shape[0], 1), 0)
    valid = jnp.bitwise_and(rowi, SAMPLE_ROWS - 1) < valid_rows
    picked = jnp.zeros(logits.shape, F32)
    for idx in idxs:
        picked = picked + jnp.where(jnp.logical_and(lane == idx, valid), 1.0, 0.0)
    cnt_scr[...] = cnt_scr[...] + jnp.sum(picked, axis=0, keepdims=True)
    cnt_ref[...] = cnt_scr[...]


def _post_mix(u, attn, proj, x, wos, woa, wout, gpm, gpf, gate1, scale2, shift2, wr, br, counts, *,
              tm, per_row, valid_rows):
    nb, rows, _ = x.shape
    row_spec = lambda w, col=0: pl.BlockSpec((1, tm, w), lambda b, i: (b, i, col // w))
    full = lambda shape: pl.BlockSpec(shape, lambda b, i: (0,) * len(shape))
    return pl.pallas_call(
        functools.partial(_post_mix_kernel, valid_rows),
        grid=(nb, rows // tm),
        in_specs=[
            row_spec(D_INNER), row_spec(D_MODEL), row_spec(D_MODEL, COL_GS), row_spec(D_MODEL, COL_GA),
            row_spec(D_MODEL),
            full((D_INNER, D_MODEL)), full((D_MODEL, D_MODEL)), full((D_MODEL, D_MODEL)),
            full((1, D_MODEL)), full((1, D_MODEL)),
            _mod_spec(per_row, tm), _mod_spec(per_row, tm), _mod_spec(per_row, tm),
            full((D_MODEL, LANES)), full((1, LANES)), full((1, LANES)),
        ],
        out_specs=[row_spec(D_MODEL), pl.BlockSpec((1, tm * TILE_ROWS, LANES), lambda b, i: (b, i, 0)),
                   row_spec(LANES), row_spec(LANES), full((1, LANES))],
        out_shape=[
            jax.ShapeDtypeStruct((nb, rows, D_MODEL), F32),
            jax.ShapeDtypeStruct((nb, rows * TILE_ROWS, LANES), F32),
            jax.ShapeDtypeStruct((nb, rows, LANES), jnp.int32),
            jax.ShapeDtypeStruct((nb, rows, LANES), F32),
            jax.ShapeDtypeStruct((1, LANES), F32),
        ],
        scratch_shapes=[pltpu.VMEM((1, LANES), F32)],
        compiler_params=pltpu.CompilerParams(
            dimension_semantics=("arbitrary", "arbitrary"), vmem_limit_bytes=MOE_VMEM_LIMIT),
        name="post_mix",
    )(u, attn, proj, proj, x, wos, woa, wout, gpm, gpf, gate1, scale2, shift2, wr, br, counts)


def _moe_kernel(n_tokens, be_ref, na_ref, idx_hbm, h_hbm, wu_ref, bu_ref, wd_ref, bd_ref, ys_hbm,
                idx_smem, xbuf, obuf, wu_bf, wd_bf, sem_idx, sem_in, sem_out):
    i = pl.program_id(0)
    na = na_ref[0]
    slot = lax.rem(i, 2)
    other = 1 - slot

    def idx_copy(block, s):
        return pltpu.make_async_copy(idx_hbm.at[block], idx_smem.at[s], sem_idx.at[s])

    block_rows = MOE_ROWS * TILE_ROWS

    def tile(ref, first_row):
        return ref.at[pl.ds(pl.multiple_of(first_row, TILE_ROWS), TILE_ROWS)]

    def start_gather(s):
        for r in range(MOE_ROWS):
            pltpu.make_async_copy(tile(h_hbm, idx_smem[s, r]), xbuf.at[s, pl.ds(r * TILE_ROWS, TILE_ROWS)],
                                  sem_in.at[s]).start()

    def wait_gather(s):
        pltpu.make_async_copy(h_hbm.at[pl.ds(0, block_rows)], xbuf.at[s], sem_in.at[s]).wait()

    def start_scatter(s):
        for r in range(MOE_ROWS):
            pltpu.make_async_copy(obuf.at[s, pl.ds(r * TILE_ROWS, TILE_ROWS)],
                                  tile(ys_hbm, idx_smem[s, MOE_ROWS + r]), sem_out.at[s]).start()

    def wait_scatter(s):
        pltpu.make_async_copy(obuf.at[s], ys_hbm.at[pl.ds(0, block_rows)], sem_out.at[s]).wait()

    @pl.when(i == 0)
    def _():
        obuf[0] = jnp.zeros((block_rows, LANES), F32)
        plane_rows = ys_hbm.shape[0] // (TOP_K * TILE_ROWS)
        spare = [pltpu.make_async_copy(
            obuf.at[0], ys_hbm.at[pl.ds((k * plane_rows + n_tokens + hf * MOE_ROWS) * TILE_ROWS, block_rows)],
            sem_out.at[0]) for k in range(TOP_K) for hf in range(2)]
        for cp in spare:
            cp.start()
        for cp in spare:
            cp.wait()
        first = idx_copy(0, 0)
        first.start()
        first.wait()
        start_gather(0)

        @pl.when(na > 1)
        def _():
            idx_copy(1, 1).start()

    @pl.when(i < na)
    def _():
        wait_gather(slot)

        @pl.when(i + 1 < na)
        def _():
            idx_copy(i + 1, other).wait()
            start_gather(other)

        @pl.when(i >= 2)
        def _():
            wait_scatter(slot)

        @pl.when(jnp.logical_or(i == 0, be_ref[i] != be_ref[jnp.maximum(i - 1, 0)]))
        def _():
            wu_bf[...] = wu_ref[0].astype(BF16)
            wd_bf[...] = wd_ref[0].astype(BF16)

        x = _load_token_tiles(xbuf.at[slot], MOE_ROWS).astype(BF16)
        up = _dot(x, wu_bf[...]) + bu_ref[0]
        glu = jnp.minimum(up[:, :D_FF], SWIGLU_LIMIT)
        lin = jnp.clip(up[:, D_FF:], -SWIGLU_LIMIT, SWIGLU_LIMIT)
        act = glu * jax.nn.sigmoid(SWIGLU_ALPHA * glu) * (lin + 1.0)
        _store_token_tiles(obuf.at[slot], _dot(act.astype(BF16), wd_bf[...]) + bd_ref[0])
        start_scatter(slot)

        @pl.when(i + 2 < na)
        def _():
            idx_copy(i + 2, slot).start()

        @pl.when(i == na - 1)
        def _():
            wait_scatter(slot)

            @pl.when(i >= 1)
            def _():
                wait_scatter(other)


def _moe(idx, h_tiles, ys_rows, block_expert, n_active, wu, bu, wd, bd):
    n_blocks = idx.shape[0]
    grid_spec = pltpu.PrefetchScalarGridSpec(
        num_scalar_prefetch=2,
        grid=(n_blocks,),
        in_specs=[
            pl.BlockSpec(memory_space=pl.ANY),
            pl.BlockSpec(memory_space=pl.ANY),
            pl.BlockSpec((1, D_MODEL, 2 * D_FF), lambda i, be, na: (be[i], 0, 0)),
            pl.BlockSpec((1, 1, 2 * D_FF), lambda i, be, na: (be[i], 0, 0)),
            pl.BlockSpec((1, D_FF, D_MODEL), lambda i, be, na: (be[i], 0, 0)),
            pl.BlockSpec((1, 1, D_MODEL), lambda i, be, na: (be[i], 0, 0)),
        ],
        out_specs=pl.BlockSpec(memory_space=pl.ANY),
        scratch_shapes=[
            pltpu.SMEM((2, 2 * MOE_ROWS), jnp.int32),
            pltpu.VMEM((2, MOE_ROWS * TILE_ROWS, LANES), F32),
            pltpu.VMEM((2, MOE_ROWS * TILE_ROWS, LANES), F32),
            pltpu.VMEM((D_MODEL, 2 * D_FF), BF16),
            pltpu.VMEM((D_FF, D_MODEL), BF16),
            pltpu.SemaphoreType.DMA((2,)),
            pltpu.SemaphoreType.DMA((2,)),
            pltpu.SemaphoreType.DMA((2,)),
        ],
    )
    return pl.pallas_call(
        functools.partial(_moe_kernel, h_tiles.shape[0] // TILE_ROWS),
        grid_spec=grid_spec,
        out_shape=jax.ShapeDtypeStruct((ys_rows * TILE_ROWS, LANES), F32),
        compiler_params=pltpu.CompilerParams(
            dimension_semantics=("arbitrary",), vmem_limit_bytes=MOE_VMEM_LIMIT),
        name="moe",
    )(block_expert, n_active, idx, h_tiles, wu, bu, wd, bd)


def _combine_kernel(ys_ref, tg_ref, x1_ref, gpost_ref, g2_ref, o_ref):
    gates = tg_ref[0]
    tm = gates.shape[0]
    f = gates[:, 0:1] * _load_token_tiles(ys_ref.at[0], tm)
    for k in range(1, TOP_K):
        f = f + gates[:, k:k + 1] * _load_token_tiles(ys_ref.at[k], tm)
    o_ref[0] = x1_ref[0] + g2_ref[0] * (_rms(f) * gpost_ref[...])


def _combine(ys, row0, tg, x1, gpost, gate2, *, tm, per_row):
    nb, rows, _ = x1.shape
    per_b = rows // tm
    base = row0 // tm
    return pl.pallas_call(
        _combine_kernel,
        grid=(nb, per_b),
        in_specs=[
            pl.BlockSpec((TOP_K, tm * TILE_ROWS, LANES), lambda b, i: (0, base + b * per_b + i, 0)),
            pl.BlockSpec((1, tm, LANES), lambda b, i: (b, i, 0)),
            pl.BlockSpec((1, tm, D_MODEL), lambda b, i: (b, i, 0)),
            pl.BlockSpec((1, D_MODEL), lambda b, i: (0, 0)),
            _mod_spec(per_row, tm),
        ],
        out_specs=pl.BlockSpec((1, tm, D_MODEL), lambda b, i: (b, i, 0)),
        out_shape=jax.ShapeDtypeStruct((nb, rows, D_MODEL), F32),
        compiler_params=pltpu.CompilerParams(
            dimension_semantics=("parallel", "parallel"), vmem_limit_bytes=VMEM_LIMIT),
        name="combine",
    )(ys, tg, x1, gpost, gate2)


def _route(top_idx, counts, plane_rows):
    n_tokens = top_idx.shape[0]
    n_slots = n_tokens * TOP_K
    experts = jnp.arange(N_EXPERTS, dtype=jnp.int32)
    padded = (counts + MOE_ROWS - 1) // MOE_ROWS * MOE_ROWS
    group_start = jnp.cumsum(counts) - counts
    padded_end = jnp.cumsum(padded)
    padded_start = padded_end - padded
    n_blocks = -(-n_slots // MOE_ROWS) + N_EXPERTS
    n_active = (padded_end[-1] // MOE_ROWS).astype(jnp.int32)
    block_start = jnp.arange(n_blocks, dtype=jnp.int32) * MOE_ROWS
    block_expert = jnp.sum(block_start[:, None] >= padded_end[None, :], axis=1)
    last_expert = jnp.max(jnp.where(counts > 0, experts, 0))
    block_expert = jnp.where(block_start < padded_end[-1], block_expert, last_expert).astype(jnp.int32)
    slot_id = jnp.arange(n_slots, dtype=jnp.int32).reshape(n_tokens, TOP_K)
    keys = jnp.sort((top_idx * n_slots + slot_id).reshape(-1))
    row = jnp.arange(n_blocks * MOE_ROWS, dtype=jnp.int32).reshape(n_blocks, MOE_ROWS)
    src = row + (group_start - padded_start)[block_expert][:, None]
    real = row < (padded_start + counts)[block_expert][:, None]
    slot = keys[jnp.clip(src, 0, n_slots - 1).reshape(-1)].reshape(n_blocks, MOE_ROWS) % n_slots
    token = slot // TOP_K
    spare = n_tokens + (jnp.arange(n_blocks, dtype=jnp.int32) % 2)[:, None] * MOE_ROWS + row % MOE_ROWS
    dest = jnp.where(real, (slot % TOP_K) * plane_rows + token, spare)
    idx = (jnp.concatenate([token, dest], axis=1) * TILE_ROWS).astype(jnp.int32)
    return idx, block_expert, n_active.reshape(1)


def _pad_lanes(v, value=0.0):
    return jnp.pad(v, [(0, 0)] * (v.ndim - 1) + [(0, LANES - v.shape[-1])], constant_values=value)


def kernel(x_prompt, x_sample, c_prompt, c_sample, cache_swa_k, cache_swa_v, state_conv, state_ssm, w_ada, b_ada, g_pre_mix, g_post_mix, g_pre_ffn, g_post_ffn, w_in, conv_w, conv_b, dt_bias, a_log, d_skip, g_ssm_norm, sinks, w_o_ssd, w_o_attn, w_out, w_router, b_router, w_up, b_up, w_down, b_down):
    depth = w_ada.shape[0]
    n_prompt, seq, _ = x_prompt.shape
    n_sample, dec_seq, _ = x_sample.shape
    yp = x_prompt
    ys_pad = jnp.pad(x_sample, ((0, 0), (0, SAMPLE_ROWS - dec_seq), (0, 0)))
    outs = [[] for _ in range(8)]
    expand = jnp.repeat(jnp.eye(LANES, N_SSD_HEADS, dtype=BF16), SSD_HEAD_DIM, axis=1)
    rows_s = n_sample * SAMPLE_ROWS
    n_p = n_prompt * seq
    n_s = n_sample * dec_seq
    n_tok = n_p + n_s
    tm_p = min(512, seq)
    tm_s = min(512, rows_s)
    tm_mix_p = min(512, seq)
    tm_mix_s = min(256, rows_s)
    assert seq % CHUNK == 0 and seq % tm_p == 0 and rows_s % tm_s == 0 and n_p % n_s == 0
    c_all = jnp.concatenate([c_prompt, c_sample], axis=0)
    c_rows = -(-c_all.shape[0] // SUBLANES) * SUBLANES
    c_all = jnp.pad(c_all, ((0, c_rows - c_all.shape[0]), (0, 0)))

    for l in range(depth):
        wi = w_in[l]
        o_xbc = D_INNER
        o_dt = o_xbc + CONV_DIM
        o_q = o_dt + N_SSD_HEADS
        o_k = o_q + D_MODEL
        o_v = o_k + KV_DIM
        o_gs = o_v + KV_DIM
        o_ga = o_gs + D_MODEL
        head_order = jnp.array(ATTN_HEAD_ORDER)
        w_q = wi[:, o_q:o_k].reshape(D_MODEL, N_ATTN_HEADS, ATTN_HEAD_DIM)[:, head_order].reshape(D_MODEL, D_MODEL)
        w_proj = jnp.concatenate([
            wi[:, :o_xbc], wi[:, o_xbc:o_xbc + D_INNER], w_q, wi[:, o_gs:o_ga], wi[:, o_ga:],
            wi[:, o_xbc + D_INNER:o_dt], wi[:, o_k:o_v], wi[:, o_v:o_gs], _pad_lanes(wi[:, o_dt:o_q])],
            axis=1).astype(BF16)
        cwx, cwbc = conv_w[l][:, :D_INNER], conv_w[l][:, D_INNER:]
        cbx, cbbc = conv_b[l][None, :D_INNER], conv_b[l][None, D_INNER:]
        dtb = _pad_lanes(dt_bias[l][None])
        alog = _pad_lanes(a_log[l][None])
        dskip = jnp.repeat(d_skip[l], SSD_HEAD_DIM)[None]
        gn = g_ssm_norm[l][None]
        wos, wout = w_o_ssd[l].astype(BF16), w_out[l].astype(BF16)
        woa = w_o_attn[l].reshape(N_ATTN_HEADS, ATTN_HEAD_DIM, D_MODEL)[head_order].reshape(D_MODEL, D_MODEL)
        woa = woa.astype(BF16)
        zero_counts = jnp.zeros((1, LANES), F32)
        wr = _pad_lanes(w_router[l])
        br = _pad_lanes(b_router[l][None], NEG_BIG)
        wu, wd = w_up[l], w_down[l]
        bu, bd = b_up[l][:, None, :], b_down[l][:, None, :]
        g_pm, g_pom, g_pf, g_pof = (v[l][None] for v in (g_pre_mix, g_post_mix, g_pre_ffn, g_post_ffn))

        ada = _ada(c_all, w_ada[l], b_ada[l][None])
        ada_p = [ada[:n_prompt, k * D_MODEL:(k + 1) * D_MODEL] for k in range(6)]
        ada_s = [ada[n_prompt:n_prompt + n_sample, k * D_MODEL:(k + 1) * D_MODEL] for k in range(6)]

        mods = [m[:, None, :] for m in ada_p]
        proj = _in_proj(yp, g_pm, mods[1], mods[0], w_proj, tm=min(1024, seq), per_row=False)
        u, ssm_p = _ssd(proj, cwx, cbx, cwbc, cbbc, dtb, alog, dskip, gn)
        attn = _attention(proj, sinks[l], WINDOW)
        x1_p, h2_p, ti_p, tg_p, cnt_p = _post_mix(
            u, attn, proj, yp, wos, woa, wout, g_pom, g_pf, mods[2], mods[4], mods[3], wr, br, zero_counts,
            tm=tm_mix_p, per_row=False, valid_rows=SAMPLE_ROWS)
        gate2_p = mods[5]
        kp = proj[:, seq - WINDOW:, COL_K:COL_K + KV_DIM].reshape(n_prompt, WINDOW, N_KV_HEADS, ATTN_HEAD_DIM)
        vp = proj[:, seq - WINDOW:, COL_V:COL_V + KV_DIM].reshape(n_prompt, WINDOW, N_KV_HEADS, ATTN_HEAD_DIM)
        cp = jnp.concatenate([proj[:, seq - (CONV_W - 1):, COL_X:COL_X + D_INNER],
                              proj[:, seq - (CONV_W - 1):, COL_BC:COL_BC + BC_DIM]], axis=-1)

        mods_s = [jnp.repeat(m, SAMPLE_ROWS, axis=0)[None] for m in ada_s]
        xs_flat = ys_pad.reshape(1, rows_s, D_MODEL)
        proj_s = _in_proj(xs_flat, g_pm, mods_s[1], mods_s[0], w_proj, tm=tm_s, per_row=True)
        proj_sb = proj_s.reshape(n_sample, SAMPLE_ROWS, PROJ_DIM)
        u_s, ssm_s = _ssd_step(
            proj_sb, state_conv[l][:, :, :D_INNER], state_conv[l][:, :, D_INNER:],
            state_ssm[l].reshape(n_sample, D_INNER, D_STATE),
            cwx, cbx, cwbc, cbbc, dtb, alog, dskip, gn, expand, dec_seq)
        k_prev = cache_swa_k[l].reshape(n_sample, -1, KV_DIM)
        v_prev = cache_swa_v[l].reshape(n_sample, -1, KV_DIM)
        attn_s = _attention(proj_sb, sinks[l], SAMPLE_ROWS, prev_kv=(k_prev, v_prev),
                            seqs=math.gcd(n_sample, 8))
        x1_s, h2_s, ti_s, tg_s, cnt_all = _post_mix(
            u_s.reshape(1, rows_s, D_INNER), attn_s.reshape(1, rows_s, D_MODEL), proj_s, xs_flat,
            wos, woa, wout, g_pom, g_pf, mods_s[2], mods_s[4], mods_s[3], wr, br, cnt_p,
            tm=tm_mix_s, per_row=True, valid_rows=dec_seq)
        wb = k_prev.shape[1]
        k_new = proj_sb[:, :dec_seq, COL_K:COL_K + KV_DIM]
        v_new = proj_sb[:, :dec_seq, COL_V:COL_V + KV_DIM]
        ks = jnp.concatenate([k_prev, k_new], axis=1)[:, -wb:].reshape(n_sample, wb, N_KV_HEADS, ATTN_HEAD_DIM)
        vs = jnp.concatenate([v_prev, v_new], axis=1)[:, -wb:].reshape(n_sample, wb, N_KV_HEADS, ATTN_HEAD_DIM)
        raw_xbc = jnp.concatenate([proj_sb[:, :dec_seq, COL_X:COL_X + D_INNER],
                                   proj_sb[:, :dec_seq, COL_BC:COL_BC + BC_DIM]], axis=-1)
        cs = jnp.concatenate([state_conv[l], raw_xbc], axis=1)[:, -(CONV_W - 1):]

        valid = lambda v: v.reshape(n_sample, SAMPLE_ROWS, -1)[:, :dec_seq].reshape(n_s, -1)
        h2_sv = h2_s.reshape(n_sample, SAMPLE_ROWS * TILE_ROWS, LANES)[:, :dec_seq * TILE_ROWS]
        h2_all = jnp.concatenate([h2_p.reshape(n_p * TILE_ROWS, LANES),
                                  h2_sv.reshape(n_s * TILE_ROWS, LANES)], axis=0)
        ti_all = jnp.concatenate([ti_p.reshape(n_p, LANES), valid(ti_s)], axis=0)
        counts = cnt_all[0, :N_EXPERTS].astype(jnp.int32)
        plane_rows = n_tok + 2 * MOE_ROWS
        idx, block_expert, n_active = _route(ti_all[:, :TOP_K], counts, plane_rows)
        ysel = _moe(idx, h2_all, TOP_K * plane_rows, block_expert, n_active, wu, bu, wd, bd)
        ysel = ysel.reshape(TOP_K, plane_rows * TILE_ROWS, LANES)

        yp = _combine(ysel, 0, tg_p, x1_p, g_pof, gate2_p, tm=tm_p, per_row=False)
        x1_sv = valid(x1_s)[None]
        tg_sv = valid(tg_s)[None]
        gate2_s = jnp.repeat(ada_s[5], dec_seq, axis=0)[None]
        ys_new = _combine(ysel, n_p, tg_sv, x1_sv, g_pof, gate2_s, tm=n_s, per_row=True)
        ys_new = ys_new.reshape(n_sample, dec_seq, D_MODEL)
        ys_pad = jnp.pad(ys_new, ((0, 0), (0, SAMPLE_ROWS - dec_seq), (0, 0)))

        for lst, v in zip(outs, (kp, vp, cp, ssm_p, ks, vs, cs,
                                 ssm_s.reshape(n_sample, N_SSD_HEADS, SSD_HEAD_DIM, D_STATE))):
            lst.append(v)

    return (yp, ys_pad[:, :dec_seq], *[jnp.stack(v) for v in outs])
```

```python
import functools
import math

import jax
import jax.numpy as jnp
from jax import lax
from jax.experimental import pallas as pl
from jax.experimental.pallas import tpu as pltpu

F32 = jnp.float32
BF16 = jnp.bfloat16

D_MODEL = 1024
D_INNER = 2 * D_MODEL
SSD_HEAD_DIM = 64
N_SSD_HEADS = D_INNER // SSD_HEAD_DIM
N_SSD_GROUPS = 4
HEADS_PER_GROUP = N_SSD_HEADS // N_SSD_GROUPS
D_STATE = 128
CONV_W = 4
BC_DIM = 2 * N_SSD_GROUPS * D_STATE
CONV_DIM = D_INNER + BC_DIM
CHUNK = 128
ATTN_HEAD_DIM = 64
N_ATTN_HEADS = D_MODEL // ATTN_HEAD_DIM
N_KV_HEADS = 4
KV_REP = N_ATTN_HEADS // N_KV_HEADS
KV_DIM = N_KV_HEADS * ATTN_HEAD_DIM
WINDOW = 128
N_EXPERTS = 32
TOP_K = 4
D_FF = D_MODEL
SWIGLU_LIMIT = 7.0
SWIGLU_ALPHA = 1.702
EPS = 1e-6

LANES = 128
SUBLANES = 8
NEG_BIG = -1e30

COL_Z = 0
COL_X = COL_Z + D_INNER
COL_Q = COL_X + D_INNER
COL_GS = COL_Q + D_MODEL
COL_GA = COL_GS + D_MODEL
COL_BC = COL_GA + D_MODEL
COL_K = COL_BC + BC_DIM
COL_V = COL_K + KV_DIM
COL_DT = COL_V + KV_DIM
PROJ_DIM = COL_DT + LANES
PROJ_TILE_N = PROJ_DIM // 3

ATTN_HEAD_ORDER = tuple(2 * KV_REP * (j // KV_REP) + (j % KV_REP) + KV_REP * hf
                        for j in range(N_ATTN_HEADS // 2) for hf in (0, 1))

TILE_ROWS = D_MODEL // LANES
assert TILE_ROWS == SUBLANES

SAMPLE_ROWS = 8
MOE_ROWS = 256
VMEM_LIMIT = 48 * 1024 * 1024
MOE_VMEM_LIMIT = 56 * 1024 * 1024


def _silu(v):
    return v * jax.nn.sigmoid(v)


def _softplus(v):
    return jnp.maximum(v, 0.0) + jnp.log(1.0 + jnp.exp(-jnp.abs(v)))


def _rms(v):
    return v * lax.rsqrt(jnp.mean(v * v, axis=-1, keepdims=True) + EPS)


def _split3(v):
    hi = v.astype(BF16)
    r1 = v - hi.astype(F32)
    mid = r1.astype(BF16)
    lo = (r1 - mid.astype(F32)).astype(BF16)
    return hi, mid, lo


def _store_token_tiles(ref, rows):
    n = rows.shape[0]
    for j in range(TILE_ROWS):
        ref[pl.ds(j, n, stride=TILE_ROWS), :] = rows[:, j * LANES:(j + 1) * LANES]


def _load_token_tiles(ref, n):
    return jnp.concatenate([ref[pl.ds(j, n, stride=TILE_ROWS), :] for j in range(TILE_ROWS)], axis=1)


def _dot(a, b):
    return jnp.dot(a, b, preferred_element_type=F32)


def _dot_nt(a, b):
    return lax.dot_general(a, b, (((1,), (1,)), ((), ())), preferred_element_type=F32)


def _dot_tn(a, b):
    return lax.dot_general(a, b, (((0,), (0,)), ((), ())), preferred_element_type=F32)


def _dot_exact_rhs(a, b_bf16):
    hi, mid, lo = _split3(a)
    return _dot(hi, b_bf16) + _dot(mid, b_bf16) + _dot(lo, b_bf16)


def _ada_kernel(c_ref, w_ref, b_ref, o_ref):
    o_ref[...] = _dot(_silu(c_ref[...]).astype(BF16), w_ref[...].astype(BF16)) + b_ref[...]


def _ada(c, w, b):
    rows, n = c.shape[0], w.shape[1]
    tn = n // 4
    return pl.pallas_call(
        _ada_kernel,
        grid=(n // tn,),
        in_specs=[
            pl.BlockSpec((rows, D_MODEL), lambda j: (0, 0)),
            pl.BlockSpec((D_MODEL, tn), lambda j: (0, j)),
            pl.BlockSpec((1, tn), lambda j: (0, j)),
        ],
        out_specs=pl.BlockSpec((rows, tn), lambda j: (0, j)),
        out_shape=jax.ShapeDtypeStruct((rows, n), F32),
        compiler_params=pltpu.CompilerParams(
            dimension_semantics=("parallel",), vmem_limit_bytes=VMEM_LIMIT),
        name="ada",
    )(c, w, b)


def _in_proj_kernel(x_ref, g_ref, sc_ref, sh_ref, w_ref, o_ref, h_scr):
    @pl.when(pl.program_id(2) == 0)
    def _():
        h = _rms(x_ref[0]) * g_ref[...] * (1.0 + sc_ref[0]) + sh_ref[0]
        h_scr[...] = h.astype(BF16)

    o_ref[0] = _dot(h_scr[...], w_ref[...])


def _mod_spec(per_row, tm):
    if per_row:
        return pl.BlockSpec((1, tm, D_MODEL), lambda b, i, *_: (b, i, 0))
    return pl.BlockSpec((1, 1, D_MODEL), lambda b, i, *_: (b, 0, 0))


def _in_proj(x, g, scale, shift, w, *, tm, per_row):
    nb, rows, _ = x.shape
    return pl.pallas_call(
        _in_proj_kernel,
        grid=(nb, rows // tm, PROJ_DIM // PROJ_TILE_N),
        in_specs=[
            pl.BlockSpec((1, tm, D_MODEL), lambda b, i, j: (b, i, 0)),
            pl.BlockSpec((1, D_MODEL), lambda b, i, j: (0, 0)),
            _mod_spec(per_row, tm),
            _mod_spec(per_row, tm),
            pl.BlockSpec((D_MODEL, PROJ_TILE_N), lambda b, i, j: (0, j)),
        ],
        out_specs=pl.BlockSpec((1, tm, PROJ_TILE_N), lambda b, i, j: (b, i, j)),
        out_shape=jax.ShapeDtypeStruct((nb, rows, PROJ_DIM), F32),
        scratch_shapes=[pltpu.VMEM((tm, D_MODEL), BF16)],
        compiler_params=pltpu.CompilerParams(
            dimension_semantics=("parallel", "parallel", "arbitrary"),
            vmem_limit_bytes=MOE_VMEM_LIMIT),
        name="in_proj",
    )(x, g, scale, shift, w)


def _causal_conv(ext_ref, raw, prev, w_ref, b_ref, rows):
    base = SUBLANES
    if prev is not None:
        ext_ref[pl.ds(base - (CONV_W - 1), CONV_W - 1), :] = prev
    ext_ref[pl.ds(base, rows), :] = raw
    out = b_ref[...] + raw * w_ref[pl.ds(CONV_W - 1, 1), :]
    for k in range(CONV_W - 1):
        out = out + ext_ref[pl.ds(base - (CONV_W - 1) + k, rows), :] * w_ref[pl.ds(k, 1), :]
    return out


def _gated_group_norm(y, z, g_ref):
    u = y * _silu(z)
    gw = D_INNER // N_SSD_GROUPS
    parts = []
    for g in range(N_SSD_GROUPS):
        ug = u[:, g * gw:(g + 1) * gw]
        parts.append(ug * lax.rsqrt(jnp.mean(ug * ug, axis=-1, keepdims=True) + EPS))
    return jnp.concatenate(parts, axis=-1) * g_ref[...]


def _ssd_kernel(z_ref, x_ref, bc_ref, dt_ref, cwx_ref, cbx_ref, cwbc_ref, cbbc_ref, dtb_ref, alog_ref,
                dskip_ref, gn_ref, u_ref, fin_ref, extx_scr, extbc_scr, st_scr, y_scr):
    q = CHUNK
    c = pl.program_id(1)

    @pl.when(c == 0)
    def _():
        extx_scr[pl.ds(0, SUBLANES), :] = jnp.zeros((SUBLANES, D_INNER), F32)
        extbc_scr[pl.ds(0, SUBLANES), :] = jnp.zeros((SUBLANES, BC_DIM), F32)
        st_scr[...] = jnp.zeros_like(st_scr)

    xs = _silu(_causal_conv(extx_scr, x_ref[0], None, cwx_ref, cbx_ref, q))
    bc = _silu(_causal_conv(extbc_scr, bc_ref[0], None, cwbc_ref, cbbc_ref, q))
    tail = SUBLANES + q - (CONV_W - 1)
    extx_scr[pl.ds(SUBLANES - (CONV_W - 1), CONV_W - 1), :] = extx_scr[pl.ds(tail, CONV_W - 1), :]
    extbc_scr[pl.ds(SUBLANES - (CONV_W - 1), CONV_W - 1), :] = extbc_scr[pl.ds(tail, CONV_W - 1), :]

    dt = _softplus(dt_ref[0] + dtb_ref[...])
    adt = dt * (-jnp.exp(alog_ref[...]))
    row = lax.broadcasted_iota(jnp.int32, (q, q), 0)
    col = lax.broadcasted_iota(jnp.int32, (q, q), 1)
    causal = row >= col
    tri = jnp.where(causal, 1.0, 0.0).astype(BF16)
    h3, m3, l3 = _split3(adt)
    acs = _dot(tri, h3) + _dot(tri, m3) + _dot(tri, l3)
    eacs = jnp.exp(acs)
    acs_t = acs.T
    dt_t = dt.T
    lane = lax.broadcasted_iota(jnp.int32, (1, LANES), 1)
    lo_half = lane < SSD_HEAD_DIM

    for g in range(N_SSD_GROUPS):
        bm = bc[:, g * D_STATE:(g + 1) * D_STATE]
        cm = bc[:, (N_SSD_GROUPS + g) * D_STATE:(N_SSD_GROUPS + g + 1) * D_STATE]
        cb = jnp.where(causal, _dot_nt(cm.astype(BF16), bm.astype(BF16)), 0.0)
        bm_t = bm.T
        for pr in range(HEADS_PER_GROUP // 2):
            h0 = g * HEADS_PER_GROUP + 2 * pr
            lanes = pl.ds(h0 * SSD_HEAD_DIM, LANES)
            x_pair = xs[:, h0 * SSD_HEAD_DIM:h0 * SSD_HEAD_DIM + LANES]
            st_pair = st_scr[:, lanes]
            lhs_y, lhs_s, decs = [], [], []
            for h in (h0, h0 + 1):
                a_col = acs[:, h:h + 1]
                a_row = acs_t[h:h + 1, :]
                dt_row = dt_t[h:h + 1, :]
                a_last = acs_t[h:h + 1, q - 1:q]
                decay = jnp.exp(jnp.minimum(a_col - a_row, 0.0))
                lhs_y.append((cb * decay * dt_row).astype(BF16))
                lhs_s.append((bm_t * (dt_row * jnp.exp(a_last - a_row))).astype(BF16))
                decs.append(jnp.exp(a_last))
            for h in (h0, h0 + 1):
                lhs_y.append((cm * eacs[:, h:h + 1]).astype(BF16))
            x_top = jnp.where(lo_half, x_pair, 0.0).astype(BF16)
            x_bot = jnp.where(lo_half, 0.0, x_pair).astype(BF16)
            s_top = jnp.where(lo_half, st_pair, 0.0).astype(BF16)
            s_bot = jnp.where(lo_half, 0.0, st_pair).astype(BF16)
            x_bd = jnp.concatenate([x_top, x_bot], axis=0)
            rhs_y = jnp.concatenate([x_bd, s_top, s_bot], axis=0)
            y_pair = _dot(jnp.concatenate(lhs_y, axis=1), rhs_y)
            ds_pair = _dot(jnp.concatenate(lhs_s, axis=1), x_bd)
            dskip = dskip_ref[:, lanes]
            y_scr[:, lanes] = y_pair + dskip * x_pair
            st_scr[:, lanes] = st_pair * jnp.where(lo_half, decs[0], decs[1]) + ds_pair

    u_ref[0] = _gated_group_norm(y_scr[...], z_ref[0], gn_ref).astype(u_ref.dtype)

    @pl.when(c == pl.num_programs(1) - 1)
    def _():
        for pr in range(N_SSD_HEADS // 2):
            t = st_scr[:, pl.ds(pr * LANES, LANES)].T
            fin_ref[0, 2 * pr] = t[:SSD_HEAD_DIM]
            fin_ref[0, 2 * pr + 1] = t[SSD_HEAD_DIM:]


def _ssd(proj, cwx, cbx, cwbc, cbbc, dtb, alog, dskip, gn):
    nb, rows, _ = proj.shape
    q = CHUNK
    full = lambda shape: pl.BlockSpec(shape, lambda b, c: (0,) * len(shape))
    return pl.pallas_call(
        _ssd_kernel,
        grid=(nb, rows // q),
        in_specs=[
            pl.BlockSpec((1, q, D_INNER), lambda b, c: (b, c, COL_Z // D_INNER)),
            pl.BlockSpec((1, q, D_INNER), lambda b, c: (b, c, COL_X // D_INNER)),
            pl.BlockSpec((1, q, BC_DIM), lambda b, c: (b, c, COL_BC // BC_DIM)),
            pl.BlockSpec((1, q, LANES), lambda b, c: (b, c, COL_DT // LANES)),
            full((CONV_W, D_INNER)), full((1, D_INNER)), full((CONV_W, BC_DIM)), full((1, BC_DIM)),
            full((1, LANES)), full((1, LANES)), full((1, D_INNER)), full((1, D_INNER)),
        ],
        out_specs=[
            pl.BlockSpec((1, q, D_INNER), lambda b, c: (b, c, 0)),
            pl.BlockSpec((1, N_SSD_HEADS, SSD_HEAD_DIM, D_STATE), lambda b, c: (b, 0, 0, 0)),
        ],
        out_shape=[
            jax.ShapeDtypeStruct((nb, rows, D_INNER), BF16),
            jax.ShapeDtypeStruct((nb, N_SSD_HEADS, SSD_HEAD_DIM, D_STATE), F32),
        ],
        scratch_shapes=[
            pltpu.VMEM((SUBLANES + q, D_INNER), F32),
            pltpu.VMEM((SUBLANES + q, BC_DIM), F32),
            pltpu.VMEM((D_STATE, D_INNER), F32),
            pltpu.VMEM((q, D_INNER), F32),
        ],
        compiler_params=pltpu.CompilerParams(
            dimension_semantics=("parallel", "arbitrary"), vmem_limit_bytes=VMEM_LIMIT),
        name="ssd",
    )(proj, proj, proj, proj, cwx, cbx, cwbc, cbbc, dtb, alog, dskip, gn)


def _ssd_step_kernel(n_valid, z_ref, x_ref, bc_ref, dt_ref, px_ref, pbc_ref, st_ref, cwx_ref, cbx_ref,
                     cwbc_ref, cbbc_ref, dtb_ref, alog_ref, dskip_ref, gn_ref, exp_ref,
                     u_ref, nst_ref, extx_scr, extbc_scr):
    for sq in range(z_ref.shape[0]):
        _ssd_step_one(n_valid, z_ref.at[sq], x_ref.at[sq], bc_ref.at[sq], dt_ref.at[sq], px_ref.at[sq],
                      pbc_ref.at[sq], st_ref.at[sq], cwx_ref, cbx_ref, cwbc_ref, cbbc_ref, dtb_ref, alog_ref,
                      dskip_ref, gn_ref, exp_ref, u_ref.at[sq], nst_ref.at[sq], extx_scr.at[sq], extbc_scr.at[sq])


def _ssd_step_one(n_valid, z_ref, x_ref, bc_ref, dt_ref, px_ref, pbc_ref, st_ref, cwx_ref, cbx_ref,
                  cwbc_ref, cbbc_ref, dtb_ref, alog_ref, dskip_ref, gn_ref, exp_ref,
                  u_ref, nst_ref, extx_scr, extbc_scr):
    q = SAMPLE_ROWS
    gw = D_INNER // N_SSD_GROUPS
    xs = _silu(_causal_conv(extx_scr, x_ref[...], px_ref[...], cwx_ref, cbx_ref, q))
    bc = _silu(_causal_conv(extbc_scr, bc_ref[...], pbc_ref[...], cwbc_ref, cbbc_ref, q))

    rowi = lax.broadcasted_iota(jnp.int32, (q, 1), 0)
    dt = jnp.where(rowi < n_valid, _softplus(dt_ref[...] + dtb_ref[...]), 0.0)
    adt = dt * (-jnp.exp(alog_ref[...]))
    acs = jnp.zeros_like(adt)
    for s in range(n_valid):
        acs = acs + jnp.where(rowi >= s, adt[s:s + 1, :], 0.0)
    expand = exp_ref[...]
    dt_e = _dot_exact_rhs(dt, expand)
    acs_e = _dot_exact_rhs(acs, expand)
    last_e = acs_e[q - 1:q, :]

    x_end = xs * dt_e * jnp.exp(last_e - acs_e)
    dec3 = _split3(jnp.exp(last_e))
    zrow = jnp.zeros((q - 3, D_INNER), BF16)
    dec_rows = jnp.concatenate([dec3[0], dec3[1], dec3[2], zrow], axis=0)
    ones = jnp.ones((q, D_STATE), BF16)

    y_off, cbs = [], []
    for g in range(N_SSD_GROUPS):
        bm = bc[:, g * D_STATE:(g + 1) * D_STATE].astype(BF16)
        cm = bc[:, (N_SSD_GROUPS + g) * D_STATE:(N_SSD_GROUPS + g + 1) * D_STATE].astype(BF16)
        rows = pl.ds(g * gw, gw)
        st = st_ref[rows, :]
        y_off.append(_dot_nt(cm, st.astype(BF16)))
        cbs.append(_dot_nt(cm, bm))
        d_st = _dot_tn(x_end[:, g * gw:(g + 1) * gw].astype(BF16), bm)
        dec = _dot_tn(dec_rows[:, g * gw:(g + 1) * gw], ones)
        nst_ref[rows, :] = st * dec + d_st

    y = jnp.concatenate(y_off, axis=-1) * jnp.exp(acs_e) + dskip_ref[...] * xs
    for s in range(n_valid):
        keep = rowi >= s
        decay = jnp.where(keep, jnp.exp(jnp.where(keep, acs_e - acs_e[s:s + 1, :], 0.0)), 0.0)
        cb_e = jnp.concatenate([jnp.broadcast_to(cb[:, s:s + 1], (q, gw)) for cb in cbs], axis=-1)
        y = y + decay * cb_e * (dt_e[s:s + 1, :] * xs[s:s + 1, :])
    u_ref[...] = _gated_group_norm(y, z_ref[...], gn_ref).astype(u_ref.dtype)


def _ssd_step(proj, prev_x, prev_bc, state, cwx, cbx, cwbc, cbbc, dtb, alog, dskip, gn, expand, n_valid):
    nb = proj.shape[0]
    q = SAMPLE_ROWS
    seqs = math.gcd(nb, 2)
    full = lambda shape: pl.BlockSpec(shape, lambda b: (0,) * len(shape))
    return pl.pallas_call(
        functools.partial(_ssd_step_kernel, n_valid),
        grid=(nb // seqs,),
        in_specs=[
            pl.BlockSpec((seqs, q, D_INNER), lambda b: (b, 0, COL_Z // D_INNER)),
            pl.BlockSpec((seqs, q, D_INNER), lambda b: (b, 0, COL_X // D_INNER)),
            pl.BlockSpec((seqs, q, BC_DIM), lambda b: (b, 0, COL_BC // BC_DIM)),
            pl.BlockSpec((seqs, q, LANES), lambda b: (b, 0, COL_DT // LANES)),
            pl.BlockSpec((seqs, CONV_W - 1, D_INNER), lambda b: (b, 0, 0)),
            pl.BlockSpec((seqs, CONV_W - 1, BC_DIM), lambda b: (b, 0, 0)),
            pl.BlockSpec((seqs, D_INNER, D_STATE), lambda b: (b, 0, 0)),
            full((CONV_W, D_INNER)), full((1, D_INNER)), full((CONV_W, BC_DIM)), full((1, BC_DIM)),
            full((1, LANES)), full((1, LANES)), full((1, D_INNER)), full((1, D_INNER)),
            full((LANES, D_INNER)),
        ],
        out_specs=[
            pl.BlockSpec((seqs, q, D_INNER), lambda b: (b, 0, 0)),
            pl.BlockSpec((seqs, D_INNER, D_STATE), lambda b: (b, 0, 0)),
        ],
        out_shape=[
            jax.ShapeDtypeStruct((nb, q, D_INNER), BF16),
            jax.ShapeDtypeStruct((nb, D_INNER, D_STATE), F32),
        ],
        scratch_shapes=[
            pltpu.VMEM((seqs, 2 * SUBLANES, D_INNER), F32),
            pltpu.VMEM((seqs, 2 * SUBLANES, BC_DIM), F32),
        ],
        compiler_params=pltpu.CompilerParams(
            dimension_semantics=("parallel",), vmem_limit_bytes=VMEM_LIMIT),
        name="ssd_step",
    )(proj, proj, proj, proj, prev_x, prev_bc, state, cwx, cbx, cwbc, cbbc, dtb, alog, dskip, gn, expand)


def _attention_kernel(qb, prev_always_valid, sinks_ref, q_ref, kp_ref, vp_ref, kc_ref, vc_ref, o_ref):
    wb = WINDOW
    lane = lax.broadcasted_iota(jnp.int32, (1, LANES), 1)
    lo_half = lane < ATTN_HEAD_DIM
    keys_major = qb % LANES == 0
    t_ax, s_ax = (1, 0) if keys_major else (0, 1)
    shape_p = (wb, qb) if keys_major else (qb, wb)
    t_p = lax.broadcasted_iota(jnp.int32, shape_p, t_ax)
    s_p = lax.broadcasted_iota(jnp.int32, shape_p, s_ax)
    rel_p = wb + t_p - s_p
    ok_p = rel_p < WINDOW
    if not prev_always_valid:
        ok_p = jnp.logical_and(ok_p, pl.program_id(1) > 0)
    t_c = lax.broadcasted_iota(jnp.int32, (qb, qb), t_ax)
    s_c = lax.broadcasted_iota(jnp.int32, (qb, qb), s_ax)
    rel_c = t_c - s_c
    ok_c = rel_c >= 0
    rel_p = rel_p.astype(F32)
    rel_c = rel_c.astype(F32)
    scale = ATTN_HEAD_DIM ** -0.5

    pairs = KV_REP
    half = pairs * qb

    def one_sequence(sq, m, bias_p, bias_c, sink):
        kv_lanes = pl.ds(m * LANES, LANES)
        kp = kp_ref[sq, :, kv_lanes].astype(BF16)
        kc = kc_ref[sq, :, kv_lanes].astype(BF16)
        vp = vp_ref[sq, :, kv_lanes]
        vc = vc_ref[sq, :, kv_lanes]
        vp_lo, vp_hi = jnp.where(lo_half, vp, 1.0).astype(BF16), jnp.where(lo_half, 1.0, vp).astype(BF16)
        vc_lo, vc_hi = jnp.where(lo_half, vc, 1.0).astype(BF16), jnp.where(lo_half, 1.0, vc).astype(BF16)
        q_lo, q_hi = [], []
        for i in range(pairs):
            q_pair = q_ref[sq, :, pl.ds((pairs * m + i) * LANES, LANES)] * scale
            q_lo.append(jnp.where(lo_half, q_pair, 0.0).astype(BF16))
            q_hi.append(jnp.where(lo_half, 0.0, q_pair).astype(BF16))
        qs = jnp.concatenate(q_lo + q_hi, axis=0)
        s_p = _dot_nt(qs, kp) + bias_p
        s_c = _dot_nt(qs, kc) + bias_c
        if qb == wb:
            mx = jnp.max(jnp.maximum(s_p, s_c), axis=-1, keepdims=True)
        else:
            mx = jnp.maximum(jnp.max(s_p, axis=-1, keepdims=True), jnp.max(s_c, axis=-1, keepdims=True))
        mx = jnp.maximum(mx, sink)
        p_p = jnp.exp(s_p - mx).astype(BF16)
        p_c = jnp.exp(s_c - mx).astype(BF16)
        e_sink = jnp.exp(sink - mx)
        o_lo = _dot(p_p[:half], vp_lo) + _dot(p_c[:half], vc_lo)
        o_hi = _dot(p_p[half:], vp_hi) + _dot(p_c[half:], vc_hi)
        for i in range(pairs):
            a = o_lo[i * qb:(i + 1) * qb]
            b = o_hi[i * qb:(i + 1) * qb]
            num = jnp.where(lo_half, a, b)
            den = (pltpu.roll(jnp.where(lo_half, b, a), ATTN_HEAD_DIM, 1)
                   + jnp.where(lo_half, e_sink[i * qb:(i + 1) * qb], e_sink[half + i * qb:half + (i + 1) * qb]))
            o_ref[sq, :, pl.ds((pairs * m + i) * LANES, LANES)] = (num / den).astype(o_ref.dtype)

    def one_sequence_keys_major(sq, m, bias_p, bias_c, sink):
        kv_lanes = pl.ds(m * LANES, LANES)
        kp = kp_ref[sq, :, kv_lanes].astype(BF16)
        kc = kc_ref[sq, :, kv_lanes].astype(BF16)
        vp = vp_ref[sq, :, kv_lanes].astype(BF16)
        vc = vc_ref[sq, :, kv_lanes].astype(BF16)
        q_lo, q_hi = [], []
        for i in range(pairs):
            q_pair = q_ref[sq, :, pl.ds((pairs * m + i) * LANES, LANES)] * scale
            q_lo.append(jnp.where(lo_half, q_pair, 0.0).astype(BF16))
            q_hi.append(jnp.where(lo_half, 0.0, q_pair).astype(BF16))
        qs = jnp.concatenate(q_lo + q_hi, axis=0)
        s_p = _dot_nt(kp, qs) + bias_p
        s_c = _dot_nt(kc, qs) + bias_c
        mx = jnp.maximum(jnp.maximum(jnp.max(s_p, axis=0, keepdims=True), jnp.max(s_c, axis=0, keepdims=True)),
                         sink)
        p_p = jnp.exp(s_p - mx)
        p_c = jnp.exp(s_c - mx)
        den = jnp.sum(p_p, axis=0, keepdims=True) + jnp.sum(p_c, axis=0, keepdims=True) + jnp.exp(sink - mx)
        o_t = (_dot_tn(vp, p_p.astype(BF16)) + _dot_tn(vc, p_c.astype(BF16))) / den
        row_lo = lax.broadcasted_iota(jnp.int32, (LANES, 1), 0) < ATTN_HEAD_DIM
        for i in range(pairs):
            blk = jnp.where(row_lo, o_t[:, i * qb:(i + 1) * qb], o_t[:, (pairs + i) * qb:(pairs + i + 1) * qb])
            o_ref[sq, :, pl.ds((pairs * m + i) * LANES, LANES)] = blk.T.astype(o_ref.dtype)

    for m in range(KV_DIM // LANES):
        heads = [2 * pairs * m + r for r in range(2 * pairs)]
        slopes = [2.0 ** (-8.0 * (h + 1) / N_ATTN_HEADS) for h in heads]
        bias_p = jnp.concatenate([jnp.where(ok_p, -sl * rel_p, NEG_BIG) for sl in slopes], axis=t_ax)
        bias_c = jnp.concatenate([jnp.where(ok_c, -sl * rel_c, NEG_BIG) for sl in slopes], axis=t_ax)
        one = (1, qb) if keys_major else (qb, 1)
        sink = jnp.concatenate([jnp.full(one, sinks_ref[h], F32) for h in heads], axis=t_ax)
        for sq in range(q_ref.shape[0]):
            (one_sequence_keys_major if keys_major else one_sequence)(sq, m, bias_p, bias_c, sink)


def _attention(proj, sinks, qb, prev_kv=None, seqs=1):
    nb, rows, _ = proj.shape
    wb = WINDOW
    cur = lambda col: pl.BlockSpec((seqs, qb, KV_DIM), lambda b, n: (b, n, col // KV_DIM))
    if prev_kv is None:
        prev = lambda col: pl.BlockSpec((seqs, wb, KV_DIM),
                                        lambda b, n: (b, jnp.maximum(n - 1, 0), col // KV_DIM))
        kp, vp = proj, proj
        prev_specs = [prev(COL_K), prev(COL_V)]
    else:
        kp, vp = prev_kv
        prev_specs = [pl.BlockSpec((seqs, wb, KV_DIM), lambda b, n: (b, 0, 0))] * 2
    return pl.pallas_call(
        functools.partial(_attention_kernel, qb, prev_kv is not None),
        grid=(nb // seqs, rows // qb),
        in_specs=[
            pl.BlockSpec(memory_space=pltpu.SMEM),
            pl.BlockSpec((seqs, qb, D_MODEL), lambda b, n: (b, n, COL_Q // D_MODEL)),
            *prev_specs,
            cur(COL_K), cur(COL_V),
        ],
        out_specs=pl.BlockSpec((seqs, qb, D_MODEL), lambda b, n: (b, n, 0)),
        out_shape=jax.ShapeDtypeStruct((nb, rows, D_MODEL), BF16),
        compiler_params=pltpu.CompilerParams(
            dimension_semantics=("parallel", "arbitrary"), vmem_limit_bytes=VMEM_LIMIT),
        name="attention",
    )(sinks, proj, kp, vp, proj, proj)


def _post_mix_kernel(valid_rows, u_ref, a_ref, gs_ref, ga_ref, x_ref, wos_ref, woa_ref, wout_ref, gpm_ref,
                     gpf_ref, g1_ref, sc2_ref, sh2_ref, wr_ref, br_ref, cnt_in_ref,
                     x1_ref, h2_ref, ti_ref, tg_ref, cnt_ref, cnt_scr):
    @pl.when(jnp.logical_and(pl.program_id(0) == 0, pl.program_id(1) == 0))
    def _():
        cnt_scr[...] = cnt_in_ref[...]

    y_ssd = _dot(u_ref[0], wos_ref[...])
    y_attn = _dot(a_ref[0], woa_ref[...])
    merged = jax.nn.sigmoid(gs_ref[0]) * y_ssd + jax.nn.sigmoid(ga_ref[0]) * y_attn
    mixed = _dot(merged.astype(BF16), wout_ref[...])
    x1 = x_ref[0] + g1_ref[0] * (_rms(mixed) * gpm_ref[...])
    x1_ref[0] = x1
    h2 = _rms(x1) * gpf_ref[...] * (1.0 + sc2_ref[0]) + sh2_ref[0]
    _store_token_tiles(h2_ref.at[0], h2)

    h_hi = h2.astype(BF16)
    h_lo = (h2 - h_hi.astype(F32)).astype(BF16)
    w = wr_ref[...]
    w_hi = w.astype(BF16)
    w_lo = (w - w_hi.astype(F32)).astype(BF16)
    logits = _dot(h_hi, w_hi) + (_dot(h_hi, w_lo) + _dot(h_lo, w_hi)) + br_ref[...]
    lane = lax.broadcasted_iota(jnp.int32, logits.shape, 1)
    idx_out = jnp.zeros(logits.shape, jnp.int32)
    val_out = jnp.zeros(logits.shape, F32)
    top = None
    denom = None
    idxs = []
    for k in range(TOP_K):
        m = jnp.max(logits, axis=-1, keepdims=True)
        idx = jnp.min(jnp.where(logits == m, lane, LANES), axis=-1, keepdims=True)
        if k == 0:
            top = m
            e = jnp.ones_like(m)
            denom = e
        else:
            e = jnp.exp(m - top)
            denom = denom + e
        idxs.append(idx)
        idx_out = jnp.where(lane == k, idx, idx_out)
        val_out = jnp.where(lane == k, e, val_out)
        logits = jnp.where(lane == idx, NEG_BIG * 2, logits)
    tg_ref[0] = val_out / denom
    ti_ref[0] = idx_out

    rowi = lax.broadcasted_iota(jnp.int32, (logits.shape[0], 1), 0)
    valid = jnp.bitwise_and(rowi, SAMPLE_ROWS - 1) < valid_rows
    picked = jnp.zeros(logits.shape, F32)
    for idx in idxs:
        picked = picked + jnp.where(jnp.logical_and(lane == idx, valid), 1.0, 0.0)
    cnt_scr[...] = cnt_scr[...] + jnp.sum(picked, axis=0, keepdims=True)
    cnt_ref[...] = cnt_scr[...]


def _post_mix(u, attn, proj, x, wos, woa, wout, gpm, gpf, gate1, scale2, shift2, wr, br, counts, *,
              tm, per_row, valid_rows):
    nb, rows, _ = x.shape
    row_spec = lambda w, col=0: pl.BlockSpec((1, tm, w), lambda b, i: (b, i, col // w))
    full = lambda shape: pl.BlockSpec(shape, lambda b, i: (0,) * len(shape))
    return pl.pallas_call(
        functools.partial(_post_mix_kernel, valid_rows),
        grid=(nb, rows // tm),
        in_specs=[
            row_spec(D_INNER), row_spec(D_MODEL), row_spec(D_MODEL, COL_GS), row_spec(D_MODEL, COL_GA),
            row_spec(D_MODEL),
            full((D_INNER, D_MODEL)), full((D_MODEL, D_MODEL)), full((D_MODEL, D_MODEL)),
            full((1, D_MODEL)), full((1, D_MODEL)),
            _mod_spec(per_row, tm), _mod_spec(per_row, tm), _mod_spec(per_row, tm),
            full((D_MODEL, LANES)), full((1, LANES)), full((1, LANES)),
        ],
        out_specs=[row_spec(D_MODEL), pl.BlockSpec((1, tm * TILE_ROWS, LANES), lambda b, i: (b, i, 0)),
                   row_spec(LANES), row_spec(LANES), full((1, LANES))],
        out_shape=[
            jax.ShapeDtypeStruct((nb, rows, D_MODEL), F32),
            jax.ShapeDtypeStruct((nb, rows * TILE_ROWS, LANES), F32),
            jax.ShapeDtypeStruct((nb, rows, LANES), jnp.int32),
            jax.ShapeDtypeStruct((nb, rows, LANES), F32),
            jax.ShapeDtypeStruct((1, LANES), F32),
        ],
        scratch_shapes=[pltpu.VMEM((1, LANES), F32)],
        compiler_params=pltpu.CompilerParams(
            dimension_semantics=("arbitrary", "arbitrary"), vmem_limit_bytes=MOE_VMEM_LIMIT),
        name="post_mix",
    )(u, attn, proj, proj, x, wos, woa, wout, gpm, gpf, gate1, scale2, shift2, wr, br, counts)


def _moe_kernel(n_tokens, be_ref, na_ref, idx_hbm, h_hbm, wu_ref, bu_ref, wd_ref, bd_ref, ys_hbm,
                idx_smem, xbuf, obuf, wu_bf, wd_bf, sem_idx, sem_in, sem_out):
    i = pl.program_id(0)
    na = na_ref[0]
    slot = lax.rem(i, 2)
    other = 1 - slot

    def idx_copy(block, s):
        return pltpu.make_async_copy(idx_hbm.at[block], idx_smem.at[s], sem_idx.at[s])

    block_rows = MOE_ROWS * TILE_ROWS

    def tile(ref, first_row):
        return ref.at[pl.ds(pl.multiple_of(first_row, TILE_ROWS), TILE_ROWS)]

    def start_gather(s):
        for r in range(MOE_ROWS):
            pltpu.make_async_copy(tile(h_hbm, idx_smem[s, r]), xbuf.at[s, pl.ds(r * TILE_ROWS, TILE_ROWS)],
                                  sem_in.at[s]).start(priority=r % 2)

    def wait_gather(s):
        pltpu.make_async_copy(h_hbm.at[pl.ds(0, block_rows)], xbuf.at[s], sem_in.at[s]).wait()

    def start_scatter(s):
        for r in range(MOE_ROWS):
            pltpu.make_async_copy(obuf.at[s, pl.ds(r * TILE_ROWS, TILE_ROWS)],
                                  tile(ys_hbm, idx_smem[s, MOE_ROWS + r]), sem_out.at[s]).start(priority=r % 2)

    def wait_scatter(s):
        pltpu.make_async_copy(obuf.at[s], ys_hbm.at[pl.ds(0, block_rows)], sem_out.at[s]).wait()

    @pl.when(i == 0)
    def _():
        obuf[0] = jnp.zeros((block_rows, LANES), F32)
        plane_rows = ys_hbm.shape[0] // (TOP_K * TILE_ROWS)
        spare = [pltpu.make_async_copy(
            obuf.at[0], ys_hbm.at[pl.ds((k * plane_rows + n_tokens + hf * MOE_ROWS) * TILE_ROWS, block_rows)],
            sem_out.at[0]) for k in range(TOP_K) for hf in range(2)]
        for cp in spare:
            cp.start()
        for cp in spare:
            cp.wait()
        first = idx_copy(0, 0)
        first.start()
        first.wait()
        start_gather(0)

        @pl.when(na > 1)
        def _():
            idx_copy(1, 1).start()

    @pl.when(i < na)
    def _():
        wait_gather(slot)

        @pl.when(i + 1 < na)
        def _():
            idx_copy(i + 1, other).wait()
            start_gather(other)

        @pl.when(i >= 2)
        def _():
            wait_scatter(slot)

        @pl.when(jnp.logical_or(i == 0, be_ref[i] != be_ref[jnp.maximum(i - 1, 0)]))
        def _():
            wu_bf[...] = wu_ref[0].astype(BF16)
            wd_bf[...] = wd_ref[0].astype(BF16)

        x = _load_token_tiles(xbuf.at[slot], MOE_ROWS).astype(BF16)
        up = _dot(x, wu_bf[...]) + bu_ref[0]
        glu = jnp.minimum(up[:, :D_FF], SWIGLU_LIMIT)
        lin = jnp.clip(up[:, D_FF:], -SWIGLU_LIMIT, SWIGLU_LIMIT)
        act = glu * jax.nn.sigmoid(SWIGLU_ALPHA * glu) * (lin + 1.0)
        _store_token_tiles(obuf.at[slot], _dot(act.astype(BF16), wd_bf[...]) + bd_ref[0])
        start_scatter(slot)

        @pl.when(i + 2 < na)
        def _():
            idx_copy(i + 2, slot).start()

        @pl.when(i == na - 1)
        def _():
            wait_scatter(slot)

            @pl.when(i >= 1)
            def _():
                wait_scatter(other)


def _moe(idx, h_tiles, ys_rows, block_expert, n_active, wu, bu, wd, bd):
    n_blocks = idx.shape[0]
    grid_spec = pltpu.PrefetchScalarGridSpec(
        num_scalar_prefetch=2,
        grid=(n_blocks,),
        in_specs=[
            pl.BlockSpec(memory_space=pl.ANY),
            pl.BlockSpec(memory_space=pl.ANY),
            pl.BlockSpec((1, D_MODEL, 2 * D_FF), lambda i, be, na: (be[i], 0, 0)),
            pl.BlockSpec((1, 1, 2 * D_FF), lambda i, be, na: (be[i], 0, 0)),
            pl.BlockSpec((1, D_FF, D_MODEL), lambda i, be, na: (be[i], 0, 0)),
            pl.BlockSpec((1, 1, D_MODEL), lambda i, be, na: (be[i], 0, 0)),
        ],
        out_specs=pl.BlockSpec(memory_space=pl.ANY),
        scratch_shapes=[
            pltpu.SMEM((2, 2 * MOE_ROWS), jnp.int32),
            pltpu.VMEM((2, MOE_ROWS * TILE_ROWS, LANES), F32),
            pltpu.VMEM((2, MOE_ROWS * TILE_ROWS, LANES), F32),
            pltpu.VMEM((D_MODEL, 2 * D_FF), BF16),
            pltpu.VMEM((D_FF, D_MODEL), BF16),
            pltpu.SemaphoreType.DMA((2,)),
            pltpu.SemaphoreType.DMA((2,)),
            pltpu.SemaphoreType.DMA((2,)),
        ],
    )
    return pl.pallas_call(
        functools.partial(_moe_kernel, h_tiles.shape[0] // TILE_ROWS),
        grid_spec=grid_spec,
        out_shape=jax.ShapeDtypeStruct((ys_rows * TILE_ROWS, LANES), F32),
        compiler_params=pltpu.CompilerParams(
            dimension_semantics=("arbitrary",), vmem_limit_bytes=MOE_VMEM_LIMIT),
        name="moe",
    )(block_expert, n_active, idx, h_tiles, wu, bu, wd, bd)


def _combine_kernel(ys_ref, tg_ref, x1_ref, gpost_ref, g2_ref, o_ref):
    gates = tg_ref[0]
    tm = gates.shape[0]
    f = gates[:, 0:1] * _load_token_tiles(ys_ref.at[0], tm)
    for k in range(1, TOP_K):
        f = f + gates[:, k:k + 1] * _load_token_tiles(ys_ref.at[k], tm)
    o_ref[0] = x1_ref[0] + g2_ref[0] * (_rms(f) * gpost_ref[...])


def _combine(ys, row0, tg, x1, gpost, gate2, *, tm, per_row):
    nb, rows, _ = x1.shape
    per_b = rows // tm
    base = row0 // tm
    return pl.pallas_call(
        _combine_kernel,
        grid=(nb, per_b),
        in_specs=[
            pl.BlockSpec((TOP_K, tm * TILE_ROWS, LANES), lambda b, i: (0, base + b * per_b + i, 0)),
            pl.BlockSpec((1, tm, LANES), lambda b, i: (b, i, 0)),
            pl.BlockSpec((1, tm, D_MODEL), lambda b, i: (b, i, 0)),
            pl.BlockSpec((1, D_MODEL), lambda b, i: (0, 0)),
            _mod_spec(per_row, tm),
        ],
        out_specs=pl.BlockSpec((1, tm, D_MODEL), lambda b, i: (b, i, 0)),
        out_shape=jax.ShapeDtypeStruct((nb, rows, D_MODEL), F32),
        compiler_params=pltpu.CompilerParams(
            dimension_semantics=("parallel", "parallel"), vmem_limit_bytes=VMEM_LIMIT),
        name="combine",
    )(ys, tg, x1, gpost, gate2)


def _route(top_idx, counts, plane_rows):
    n_tokens = top_idx.shape[0]
    n_slots = n_tokens * TOP_K
    experts = jnp.arange(N_EXPERTS, dtype=jnp.int32)
    padded = (counts + MOE_ROWS - 1) // MOE_ROWS * MOE_ROWS
    group_start = jnp.cumsum(counts) - counts
    padded_end = jnp.cumsum(padded)
    padded_start = padded_end - padded
    n_blocks = -(-n_slots // MOE_ROWS) + N_EXPERTS
    n_active = (padded_end[-1] // MOE_ROWS).astype(jnp.int32)
    block_start = jnp.arange(n_blocks, dtype=jnp.int32) * MOE_ROWS
    block_expert = jnp.sum(block_start[:, None] >= padded_end[None, :], axis=1)
    last_expert = jnp.max(jnp.where(counts > 0, experts, 0))
    block_expert = jnp.where(block_start < padded_end[-1], block_expert, last_expert).astype(jnp.int32)
    slot_id = jnp.arange(n_slots, dtype=jnp.int32).reshape(n_tokens, TOP_K)
    keys = jnp.sort((top_idx * n_slots + slot_id).reshape(-1))
    row = jnp.arange(n_blocks * MOE_ROWS, dtype=jnp.int32).reshape(n_blocks, MOE_ROWS)
    src = row + (group_start - padded_start)[block_expert][:, None]
    real = row < (padded_start + counts)[block_expert][:, None]
    slot = keys[jnp.clip(src, 0, n_slots - 1).reshape(-1)].reshape(n_blocks, MOE_ROWS) % n_slots
    token = slot // TOP_K
    spare = n_tokens + (jnp.arange(n_blocks, dtype=jnp.int32) % 2)[:, None] * MOE_ROWS + row % MOE_ROWS
    dest = jnp.where(real, (slot % TOP_K) * plane_rows + token, spare)
    idx = (jnp.concatenate([token, dest], axis=1) * TILE_ROWS).astype(jnp.int32)
    return idx, block_expert, n_active.reshape(1)


def _pad_lanes(v, value=0.0):
    return jnp.pad(v, [(0, 0)] * (v.ndim - 1) + [(0, LANES - v.shape[-1])], constant_values=value)


def kernel(x_prompt, x_sample, c_prompt, c_sample, cache_swa_k, cache_swa_v, state_conv, state_ssm, w_ada, b_ada, g_pre_mix, g_post_mix, g_pre_ffn, g_post_ffn, w_in, conv_w, conv_b, dt_bias, a_log, d_skip, g_ssm_norm, sinks, w_o_ssd, w_o_attn, w_out, w_router, b_router, w_up, b_up, w_down, b_down):
    depth = w_ada.shape[0]
    n_prompt, seq, _ = x_prompt.shape
    n_sample, dec_seq, _ = x_sample.shape
    yp = x_prompt
    ys_pad = jnp.pad(x_sample, ((0, 0), (0, SAMPLE_ROWS - dec_seq), (0, 0)))
    outs = [[] for _ in range(8)]
    expand = jnp.repeat(jnp.eye(LANES, N_SSD_HEADS, dtype=BF16), SSD_HEAD_DIM, axis=1)
    rows_s = n_sample * SAMPLE_ROWS
    n_p = n_prompt * seq
    n_s = n_sample * dec_seq
    n_tok = n_p + n_s
    tm_p = min(512, seq)
    tm_s = min(512, rows_s)
    tm_mix_p = min(512, seq)
    tm_mix_s = min(256, rows_s)
    assert seq % CHUNK == 0 and seq % tm_p == 0 and rows_s % tm_s == 0 and n_p % n_s == 0
    c_all = jnp.concatenate([c_prompt, c_sample], axis=0)
    c_rows = -(-c_all.shape[0] // SUBLANES) * SUBLANES
    c_all = jnp.pad(c_all, ((0, c_rows - c_all.shape[0]), (0, 0)))

    for l in range(depth):
        wi = w_in[l]
        o_xbc = D_INNER
        o_dt = o_xbc + CONV_DIM
        o_q = o_dt + N_SSD_HEADS
        o_k = o_q + D_MODEL
        o_v = o_k + KV_DIM
        o_gs = o_v + KV_DIM
        o_ga = o_gs + D_MODEL
        head_order = jnp.array(ATTN_HEAD_ORDER)
        w_q = wi[:, o_q:o_k].reshape(D_MODEL, N_ATTN_HEADS, ATTN_HEAD_DIM)[:, head_order].reshape(D_MODEL, D_MODEL)
        w_proj = jnp.concatenate([
            wi[:, :o_xbc], wi[:, o_xbc:o_xbc + D_INNER], w_q, wi[:, o_gs:o_ga], wi[:, o_ga:],
            wi[:, o_xbc + D_INNER:o_dt], wi[:, o_k:o_v], wi[:, o_v:o_gs], _pad_lanes(wi[:, o_dt:o_q])],
            axis=1).astype(BF16)
        cwx, cwbc = conv_w[l][:, :D_INNER], conv_w[l][:, D_INNER:]
        cbx, cbbc = conv_b[l][None, :D_INNER], conv_b[l][None, D_INNER:]
        dtb = _pad_lanes(dt_bias[l][None])
        alog = _pad_lanes(a_log[l][None])
        dskip = jnp.repeat(d_skip[l], SSD_HEAD_DIM)[None]
        gn = g_ssm_norm[l][None]
        wos, wout = w_o_ssd[l].astype(BF16), w_out[l].astype(BF16)
        woa = w_o_attn[l].reshape(N_ATTN_HEADS, ATTN_HEAD_DIM, D_MODEL)[head_order].reshape(D_MODEL, D_MODEL)
        woa = woa.astype(BF16)
        zero_counts = jnp.zeros((1, LANES), F32)
        wr = _pad_lanes(w_router[l])
        br = _pad_lanes(b_router[l][None], NEG_BIG)
        wu, wd = w_up[l], w_down[l]
        bu, bd = b_up[l][:, None, :], b_down[l][:, None, :]
        g_pm, g_pom, g_pf, g_pof = (v[l][None] for v in (g_pre_mix, g_post_mix, g_pre_ffn, g_post_ffn))

        ada = _ada(c_all, w_ada[l], b_ada[l][None])
        ada_p = [ada[:n_prompt, k * D_MODEL:(k + 1) * D_MODEL] for k in range(6)]
        ada_s = [ada[n_prompt:n_prompt + n_sample, k * D_MODEL:(k + 1) * D_MODEL] for k in range(6)]

        mods = [m[:, None, :] for m in ada_p]
        proj = _in_proj(yp, g_pm, mods[1], mods[0], w_proj, tm=min(1024, seq), per_row=False)
        u, ssm_p = _ssd(proj, cwx, cbx, cwbc, cbbc, dtb, alog, dskip, gn)
        attn = _attention(proj, sinks[l], WINDOW)
        x1_p, h2_p, ti_p, tg_p, cnt_p = _post_mix(
            u, attn, proj, yp, wos, woa, wout, g_pom, g_pf, mods[2], mods[4], mods[3], wr, br, zero_counts,
            tm=tm_mix_p, per_row=False, valid_rows=SAMPLE_ROWS)
        gate2_p = mods[5]
        kp = proj[:, seq - WINDOW:, COL_K:COL_K + KV_DIM].reshape(n_prompt, WINDOW, N_KV_HEADS, ATTN_HEAD_DIM)
        vp = proj[:, seq - WINDOW:, COL_V:COL_V + KV_DIM].reshape(n_prompt, WINDOW, N_KV_HEADS, ATTN_HEAD_DIM)
        cp = jnp.concatenate([proj[:, seq - (CONV_W - 1):, COL_X:COL_X + D_INNER],
                              proj[:, seq - (CONV_W - 1):, COL_BC:COL_BC + BC_DIM]], axis=-1)

        mods_s = [jnp.repeat(m, SAMPLE_ROWS, axis=0)[None] for m in ada_s]
        xs_flat = ys_pad.reshape(1, rows_s, D_MODEL)
        proj_s = _in_proj(xs_flat, g_pm, mods_s[1], mods_s[0], w_proj, tm=tm_s, per_row=True)
        proj_sb = proj_s.reshape(n_sample, SAMPLE_ROWS, PROJ_DIM)
        u_s, ssm_s = _ssd_step(
            proj_sb, state_conv[l][:, :, :D_INNER], state_conv[l][:, :, D_INNER:],
            state_ssm[l].reshape(n_sample, D_INNER, D_STATE),
            cwx, cbx, cwbc, cbbc, dtb, alog, dskip, gn, expand, dec_seq)
        k_prev = cache_swa_k[l].reshape(n_sample, -1, KV_DIM)
        v_prev = cache_swa_v[l].reshape(n_sample, -1, KV_DIM)
        attn_s = _attention(proj_sb, sinks[l], SAMPLE_ROWS, prev_kv=(k_prev, v_prev),
                            seqs=math.gcd(n_sample, 8))
        x1_s, h2_s, ti_s, tg_s, cnt_all = _post_mix(
            u_s.reshape(1, rows_s, D_INNER), attn_s.reshape(1, rows_s, D_MODEL), proj_s, xs_flat,
            wos, woa, wout, g_pom, g_pf, mods_s[2], mods_s[4], mods_s[3], wr, br, cnt_p,
            tm=tm_mix_s, per_row=True, valid_rows=dec_seq)
        wb = k_prev.shape[1]
        k_new = proj_sb[:, :dec_seq, COL_K:COL_K + KV_DIM]
        v_new = proj_sb[:, :dec_seq, COL_V:COL_V + KV_DIM]
        ks = jnp.concatenate([k_prev, k_new], axis=1)[:, -wb:].reshape(n_sample, wb, N_KV_HEADS, ATTN_HEAD_DIM)
        vs = jnp.concatenate([v_prev, v_new], axis=1)[:, -wb:].reshape(n_sample, wb, N_KV_HEADS, ATTN_HEAD_DIM)
        raw_xbc = jnp.concatenate([proj_sb[:, :dec_seq, COL_X:COL_X + D_INNER],
                                   proj_sb[:, :dec_seq, COL_BC:COL_BC + BC_DIM]], axis=-1)
        cs = jnp.concatenate([state_conv[l], raw_xbc], axis=1)[:, -(CONV_W - 1):]

        valid = lambda v: v.reshape(n_sample, SAMPLE_ROWS, -1)[:, :dec_seq].reshape(n_s, -1)
        h2_sv = h2_s.reshape(n_sample, SAMPLE_ROWS * TILE_ROWS, LANES)[:, :dec_seq * TILE_ROWS]
        h2_all = jnp.concatenate([h2_p.reshape(n_p * TILE_ROWS, LANES),
                                  h2_sv.reshape(n_s * TILE_ROWS, LANES)], axis=0)
        ti_all = jnp.concatenate([ti_p.reshape(n_p, LANES), valid(ti_s)], axis=0)
        counts = cnt_all[0, :N_EXPERTS].astype(jnp.int32)
        plane_rows = n_tok + 2 * MOE_ROWS
        idx, block_expert, n_active = _route(ti_all[:, :TOP_K], counts, plane_rows)
        ysel = _moe(idx, h2_all, TOP_K * plane_rows, block_expert, n_active, wu, bu, wd, bd)
        ysel = ysel.reshape(TOP_K, plane_rows * TILE_ROWS, LANES)

        yp = _combine(ysel, 0, tg_p, x1_p, g_pof, gate2_p, tm=tm_p, per_row=False)
        x1_sv = valid(x1_s)[None]
        tg_sv = valid(tg_s)[None]
        gate2_s = jnp.repeat(ada_s[5], dec_seq, axis=0)[None]
        ys_new = _combine(ysel, n_p, tg_sv, x1_sv, g_pof, gate2_s, tm=n_s, per_row=True)
        ys_new = ys_new.reshape(n_sample, dec_seq, D_MODEL)
        ys_pad = jnp.pad(ys_new, ((0, 0), (0, SAMPLE_ROWS - dec_seq), (0, 0)))

        for lst, v in zip(outs, (kp, vp, cp, ssm_p, ks, vs, cs,
                                 ssm_s.reshape(n_sample, N_SSD_HEADS, SSD_HEAD_DIM, D_STATE))):
            lst.append(v)

    return (yp, ys_pad[:, :dec_seq], *[jnp.stack(v) for v in outs])
```

```python
import functools
import math

import jax
import jax.numpy as jnp
from jax import lax
from jax.experimental import pallas as pl
from jax.experimental.pallas import tpu as pltpu

F32 = jnp.float32
BF16 = jnp.bfloat16

D_MODEL = 1024
D_INNER = 2 * D_MODEL
SSD_HEAD_DIM = 64
N_SSD_HEADS = D_INNER // SSD_HEAD_DIM
N_SSD_GROUPS = 4
HEADS_PER_GROUP = N_SSD_HEADS // N_SSD_GROUPS
D_STATE = 128
CONV_W = 4
BC_DIM = 2 * N_SSD_GROUPS * D_STATE
CONV_DIM = D_INNER + BC_DIM
CHUNK = 128
ATTN_HEAD_DIM = 64
N_ATTN_HEADS = D_MODEL // ATTN_HEAD_DIM
N_KV_HEADS = 4
KV_REP = N_ATTN_HEADS // N_KV_HEADS
KV_DIM = N_KV_HEADS * ATTN_HEAD_DIM
WINDOW = 128
N_EXPERTS = 32
TOP_K = 4
D_FF = D_MODEL
SWIGLU_LIMIT = 7.0
SWIGLU_ALPHA = 1.702
EPS = 1e-6

LANES = 128
SUBLANES = 8
NEG_BIG = -1e30

COL_Z = 0
COL_X = COL_Z + D_INNER
COL_Q = COL_X + D_INNER
COL_GS = COL_Q + D_MODEL
COL_GA = COL_GS + D_MODEL
COL_BC = COL_GA + D_MODEL
COL_K = COL_BC + BC_DIM
COL_V = COL_K + KV_DIM
COL_DT = COL_V + KV_DIM
PROJ_DIM = COL_DT + LANES
PROJ_TILE_N = PROJ_DIM // 3

ATTN_HEAD_ORDER = tuple(2 * KV_REP * (j // KV_REP) + (j % KV_REP) + KV_REP * hf
                        for j in range(N_ATTN_HEADS // 2) for hf in (0, 1))

TILE_ROWS = D_MODEL // LANES
assert TILE_ROWS == SUBLANES

SAMPLE_ROWS = 8
MOE_ROWS = 512
VMEM_LIMIT = 48 * 1024 * 1024
MOE_VMEM_LIMIT = 56 * 1024 * 1024


def _silu(v):
    return v * jax.nn.sigmoid(v)


def _softplus(v):
    return jnp.maximum(v, 0.0) + jnp.log(1.0 + jnp.exp(-jnp.abs(v)))


def _rms(v):
    return v * lax.rsqrt(jnp.mean(v * v, axis=-1, keepdims=True) + EPS)


def _split3(v):
    hi = v.astype(BF16)
    r1 = v - hi.astype(F32)
    mid = r1.astype(BF16)
    lo = (r1 - mid.astype(F32)).astype(BF16)
    return hi, mid, lo


def _store_token_tiles(ref, rows):
    n = rows.shape[0]
    for j in range(TILE_ROWS):
        ref[pl.ds(j, n, stride=TILE_ROWS), :] = rows[:, j * LANES:(j + 1) * LANES]


def _load_token_tiles(ref, n):
    return jnp.concatenate([ref[pl.ds(j, n, stride=TILE_ROWS), :] for j in range(TILE_ROWS)], axis=1)


def _dot(a, b):
    return jnp.dot(a, b, preferred_element_type=F32)


def _dot_nt(a, b):
    return lax.dot_general(a, b, (((1,), (1,)), ((), ())), preferred_element_type=F32)


def _dot_tn(a, b):
    return lax.dot_general(a, b, (((0,), (0,)), ((), ())), preferred_element_type=F32)


def _dot_exact_rhs(a, b_bf16):
    hi, mid, lo = _split3(a)
    return _dot(hi, b_bf16) + _dot(mid, b_bf16) + _dot(lo, b_bf16)


def _ada_kernel(c_ref, w_ref, b_ref, o_ref):
    o_ref[...] = _dot(_silu(c_ref[...]).astype(BF16), w_ref[...].astype(BF16)) + b_ref[...]


def _ada(c, w, b):
    rows, n = c.shape[0], w.shape[1]
    tn = n // 4
    return pl.pallas_call(
        _ada_kernel,
        grid=(n // tn,),
        in_specs=[
            pl.BlockSpec((rows, D_MODEL), lambda j: (0, 0)),
            pl.BlockSpec((D_MODEL, tn), lambda j: (0, j)),
            pl.BlockSpec((1, tn), lambda j: (0, j)),
        ],
        out_specs=pl.BlockSpec((rows, tn), lambda j: (0, j)),
        out_shape=jax.ShapeDtypeStruct((rows, n), F32),
        compiler_params=pltpu.CompilerParams(
            dimension_semantics=("parallel",), vmem_limit_bytes=VMEM_LIMIT),
        name="ada",
    )(c, w, b)


def _in_proj_kernel(x_ref, g_ref, sc_ref, sh_ref, w_ref, o_ref, h_scr):
    @pl.when(pl.program_id(2) == 0)
    def _():
        h = _rms(x_ref[0]) * g_ref[...] * (1.0 + sc_ref[0]) + sh_ref[0]
        h_scr[...] = h.astype(BF16)

    o_ref[0] = _dot(h_scr[...], w_ref[...])


def _mod_spec(per_row, tm):
    if per_row:
        return pl.BlockSpec((1, tm, D_MODEL), lambda b, i, *_: (b, i, 0))
    return pl.BlockSpec((1, 1, D_MODEL), lambda b, i, *_: (b, 0, 0))


def _in_proj(x, g, scale, shift, w, *, tm, per_row):
    nb, rows, _ = x.shape
    return pl.pallas_call(
        _in_proj_kernel,
        grid=(nb, rows // tm, PROJ_DIM // PROJ_TILE_N),
        in_specs=[
            pl.BlockSpec((1, tm, D_MODEL), lambda b, i, j: (b, i, 0)),
            pl.BlockSpec((1, D_MODEL), lambda b, i, j: (0, 0)),
            _mod_spec(per_row, tm),
            _mod_spec(per_row, tm),
            pl.BlockSpec((D_MODEL, PROJ_TILE_N), lambda b, i, j: (0, j)),
        ],
        out_specs=pl.BlockSpec((1, tm, PROJ_TILE_N), lambda b, i, j: (b, i, j)),
        out_shape=jax.ShapeDtypeStruct((nb, rows, PROJ_DIM), F32),
        scratch_shapes=[pltpu.VMEM((tm, D_MODEL), BF16)],
        compiler_params=pltpu.CompilerParams(
            dimension_semantics=("parallel", "parallel", "arbitrary"),
            vmem_limit_bytes=MOE_VMEM_LIMIT),
        name="in_proj",
    )(x, g, scale, shift, w)


def _causal_conv(ext_ref, raw, prev, w_ref, b_ref, rows):
    base = SUBLANES
    if prev is not None:
        ext_ref[pl.ds(base - (CONV_W - 1), CONV_W - 1), :] = prev
    ext_ref[pl.ds(base, rows), :] = raw
    out = b_ref[...] + raw * w_ref[pl.ds(CONV_W - 1, 1), :]
    for k in range(CONV_W - 1):
        out = out + ext_ref[pl.ds(base - (CONV_W - 1) + k, rows), :] * w_ref[pl.ds(k, 1), :]
    return out


def _gated_group_norm(y, z, g_ref):
    u = y * _silu(z)
    gw = D_INNER // N_SSD_GROUPS
    parts = []
    for g in range(N_SSD_GROUPS):
        ug = u[:, g * gw:(g + 1) * gw]
        parts.append(ug * lax.rsqrt(jnp.mean(ug * ug, axis=-1, keepdims=True) + EPS))
    return jnp.concatenate(parts, axis=-1) * g_ref[...]


def _ssd_kernel(z_ref, x_ref, bc_ref, dt_ref, cwx_ref, cbx_ref, cwbc_ref, cbbc_ref, dtb_ref, alog_ref,
                dskip_ref, gn_ref, u_ref, fin_ref, extx_scr, extbc_scr, st_scr, y_scr):
    q = CHUNK
    c = pl.program_id(1)

    @pl.when(c == 0)
    def _():
        extx_scr[pl.ds(0, SUBLANES), :] = jnp.zeros((SUBLANES, D_INNER), F32)
        extbc_scr[pl.ds(0, SUBLANES), :] = jnp.zeros((SUBLANES, BC_DIM), F32)
        st_scr[...] = jnp.zeros_like(st_scr)

    xs = _silu(_causal_conv(extx_scr, x_ref[0], None, cwx_ref, cbx_ref, q))
    bc = _silu(_causal_conv(extbc_scr, bc_ref[0], None, cwbc_ref, cbbc_ref, q))
    tail = SUBLANES + q - (CONV_W - 1)
    extx_scr[pl.ds(SUBLANES - (CONV_W - 1), CONV_W - 1), :] = extx_scr[pl.ds(tail, CONV_W - 1), :]
    extbc_scr[pl.ds(SUBLANES - (CONV_W - 1), CONV_W - 1), :] = extbc_scr[pl.ds(tail, CONV_W - 1), :]

    dt = _softplus(dt_ref[0] + dtb_ref[...])
    adt = dt * (-jnp.exp(alog_ref[...]))
    row = lax.broadcasted_iota(jnp.int32, (q, q), 0)
    col = lax.broadcasted_iota(jnp.int32, (q, q), 1)
    causal = row >= col
    tri = jnp.where(causal, 1.0, 0.0).astype(BF16)
    h3, m3, l3 = _split3(adt)
    acs = _dot(tri, h3) + _dot(tri, m3) + _dot(tri, l3)
    eacs = jnp.exp(acs)
    acs_t = acs.T
    dt_t = dt.T
    lane = lax.broadcasted_iota(jnp.int32, (1, LANES), 1)
    lo_half = lane < SSD_HEAD_DIM

    for g in range(N_SSD_GROUPS):
        bm = bc[:, g * D_STATE:(g + 1) * D_STATE]
        cm = bc[:, (N_SSD_GROUPS + g) * D_STATE:(N_SSD_GROUPS + g + 1) * D_STATE]
        cb = jnp.where(causal, _dot_nt(cm.astype(BF16), bm.astype(BF16)), 0.0)
        bm_t = bm.T
        for pr in range(HEADS_PER_GROUP // 2):
            h0 = g * HEADS_PER_GROUP + 2 * pr
            lanes = pl.ds(h0 * SSD_HEAD_DIM, LANES)
            x_pair = xs[:, h0 * SSD_HEAD_DIM:h0 * SSD_HEAD_DIM + LANES]
            st_pair = st_scr[:, lanes]
            lhs_y, lhs_s, decs = [], [], []
            for h in (h0, h0 + 1):
                a_col = acs[:, h:h + 1]
                a_row = acs_t[h:h + 1, :]
                dt_row = dt_t[h:h + 1, :]
                a_last = acs_t[h:h + 1, q - 1:q]
                decay = jnp.exp(jnp.minimum(a_col - a_row, 0.0))
                lhs_y.append((cb * decay * dt_row).astype(BF16))
                lhs_s.append((bm_t * (dt_row * jnp.exp(a_last - a_row))).astype(BF16))
                decs.append(jnp.exp(a_last))
            for h in (h0, h0 + 1):
                lhs_y.append((cm * eacs[:, h:h + 1]).astype(BF16))
            x_top = jnp.where(lo_half, x_pair, 0.0).astype(BF16)
            x_bot = jnp.where(lo_half, 0.0, x_pair).astype(BF16)
            s_top = jnp.where(lo_half, st_pair, 0.0).astype(BF16)
            s_bot = jnp.where(lo_half, 0.0, st_pair).astype(BF16)
            x_bd = jnp.concatenate([x_top, x_bot], axis=0)
            rhs_y = jnp.concatenate([x_bd, s_top, s_bot], axis=0)
            y_pair = _dot(jnp.concatenate(lhs_y, axis=1), rhs_y)
            ds_pair = _dot(jnp.concatenate(lhs_s, axis=1), x_bd)
            dskip = dskip_ref[:, lanes]
            y_scr[:, lanes] = y_pair + dskip * x_pair
            st_scr[:, lanes] = st_pair * jnp.where(lo_half, decs[0], decs[1]) + ds_pair

    u_ref[0] = _gated_group_norm(y_scr[...], z_ref[0], gn_ref).astype(u_ref.dtype)

    @pl.when(c == pl.num_programs(1) - 1)
    def _():
        for pr in range(N_SSD_HEADS // 2):
            t = st_scr[:, pl.ds(pr * LANES, LANES)].T
            fin_ref[0, 2 * pr] = t[:SSD_HEAD_DIM]
            fin_ref[0, 2 * pr + 1] = t[SSD_HEAD_DIM:]


def _ssd(proj, cwx, cbx, cwbc, cbbc, dtb, alog, dskip, gn):
    nb, rows, _ = proj.shape
    q = CHUNK
    full = lambda shape: pl.BlockSpec(shape, lambda b, c: (0,) * len(shape))
    return pl.pallas_call(
        _ssd_kernel,
        grid=(nb, rows // q),
        in_specs=[
            pl.BlockSpec((1, q, D_INNER), lambda b, c: (b, c, COL_Z // D_INNER)),
            pl.BlockSpec((1, q, D_INNER), lambda b, c: (b, c, COL_X // D_INNER)),
            pl.BlockSpec((1, q, BC_DIM), lambda b, c: (b, c, COL_BC // BC_DIM)),
            pl.BlockSpec((1, q, LANES), lambda b, c: (b, c, COL_DT // LANES)),
            full((CONV_W, D_INNER)), full((1, D_INNER)), full((CONV_W, BC_DIM)), full((1, BC_DIM)),
            full((1, LANES)), full((1, LANES)), full((1, D_INNER)), full((1, D_INNER)),
        ],
        out_specs=[
            pl.BlockSpec((1, q, D_INNER), lambda b, c: (b, c, 0)),
            pl.BlockSpec((1, N_SSD_HEADS, SSD_HEAD_DIM, D_STATE), lambda b, c: (b, 0, 0, 0)),
        ],
        out_shape=[
            jax.ShapeDtypeStruct((nb, rows, D_INNER), BF16),
            jax.ShapeDtypeStruct((nb, N_SSD_HEADS, SSD_HEAD_DIM, D_STATE), F32),
        ],
        scratch_shapes=[
            pltpu.VMEM((SUBLANES + q, D_INNER), F32),
            pltpu.VMEM((SUBLANES + q, BC_DIM), F32),
            pltpu.VMEM((D_STATE, D_INNER), F32),
            pltpu.VMEM((q, D_INNER), F32),
        ],
        compiler_params=pltpu.CompilerParams(
            dimension_semantics=("parallel", "arbitrary"), vmem_limit_bytes=VMEM_LIMIT),
        name="ssd",
    )(proj, proj, proj, proj, cwx, cbx, cwbc, cbbc, dtb, alog, dskip, gn)


def _ssd_step_kernel(n_valid, z_ref, x_ref, bc_ref, dt_ref, px_ref, pbc_ref, st_ref, cwx_ref, cbx_ref,
                     cwbc_ref, cbbc_ref, dtb_ref, alog_ref, dskip_ref, gn_ref, exp_ref,
                     u_ref, nst_ref, extx_scr, extbc_scr):
    for sq in range(z_ref.shape[0]):
        _ssd_step_one(n_valid, z_ref.at[sq], x_ref.at[sq], bc_ref.at[sq], dt_ref.at[sq], px_ref.at[sq],
                      pbc_ref.at[sq], st_ref.at[sq], cwx_ref, cbx_ref, cwbc_ref, cbbc_ref, dtb_ref, alog_ref,
                      dskip_ref, gn_ref, exp_ref, u_ref.at[sq], nst_ref.at[sq], extx_scr.at[sq], extbc_scr.at[sq])


def _ssd_step_one(n_valid, z_ref, x_ref, bc_ref, dt_ref, px_ref, pbc_ref, st_ref, cwx_ref, cbx_ref,
                  cwbc_ref, cbbc_ref, dtb_ref, alog_ref, dskip_ref, gn_ref, exp_ref,
                  u_ref, nst_ref, extx_scr, extbc_scr):
    q = SAMPLE_ROWS
    gw = D_INNER // N_SSD_GROUPS
    xs = _silu(_causal_conv(extx_scr, x_ref[...], px_ref[...], cwx_ref, cbx_ref, q))
    bc = _silu(_causal_conv(extbc_scr, bc_ref[...], pbc_ref[...], cwbc_ref, cbbc_ref, q))

    rowi = lax.broadcasted_iota(jnp.int32, (q, 1), 0)
    dt = jnp.where(rowi < n_valid, _softplus(dt_ref[...] + dtb_ref[...]), 0.0)
    adt = dt * (-jnp.exp(alog_ref[...]))
    acs = jnp.zeros_like(adt)
    for s in range(n_valid):
        acs = acs + jnp.where(rowi >= s, adt[s:s + 1, :], 0.0)
    expand = exp_ref[...]
    dt_e = _dot_exact_rhs(dt, expand)
    acs_e = _dot_exact_rhs(acs, expand)
    last_e = acs_e[q - 1:q, :]

    x_end = xs * dt_e * jnp.exp(last_e - acs_e)
    dec3 = _split3(jnp.exp(last_e))
    zrow = jnp.zeros((q - 3, D_INNER), BF16)
    dec_rows = jnp.concatenate([dec3[0], dec3[1], dec3[2], zrow], axis=0)
    ones = jnp.ones((q, D_STATE), BF16)

    y_off, cbs = [], []
    for g in range(N_SSD_GROUPS):
        bm = bc[:, g * D_STATE:(g + 1) * D_STATE].astype(BF16)
        cm = bc[:, (N_SSD_GROUPS + g) * D_STATE:(N_SSD_GROUPS + g + 1) * D_STATE].astype(BF16)
        rows = pl.ds(g * gw, gw)
        st = st_ref[rows, :]
        y_off.append(_dot_nt(cm, st.astype(BF16)))
        cbs.append(_dot_nt(cm, bm))
        d_st = _dot_tn(x_end[:, g * gw:(g + 1) * gw].astype(BF16), bm)
        dec = _dot_tn(dec_rows[:, g * gw:(g + 1) * gw], ones)
        nst_ref[rows, :] = st * dec + d_st

    y = jnp.concatenate(y_off, axis=-1) * jnp.exp(acs_e) + dskip_ref[...] * xs
    for s in range(n_valid):
        keep = rowi >= s
        decay = jnp.where(keep, jnp.exp(jnp.where(keep, acs_e - acs_e[s:s + 1, :], 0.0)), 0.0)
        cb_e = jnp.concatenate([jnp.broadcast_to(cb[:, s:s + 1], (q, gw)) for cb in cbs], axis=-1)
        y = y + decay * cb_e * (dt_e[s:s + 1, :] * xs[s:s + 1, :])
    u_ref[...] = _gated_group_norm(y, z_ref[...], gn_ref).astype(u_ref.dtype)


def _ssd_step(proj, prev_x, prev_bc, state, cwx, cbx, cwbc, cbbc, dtb, alog, dskip, gn, expand, n_valid):
    nb = proj.shape[0]
    q = SAMPLE_ROWS
    seqs = math.gcd(nb, 2)
    full = lambda shape: pl.BlockSpec(shape, lambda b: (0,) * len(shape))
    return pl.pallas_call(
        functools.partial(_ssd_step_kernel, n_valid),
        grid=(nb // seqs,),
        in_specs=[
            pl.BlockSpec((seqs, q, D_INNER), lambda b: (b, 0, COL_Z // D_INNER)),
            pl.BlockSpec((seqs, q, D_INNER), lambda b: (b, 0, COL_X // D_INNER)),
            pl.BlockSpec((seqs, q, BC_DIM), lambda b: (b, 0, COL_BC // BC_DIM)),
            pl.BlockSpec((seqs, q, LANES), lambda b: (b, 0, COL_DT // LANES)),
            pl.BlockSpec((seqs, CONV_W - 1, D_INNER), lambda b: (b, 0, 0)),
            pl.BlockSpec((seqs, CONV_W - 1, BC_DIM), lambda b: (b, 0, 0)),
            pl.BlockSpec((seqs, D_INNER, D_STATE), lambda b: (b, 0, 0)),
            full((CONV_W, D_INNER)), full((1, D_INNER)), full((CONV_W, BC_DIM)), full((1, BC_DIM)),
            full((1, LANES)), full((1, LANES)), full((1, D_INNER)), full((1, D_INNER)),
            full((LANES, D_INNER)),
        ],
        out_specs=[
            pl.BlockSpec((seqs, q, D_INNER), lambda b: (b, 0, 0)),
            pl.BlockSpec((seqs, D_INNER, D_STATE), lambda b: (b, 0, 0)),
        ],
        out_shape=[
            jax.ShapeDtypeStruct((nb, q, D_INNER), BF16),
            jax.ShapeDtypeStruct((nb, D_INNER, D_STATE), F32),
        ],
        scratch_shapes=[
            pltpu.VMEM((seqs, 2 * SUBLANES, D_INNER), F32),
            pltpu.VMEM((seqs, 2 * SUBLANES, BC_DIM), F32),
        ],
        compiler_params=pltpu.CompilerParams(
            dimension_semantics=("parallel",), vmem_limit_bytes=VMEM_LIMIT),
        name="ssd_step",
    )(proj, proj, proj, proj, prev_x, prev_bc, state, cwx, cbx, cwbc, cbbc, dtb, alog, dskip, gn, expand)


def _attention_kernel(qb, prev_always_valid, sinks_ref, q_ref, kp_ref, vp_ref, kc_ref, vc_ref, o_ref):
    wb = WINDOW
    lane = lax.broadcasted_iota(jnp.int32, (1, LANES), 1)
    lo_half = lane < ATTN_HEAD_DIM
    keys_major = qb % LANES == 0
    t_ax, s_ax = (1, 0) if keys_major else (0, 1)
    shape_p = (wb, qb) if keys_major else (qb, wb)
    t_p = lax.broadcasted_iota(jnp.int32, shape_p, t_ax)
    s_p = lax.broadcasted_iota(jnp.int32, shape_p, s_ax)
    rel_p = wb + t_p - s_p
    ok_p = rel_p < WINDOW
    if not prev_always_valid:
        ok_p = jnp.logical_and(ok_p, pl.program_id(1) > 0)
    t_c = lax.broadcasted_iota(jnp.int32, (qb, qb), t_ax)
    s_c = lax.broadcasted_iota(jnp.int32, (qb, qb), s_ax)
    rel_c = t_c - s_c
    ok_c = rel_c >= 0
    rel_p = rel_p.astype(F32)
    rel_c = rel_c.astype(F32)
    scale = ATTN_HEAD_DIM ** -0.5

    pairs = KV_REP
    half = pairs * qb

    def one_sequence(sq, m, bias_p, bias_c, sink):
        kv_lanes = pl.ds(m * LANES, LANES)
        kp = kp_ref[sq, :, kv_lanes].astype(BF16)
        kc = kc_ref[sq, :, kv_lanes].astype(BF16)
        vp = vp_ref[sq, :, kv_lanes]
        vc = vc_ref[sq, :, kv_lanes]
        vp_lo, vp_hi = jnp.where(lo_half, vp, 1.0).astype(BF16), jnp.where(lo_half, 1.0, vp).astype(BF16)
        vc_lo, vc_hi = jnp.where(lo_half, vc, 1.0).astype(BF16), jnp.where(lo_half, 1.0, vc).astype(BF16)
        q_lo, q_hi = [], []
        for i in range(pairs):
            q_pair = q_ref[sq, :, pl.ds((pairs * m + i) * LANES, LANES)] * scale
            q_lo.append(jnp.where(lo_half, q_pair, 0.0).astype(BF16))
            q_hi.append(jnp.where(lo_half, 0.0, q_pair).astype(BF16))
        qs = jnp.concatenate(q_lo + q_hi, axis=0)
        s_p = _dot_nt(qs, kp) + bias_p
        s_c = _dot_nt(qs, kc) + bias_c
        if qb == wb:
            mx = jnp.max(jnp.maximum(s_p, s_c), axis=-1, keepdims=True)
        else:
            mx = jnp.maximum(jnp.max(s_p, axis=-1, keepdims=True), jnp.max(s_c, axis=-1, keepdims=True))
        mx = jnp.maximum(mx, sink)
        p_p = jnp.exp(s_p - mx).astype(BF16)
        p_c = jnp.exp(s_c - mx).astype(BF16)
        e_sink = jnp.exp(sink - mx)
        o_lo = _dot(p_p[:half], vp_lo) + _dot(p_c[:half], vc_lo)
        o_hi = _dot(p_p[half:], vp_hi) + _dot(p_c[half:], vc_hi)
        for i in range(pairs):
            a = o_lo[i * qb:(i + 1) * qb]
            b = o_hi[i * qb:(i + 1) * qb]
            num = jnp.where(lo_half, a, b)
            den = (pltpu.roll(jnp.where(lo_half, b, a), ATTN_HEAD_DIM, 1)
                   + jnp.where(lo_half, e_sink[i * qb:(i + 1) * qb], e_sink[half + i * qb:half + (i + 1) * qb]))
            o_ref[sq, :, pl.ds((pairs * m + i) * LANES, LANES)] = (num / den).astype(o_ref.dtype)

    def one_sequence_keys_major(sq, m, bias_p, bias_c, sink):
        kv_lanes = pl.ds(m * LANES, LANES)
        kp = kp_ref[sq, :, kv_lanes].astype(BF16)
        kc = kc_ref[sq, :, kv_lanes].astype(BF16)
        vp = vp_ref[sq, :, kv_lanes].astype(BF16)
        vc = vc_ref[sq, :, kv_lanes].astype(BF16)
        q_lo, q_hi = [], []
        for i in range(pairs):
            q_pair = q_ref[sq, :, pl.ds((pairs * m + i) * LANES, LANES)] * scale
            q_lo.append(jnp.where(lo_half, q_pair, 0.0).astype(BF16))
            q_hi.append(jnp.where(lo_half, 0.0, q_pair).astype(BF16))
        qs = jnp.concatenate(q_lo + q_hi, axis=0)
        s_p = _dot_nt(kp, qs) + bias_p
        s_c = _dot_nt(kc, qs) + bias_c
        mx = jnp.maximum(jnp.maximum(jnp.max(s_p, axis=0, keepdims=True), jnp.max(s_c, axis=0, keepdims=True)),
                         sink)
        p_p = jnp.exp(s_p - mx)
        p_c = jnp.exp(s_c - mx)
        den = jnp.sum(p_p, axis=0, keepdims=True) + jnp.sum(p_c, axis=0, keepdims=True) + jnp.exp(sink - mx)
        o_t = (_dot_tn(vp, p_p.astype(BF16)) + _dot_tn(vc, p_c.astype(BF16))) / den
        row_lo = lax.broadcasted_iota(jnp.int32, (LANES, 1), 0) < ATTN_HEAD_DIM
        for i in range(pairs):
            blk = jnp.where(row_lo, o_t[:, i * qb:(i + 1) * qb], o_t[:, (pairs + i) * qb:(pairs + i + 1) * qb])
            o_ref[sq, :, pl.ds((pairs * m + i) * LANES, LANES)] = blk.T.astype(o_ref.dtype)

    for m in range(KV_DIM // LANES):
        heads = [2 * pairs * m + r for r in range(2 * pairs)]
        slopes = [2.0 ** (-8.0 * (h + 1) / N_ATTN_HEADS) for h in heads]
        bias_p = jnp.concatenate([jnp.where(ok_p, -sl * rel_p, NEG_BIG) for sl in slopes], axis=t_ax)
        bias_c = jnp.concatenate([jnp.where(ok_c, -sl * rel_c, NEG_BIG) for sl in slopes], axis=t_ax)
        one = (1, qb) if keys_major else (qb, 1)
        sink = jnp.concatenate([jnp.full(one, sinks_ref[h], F32) for h in heads], axis=t_ax)
        for sq in range(q_ref.shape[0]):
            (one_sequence_keys_major if keys_major else one_sequence)(sq, m, bias_p, bias_c, sink)


def _attention(proj, sinks, qb, prev_kv=None, seqs=1):
    nb, rows, _ = proj.shape
    wb = WINDOW
    cur = lambda col: pl.BlockSpec((seqs, qb, KV_DIM), lambda b, n: (b, n, col // KV_DIM))
    if prev_kv is None:
        prev = lambda col: pl.BlockSpec((seqs, wb, KV_DIM),
                                        lambda b, n: (b, jnp.maximum(n - 1, 0), col // KV_DIM))
        kp, vp = proj, proj
        prev_specs = [prev(COL_K), prev(COL_V)]
    else:
        kp, vp = prev_kv
        prev_specs = [pl.BlockSpec((seqs, wb, KV_DIM), lambda b, n: (b, 0, 0))] * 2
    return pl.pallas_call(
        functools.partial(_attention_kernel, qb, prev_kv is not None),
        grid=(nb // seqs, rows // qb),
        in_specs=[
            pl.BlockSpec(memory_space=pltpu.SMEM),
            pl.BlockSpec((seqs, qb, D_MODEL), lambda b, n: (b, n, COL_Q // D_MODEL)),
            *prev_specs,
            cur(COL_K), cur(COL_V),
        ],
        out_specs=pl.BlockSpec((seqs, qb, D_MODEL), lambda b, n: (b, n, 0)),
        out_shape=jax.ShapeDtypeStruct((nb, rows, D_MODEL), BF16),
        compiler_params=pltpu.CompilerParams(
            dimension_semantics=("parallel", "arbitrary"), vmem_limit_bytes=VMEM_LIMIT),
        name="attention",
    )(sinks, proj, kp, vp, proj, proj)


def _post_mix_kernel(valid_rows, u_ref, a_ref, gs_ref, ga_ref, x_ref, wos_ref, woa_ref, wout_ref, gpm_ref,
                     gpf_ref, g1_ref, sc2_ref, sh2_ref, wr_ref, br_ref, cnt_in_ref,
                     x1_ref, h2_ref, ti_ref, tg_ref, cnt_ref, cnt_scr):
    @pl.when(jnp.logical_and(pl.program_id(0) == 0, pl.program_id(1) == 0))
    def _():
        cnt_scr[...] = cnt_in_ref[...]

    y_ssd = _dot(u_ref[0], wos_ref[...])
    y_attn = _dot(a_ref[0], woa_ref[...])
    merged = jax.nn.sigmoid(gs_ref[0]) * y_ssd + jax.nn.sigmoid(ga_ref[0]) * y_attn
    mixed = _dot(merged.astype(BF16), wout_ref[...])
    x1 = x_ref[0] + g1_ref[0] * (_rms(mixed) * gpm_ref[...])
    x1_ref[0] = x1
    h2 = _rms(x1) * gpf_ref[...] * (1.0 + sc2_ref[0]) + sh2_ref[0]
    _store_token_tiles(h2_ref.at[0], h2)

    h_hi = h2.astype(BF16)
    h_lo = (h2 - h_hi.astype(F32)).astype(BF16)
    w = wr_ref[...]
    w_hi = w.astype(BF16)
    w_lo = (w - w_hi.astype(F32)).astype(BF16)
    logits = _dot(h_hi, w_hi) + (_dot(h_hi, w_lo) + _dot(h_lo, w_hi)) + br_ref[...]
    lane = lax.broadcasted_iota(jnp.int32, logits.shape, 1)
    idx_out = jnp.zeros(logits.shape, jnp.int32)
    val_out = jnp.zeros(logits.shape, F32)
    top = None
    denom = None
    idxs = []
    for k in range(TOP_K):
        m = jnp.max(logits, axis=-1, keepdims=True)
        idx = jnp.min(jnp.where(logits == m, lane, LANES), axis=-1, keepdims=True)
        if k == 0:
            top = m
            e = jnp.ones_like(m)
            denom = e
        else:
            e = jnp.exp(m - top)
            denom = denom + e
        idxs.append(idx)
        idx_out = jnp.where(lane == k, idx, idx_out)
        val_out = jnp.where(lane == k, e, val_out)
        logits = jnp.where(lane == idx, NEG_BIG * 2, logits)
    tg_ref[0] = val_out / denom
    ti_ref[0] = idx_out

    rowi = lax.broadcasted_iota(jnp.int32, (logits.shape[0], 1), 0)
    valid = jnp.bitwise_and(rowi, SAMPLE_ROWS - 1) < valid_rows
    picked = jnp.zeros(logits.shape, F32)
    for idx in idxs:
        picked = picked + jnp.where(jnp.logical_and(lane == idx, valid), 1.0, 0.0)
    cnt_scr[...] = cnt_scr[...] + jnp.sum(picked, axis=0, keepdims=True)
    cnt_ref[...] = cnt_scr[...]


def _post_mix(u, attn, proj, x, wos, woa, wout, gpm, gpf, gate1, scale2, shift2, wr, br, counts, *,
              tm, per_row, valid_rows):
    nb, rows, _ = x.shape
    row_spec = lambda w, col=0: pl.BlockSpec((1, tm, w), lambda b, i: (b, i, col // w))
    full = lambda shape: pl.BlockSpec(shape, lambda b, i: (0,) * len(shape))
    return pl.pallas_call(
        functools.partial(_post_mix_kernel, valid_rows),
        grid=(nb, rows // tm),
        in_specs=[
            row_spec(D_INNER), row_spec(D_MODEL), row_spec(D_MODEL, COL_GS), row_spec(D_MODEL, COL_GA),
            row_spec(D_MODEL),
            full((D_INNER, D_MODEL)), full((D_MODEL, D_MODEL)), full((D_MODEL, D_MODEL)),
            full((1, D_MODEL)), full((1, D_MODEL)),
            _mod_spec(per_row, tm), _mod_spec(per_row, tm), _mod_spec(per_row, tm),
            full((D_MODEL, LANES)), full((1, LANES)), full((1, LANES)),
        ],
        out_specs=[row_spec(D_MODEL), pl.BlockSpec((1, tm * TILE_ROWS, LANES), lambda b, i: (b, i, 0)),
                   row_spec(LANES), row_spec(LANES), full((1, LANES))],
        out_shape=[
            jax.ShapeDtypeStruct((nb, rows, D_MODEL), F32),
            jax.ShapeDtypeStruct((nb, rows * TILE_ROWS, LANES), F32),
            jax.ShapeDtypeStruct((nb, rows, LANES), jnp.int32),
            jax.ShapeDtypeStruct((nb, rows, LANES), F32),
            jax.ShapeDtypeStruct((1, LANES), F32),
        ],
        scratch_shapes=[pltpu.VMEM((1, LANES), F32)],
        compiler_params=pltpu.CompilerParams(
            dimension_semantics=("arbitrary", "arbitrary"), vmem_limit_bytes=MOE_VMEM_LIMIT),
        name="post_mix",
    )(u, attn, proj, proj, x, wos, woa, wout, gpm, gpf, gate1, scale2, shift2, wr, br, counts)


def _moe_kernel(n_tokens, be_ref, na_ref, idx_hbm, h_hbm, wu_ref, bu_ref, wd_ref, bd_ref, ys_hbm,
                idx_smem, xbuf, obuf, wu_bf, wd_bf, sem_idx, sem_in, sem_out):
    i = pl.program_id(0)
    na = na_ref[0]
    slot = lax.rem(i, 2)
    other = 1 - slot

    def idx_copy(block, s):
        return pltpu.make_async_copy(idx_hbm.at[block], idx_smem.at[s], sem_idx.at[s])

    block_rows = MOE_ROWS * TILE_ROWS

    def tile(ref, first_row):
        return ref.at[pl.ds(pl.multiple_of(first_row, TILE_ROWS), TILE_ROWS)]

    def start_gather(s):
        for r in range(MOE_ROWS):
            pltpu.make_async_copy(tile(h_hbm, idx_smem[s, r]), xbuf.at[s, pl.ds(r * TILE_ROWS, TILE_ROWS)],
                                  sem_in.at[s]).start(priority=r % 2)

    def wait_gather(s):
        pltpu.make_async_copy(h_hbm.at[pl.ds(0, block_rows)], xbuf.at[s], sem_in.at[s]).wait()

    def start_scatter(s):
        for r in range(MOE_ROWS):
            pltpu.make_async_copy(obuf.at[s, pl.ds(r * TILE_ROWS, TILE_ROWS)],
                                  tile(ys_hbm, idx_smem[s, MOE_ROWS + r]), sem_out.at[s]).start(priority=r % 2)

    def wait_scatter(s):
        pltpu.make_async_copy(obuf.at[s], ys_hbm.at[pl.ds(0, block_rows)], sem_out.at[s]).wait()

    @pl.when(i == 0)
    def _():
        obuf[0] = jnp.zeros((block_rows, LANES), F32)
        plane_rows = ys_hbm.shape[0] // (TOP_K * TILE_ROWS)
        spare = [pltpu.make_async_copy(
            obuf.at[0], ys_hbm.at[pl.ds((k * plane_rows + n_tokens + hf * MOE_ROWS) * TILE_ROWS, block_rows)],
            sem_out.at[0]) for k in range(TOP_K) for hf in range(2)]
        for cp in spare:
            cp.start()
        for cp in spare:
            cp.wait()
        first = idx_copy(0, 0)
        first.start()
        first.wait()
        start_gather(0)

        @pl.when(na > 1)
        def _():
            idx_copy(1, 1).start()

    @pl.when(i < na)
    def _():
        wait_gather(slot)

        @pl.when(i + 1 < na)
        def _():
            idx_copy(i + 1, other).wait()
            start_gather(other)

        @pl.when(i >= 2)
        def _():
            wait_scatter(slot)

        @pl.when(jnp.logical_or(i == 0, be_ref[i] != be_ref[jnp.maximum(i - 1, 0)]))
        def _():
            wu_bf[...] = wu_ref[0].astype(BF16)
            wd_bf[...] = wd_ref[0].astype(BF16)

        x = _load_token_tiles(xbuf.at[slot], MOE_ROWS).astype(BF16)
        up = _dot(x, wu_bf[...]) + bu_ref[0]
        glu = jnp.minimum(up[:, :D_FF], SWIGLU_LIMIT)
        lin = jnp.clip(up[:, D_FF:], -SWIGLU_LIMIT, SWIGLU_LIMIT)
        act = glu * jax.nn.sigmoid(SWIGLU_ALPHA * glu) * (lin + 1.0)
        _store_token_tiles(obuf.at[slot], _dot(act.astype(BF16), wd_bf[...]) + bd_ref[0])
        start_scatter(slot)

        @pl.when(i + 2 < na)
        def _():
            idx_copy(i + 2, slot).start()

        @pl.when(i == na - 1)
        def _():
            wait_scatter(slot)

            @pl.when(i >= 1)
            def _():
                wait_scatter(other)


def _moe(idx, h_tiles, ys_rows, block_expert, n_active, wu, bu, wd, bd):
    n_blocks = idx.shape[0]
    grid_spec = pltpu.PrefetchScalarGridSpec(
        num_scalar_prefetch=2,
        grid=(n_blocks,),
        in_specs=[
            pl.BlockSpec(memory_space=pl.ANY),
            pl.BlockSpec(memory_space=pl.ANY),
            pl.BlockSpec((1, D_MODEL, 2 * D_FF), lambda i, be, na: (be[i], 0, 0)),
            pl.BlockSpec((1, 1, 2 * D_FF), lambda i, be, na: (be[i], 0, 0)),
            pl.BlockSpec((1, D_FF, D_MODEL), lambda i, be, na: (be[i], 0, 0)),
            pl.BlockSpec((1, 1, D_MODEL), lambda i, be, na: (be[i], 0, 0)),
        ],
        out_specs=pl.BlockSpec(memory_space=pl.ANY),
        scratch_shapes=[
            pltpu.SMEM((2, 2 * MOE_ROWS), jnp.int32),
            pltpu.VMEM((2, MOE_ROWS * TILE_ROWS, LANES), F32),
            pltpu.VMEM((2, MOE_ROWS * TILE_ROWS, LANES), F32),
            pltpu.VMEM((D_MODEL, 2 * D_FF), BF16),
            pltpu.VMEM((D_FF, D_MODEL), BF16),
            pltpu.SemaphoreType.DMA((2,)),
            pltpu.SemaphoreType.DMA((2,)),
            pltpu.SemaphoreType.DMA((2,)),
        ],
    )
    return pl.pallas_call(
        functools.partial(_moe_kernel, h_tiles.shape[0] // TILE_ROWS),
        grid_spec=grid_spec,
        out_shape=jax.ShapeDtypeStruct((ys_rows * TILE_ROWS, LANES), F32),
        compiler_params=pltpu.CompilerParams(
            dimension_semantics=("arbitrary",), vmem_limit_bytes=MOE_VMEM_LIMIT),
        name="moe",
    )(block_expert, n_active, idx, h_tiles, wu, bu, wd, bd)


def _combine_kernel(ys_ref, tg_ref, x1_ref, gpost_ref, g2_ref, o_ref):
    gates = tg_ref[0]
    tm = gates.shape[0]
    f = gates[:, 0:1] * _load_token_tiles(ys_ref.at[0], tm)
    for k in range(1, TOP_K):
        f = f + gates[:, k:k + 1] * _load_token_tiles(ys_ref.at[k], tm)
    o_ref[0] = x1_ref[0] + g2_ref[0] * (_rms(f) * gpost_ref[...])


def _combine(ys, row0, tg, x1, gpost, gate2, *, tm, per_row):
    nb, rows, _ = x1.shape
    per_b = rows // tm
    base = row0 // tm
    return pl.pallas_call(
        _combine_kernel,
        grid=(nb, per_b),
        in_specs=[
            pl.BlockSpec((TOP_K, tm * TILE_ROWS, LANES), lambda b, i: (0, base + b * per_b + i, 0)),
            pl.BlockSpec((1, tm, LANES), lambda b, i: (b, i, 0)),
            pl.BlockSpec((1, tm, D_MODEL), lambda b, i: (b, i, 0)),
            pl.BlockSpec((1, D_MODEL), lambda b, i: (0, 0)),
            _mod_spec(per_row, tm),
        ],
        out_specs=pl.BlockSpec((1, tm, D_MODEL), lambda b, i: (b, i, 0)),
        out_shape=jax.ShapeDtypeStruct((nb, rows, D_MODEL), F32),
        compiler_params=pltpu.CompilerParams(
            dimension_semantics=("parallel", "parallel"), vmem_limit_bytes=VMEM_LIMIT),
        name="combine",
    )(ys, tg, x1, gpost, gate2)


def _route(top_idx, counts, plane_rows):
    n_tokens = top_idx.shape[0]
    n_slots = n_tokens * TOP_K
    experts = jnp.arange(N_EXPERTS, dtype=jnp.int32)
    padded = (counts + MOE_ROWS - 1) // MOE_ROWS * MOE_ROWS
    group_start = jnp.cumsum(counts) - counts
    padded_end = jnp.cumsum(padded)
    padded_start = padded_end - padded
    n_blocks = -(-n_slots // MOE_ROWS) + N_EXPERTS
    n_active = (padded_end[-1] // MOE_ROWS).astype(jnp.int32)
    block_start = jnp.arange(n_blocks, dtype=jnp.int32) * MOE_ROWS
    block_expert = jnp.sum(block_start[:, None] >= padded_end[None, :], axis=1)
    last_expert = jnp.max(jnp.where(counts > 0, experts, 0))
    block_expert = jnp.where(block_start < padded_end[-1], block_expert, last_expert).astype(jnp.int32)
    slot_id = jnp.arange(n_slots, dtype=jnp.int32).reshape(n_tokens, TOP_K)
    keys = jnp.sort((top_idx * n_slots + slot_id).reshape(-1))
    row = jnp.arange(n_blocks * MOE_ROWS, dtype=jnp.int32).reshape(n_blocks, MOE_ROWS)
    src = row + (group_start - padded_start)[block_expert][:, None]
    real = row < (padded_start + counts)[block_expert][:, None]
    slot = keys[jnp.clip(src, 0, n_slots - 1).reshape(-1)].reshape(n_blocks, MOE_ROWS) % n_slots
    token = slot // TOP_K
    spare = n_tokens + (jnp.arange(n_blocks, dtype=jnp.int32) % 2)[:, None] * MOE_ROWS + row % MOE_ROWS
    dest = jnp.where(real, (slot % TOP_K) * plane_rows + token, spare)
    idx = (jnp.concatenate([token, dest], axis=1) * TILE_ROWS).astype(jnp.int32)
    return idx, block_expert, n_active.reshape(1)


def _pad_lanes(v, value=0.0):
    return jnp.pad(v, [(0, 0)] * (v.ndim - 1) + [(0, LANES - v.shape[-1])], constant_values=value)


def kernel(x_prompt, x_sample, c_prompt, c_sample, cache_swa_k, cache_swa_v, state_conv, state_ssm, w_ada, b_ada, g_pre_mix, g_post_mix, g_pre_ffn, g_post_ffn, w_in, conv_w, conv_b, dt_bias, a_log, d_skip, g_ssm_norm, sinks, w_o_ssd, w_o_attn, w_out, w_router, b_router, w_up, b_up, w_down, b_down):
    depth = w_ada.shape[0]
    n_prompt, seq, _ = x_prompt.shape
    n_sample, dec_seq, _ = x_sample.shape
    yp = x_prompt
    ys_pad = jnp.pad(x_sample, ((0, 0), (0, SAMPLE_ROWS - dec_seq), (0, 0)))
    outs = [[] for _ in range(8)]
    expand = jnp.repeat(jnp.eye(LANES, N_SSD_HEADS, dtype=BF16), SSD_HEAD_DIM, axis=1)
    rows_s = n_sample * SAMPLE_ROWS
    n_p = n_prompt * seq
    n_s = n_sample * dec_seq
    n_tok = n_p + n_s
    tm_p = min(512, seq)
    tm_s = min(512, rows_s)
    tm_mix_p = min(512, seq)
    tm_mix_s = min(256, rows_s)
    assert seq % CHUNK == 0 and seq % tm_p == 0 and rows_s % tm_s == 0 and n_p % n_s == 0
    c_all = jnp.concatenate([c_prompt, c_sample], axis=0)
    c_rows = -(-c_all.shape[0] // SUBLANES) * SUBLANES
    c_all = jnp.pad(c_all, ((0, c_rows - c_all.shape[0]), (0, 0)))

    for l in range(depth):
        wi = w_in[l]
        o_xbc = D_INNER
        o_dt = o_xbc + CONV_DIM
        o_q = o_dt + N_SSD_HEADS
        o_k = o_q + D_MODEL
        o_v = o_k + KV_DIM
        o_gs = o_v + KV_DIM
        o_ga = o_gs + D_MODEL
        head_order = jnp.array(ATTN_HEAD_ORDER)
        w_q = wi[:, o_q:o_k].reshape(D_MODEL, N_ATTN_HEADS, ATTN_HEAD_DIM)[:, head_order].reshape(D_MODEL, D_MODEL)
        w_proj = jnp.concatenate([
            wi[:, :o_xbc], wi[:, o_xbc:o_xbc + D_INNER], w_q, wi[:, o_gs:o_ga], wi[:, o_ga:],
            wi[:, o_xbc + D_INNER:o_dt], wi[:, o_k:o_v], wi[:, o_v:o_gs], _pad_lanes(wi[:, o_dt:o_q])],
            axis=1).astype(BF16)
        cwx, cwbc = conv_w[l][:, :D_INNER], conv_w[l][:, D_INNER:]
        cbx, cbbc = conv_b[l][None, :D_INNER], conv_b[l][None, D_INNER:]
        dtb = _pad_lanes(dt_bias[l][None])
        alog = _pad_lanes(a_log[l][None])
        dskip = jnp.repeat(d_skip[l], SSD_HEAD_DIM)[None]
        gn = g_ssm_norm[l][None]
        wos, wout = w_o_ssd[l].astype(BF16), w_out[l].astype(BF16)
        woa = w_o_attn[l].reshape(N_ATTN_HEADS, ATTN_HEAD_DIM, D_MODEL)[head_order].reshape(D_MODEL, D_MODEL)
        woa = woa.astype(BF16)
        zero_counts = jnp.zeros((1, LANES), F32)
        wr = _pad_lanes(w_router[l])
        br = _pad_lanes(b_router[l][None], NEG_BIG)
        wu, wd = w_up[l], w_down[l]
        bu, bd = b_up[l][:, None, :], b_down[l][:, None, :]
        g_pm, g_pom, g_pf, g_pof = (v[l][None] for v in (g_pre_mix, g_post_mix, g_pre_ffn, g_post_ffn))

        ada = _ada(c_all, w_ada[l], b_ada[l][None])
        ada_p = [ada[:n_prompt, k * D_MODEL:(k + 1) * D_MODEL] for k in range(6)]
        ada_s = [ada[n_prompt:n_prompt + n_sample, k * D_MODEL:(k + 1) * D_MODEL] for k in range(6)]

        mods = [m[:, None, :] for m in ada_p]
        proj = _in_proj(yp, g_pm, mods[1], mods[0], w_proj, tm=min(1024, seq), per_row=False)
        u, ssm_p = _ssd(proj, cwx, cbx, cwbc, cbbc, dtb, alog, dskip, gn)
        attn = _attention(proj, sinks[l], WINDOW)
        x1_p, h2_p, ti_p, tg_p, cnt_p = _post_mix(
            u, attn, proj, yp, wos, woa, wout, g_pom, g_pf, mods[2], mods[4], mods[3], wr, br, zero_counts,
            tm=tm_mix_p, per_row=False, valid_rows=SAMPLE_ROWS)
        gate2_p = mods[5]
        kp = proj[:, seq - WINDOW:, COL_K:COL_K + KV_DIM].reshape(n_prompt, WINDOW, N_KV_HEADS, ATTN_HEAD_DIM)
        vp = proj[:, seq - WINDOW:, COL_V:COL_V + KV_DIM].reshape(n_prompt, WINDOW, N_KV_HEADS, ATTN_HEAD_DIM)
        cp = jnp.concatenate([proj[:, seq - (CONV_W - 1):, COL_X:COL_X + D_INNER],
                              proj[:, seq - (CONV_W - 1):, COL_BC:COL_BC + BC_DIM]], axis=-1)

        mods_s = [jnp.repeat(m, SAMPLE_ROWS, axis=0)[None] for m in ada_s]
        xs_flat = ys_pad.reshape(1, rows_s, D_MODEL)
        proj_s = _in_proj(xs_flat, g_pm, mods_s[1], mods_s[0], w_proj, tm=tm_s, per_row=True)
        proj_sb = proj_s.reshape(n_sample, SAMPLE_ROWS, PROJ_DIM)
        u_s, ssm_s = _ssd_step(
            proj_sb, state_conv[l][:, :, :D_INNER], state_conv[l][:, :, D_INNER:],
            state_ssm[l].reshape(n_sample, D_INNER, D_STATE),
            cwx, cbx, cwbc, cbbc, dtb, alog, dskip, gn, expand, dec_seq)
        k_prev = cache_swa_k[l].reshape(n_sample, -1, KV_DIM)
        v_prev = cache_swa_v[l].reshape(n_sample, -1, KV_DIM)
        attn_s = _attention(proj_sb, sinks[l], SAMPLE_ROWS, prev_kv=(k_prev, v_prev),
                            seqs=math.gcd(n_sample, 8))
        x1_s, h2_s, ti_s, tg_s, cnt_all = _post_mix(
            u_s.reshape(1, rows_s, D_INNER), attn_s.reshape(1, rows_s, D_MODEL), proj_s, xs_flat,
            wos, woa, wout, g_pom, g_pf, mods_s[2], mods_s[4], mods_s[3], wr, br, cnt_p,
            tm=tm_mix_s, per_row=True, valid_rows=dec_seq)
        wb = k_prev.shape[1]
        k_new = proj_sb[:, :dec_seq, COL_K:COL_K + KV_DIM]
        v_new = proj_sb[:, :dec_seq, COL_V:COL_V + KV_DIM]
        ks = jnp.concatenate([k_prev, k_new], axis=1)[:, -wb:].reshape(n_sample, wb, N_KV_HEADS, ATTN_HEAD_DIM)
        vs = jnp.concatenate([v_prev, v_new], axis=1)[:, -wb:].reshape(n_sample, wb, N_KV_HEADS, ATTN_HEAD_DIM)
        raw_xbc = jnp.concatenate([proj_sb[:, :dec_seq, COL_X:COL_X + D_INNER],
                                   proj_sb[:, :dec_seq, COL_BC:COL_BC + BC_DIM]], axis=-1)
        cs = jnp.concatenate([state_conv[l], raw_xbc], axis=1)[:, -(CONV_W - 1):]

        valid = lambda v: v.reshape(n_sample, SAMPLE_ROWS, -1)[:, :dec_seq].reshape(n_s, -1)
        h2_sv = h2_s.reshape(n_sample, SAMPLE_ROWS * TILE_ROWS, LANES)[:, :dec_seq * TILE_ROWS]
        h2_all = jnp.concatenate([h2_p.reshape(n_p * TILE_ROWS, LANES),
                                  h2_sv.reshape(n_s * TILE_ROWS, LANES)], axis=0)
        ti_all = jnp.concatenate([ti_p.reshape(n_p, LANES), valid(ti_s)], axis=0)
        counts = cnt_all[0, :N_EXPERTS].astype(jnp.int32)
        plane_rows = n_tok + 2 * MOE_ROWS
        idx, block_expert, n_active = _route(ti_all[:, :TOP_K], counts, plane_rows)
        ysel = _moe(idx, h2_all, TOP_K * plane_rows, block_expert, n_active, wu, bu, wd, bd)
        ysel = ysel.reshape(TOP_K, plane_rows * TILE_ROWS, LANES)

        yp = _combine(ysel, 0, tg_p, x1_p, g_pof, gate2_p, tm=tm_p, per_row=False)
        x1_sv = valid(x1_s)[None]
        tg_sv = valid(tg_s)[None]
        gate2_s = jnp.repeat(ada_s[5], dec_seq, axis=0)[None]
        ys_new = _combine(ysel, n_p, tg_sv, x1_sv, g_pof, gate2_s, tm=n_s, per_row=True)
        ys_new = ys_new.reshape(n_sample, dec_seq, D_MODEL)
        ys_pad = jnp.pad(ys_new, ((0, 0), (0, SAMPLE_ROWS - dec_seq), (0, 0)))

        for lst, v in zip(outs, (kp, vp, cp, ssm_p, ks, vs, cs,
                                 ssm_s.reshape(n_sample, N_SSD_HEADS, SSD_HEAD_DIM, D_STATE))):
            lst.append(v)

    return (yp, ys_pad[:, :dec_seq], *[jnp.stack(v) for v in outs])
```

```python
import functools
import math

import jax
import jax.numpy as jnp
from jax import lax
from jax.experimental import pallas as pl
from jax.experimental.pallas import tpu as pltpu

F32 = jnp.float32
BF16 = jnp.bfloat16

D_MODEL = 1024
D_INNER = 2 * D_MODEL
SSD_HEAD_DIM = 64
N_SSD_HEADS = D_INNER // SSD_HEAD_DIM
N_SSD_GROUPS = 4
HEADS_PER_GROUP = N_SSD_HEADS // N_SSD_GROUPS
D_STATE = 128
CONV_W = 4
BC_DIM = 2 * N_SSD_GROUPS * D_STATE
CONV_DIM = D_INNER + BC_DIM
CHUNK = 128
ATTN_HEAD_DIM = 64
N_ATTN_HEADS = D_MODEL // ATTN_HEAD_DIM
N_KV_HEADS = 4
KV_REP = N_ATTN_HEADS // N_KV_HEADS
KV_DIM = N_KV_HEADS * ATTN_HEAD_DIM
WINDOW = 128
N_EXPERTS = 32
TOP_K = 4
D_FF = D_MODEL
SWIGLU_LIMIT = 7.0
SWIGLU_ALPHA = 1.702
EPS = 1e-6

LANES = 128
SUBLANES = 8
NEG_BIG = -1e30

COL_Z = 0
COL_X = COL_Z + D_INNER
COL_Q = COL_X + D_INNER
COL_GS = COL_Q + D_MODEL
COL_GA = COL_GS + D_MODEL
COL_BC = COL_GA + D_MODEL
COL_K = COL_BC + BC_DIM
COL_V = COL_K + KV_DIM
COL_DT = COL_V + KV_DIM
PROJ_DIM = COL_DT + LANES
PROJ_TILE_N = PROJ_DIM // 3

ATTN_HEAD_ORDER = tuple(2 * KV_REP * (j // KV_REP) + (j % KV_REP) + KV_REP * hf
                        for j in range(N_ATTN_HEADS // 2) for hf in (0, 1))

TILE_ROWS = D_MODEL // LANES
assert TILE_ROWS == SUBLANES

SAMPLE_ROWS = 8
MOE_ROWS = 512
VMEM_LIMIT = 48 * 1024 * 1024
MOE_VMEM_LIMIT = 56 * 1024 * 1024


def _silu(v):
    return v * jax.nn.sigmoid(v)


def _softplus(v):
    return jnp.maximum(v, 0.0) + jnp.log(1.0 + jnp.exp(-jnp.abs(v)))


def _rms(v):
    return v * lax.rsqrt(jnp.mean(v * v, axis=-1, keepdims=True) + EPS)


def _split3(v):
    hi = v.astype(BF16)
    r1 = v - hi.astype(F32)
    mid = r1.astype(BF16)
    lo = (r1 - mid.astype(F32)).astype(BF16)
    return hi, mid, lo


def _store_token_tiles(ref, rows):
    n = rows.shape[0]
    for j in range(TILE_ROWS):
        ref[pl.ds(j, n, stride=TILE_ROWS), :] = rows[:, j * LANES:(j + 1) * LANES]


def _load_token_tiles(ref, n):
    return jnp.concatenate([ref[pl.ds(j, n, stride=TILE_ROWS), :] for j in range(TILE_ROWS)], axis=1)


def _dot(a, b):
    return jnp.dot(a, b, preferred_element_type=F32)


def _dot_nt(a, b):
    return lax.dot_general(a, b, (((1,), (1,)), ((), ())), preferred_element_type=F32)


def _dot_tn(a, b):
    return lax.dot_general(a, b, (((0,), (0,)), ((), ())), preferred_element_type=F32)


def _dot_exact_rhs(a, b_bf16):
    hi, mid, lo = _split3(a)
    return _dot(hi, b_bf16) + _dot(mid, b_bf16) + _dot(lo, b_bf16)


def _ada_kernel(c_ref, w_ref, b_ref, o_ref):
    o_ref[...] = _dot(_silu(c_ref[...]).astype(BF16), w_ref[...].astype(BF16)) + b_ref[...]


def _ada(c, w, b):
    rows, n = c.shape[0], w.shape[1]
    tn = n // 4
    return pl.pallas_call(
        _ada_kernel,
        grid=(n // tn,),
        in_specs=[
            pl.BlockSpec((rows, D_MODEL), lambda j: (0, 0)),
            pl.BlockSpec((D_MODEL, tn), lambda j: (0, j)),
            pl.BlockSpec((1, tn), lambda j: (0, j)),
        ],
        out_specs=pl.BlockSpec((rows, tn), lambda j: (0, j)),
        out_shape=jax.ShapeDtypeStruct((rows, n), F32),
        compiler_params=pltpu.CompilerParams(
            dimension_semantics=("parallel",), vmem_limit_bytes=VMEM_LIMIT),
        name="ada",
    )(c, w, b)


def _in_proj_kernel(x_ref, g_ref, sc_ref, sh_ref, w_ref, o_ref, h_scr):
    @pl.when(pl.program_id(2) == 0)
    def _():
        h = _rms(x_ref[0]) * g_ref[...] * (1.0 + sc_ref[0]) + sh_ref[0]
        h_scr[...] = h.astype(BF16)

    o_ref[0] = _dot(h_scr[...], w_ref[...])


def _mod_spec(per_row, tm):
    if per_row:
        return pl.BlockSpec((1, tm, D_MODEL), lambda b, i, *_: (b, i, 0))
    return pl.BlockSpec((1, 1, D_MODEL), lambda b, i, *_: (b, 0, 0))


def _in_proj(x, g, scale, shift, w, *, tm, per_row):
    nb, rows, _ = x.shape
    return pl.pallas_call(
        _in_proj_kernel,
        grid=(nb, rows // tm, PROJ_DIM // PROJ_TILE_N),
        in_specs=[
            pl.BlockSpec((1, tm, D_MODEL), lambda b, i, j: (b, i, 0)),
            pl.BlockSpec((1, D_MODEL), lambda b, i, j: (0, 0)),
            _mod_spec(per_row, tm),
            _mod_spec(per_row, tm),
            pl.BlockSpec((D_MODEL, PROJ_TILE_N), lambda b, i, j: (0, j)),
        ],
        out_specs=pl.BlockSpec((1, tm, PROJ_TILE_N), lambda b, i, j: (b, i, j)),
        out_shape=jax.ShapeDtypeStruct((nb, rows, PROJ_DIM), F32),
        scratch_shapes=[pltpu.VMEM((tm, D_MODEL), BF16)],
        compiler_params=pltpu.CompilerParams(
            dimension_semantics=("parallel", "parallel", "arbitrary"),
            vmem_limit_bytes=MOE_VMEM_LIMIT),
        name="in_proj",
    )(x, g, scale, shift, w)


def _causal_conv(ext_ref, raw, prev, w_ref, b_ref, rows):
    base = SUBLANES
    if prev is not None:
        ext_ref[pl.ds(base - (CONV_W - 1), CONV_W - 1), :] = prev
    ext_ref[pl.ds(base, rows), :] = raw
    out = b_ref[...] + raw * w_ref[pl.ds(CONV_W - 1, 1), :]
    for k in range(CONV_W - 1):
        out = out + ext_ref[pl.ds(base - (CONV_W - 1) + k, rows), :] * w_ref[pl.ds(k, 1), :]
    return out


def _gated_group_norm(y, z, g_ref):
    u = y * _silu(z)
    gw = D_INNER // N_SSD_GROUPS
    parts = []
    for g in range(N_SSD_GROUPS):
        ug = u[:, g * gw:(g + 1) * gw]
        parts.append(ug * lax.rsqrt(jnp.mean(ug * ug, axis=-1, keepdims=True) + EPS))
    return jnp.concatenate(parts, axis=-1) * g_ref[...]


def _ssd_kernel(z_ref, x_ref, bc_ref, dt_ref, cwx_ref, cbx_ref, cwbc_ref, cbbc_ref, dtb_ref, alog_ref,
                dskip_ref, gn_ref, u_ref, fin_ref, extx_scr, extbc_scr, st_scr, y_scr):
    q = CHUNK
    c = pl.program_id(1)

    @pl.when(c == 0)
    def _():
        extx_scr[pl.ds(0, SUBLANES), :] = jnp.zeros((SUBLANES, D_INNER), F32)
        extbc_scr[pl.ds(0, SUBLANES), :] = jnp.zeros((SUBLANES, BC_DIM), F32)
        st_scr[...] = jnp.zeros_like(st_scr)

    xs = _silu(_causal_conv(extx_scr, x_ref[0], None, cwx_ref, cbx_ref, q))
    bc = _silu(_causal_conv(extbc_scr, bc_ref[0], None, cwbc_ref, cbbc_ref, q))
    tail = SUBLANES + q - (CONV_W - 1)
    extx_scr[pl.ds(SUBLANES - (CONV_W - 1), CONV_W - 1), :] = extx_scr[pl.ds(tail, CONV_W - 1), :]
    extbc_scr[pl.ds(SUBLANES - (CONV_W - 1), CONV_W - 1), :] = extbc_scr[pl.ds(tail, CONV_W - 1), :]

    dt = _softplus(dt_ref[0] + dtb_ref[...])
    adt = dt * (-jnp.exp(alog_ref[...]))
    row = lax.broadcasted_iota(jnp.int32, (q, q), 0)
    col = lax.broadcasted_iota(jnp.int32, (q, q), 1)
    causal = row >= col
    tri = jnp.where(causal, 1.0, 0.0).astype(BF16)
    h3, m3, l3 = _split3(adt)
    acs = _dot(tri, h3) + _dot(tri, m3) + _dot(tri, l3)
    eacs = jnp.exp(acs)
    acs_t = acs.T
    dt_t = dt.T
    lane = lax.broadcasted_iota(jnp.int32, (1, LANES), 1)
    lo_half = lane < SSD_HEAD_DIM

    for g in range(N_SSD_GROUPS):
        bm = bc[:, g * D_STATE:(g + 1) * D_STATE]
        cm = bc[:, (N_SSD_GROUPS + g) * D_STATE:(N_SSD_GROUPS + g + 1) * D_STATE]
        cb = jnp.where(causal, _dot_nt(cm.astype(BF16), bm.astype(BF16)), 0.0)
        bm_t = bm.T
        for pr in range(HEADS_PER_GROUP // 2):
            h0 = g * HEADS_PER_GROUP + 2 * pr
            lanes = pl.ds(h0 * SSD_HEAD_DIM, LANES)
            x_pair = xs[:, h0 * SSD_HEAD_DIM:h0 * SSD_HEAD_DIM + LANES]
            st_pair = st_scr[:, lanes]
            lhs_y, lhs_s, decs = [], [], []
            for h in (h0, h0 + 1):
                a_col = acs[:, h:h + 1]
                a_row = acs_t[h:h + 1, :]
                dt_row = dt_t[h:h + 1, :]
                a_last = acs_t[h:h + 1, q - 1:q]
                decay = jnp.exp(jnp.minimum(a_col - a_row, 0.0))
                lhs_y.append((cb * decay * dt_row).astype(BF16))
                lhs_s.append((bm_t * (dt_row * jnp.exp(a_last - a_row))).astype(BF16))
                decs.append(jnp.exp(a_last))
            for h in (h0, h0 + 1):
                lhs_y.append((cm * eacs[:, h:h + 1]).astype(BF16))
            x_top = jnp.where(lo_half, x_pair, 0.0).astype(BF16)
            x_bot = jnp.where(lo_half, 0.0, x_pair).astype(BF16)
            s_top = jnp.where(lo_half, st_pair, 0.0).astype(BF16)
            s_bot = jnp.where(lo_half, 0.0, st_pair).astype(BF16)
            x_bd = jnp.concatenate([x_top, x_bot], axis=0)
            rhs_y = jnp.concatenate([x_bd, s_top, s_bot], axis=0)
            y_pair = _dot(jnp.concatenate(lhs_y, axis=1), rhs_y)
            ds_pair = _dot(jnp.concatenate(lhs_s, axis=1), x_bd)
            dskip = dskip_ref[:, lanes]
            y_scr[:, lanes] = y_pair + dskip * x_pair
            st_scr[:, lanes] = st_pair * jnp.where(lo_half, decs[0], decs[1]) + ds_pair

    u_ref[0] = _gated_group_norm(y_scr[...], z_ref[0], gn_ref).astype(u_ref.dtype)

    @pl.when(c == pl.num_programs(1) - 1)
    def _():
        for pr in range(N_SSD_HEADS // 2):
            t = st_scr[:, pl.ds(pr * LANES, LANES)].T
            fin_ref[0, 2 * pr] = t[:SSD_HEAD_DIM]
            fin_ref[0, 2 * pr + 1] = t[SSD_HEAD_DIM:]


def _ssd(proj, cwx, cbx, cwbc, cbbc, dtb, alog, dskip, gn):
    nb, rows, _ = proj.shape
    q = CHUNK
    full = lambda shape: pl.BlockSpec(shape, lambda b, c: (0,) * len(shape))
    return pl.pallas_call(
        _ssd_kernel,
        grid=(nb, rows // q),
        in_specs=[
            pl.BlockSpec((1, q, D_INNER), lambda b, c: (b, c, COL_Z // D_INNER)),
            pl.BlockSpec((1, q, D_INNER), lambda b, c: (b, c, COL_X // D_INNER)),
            pl.BlockSpec((1, q, BC_DIM), lambda b, c: (b, c, COL_BC // BC_DIM)),
            pl.BlockSpec((1, q, LANES), lambda b, c: (b, c, COL_DT // LANES)),
            full((CONV_W, D_INNER)), full((1, D_INNER)), full((CONV_W, BC_DIM)), full((1, BC_DIM)),
            full((1, LANES)), full((1, LANES)), full((1, D_INNER)), full((1, D_INNER)),
        ],
        out_specs=[
            pl.BlockSpec((1, q, D_INNER), lambda b, c: (b, c, 0)),
            pl.BlockSpec((1, N_SSD_HEADS, SSD_HEAD_DIM, D_STATE), lambda b, c: (b, 0, 0, 0)),
        ],
        out_shape=[
            jax.ShapeDtypeStruct((nb, rows, D_INNER), BF16),
            jax.ShapeDtypeStruct((nb, N_SSD_HEADS, SSD_HEAD_DIM, D_STATE), F32),
        ],
        scratch_shapes=[
            pltpu.VMEM((SUBLANES + q, D_INNER), F32),
            pltpu.VMEM((SUBLANES + q, BC_DIM), F32),
            pltpu.VMEM((D_STATE, D_INNER), F32),
            pltpu.VMEM((q, D_INNER), F32),
        ],
        compiler_params=pltpu.CompilerParams(
            dimension_semantics=("parallel", "arbitrary"), vmem_limit_bytes=VMEM_LIMIT),
        name="ssd",
    )(proj, proj, proj, proj, cwx, cbx, cwbc, cbbc, dtb, alog, dskip, gn)


def _ssd_step_kernel(n_valid, z_ref, x_ref, bc_ref, dt_ref, px_ref, pbc_ref, st_ref, cwx_ref, cbx_ref,
                     cwbc_ref, cbbc_ref, dtb_ref, alog_ref, dskip_ref, gn_ref, exp_ref,
                     u_ref, nst_ref, extx_scr, extbc_scr):
    for sq in range(z_ref.shape[0]):
        _ssd_step_one(n_valid, z_ref.at[sq], x_ref.at[sq], bc_ref.at[sq], dt_ref.at[sq], px_ref.at[sq],
                      pbc_ref.at[sq], st_ref.at[sq], cwx_ref, cbx_ref, cwbc_ref, cbbc_ref, dtb_ref, alog_ref,
                      dskip_ref, gn_ref, exp_ref, u_ref.at[sq], nst_ref.at[sq], extx_scr.at[sq], extbc_scr.at[sq])


def _ssd_step_one(n_valid, z_ref, x_ref, bc_ref, dt_ref, px_ref, pbc_ref, st_ref, cwx_ref, cbx_ref,
                  cwbc_ref, cbbc_ref, dtb_ref, alog_ref, dskip_ref, gn_ref, exp_ref,
                  u_ref, nst_ref, extx_scr, extbc_scr):
    q = SAMPLE_ROWS
    gw = D_INNER // N_SSD_GROUPS
    xs = _silu(_causal_conv(extx_scr, x_ref[...], px_ref[...], cwx_ref, cbx_ref, q))
    bc = _silu(_causal_conv(extbc_scr, bc_ref[...], pbc_ref[...], cwbc_ref, cbbc_ref, q))

    rowi = lax.broadcasted_iota(jnp.int32, (q, 1), 0)
    dt = jnp.where(rowi < n_valid, _softplus(dt_ref[...] + dtb_ref[...]), 0.0)
    adt = dt * (-jnp.exp(alog_ref[...]))
    acs = jnp.zeros_like(adt)
    for s in range(n_valid):
        acs = acs + jnp.where(rowi >= s, adt[s:s + 1, :], 0.0)
    expand = exp_ref[...]
    dt_e = _dot_exact_rhs(dt, expand)
    acs_e = _dot_exact_rhs(acs, expand)
    last_e = acs_e[q - 1:q, :]

    x_end = xs * dt_e * jnp.exp(last_e - acs_e)
    dec3 = _split3(jnp.exp(last_e))
    zrow = jnp.zeros((q - 3, D_INNER), BF16)
    dec_rows = jnp.concatenate([dec3[0], dec3[1], dec3[2], zrow], axis=0)
    ones = jnp.ones((q, D_STATE), BF16)

    y_off, cbs = [], []
    for g in range(N_SSD_GROUPS):
        bm = bc[:, g * D_STATE:(g + 1) * D_STATE].astype(BF16)
        cm = bc[:, (N_SSD_GROUPS + g) * D_STATE:(N_SSD_GROUPS + g + 1) * D_STATE].astype(BF16)
        rows = pl.ds(g * gw, gw)
        st = st_ref[rows, :]
        y_off.append(_dot_nt(cm, st.astype(BF16)))
        cbs.append(_dot_nt(cm, bm))
        d_st = _dot_tn(x_end[:, g * gw:(g + 1) * gw].astype(BF16), bm)
        dec = _dot_tn(dec_rows[:, g * gw:(g + 1) * gw], ones)
        nst_ref[rows, :] = st * dec + d_st

    y = jnp.concatenate(y_off, axis=-1) * jnp.exp(acs_e) + dskip_ref[...] * xs
    for s in range(n_valid):
        keep = rowi >= s
        decay = jnp.where(keep, jnp.exp(jnp.where(keep, acs_e - acs_e[s:s + 1, :], 0.0)), 0.0)
        cb_e = jnp.concatenate([jnp.broadcast_to(cb[:, s:s + 1], (q, gw)) for cb in cbs], axis=-1)
        y = y + decay * cb_e * (dt_e[s:s + 1, :] * xs[s:s + 1, :])
    u_ref[...] = _gated_group_norm(y, z_ref[...], gn_ref).astype(u_ref.dtype)


def _ssd_step(proj, prev_x, prev_bc, state, cwx, cbx, cwbc, cbbc, dtb, alog, dskip, gn, expand, n_valid):
    nb = proj.shape[0]
    q = SAMPLE_ROWS
    seqs = math.gcd(nb, 2)
    full = lambda shape: pl.BlockSpec(shape, lambda b: (0,) * len(shape))
    return pl.pallas_call(
        functools.partial(_ssd_step_kernel, n_valid),
        grid=(nb // seqs,),
        in_specs=[
            pl.BlockSpec((seqs, q, D_INNER), lambda b: (b, 0, COL_Z // D_INNER)),
            pl.BlockSpec((seqs, q, D_INNER), lambda b: (b, 0, COL_X // D_INNER)),
            pl.BlockSpec((seqs, q, BC_DIM), lambda b: (b, 0, COL_BC // BC_DIM)),
            pl.BlockSpec((seqs, q, LANES), lambda b: (b, 0, COL_DT // LANES)),
            pl.BlockSpec((seqs, CONV_W - 1, D_INNER), lambda b: (b, 0, 0)),
            pl.BlockSpec((seqs, CONV_W - 1, BC_DIM), lambda b: (b, 0, 0)),
            pl.BlockSpec((seqs, D_INNER, D_STATE), lambda b: (b, 0, 0)),
            full((CONV_W, D_INNER)), full((1, D_INNER)), full((CONV_W, BC_DIM)), full((1, BC_DIM)),
            full((1, LANES)), full((1, LANES)), full((1, D_INNER)), full((1, D_INNER)),
            full((LANES, D_INNER)),
        ],
        out_specs=[
            pl.BlockSpec((seqs, q, D_INNER), lambda b: (b, 0, 0)),
            pl.BlockSpec((seqs, D_INNER, D_STATE), lambda b: (b, 0, 0)),
        ],
        out_shape=[
            jax.ShapeDtypeStruct((nb, q, D_INNER), BF16),
            jax.ShapeDtypeStruct((nb, D_INNER, D_STATE), F32),
        ],
        scratch_shapes=[
            pltpu.VMEM((seqs, 2 * SUBLANES, D_INNER), F32),
            pltpu.VMEM((seqs, 2 * SUBLANES, BC_DIM), F32),
        ],
        compiler_params=pltpu.CompilerParams(
            dimension_semantics=("parallel",), vmem_limit_bytes=VMEM_LIMIT),
        name="ssd_step",
    )(proj, proj, proj, proj, prev_x, prev_bc, state, cwx, cbx, cwbc, cbbc, dtb, alog, dskip, gn, expand)


def _attention_kernel(qb, prev_always_valid, sinks_ref, q_ref, kp_ref, vp_ref, kc_ref, vc_ref, o_ref):
    wb = WINDOW
    lane = lax.broadcasted_iota(jnp.int32, (1, LANES), 1)
    lo_half = lane < ATTN_HEAD_DIM
    keys_major = qb % LANES == 0
    t_ax, s_ax = (1, 0) if keys_major else (0, 1)
    shape_p = (wb, qb) if keys_major else (qb, wb)
    t_p = lax.broadcasted_iota(jnp.int32, shape_p, t_ax)
    s_p = lax.broadcasted_iota(jnp.int32, shape_p, s_ax)
    rel_p = wb + t_p - s_p
    ok_p = rel_p < WINDOW
    if not prev_always_valid:
        ok_p = jnp.logical_and(ok_p, pl.program_id(1) > 0)
    t_c = lax.broadcasted_iota(jnp.int32, (qb, qb), t_ax)
    s_c = lax.broadcasted_iota(jnp.int32, (qb, qb), s_ax)
    rel_c = t_c - s_c
    ok_c = rel_c >= 0
    rel_p = rel_p.astype(F32)
    rel_c = rel_c.astype(F32)
    scale = ATTN_HEAD_DIM ** -0.5

    pairs = KV_REP
    half = pairs * qb

    def one_sequence(sq, m, bias_p, bias_c, sink):
        kv_lanes = pl.ds(m * LANES, LANES)
        kp = kp_ref[sq, :, kv_lanes].astype(BF16)
        kc = kc_ref[sq, :, kv_lanes].astype(BF16)
        vp = vp_ref[sq, :, kv_lanes]
        vc = vc_ref[sq, :, kv_lanes]
        vp_lo, vp_hi = jnp.where(lo_half, vp, 1.0).astype(BF16), jnp.where(lo_half, 1.0, vp).astype(BF16)
        vc_lo, vc_hi = jnp.where(lo_half, vc, 1.0).astype(BF16), jnp.where(lo_half, 1.0, vc).astype(BF16)
        q_lo, q_hi = [], []
        for i in range(pairs):
            q_pair = q_ref[sq, :, pl.ds((pairs * m + i) * LANES, LANES)] * scale
            q_lo.append(jnp.where(lo_half, q_pair, 0.0).astype(BF16))
            q_hi.append(jnp.where(lo_half, 0.0, q_pair).astype(BF16))
        qs = jnp.concatenate(q_lo + q_hi, axis=0)
        s_p = _dot_nt(qs, kp) + bias_p
        s_c = _dot_nt(qs, kc) + bias_c
        if qb == wb:
            mx = jnp.max(jnp.maximum(s_p, s_c), axis=-1, keepdims=True)
        else:
            mx = jnp.maximum(jnp.max(s_p, axis=-1, keepdims=True), jnp.max(s_c, axis=-1, keepdims=True))
        mx = jnp.maximum(mx, sink)
        p_p = jnp.exp(s_p - mx).astype(BF16)
        p_c = jnp.exp(s_c - mx).astype(BF16)
        e_sink = jnp.exp(sink - mx)
        o_lo = _dot(p_p[:half], vp_lo) + _dot(p_c[:half], vc_lo)
        o_hi = _dot(p_p[half:], vp_hi) + _dot(p_c[half:], vc_hi)
        for i in range(pairs):
            a = o_lo[i * qb:(i + 1) * qb]
            b = o_hi[i * qb:(i + 1) * qb]
            num = jnp.where(lo_half, a, b)
            den = (pltpu.roll(jnp.where(lo_half, b, a), ATTN_HEAD_DIM, 1)
                   + jnp.where(lo_half, e_sink[i * qb:(i + 1) * qb], e_sink[half + i * qb:half + (i + 1) * qb]))
            o_ref[sq, :, pl.ds((pairs * m + i) * LANES, LANES)] = (num / den).astype(o_ref.dtype)

    def one_sequence_keys_major(sq, m, bias_p, bias_c, sink):
        kv_lanes = pl.ds(m * LANES, LANES)
        kp = kp_ref[sq, :, kv_lanes].astype(BF16)
        kc = kc_ref[sq, :, kv_lanes].astype(BF16)
        vp = vp_ref[sq, :, kv_lanes].astype(BF16)
        vc = vc_ref[sq, :, kv_lanes].astype(BF16)
        q_lo, q_hi = [], []
        for i in range(pairs):
            q_pair = q_ref[sq, :, pl.ds((pairs * m + i) * LANES, LANES)] * scale
            q_lo.append(jnp.where(lo_half, q_pair, 0.0).astype(BF16))
            q_hi.append(jnp.where(lo_half, 0.0, q_pair).astype(BF16))
        qs = jnp.concatenate(q_lo + q_hi, axis=0)
        s_p = _dot_nt(kp, qs) + bias_p
        s_c = _dot_nt(kc, qs) + bias_c
        mx = jnp.maximum(jnp.maximum(jnp.max(s_p, axis=0, keepdims=True), jnp.max(s_c, axis=0, keepdims=True)),
                         sink)
        p_p = jnp.exp(s_p - mx)
        p_c = jnp.exp(s_c - mx)
        den = jnp.sum(p_p, axis=0, keepdims=True) + jnp.sum(p_c, axis=0, keepdims=True) + jnp.exp(sink - mx)
        o_t = (_dot_tn(vp, p_p.astype(BF16)) + _dot_tn(vc, p_c.astype(BF16))) / den
        row_lo = lax.broadcasted_iota(jnp.int32, (LANES, 1), 0) < ATTN_HEAD_DIM
        for i in range(pairs):
            blk = jnp.where(row_lo, o_t[:, i * qb:(i + 1) * qb], o_t[:, (pairs + i) * qb:(pairs + i + 1) * qb])
            o_ref[sq, :, pl.ds((pairs * m + i) * LANES, LANES)] = blk.T.astype(o_ref.dtype)

    for m in range(KV_DIM // LANES):
        heads = [2 * pairs * m + r for r in range(2 * pairs)]
        slopes = [2.0 ** (-8.0 * (h + 1) / N_ATTN_HEADS) for h in heads]
        bias_p = jnp.concatenate([jnp.where(ok_p, -sl * rel_p, NEG_BIG) for sl in slopes], axis=t_ax)
        bias_c = jnp.concatenate([jnp.where(ok_c, -sl * rel_c, NEG_BIG) for sl in slopes], axis=t_ax)
        one = (1, qb) if keys_major else (qb, 1)
        sink = jnp.concatenate([jnp.full(one, sinks_ref[h], F32) for h in heads], axis=t_ax)
        for sq in range(q_ref.shape[0]):
            (one_sequence_keys_major if keys_major else one_sequence)(sq, m, bias_p, bias_c, sink)


def _attention(proj, sinks, qb, prev_kv=None, seqs=1):
    nb, rows, _ = proj.shape
    wb = WINDOW
    cur = lambda col: pl.BlockSpec((seqs, qb, KV_DIM), lambda b, n: (b, n, col // KV_DIM))
    if prev_kv is None:
        prev = lambda col: pl.BlockSpec((seqs, wb, KV_DIM),
                                        lambda b, n: (b, jnp.maximum(n - 1, 0), col // KV_DIM))
        kp, vp = proj, proj
        prev_specs = [prev(COL_K), prev(COL_V)]
    else:
        kp, vp = prev_kv
        prev_specs = [pl.BlockSpec((seqs, wb, KV_DIM), lambda b, n: (b, 0, 0))] * 2
    return pl.pallas_call(
        functools.partial(_attention_kernel, qb, prev_kv is not None),
        grid=(nb // seqs, rows // qb),
        in_specs=[
            pl.BlockSpec(memory_space=pltpu.SMEM),
            pl.BlockSpec((seqs, qb, D_MODEL), lambda b, n: (b, n, COL_Q // D_MODEL)),
            *prev_specs,
            cur(COL_K), cur(COL_V),
        ],
        out_specs=pl.BlockSpec((seqs, qb, D_MODEL), lambda b, n: (b, n, 0)),
        out_shape=jax.ShapeDtypeStruct((nb, rows, D_MODEL), BF16),
        compiler_params=pltpu.CompilerParams(
            dimension_semantics=("parallel", "arbitrary"), vmem_limit_bytes=VMEM_LIMIT),
        name="attention",
    )(sinks, proj, kp, vp, proj, proj)


def _post_mix_kernel(valid_rows, u_ref, a_ref, gs_ref, ga_ref, x_ref, wos_ref, woa_ref, wout_ref, gpm_ref,
                     gpf_ref, g1_ref, sc2_ref, sh2_ref, wr_ref, br_ref, cnt_in_ref,
                     x1_ref, h2_ref, ti_ref, tg_ref, cnt_ref, cnt_scr):
    @pl.when(jnp.logical_and(pl.program_id(0) == 0, pl.program_id(1) == 0))
    def _():
        cnt_scr[...] = cnt_in_ref[...]

    y_ssd = _dot(u_ref[0], wos_ref[...])
    y_attn = _dot(a_ref[0], woa_ref[...])
    merged = jax.nn.sigmoid(gs_ref[0]) * y_ssd + jax.nn.sigmoid(ga_ref[0]) * y_attn
    mixed = _dot(merged.astype(BF16), wout_ref[...])
    x1 = x_ref[0] + g1_ref[0] * (_rms(mixed) * gpm_ref[...])
    x1_ref[0] = x1
    h2 = _rms(x1) * gpf_ref[...] * (1.0 + sc2_ref[0]) + sh2_ref[0]
    _store_token_tiles(h2_ref.at[0], h2)

    h_hi = h2.astype(BF16)
    h_lo = (h2 - h_hi.astype(F32)).astype(BF16)
    w = wr_ref[...]
    w_hi = w.astype(BF16)
    w_lo = (w - w_hi.astype(F32)).astype(BF16)
    logits = _dot(h_hi, w_hi) + (_dot(h_hi, w_lo) + _dot(h_lo, w_hi)) + br_ref[...]
    lane = lax.broadcasted_iota(jnp.int32, logits.shape, 1)
    idx_out = jnp.zeros(logits.shape, jnp.int32)
    val_out = jnp.zeros(logits.shape, F32)
    top = None
    denom = None
    idxs = []
    for k in range(TOP_K):
        m = jnp.max(logits, axis=-1, keepdims=True)
        idx = jnp.min(jnp.where(logits == m, lane, LANES), axis=-1, keepdims=True)
        if k == 0:
            top = m
            e = jnp.ones_like(m)
            denom = e
        else:
            e = jnp.exp(m - top)
            denom = denom + e
        idxs.append(idx)
        idx_out = jnp.where(lane == k, idx, idx_out)
        val_out = jnp.where(lane == k, e, val_out)
        logits = jnp.where(lane == idx, NEG_BIG * 2, logits)
    tg_ref[0] = val_out / denom
    ti_ref[0] = idx_out

    rowi = lax.broadcasted_iota(jnp.int32, (logits.shape[0], 1), 0)
    valid = jnp.bitwise_and(rowi, SAMPLE_ROWS - 1) < valid_rows
    picked = jnp.zeros(logits.shape, F32)
    for idx in idxs:
        picked = picked + jnp.where(jnp.logical_and(lane == idx, valid), 1.0, 0.0)
    cnt_scr[...] = cnt_scr[...] + jnp.sum(picked, axis=0, keepdims=True)
    cnt_ref[...] = cnt_scr[...]


def _post_mix(u, attn, proj, x, wos, woa, wout, gpm, gpf, gate1, scale2, shift2, wr, br, counts, *,
              tm, per_row, valid_rows):
    nb, rows, _ = x.shape
    row_spec = lambda w, col=0: pl.BlockSpec((1, tm, w), lambda b, i: (b, i, col // w))
    full = lambda shape: pl.BlockSpec(shape, lambda b, i: (0,) * len(shape))
    return pl.pallas_call(
        functools.partial(_post_mix_kernel, valid_rows),
        grid=(nb, rows // tm),
        in_specs=[
            row_spec(D_INNER), row_spec(D_MODEL), row_spec(D_MODEL, COL_GS), row_spec(D_MODEL, COL_GA),
            row_spec(D_MODEL),
            full((D_INNER, D_MODEL)), full((D_MODEL, D_MODEL)), full((D_MODEL, D_MODEL)),
            full((1, D_MODEL)), full((1, D_MODEL)),
            _mod_spec(per_row, tm), _mod_spec(per_row, tm), _mod_spec(per_row, tm),
            full((D_MODEL, LANES)), full((1, LANES)), full((1, LANES)),
        ],
        out_specs=[row_spec(D_MODEL), pl.BlockSpec((1, tm * TILE_ROWS, LANES), lambda b, i: (b, i, 0)),
                   row_spec(LANES), row_spec(LANES), full((1, LANES))],
        out_shape=[
            jax.ShapeDtypeStruct((nb, rows, D_MODEL), F32),
            jax.ShapeDtypeStruct((nb, rows * TILE_ROWS, LANES), F32),
            jax.ShapeDtypeStruct((nb, rows, LANES), jnp.int32),
            jax.ShapeDtypeStruct((nb, rows, LANES), F32),
            jax.ShapeDtypeStruct((1, LANES), F32),
        ],
        scratch_shapes=[pltpu.VMEM((1, LANES), F32)],
        compiler_params=pltpu.CompilerParams(
            dimension_semantics=("arbitrary", "arbitrary"), vmem_limit_bytes=MOE_VMEM_LIMIT),
        name="post_mix",
    )(u, attn, proj, proj, x, wos, woa, wout, gpm, gpf, gate1, scale2, shift2, wr, br, counts)


def _moe_kernel(n_tokens, be_ref, na_ref, idx_hbm, h_hbm, wu_ref, bu_ref, wd_ref, bd_ref, ys_hbm,
                idx_smem, xbuf, obuf, x_scr, act_scr, wu_bf, wd_bf, sem_idx, sem_in, sem_out):
    i = pl.program_id(0)
    na = na_ref[0]
    slot = lax.rem(i, 2)
    other = 1 - slot
    ring = lax.rem(i, 3)
    ring_prev = lax.rem(i + 2, 3)
    ring_next = lax.rem(i + 1, 3)
    has_prev = i >= 1
    has_next = i + 1 < na

    def idx_copy(block, s):
        return pltpu.make_async_copy(idx_hbm.at[block], idx_smem.at[s], sem_idx.at[s])

    block_rows = MOE_ROWS * TILE_ROWS

    def tile(ref, first_row):
        return ref.at[pl.ds(pl.multiple_of(first_row, TILE_ROWS), TILE_ROWS)]

    def start_gather(s, idx_slot, rows):
        for r in rows:
            pltpu.make_async_copy(tile(h_hbm, idx_smem[idx_slot, r]),
                                  xbuf.at[s, pl.ds(r * TILE_ROWS, TILE_ROWS)], sem_in.at[s]).start()

    def wait_gather(s):
        pltpu.make_async_copy(h_hbm.at[pl.ds(0, block_rows)], xbuf.at[s], sem_in.at[s]).wait()

    def start_scatter(s, idx_slot, rows):
        for r in rows:
            pltpu.make_async_copy(obuf.at[s, pl.ds(r * TILE_ROWS, TILE_ROWS)],
                                  tile(ys_hbm, idx_smem[idx_slot, MOE_ROWS + r]), sem_out.at[s]).start()

    def wait_scatter(s):
        pltpu.make_async_copy(obuf.at[s], ys_hbm.at[pl.ds(0, block_rows)], sem_out.at[s]).wait()

    all_rows = range(MOE_ROWS)

    @pl.when(i == 0)
    def _():
        obuf[0] = jnp.zeros((block_rows, LANES), F32)
        plane_rows = ys_hbm.shape[0] // (TOP_K * TILE_ROWS)
        spare = [pltpu.make_async_copy(
            obuf.at[0], ys_hbm.at[pl.ds((k * plane_rows + n_tokens + hf * MOE_ROWS) * TILE_ROWS, block_rows)],
            sem_out.at[0]) for k in range(TOP_K) for hf in range(2)]
        for cp in spare:
            cp.start()
        for cp in spare:
            cp.wait()
        first = idx_copy(0, 0)
        first.start()
        first.wait()
        start_gather(0, 0, all_rows)

        @pl.when(na > 1)
        def _():
            idx_copy(1, 1).start()

    @pl.when(i < na)
    def _():
        wait_gather(slot)

        @pl.when(has_next)
        def _():
            idx_copy(i + 1, ring_next).wait()

        @pl.when(i >= 2)
        def _():
            wait_scatter(slot)

        @pl.when(jnp.logical_or(i == 0, be_ref[i] != be_ref[jnp.maximum(i - 1, 0)]))
        def _():
            wu_bf[...] = wu_ref[0].astype(BF16)
            wd_bf[...] = wd_ref[0].astype(BF16)

        piece = 2 * LANES
        n_up, n_down = D_FF // piece, D_MODEL // piece
        per_group = MOE_ROWS // (n_up + n_down)

        def neighbour_rows(g):
            rows = range(g * per_group, (g + 1) * per_group)
            pl.when(has_next)(lambda: start_gather(other, ring_next, rows))
            pl.when(has_prev)(lambda: start_scatter(other, ring_prev, rows))

        x_scr[...] = _load_token_tiles(xbuf.at[slot], MOE_ROWS).astype(BF16)
        for c in range(n_up):
            cols = pl.ds(c * piece, piece)
            lin_cols = pl.ds(D_FF + c * piece, piece)
            glu = jnp.minimum(_dot(x_scr[...], wu_bf[:, cols]) + bu_ref[0, :, cols], SWIGLU_LIMIT)
            lin = jnp.clip(_dot(x_scr[...], wu_bf[:, lin_cols]) + bu_ref[0, :, lin_cols],
                           -SWIGLU_LIMIT, SWIGLU_LIMIT)
            act_scr[:, cols] = (glu * jax.nn.sigmoid(SWIGLU_ALPHA * glu) * (lin + 1.0)).astype(BF16)
            neighbour_rows(c)
        out_ref = obuf.at[slot]
        for c in range(n_down):
            cols = pl.ds(c * piece, piece)
            y = _dot(act_scr[...], wd_bf[:, cols]) + bd_ref[0, :, cols]
            for j in range(piece // LANES):
                out_ref[pl.ds(c * (piece // LANES) + j, MOE_ROWS, stride=TILE_ROWS), :] = (
                    y[:, j * LANES:(j + 1) * LANES])
            neighbour_rows(n_up + c)

        @pl.when(i + 2 < na)
        def _():
            idx_copy(i + 2, ring_prev).start()

        @pl.when(i == na - 1)
        def _():
            start_scatter(slot, ring, all_rows)
            wait_scatter(slot)

            @pl.when(has_prev)
            def _():
                wait_scatter(other)


def _moe(idx, h_tiles, ys_rows, block_expert, n_active, wu, bu, wd, bd):
    n_blocks = idx.shape[0]
    grid_spec = pltpu.PrefetchScalarGridSpec(
        num_scalar_prefetch=2,
        grid=(n_blocks,),
        in_specs=[
            pl.BlockSpec(memory_space=pl.ANY),
            pl.BlockSpec(memory_space=pl.ANY),
            pl.BlockSpec((1, D_MODEL, 2 * D_FF), lambda i, be, na: (be[i], 0, 0)),
            pl.BlockSpec((1, 1, 2 * D_FF), lambda i, be, na: (be[i], 0, 0)),
            pl.BlockSpec((1, D_FF, D_MODEL), lambda i, be, na: (be[i], 0, 0)),
            pl.BlockSpec((1, 1, D_MODEL), lambda i, be, na: (be[i], 0, 0)),
        ],
        out_specs=pl.BlockSpec(memory_space=pl.ANY),
        scratch_shapes=[
            pltpu.SMEM((3, 2 * MOE_ROWS), jnp.int32),
            pltpu.VMEM((2, MOE_ROWS * TILE_ROWS, LANES), F32),
            pltpu.VMEM((2, MOE_ROWS * TILE_ROWS, LANES), F32),
            pltpu.VMEM((MOE_ROWS, D_MODEL), BF16),
            pltpu.VMEM((MOE_ROWS, D_FF), BF16),
            pltpu.VMEM((D_MODEL, 2 * D_FF), BF16),
            pltpu.VMEM((D_FF, D_MODEL), BF16),
            pltpu.SemaphoreType.DMA((3,)),
            pltpu.SemaphoreType.DMA((2,)),
            pltpu.SemaphoreType.DMA((2,)),
        ],
    )
    return pl.pallas_call(
        functools.partial(_moe_kernel, h_tiles.shape[0] // TILE_ROWS),
        grid_spec=grid_spec,
        out_shape=jax.ShapeDtypeStruct((ys_rows * TILE_ROWS, LANES), F32),
        compiler_params=pltpu.CompilerParams(
            dimension_semantics=("arbitrary",), vmem_limit_bytes=MOE_VMEM_LIMIT),
        name="moe",
    )(block_expert, n_active, idx, h_tiles, wu, bu, wd, bd)


def _combine_kernel(ys_ref, tg_ref, x1_ref, gpost_ref, g2_ref, o_ref):
    gates = tg_ref[0]
    tm = gates.shape[0]
    f = gates[:, 0:1] * _load_token_tiles(ys_ref.at[0], tm)
    for k in range(1, TOP_K):
        f = f + gates[:, k:k + 1] * _load_token_tiles(ys_ref.at[k], tm)
    o_ref[0] = x1_ref[0] + g2_ref[0] * (_rms(f) * gpost_ref[...])


def _combine(ys, row0, tg, x1, gpost, gate2, *, tm, per_row):
    nb, rows, _ = x1.shape
    per_b = rows // tm
    base = row0 // tm
    return pl.pallas_call(
        _combine_kernel,
        grid=(nb, per_b),
        in_specs=[
            pl.BlockSpec((TOP_K, tm * TILE_ROWS, LANES), lambda b, i: (0, base + b * per_b + i, 0)),
            pl.BlockSpec((1, tm, LANES), lambda b, i: (b, i, 0)),
            pl.BlockSpec((1, tm, D_MODEL), lambda b, i: (b, i, 0)),
            pl.BlockSpec((1, D_MODEL), lambda b, i: (0, 0)),
            _mod_spec(per_row, tm),
        ],
        out_specs=pl.BlockSpec((1, tm, D_MODEL), lambda b, i: (b, i, 0)),
        out_shape=jax.ShapeDtypeStruct((nb, rows, D_MODEL), F32),
        compiler_params=pltpu.CompilerParams(
            dimension_semantics=("parallel", "parallel"), vmem_limit_bytes=VMEM_LIMIT),
        name="combine",
    )(ys, tg, x1, gpost, gate2)


def _route(top_idx, counts, plane_rows):
    n_tokens = top_idx.shape[0]
    n_slots = n_tokens * TOP_K
    experts = jnp.arange(N_EXPERTS, dtype=jnp.int32)
    padded = (counts + MOE_ROWS - 1) // MOE_ROWS * MOE_ROWS
    group_start = jnp.cumsum(counts) - counts
    padded_end = jnp.cumsum(padded)
    padded_start = padded_end - padded
    n_blocks = -(-n_slots // MOE_ROWS) + N_EXPERTS
    n_active = (padded_end[-1] // MOE_ROWS).astype(jnp.int32)
    block_start = jnp.arange(n_blocks, dtype=jnp.int32) * MOE_ROWS
    block_expert = jnp.sum(block_start[:, None] >= padded_end[None, :], axis=1)
    last_expert = jnp.max(jnp.where(counts > 0, experts, 0))
    block_expert = jnp.where(block_start < padded_end[-1], block_expert, last_expert).astype(jnp.int32)
    slot_id = jnp.arange(n_slots, dtype=jnp.int32).reshape(n_tokens, TOP_K)
    keys = jnp.sort((top_idx * n_slots + slot_id).reshape(-1))
    row = jnp.arange(n_blocks * MOE_ROWS, dtype=jnp.int32).reshape(n_blocks, MOE_ROWS)
    src = row + (group_start - padded_start)[block_expert][:, None]
    real = row < (padded_start + counts)[block_expert][:, None]
    slot = keys[jnp.clip(src, 0, n_slots - 1).reshape(-1)].reshape(n_blocks, MOE_ROWS) % n_slots
    token = slot // TOP_K
    spare = n_tokens + (jnp.arange(n_blocks, dtype=jnp.int32) % 2)[:, None] * MOE_ROWS + row % MOE_ROWS
    dest = jnp.where(real, (slot % TOP_K) * plane_rows + token, spare)
    idx = (jnp.concatenate([token, dest], axis=1) * TILE_ROWS).astype(jnp.int32)
    return idx, block_expert, n_active.reshape(1)


def _pad_lanes(v, value=0.0):
    return jnp.pad(v, [(0, 0)] * (v.ndim - 1) + [(0, LANES - v.shape[-1])], constant_values=value)


def kernel(x_prompt, x_sample, c_prompt, c_sample, cache_swa_k, cache_swa_v, state_conv, state_ssm, w_ada, b_ada, g_pre_mix, g_post_mix, g_pre_ffn, g_post_ffn, w_in, conv_w, conv_b, dt_bias, a_log, d_skip, g_ssm_norm, sinks, w_o_ssd, w_o_attn, w_out, w_router, b_router, w_up, b_up, w_down, b_down):
    depth = w_ada.shape[0]
    n_prompt, seq, _ = x_prompt.shape
    n_sample, dec_seq, _ = x_sample.shape
    yp = x_prompt
    ys_pad = jnp.pad(x_sample, ((0, 0), (0, SAMPLE_ROWS - dec_seq), (0, 0)))
    outs = [[] for _ in range(8)]
    expand = jnp.repeat(jnp.eye(LANES, N_SSD_HEADS, dtype=BF16), SSD_HEAD_DIM, axis=1)
    rows_s = n_sample * SAMPLE_ROWS
    n_p = n_prompt * seq
    n_s = n_sample * dec_seq
    n_tok = n_p + n_s
    tm_p = min(512, seq)
    tm_s = min(512, rows_s)
    tm_mix_p = min(512, seq)
    tm_mix_s = min(256, rows_s)
    assert seq % CHUNK == 0 and seq % tm_p == 0 and rows_s % tm_s == 0 and n_p % n_s == 0
    c_all = jnp.concatenate([c_prompt, c_sample], axis=0)
    c_rows = -(-c_all.shape[0] // SUBLANES) * SUBLANES
    c_all = jnp.pad(c_all, ((0, c_rows - c_all.shape[0]), (0, 0)))

    for l in range(depth):
        wi = w_in[l]
        o_xbc = D_INNER
        o_dt = o_xbc + CONV_DIM
        o_q = o_dt + N_SSD_HEADS
        o_k = o_q + D_MODEL
        o_v = o_k + KV_DIM
        o_gs = o_v + KV_DIM
        o_ga = o_gs + D_MODEL
        head_order = jnp.array(ATTN_HEAD_ORDER)
        w_q = wi[:, o_q:o_k].reshape(D_MODEL, N_ATTN_HEADS, ATTN_HEAD_DIM)[:, head_order].reshape(D_MODEL, D_MODEL)
        w_proj = jnp.concatenate([
            wi[:, :o_xbc], wi[:, o_xbc:o_xbc + D_INNER], w_q, wi[:, o_gs:o_ga], wi[:, o_ga:],
            wi[:, o_xbc + D_INNER:o_dt], wi[:, o_k:o_v], wi[:, o_v:o_gs], _pad_lanes(wi[:, o_dt:o_q])],
            axis=1).astype(BF16)
        cwx, cwbc = conv_w[l][:, :D_INNER], conv_w[l][:, D_INNER:]
        cbx, cbbc = conv_b[l][None, :D_INNER], conv_b[l][None, D_INNER:]
        dtb = _pad_lanes(dt_bias[l][None])
        alog = _pad_lanes(a_log[l][None])
        dskip = jnp.repeat(d_skip[l], SSD_HEAD_DIM)[None]
        gn = g_ssm_norm[l][None]
        wos, wout = w_o_ssd[l].astype(BF16), w_out[l].astype(BF16)
        woa = w_o_attn[l].reshape(N_ATTN_HEADS, ATTN_HEAD_DIM, D_MODEL)[head_order].reshape(D_MODEL, D_MODEL)
        woa = woa.astype(BF16)
        zero_counts = jnp.zeros((1, LANES), F32)
        wr = _pad_lanes(w_router[l])
        br = _pad_lanes(b_router[l][None], NEG_BIG)
        wu, wd = w_up[l], w_down[l]
        bu, bd = b_up[l][:, None, :], b_down[l][:, None, :]
        g_pm, g_pom, g_pf, g_pof = (v[l][None] for v in (g_pre_mix, g_post_mix, g_pre_ffn, g_post_ffn))

        ada = _ada(c_all, w_ada[l], b_ada[l][None])
        ada_p = [ada[:n_prompt, k * D_MODEL:(k + 1) * D_MODEL] for k in range(6)]
        ada_s = [ada[n_prompt:n_prompt + n_sample, k * D_MODEL:(k + 1) * D_MODEL] for k in range(6)]

        mods = [m[:, None, :] for m in ada_p]
        proj = _in_proj(yp, g_pm, mods[1], mods[0], w_proj, tm=min(1024, seq), per_row=False)
        u, ssm_p = _ssd(proj, cwx, cbx, cwbc, cbbc, dtb, alog, dskip, gn)
        attn = _attention(proj, sinks[l], WINDOW)
        x1_p, h2_p, ti_p, tg_p, cnt_p = _post_mix(
            u, attn, proj, yp, wos, woa, wout, g_pom, g_pf, mods[2], mods[4], mods[3], wr, br, zero_counts,
            tm=tm_mix_p, per_row=False, valid_rows=SAMPLE_ROWS)
        gate2_p = mods[5]
        kp = proj[:, seq - WINDOW:, COL_K:COL_K + KV_DIM].reshape(n_prompt, WINDOW, N_KV_HEADS, ATTN_HEAD_DIM)
        vp = proj[:, seq - WINDOW:, COL_V:COL_V + KV_DIM].reshape(n_prompt, WINDOW, N_KV_HEADS, ATTN_HEAD_DIM)
        cp = jnp.concatenate([proj[:, seq - (CONV_W - 1):, COL_X:COL_X + D_INNER],
                              proj[:, seq - (CONV_W - 1):, COL_BC:COL_BC + BC_DIM]], axis=-1)

        mods_s = [jnp.repeat(m, SAMPLE_ROWS, axis=0)[None] for m in ada_s]
        xs_flat = ys_pad.reshape(1, rows_s, D_MODEL)
        proj_s = _in_proj(xs_flat, g_pm, mods_s[1], mods_s[0], w_proj, tm=tm_s, per_row=True)
        proj_sb = proj_s.reshape(n_sample, SAMPLE_ROWS, PROJ_DIM)
        u_s, ssm_s = _ssd_step(
            proj_sb, state_conv[l][:, :, :D_INNER], state_conv[l][:, :, D_INNER:],
            state_ssm[l].reshape(n_sample, D_INNER, D_STATE),
            cwx, cbx, cwbc, cbbc, dtb, alog, dskip, gn, expand, dec_seq)
        k_prev = cache_swa_k[l].reshape(n_sample, -1, KV_DIM)
        v_prev = cache_swa_v[l].reshape(n_sample, -1, KV_DIM)
        attn_s = _attention(proj_sb, sinks[l], SAMPLE_ROWS, prev_kv=(k_prev, v_prev),
                            seqs=math.gcd(n_sample, 8))
        x1_s, h2_s, ti_s, tg_s, cnt_all = _post_mix(
            u_s.reshape(1, rows_s, D_INNER), attn_s.reshape(1, rows_s, D_MODEL), proj_s, xs_flat,
            wos, woa, wout, g_pom, g_pf, mods_s[2], mods_s[4], mods_s[3], wr, br, cnt_p,
            tm=tm_mix_s, per_row=True, valid_rows=dec_seq)
        wb = k_prev.shape[1]
        k_new = proj_sb[:, :dec_seq, COL_K:COL_K + KV_DIM]
        v_new = proj_sb[:, :dec_seq, COL_V:COL_V + KV_DIM]
        ks = jnp.concatenate([k_prev, k_new], axis=1)[:, -wb:].reshape(n_sample, wb, N_KV_HEADS, ATTN_HEAD_DIM)
        vs = jnp.concatenate([v_prev, v_new], axis=1)[:, -wb:].reshape(n_sample, wb, N_KV_HEADS, ATTN_HEAD_DIM)
        raw_xbc = jnp.concatenate([proj_sb[:, :dec_seq, COL_X:COL_X + D_INNER],
                                   proj_sb[:, :dec_seq, COL_BC:COL_BC + BC_DIM]], axis=-1)
        cs = jnp.concatenate([state_conv[l], raw_xbc], axis=1)[:, -(CONV_W - 1):]

        valid = lambda v: v.reshape(n_sample, SAMPLE_ROWS, -1)[:, :dec_seq].reshape(n_s, -1)
        h2_sv = h2_s.reshape(n_sample, SAMPLE_ROWS * TILE_ROWS, LANES)[:, :dec_seq * TILE_ROWS]
        h2_all = jnp.concatenate([h2_p.reshape(n_p * TILE_ROWS, LANES),
                                  h2_sv.reshape(n_s * TILE_ROWS, LANES)], axis=0)
        ti_all = jnp.concatenate([ti_p.reshape(n_p, LANES), valid(ti_s)], axis=0)
        counts = cnt_all[0, :N_EXPERTS].astype(jnp.int32)
        plane_rows = n_tok + 2 * MOE_ROWS
        idx, block_expert, n_active = _route(ti_all[:, :TOP_K], counts, plane_rows)
        ysel = _moe(idx, h2_all, TOP_K * plane_rows, block_expert, n_active, wu, bu, wd, bd)
        ysel = ysel.reshape(TOP_K, plane_rows * TILE_ROWS, LANES)

        yp = _combine(ysel, 0, tg_p, x1_p, g_pof, gate2_p, tm=tm_p, per_row=False)
        x1_sv = valid(x1_s)[None]
        tg_sv = valid(tg_s)[None]
        gate2_s = jnp.repeat(ada_s[5], dec_seq, axis=0)[None]
        ys_new = _combine(ysel, n_p, tg_sv, x1_sv, g_pof, gate2_s, tm=n_s, per_row=True)
        ys_new = ys_new.reshape(n_sample, dec_seq, D_MODEL)
        ys_pad = jnp.pad(ys_new, ((0, 0), (0, SAMPLE_ROWS - dec_seq), (0, 0)))

        for lst, v in zip(outs, (kp, vp, cp, ssm_p, ks, vs, cs,
                                 ssm_s.reshape(n_sample, N_SSD_HEADS, SSD_HEAD_DIM, D_STATE))):
            lst.append(v)

    return (yp, ys_pad[:, :dec_seq], *[jnp.stack(v) for v in outs])
```

```python
import functools
import math

import jax
import jax.numpy as jnp
from jax import lax
from jax.experimental import pallas as pl
from jax.experimental.pallas import tpu as pltpu

F32 = jnp.float32
BF16 = jnp.bfloat16

D_MODEL = 1024
D_INNER = 2 * D_MODEL
SSD_HEAD_DIM = 64
N_SSD_HEADS = D_INNER // SSD_HEAD_DIM
N_SSD_GROUPS = 4
HEADS_PER_GROUP = N_SSD_HEADS // N_SSD_GROUPS
D_STATE = 128
CONV_W = 4
BC_DIM = 2 * N_SSD_GROUPS * D_STATE
CONV_DIM = D_INNER + BC_DIM
CHUNK = 128
ATTN_HEAD_DIM = 64
N_ATTN_HEADS = D_MODEL // ATTN_HEAD_DIM
N_KV_HEADS = 4
KV_REP = N_ATTN_HEADS // N_KV_HEADS
KV_DIM = N_KV_HEADS * ATTN_HEAD_DIM
WINDOW = 128
N_EXPERTS = 32
TOP_K = 4
D_FF = D_MODEL
SWIGLU_LIMIT = 7.0
SWIGLU_ALPHA = 1.702
EPS = 1e-6

LANES = 128
SUBLANES = 8
NEG_BIG = -1e30

COL_Z = 0
COL_X = COL_Z + D_INNER
COL_Q = COL_X + D_INNER
COL_GS = COL_Q + D_MODEL
COL_GA = COL_GS + D_MODEL
COL_BC = COL_GA + D_MODEL
COL_K = COL_BC + BC_DIM
COL_V = COL_K + KV_DIM
COL_DT = COL_V + KV_DIM
PROJ_DIM = COL_DT + LANES
PROJ_TILE_N = PROJ_DIM // 3

ATTN_HEAD_ORDER = tuple(2 * KV_REP * (j // KV_REP) + (j % KV_REP) + KV_REP * hf
                        for j in range(N_ATTN_HEADS // 2) for hf in (0, 1))

TILE_ROWS = D_MODEL // LANES
assert TILE_ROWS == SUBLANES

SAMPLE_ROWS = 8
MOE_ROWS = 512
MOE_GROUP = 64
VMEM_LIMIT = 48 * 1024 * 1024
MOE_VMEM_LIMIT = 56 * 1024 * 1024


def _silu(v):
    return v * jax.nn.sigmoid(v)


def _softplus(v):
    return jnp.maximum(v, 0.0) + jnp.log(1.0 + jnp.exp(-jnp.abs(v)))


def _rms(v):
    return v * lax.rsqrt(jnp.mean(v * v, axis=-1, keepdims=True) + EPS)


def _split3(v):
    hi = v.astype(BF16)
    r1 = v - hi.astype(F32)
    mid = r1.astype(BF16)
    lo = (r1 - mid.astype(F32)).astype(BF16)
    return hi, mid, lo


def _store_token_tiles(ref, rows):
    n = rows.shape[0]
    for j in range(TILE_ROWS):
        ref[pl.ds(j, n, stride=TILE_ROWS), :] = rows[:, j * LANES:(j + 1) * LANES]


def _load_token_tiles(ref, n):
    return jnp.concatenate([ref[pl.ds(j, n, stride=TILE_ROWS), :] for j in range(TILE_ROWS)], axis=1)


def _dot(a, b):
    return jnp.dot(a, b, preferred_element_type=F32)


def _dot_nt(a, b):
    return lax.dot_general(a, b, (((1,), (1,)), ((), ())), preferred_element_type=F32)


def _dot_tn(a, b):
    return lax.dot_general(a, b, (((0,), (0,)), ((), ())), preferred_element_type=F32)


def _dot_exact_rhs(a, b_bf16):
    hi, mid, lo = _split3(a)
    return _dot(hi, b_bf16) + _dot(mid, b_bf16) + _dot(lo, b_bf16)


def _ada_kernel(c_ref, w_ref, b_ref, o_ref):
    o_ref[...] = _dot(_silu(c_ref[...]).astype(BF16), w_ref[...].astype(BF16)) + b_ref[...]


def _ada(c, w, b):
    rows, n = c.shape[0], w.shape[1]
    tn = n // 4
    return pl.pallas_call(
        _ada_kernel,
        grid=(n // tn,),
        in_specs=[
            pl.BlockSpec((rows, D_MODEL), lambda j: (0, 0)),
            pl.BlockSpec((D_MODEL, tn), lambda j: (0, j)),
            pl.BlockSpec((1, tn), lambda j: (0, j)),
        ],
        out_specs=pl.BlockSpec((rows, tn), lambda j: (0, j)),
        out_shape=jax.ShapeDtypeStruct((rows, n), F32),
        compiler_params=pltpu.CompilerParams(
            dimension_semantics=("parallel",), vmem_limit_bytes=VMEM_LIMIT),
        name="ada",
    )(c, w, b)


def _in_proj_kernel(x_ref, g_ref, sc_ref, sh_ref, w_ref, o_ref, h_scr):
    @pl.when(pl.program_id(2) == 0)
    def _():
        h = _rms(x_ref[0]) * g_ref[...] * (1.0 + sc_ref[0]) + sh_ref[0]
        h_scr[...] = h.astype(BF16)

    o_ref[0] = _dot(h_scr[...], w_ref[...])


def _mod_spec(per_row, tm):
    if per_row:
        return pl.BlockSpec((1, tm, D_MODEL), lambda b, i, *_: (b, i, 0))
    return pl.BlockSpec((1, 1, D_MODEL), lambda b, i, *_: (b, 0, 0))


def _in_proj(x, g, scale, shift, w, *, tm, per_row):
    nb, rows, _ = x.shape
    return pl.pallas_call(
        _in_proj_kernel,
        grid=(nb, rows // tm, PROJ_DIM // PROJ_TILE_N),
        in_specs=[
            pl.BlockSpec((1, tm, D_MODEL), lambda b, i, j: (b, i, 0)),
            pl.BlockSpec((1, D_MODEL), lambda b, i, j: (0, 0)),
            _mod_spec(per_row, tm),
            _mod_spec(per_row, tm),
            pl.BlockSpec((D_MODEL, PROJ_TILE_N), lambda b, i, j: (0, j)),
        ],
        out_specs=pl.BlockSpec((1, tm, PROJ_TILE_N), lambda b, i, j: (b, i, j)),
        out_shape=jax.ShapeDtypeStruct((nb, rows, PROJ_DIM), F32),
        scratch_shapes=[pltpu.VMEM((tm, D_MODEL), BF16)],
        compiler_params=pltpu.CompilerParams(
            dimension_semantics=("parallel", "parallel", "arbitrary"),
            vmem_limit_bytes=MOE_VMEM_LIMIT),
        name="in_proj",
    )(x, g, scale, shift, w)


def _causal_conv(ext_ref, raw, prev, w_ref, b_ref, rows):
    base = SUBLANES
    if prev is not None:
        ext_ref[pl.ds(base - (CONV_W - 1), CONV_W - 1), :] = prev
    ext_ref[pl.ds(base, rows), :] = raw
    out = b_ref[...] + raw * w_ref[pl.ds(CONV_W - 1, 1), :]
    for k in range(CONV_W - 1):
        out = out + ext_ref[pl.ds(base - (CONV_W - 1) + k, rows), :] * w_ref[pl.ds(k, 1), :]
    return out


def _gated_group_norm(y, z, g_ref):
    u = y * _silu(z)
    gw = D_INNER // N_SSD_GROUPS
    parts = []
    for g in range(N_SSD_GROUPS):
        ug = u[:, g * gw:(g + 1) * gw]
        parts.append(ug * lax.rsqrt(jnp.mean(ug * ug, axis=-1, keepdims=True) + EPS))
    return jnp.concatenate(parts, axis=-1) * g_ref[...]


def _ssd_kernel(z_ref, x_ref, bc_ref, dt_ref, cwx_ref, cbx_ref, cwbc_ref, cbbc_ref, dtb_ref, alog_ref,
                dskip_ref, gn_ref, u_ref, fin_ref, extx_scr, extbc_scr, st_scr, y_scr):
    q = CHUNK
    c = pl.program_id(1)

    @pl.when(c == 0)
    def _():
        extx_scr[pl.ds(0, SUBLANES), :] = jnp.zeros((SUBLANES, D_INNER), F32)
        extbc_scr[pl.ds(0, SUBLANES), :] = jnp.zeros((SUBLANES, BC_DIM), F32)
        st_scr[...] = jnp.zeros_like(st_scr)

    xs = _silu(_causal_conv(extx_scr, x_ref[0], None, cwx_ref, cbx_ref, q))
    bc = _silu(_causal_conv(extbc_scr, bc_ref[0], None, cwbc_ref, cbbc_ref, q))
    tail = SUBLANES + q - (CONV_W - 1)
    extx_scr[pl.ds(SUBLANES - (CONV_W - 1), CONV_W - 1), :] = extx_scr[pl.ds(tail, CONV_W - 1), :]
    extbc_scr[pl.ds(SUBLANES - (CONV_W - 1), CONV_W - 1), :] = extbc_scr[pl.ds(tail, CONV_W - 1), :]

    dt = _softplus(dt_ref[0] + dtb_ref[...])
    adt = dt * (-jnp.exp(alog_ref[...]))
    row = lax.broadcasted_iota(jnp.int32, (q, q), 0)
    col = lax.broadcasted_iota(jnp.int32, (q, q), 1)
    causal = row >= col
    tri = jnp.where(causal, 1.0, 0.0).astype(BF16)
    h3, m3, l3 = _split3(adt)
    acs = _dot(tri, h3) + _dot(tri, m3) + _dot(tri, l3)
    eacs = jnp.exp(acs)
    acs_t = acs.T
    dt_t = dt.T
    lane = lax.broadcasted_iota(jnp.int32, (1, LANES), 1)
    lo_half = lane < SSD_HEAD_DIM

    for g in range(N_SSD_GROUPS):
        bm = bc[:, g * D_STATE:(g + 1) * D_STATE]
        cm = bc[:, (N_SSD_GROUPS + g) * D_STATE:(N_SSD_GROUPS + g + 1) * D_STATE]
        cb = jnp.where(causal, _dot_nt(cm.astype(BF16), bm.astype(BF16)), 0.0)
        bm_t = bm.T
        for pr in range(HEADS_PER_GROUP // 2):
            h0 = g * HEADS_PER_GROUP + 2 * pr
            lanes = pl.ds(h0 * SSD_HEAD_DIM, LANES)
            x_pair = xs[:, h0 * SSD_HEAD_DIM:h0 * SSD_HEAD_DIM + LANES]
            st_pair = st_scr[:, lanes]
            lhs_y, lhs_s, decs = [], [], []
            for h in (h0, h0 + 1):
                a_col = acs[:, h:h + 1]
                a_row = acs_t[h:h + 1, :]
                dt_row = dt_t[h:h + 1, :]
                a_last = acs_t[h:h + 1, q - 1:q]
                decay = jnp.exp(jnp.minimum(a_col - a_row, 0.0))
                lhs_y.append((cb * decay * dt_row).astype(BF16))
                lhs_s.append((bm_t * (dt_row * jnp.exp(a_last - a_row))).astype(BF16))
                decs.append(jnp.exp(a_last))
            for h in (h0, h0 + 1):
                lhs_y.append((cm * eacs[:, h:h + 1]).astype(BF16))
            x_top = jnp.where(lo_half, x_pair, 0.0).astype(BF16)
            x_bot = jnp.where(lo_half, 0.0, x_pair).astype(BF16)
            s_top = jnp.where(lo_half, st_pair, 0.0).astype(BF16)
            s_bot = jnp.where(lo_half, 0.0, st_pair).astype(BF16)
            x_bd = jnp.concatenate([x_top, x_bot], axis=0)
            rhs_y = jnp.concatenate([x_bd, s_top, s_bot], axis=0)
            y_pair = _dot(jnp.concatenate(lhs_y, axis=1), rhs_y)
            ds_pair = _dot(jnp.concatenate(lhs_s, axis=1), x_bd)
            dskip = dskip_ref[:, lanes]
            y_scr[:, lanes] = y_pair + dskip * x_pair
            st_scr[:, lanes] = st_pair * jnp.where(lo_half, decs[0], decs[1]) + ds_pair

    u_ref[0] = _gated_group_norm(y_scr[...], z_ref[0], gn_ref).astype(u_ref.dtype)

    @pl.when(c == pl.num_programs(1) - 1)
    def _():
        for pr in range(N_SSD_HEADS // 2):
            t = st_scr[:, pl.ds(pr * LANES, LANES)].T
            fin_ref[0, 2 * pr] = t[:SSD_HEAD_DIM]
            fin_ref[0, 2 * pr + 1] = t[SSD_HEAD_DIM:]


def _ssd(proj, cwx, cbx, cwbc, cbbc, dtb, alog, dskip, gn):
    nb, rows, _ = proj.shape
    q = CHUNK
    full = lambda shape: pl.BlockSpec(shape, lambda b, c: (0,) * len(shape))
    return pl.pallas_call(
        _ssd_kernel,
        grid=(nb, rows // q),
        in_specs=[
            pl.BlockSpec((1, q, D_INNER), lambda b, c: (b, c, COL_Z // D_INNER)),
            pl.BlockSpec((1, q, D_INNER), lambda b, c: (b, c, COL_X // D_INNER)),
            pl.BlockSpec((1, q, BC_DIM), lambda b, c: (b, c, COL_BC // BC_DIM)),
            pl.BlockSpec((1, q, LANES), lambda b, c: (b, c, COL_DT // LANES)),
            full((CONV_W, D_INNER)), full((1, D_INNER)), full((CONV_W, BC_DIM)), full((1, BC_DIM)),
            full((1, LANES)), full((1, LANES)), full((1, D_INNER)), full((1, D_INNER)),
        ],
        out_specs=[
            pl.BlockSpec((1, q, D_INNER), lambda b, c: (b, c, 0)),
            pl.BlockSpec((1, N_SSD_HEADS, SSD_HEAD_DIM, D_STATE), lambda b, c: (b, 0, 0, 0)),
        ],
        out_shape=[
            jax.ShapeDtypeStruct((nb, rows, D_INNER), BF16),
            jax.ShapeDtypeStruct((nb, N_SSD_HEADS, SSD_HEAD_DIM, D_STATE), F32),
        ],
        scratch_shapes=[
            pltpu.VMEM((SUBLANES + q, D_INNER), F32),
            pltpu.VMEM((SUBLANES + q, BC_DIM), F32),
            pltpu.VMEM((D_STATE, D_INNER), F32),
            pltpu.VMEM((q, D_INNER), F32),
        ],
        compiler_params=pltpu.CompilerParams(
            dimension_semantics=("parallel", "arbitrary"), vmem_limit_bytes=VMEM_LIMIT),
        name="ssd",
    )(proj, proj, proj, proj, cwx, cbx, cwbc, cbbc, dtb, alog, dskip, gn)


def _ssd_step_kernel(n_valid, z_ref, x_ref, bc_ref, dt_ref, px_ref, pbc_ref, st_ref, cwx_ref, cbx_ref,
                     cwbc_ref, cbbc_ref, dtb_ref, alog_ref, dskip_ref, gn_ref, exp_ref,
                     u_ref, nst_ref, extx_scr, extbc_scr):
    for sq in range(z_ref.shape[0]):
        _ssd_step_one(n_valid, z_ref.at[sq], x_ref.at[sq], bc_ref.at[sq], dt_ref.at[sq], px_ref.at[sq],
                      pbc_ref.at[sq], st_ref.at[sq], cwx_ref, cbx_ref, cwbc_ref, cbbc_ref, dtb_ref, alog_ref,
                      dskip_ref, gn_ref, exp_ref, u_ref.at[sq], nst_ref.at[sq], extx_scr.at[sq], extbc_scr.at[sq])


def _ssd_step_one(n_valid, z_ref, x_ref, bc_ref, dt_ref, px_ref, pbc_ref, st_ref, cwx_ref, cbx_ref,
                  cwbc_ref, cbbc_ref, dtb_ref, alog_ref, dskip_ref, gn_ref, exp_ref,
                  u_ref, nst_ref, extx_scr, extbc_scr):
    q = SAMPLE_ROWS
    gw = D_INNER // N_SSD_GROUPS
    xs = _silu(_causal_conv(extx_scr, x_ref[...], px_ref[...], cwx_ref, cbx_ref, q))
    bc = _silu(_causal_conv(extbc_scr, bc_ref[...], pbc_ref[...], cwbc_ref, cbbc_ref, q))

    rowi = lax.broadcasted_iota(jnp.int32, (q, 1), 0)
    dt = jnp.where(rowi < n_valid, _softplus(dt_ref[...] + dtb_ref[...]), 0.0)
    adt = dt * (-jnp.exp(alog_ref[...]))
    acs = jnp.zeros_like(adt)
    for s in range(n_valid):
        acs = acs + jnp.where(rowi >= s, adt[s:s + 1, :], 0.0)
    expand = exp_ref[...]
    dt_e = _dot_exact_rhs(dt, expand)
    acs_e = _dot_exact_rhs(acs, expand)
    last_e = acs_e[q - 1:q, :]

    x_end = xs * dt_e * jnp.exp(last_e - acs_e)
    dec3 = _split3(jnp.exp(last_e))
    zrow = jnp.zeros((q - 3, D_INNER), BF16)
    dec_rows = jnp.concatenate([dec3[0], dec3[1], dec3[2], zrow], axis=0)
    ones = jnp.ones((q, D_STATE), BF16)

    y_off, cbs = [], []
    for g in range(N_SSD_GROUPS):
        bm = bc[:, g * D_STATE:(g + 1) * D_STATE].astype(BF16)
        cm = bc[:, (N_SSD_GROUPS + g) * D_STATE:(N_SSD_GROUPS + g + 1) * D_STATE].astype(BF16)
        rows = pl.ds(g * gw, gw)
        st = st_ref[rows, :]
        y_off.append(_dot_nt(cm, st.astype(BF16)))
        cbs.append(_dot_nt(cm, bm))
        d_st = _dot_tn(x_end[:, g * gw:(g + 1) * gw].astype(BF16), bm)
        dec = _dot_tn(dec_rows[:, g * gw:(g + 1) * gw], ones)
        nst_ref[rows, :] = st * dec + d_st

    y = jnp.concatenate(y_off, axis=-1) * jnp.exp(acs_e) + dskip_ref[...] * xs
    for s in range(n_valid):
        keep = rowi >= s
        decay = jnp.where(keep, jnp.exp(jnp.where(keep, acs_e - acs_e[s:s + 1, :], 0.0)), 0.0)
        cb_e = jnp.concatenate([jnp.broadcast_to(cb[:, s:s + 1], (q, gw)) for cb in cbs], axis=-1)
        y = y + decay * cb_e * (dt_e[s:s + 1, :] * xs[s:s + 1, :])
    u_ref[...] = _gated_group_norm(y, z_ref[...], gn_ref).astype(u_ref.dtype)


def _ssd_step(proj, prev_x, prev_bc, state, cwx, cbx, cwbc, cbbc, dtb, alog, dskip, gn, expand, n_valid):
    nb = proj.shape[0]
    q = SAMPLE_ROWS
    seqs = math.gcd(nb, 4)
    full = lambda shape: pl.BlockSpec(shape, lambda b: (0,) * len(shape))
    return pl.pallas_call(
        functools.partial(_ssd_step_kernel, n_valid),
        grid=(nb // seqs,),
        in_specs=[
            pl.BlockSpec((seqs, q, D_INNER), lambda b: (b, 0, COL_Z // D_INNER)),
            pl.BlockSpec((seqs, q, D_INNER), lambda b: (b, 0, COL_X // D_INNER)),
            pl.BlockSpec((seqs, q, BC_DIM), lambda b: (b, 0, COL_BC // BC_DIM)),
            pl.BlockSpec((seqs, q, LANES), lambda b: (b, 0, COL_DT // LANES)),
            pl.BlockSpec((seqs, CONV_W - 1, D_INNER), lambda b: (b, 0, 0)),
            pl.BlockSpec((seqs, CONV_W - 1, BC_DIM), lambda b: (b, 0, 0)),
            pl.BlockSpec((seqs, D_INNER, D_STATE), lambda b: (b, 0, 0)),
            full((CONV_W, D_INNER)), full((1, D_INNER)), full((CONV_W, BC_DIM)), full((1, BC_DIM)),
            full((1, LANES)), full((1, LANES)), full((1, D_INNER)), full((1, D_INNER)),
            full((LANES, D_INNER)),
        ],
        out_specs=[
            pl.BlockSpec((seqs, q, D_INNER), lambda b: (b, 0, 0)),
            pl.BlockSpec((seqs, D_INNER, D_STATE), lambda b: (b, 0, 0)),
        ],
        out_shape=[
            jax.ShapeDtypeStruct((nb, q, D_INNER), BF16),
            jax.ShapeDtypeStruct((nb, D_INNER, D_STATE), F32),
        ],
        scratch_shapes=[
            pltpu.VMEM((seqs, 2 * SUBLANES, D_INNER), F32),
            pltpu.VMEM((seqs, 2 * SUBLANES, BC_DIM), F32),
        ],
        compiler_params=pltpu.CompilerParams(
            dimension_semantics=("parallel",), vmem_limit_bytes=VMEM_LIMIT),
        name="ssd_step",
    )(proj, proj, proj, proj, prev_x, prev_bc, state, cwx, cbx, cwbc, cbbc, dtb, alog, dskip, gn, expand)


def _attention_kernel(qb, prev_always_valid, sinks_ref, q_ref, kp_ref, vp_ref, kc_ref, vc_ref, o_ref):
    wb = WINDOW
    lane = lax.broadcasted_iota(jnp.int32, (1, LANES), 1)
    lo_half = lane < ATTN_HEAD_DIM
    keys_major = qb % LANES == 0
    t_ax, s_ax = (1, 0) if keys_major else (0, 1)
    shape_p = (wb, qb) if keys_major else (qb, wb)
    t_p = lax.broadcasted_iota(jnp.int32, shape_p, t_ax)
    s_p = lax.broadcasted_iota(jnp.int32, shape_p, s_ax)
    rel_p = wb + t_p - s_p
    ok_p = rel_p < WINDOW
    if not prev_always_valid:
        ok_p = jnp.logical_and(ok_p, pl.program_id(1) > 0)
    t_c = lax.broadcasted_iota(jnp.int32, (qb, qb), t_ax)
    s_c = lax.broadcasted_iota(jnp.int32, (qb, qb), s_ax)
    rel_c = t_c - s_c
    ok_c = rel_c >= 0
    rel_p = rel_p.astype(F32)
    rel_c = rel_c.astype(F32)
    scale = ATTN_HEAD_DIM ** -0.5

    pairs = KV_REP
    half = pairs * qb

    def one_sequence(sq, m, bias_p, bias_c, sink):
        kv_lanes = pl.ds(m * LANES, LANES)
        kp = kp_ref[sq, :, kv_lanes].astype(BF16)
        kc = kc_ref[sq, :, kv_lanes].astype(BF16)
        vp = vp_ref[sq, :, kv_lanes]
        vc = vc_ref[sq, :, kv_lanes]
        vp_lo, vp_hi = jnp.where(lo_half, vp, 1.0).astype(BF16), jnp.where(lo_half, 1.0, vp).astype(BF16)
        vc_lo, vc_hi = jnp.where(lo_half, vc, 1.0).astype(BF16), jnp.where(lo_half, 1.0, vc).astype(BF16)
        q_lo, q_hi = [], []
        for i in range(pairs):
            q_pair = q_ref[sq, :, pl.ds((pairs * m + i) * LANES, LANES)] * scale
            q_lo.append(jnp.where(lo_half, q_pair, 0.0).astype(BF16))
            q_hi.append(jnp.where(lo_half, 0.0, q_pair).astype(BF16))
        qs = jnp.concatenate(q_lo + q_hi, axis=0)
        s_p = _dot_nt(qs, kp) + bias_p
        s_c = _dot_nt(qs, kc) + bias_c
        if qb == wb:
            mx = jnp.max(jnp.maximum(s_p, s_c), axis=-1, keepdims=True)
        else:
            mx = jnp.maximum(jnp.max(s_p, axis=-1, keepdims=True), jnp.max(s_c, axis=-1, keepdims=True))
        mx = jnp.maximum(mx, sink)
        p_p = jnp.exp(s_p - mx).astype(BF16)
        p_c = jnp.exp(s_c - mx).astype(BF16)
        e_sink = jnp.exp(sink - mx)
        o_lo = _dot(p_p[:half], vp_lo) + _dot(p_c[:half], vc_lo)
        o_hi = _dot(p_p[half:], vp_hi) + _dot(p_c[half:], vc_hi)
        for i in range(pairs):
            a = o_lo[i * qb:(i + 1) * qb]
            b = o_hi[i * qb:(i + 1) * qb]
            num = jnp.where(lo_half, a, b)
            den = (pltpu.roll(jnp.where(lo_half, b, a), ATTN_HEAD_DIM, 1)
                   + jnp.where(lo_half, e_sink[i * qb:(i + 1) * qb], e_sink[half + i * qb:half + (i + 1) * qb]))
            o_ref[sq, :, pl.ds((pairs * m + i) * LANES, LANES)] = (num / den).astype(o_ref.dtype)

    def one_sequence_keys_major(sq, m, bias_p, bias_c, sink):
        kv_lanes = pl.ds(m * LANES, LANES)
        kp = kp_ref[sq, :, kv_lanes].astype(BF16)
        kc = kc_ref[sq, :, kv_lanes].astype(BF16)
        vp = vp_ref[sq, :, kv_lanes].astype(BF16)
        vc = vc_ref[sq, :, kv_lanes].astype(BF16)
        q_lo, q_hi = [], []
        for i in range(pairs):
            q_pair = q_ref[sq, :, pl.ds((pairs * m + i) * LANES, LANES)] * scale
            q_lo.append(jnp.where(lo_half, q_pair, 0.0).astype(BF16))
            q_hi.append(jnp.where(lo_half, 0.0, q_pair).astype(BF16))
        qs = jnp.concatenate(q_lo + q_hi, axis=0)
        s_p = _dot_nt(kp, qs) + bias_p
        s_c = _dot_nt(kc, qs) + bias_c
        mx = jnp.maximum(jnp.maximum(jnp.max(s_p, axis=0, keepdims=True), jnp.max(s_c, axis=0, keepdims=True)),
                         sink)
        p_p = jnp.exp(s_p - mx)
        p_c = jnp.exp(s_c - mx)
        den = jnp.sum(p_p, axis=0, keepdims=True) + jnp.sum(p_c, axis=0, keepdims=True) + jnp.exp(sink - mx)
        o_t = (_dot_tn(vp, p_p.astype(BF16)) + _dot_tn(vc, p_c.astype(BF16))) / den
        row_lo = lax.broadcasted_iota(jnp.int32, (LANES, 1), 0) < ATTN_HEAD_DIM
        for i in range(pairs):
            blk = jnp.where(row_lo, o_t[:, i * qb:(i + 1) * qb], o_t[:, (pairs + i) * qb:(pairs + i + 1) * qb])
            o_ref[sq, :, pl.ds((pairs * m + i) * LANES, LANES)] = blk.T.astype(o_ref.dtype)

    for m in range(KV_DIM // LANES):
        heads = [2 * pairs * m + r for r in range(2 * pairs)]
        slopes = [2.0 ** (-8.0 * (h + 1) / N_ATTN_HEADS) for h in heads]
        bias_p = jnp.concatenate([jnp.where(ok_p, -sl * rel_p, NEG_BIG) for sl in slopes], axis=t_ax)
        bias_c = jnp.concatenate([jnp.where(ok_c, -sl * rel_c, NEG_BIG) for sl in slopes], axis=t_ax)
        one = (1, qb) if keys_major else (qb, 1)
        sink = jnp.concatenate([jnp.full(one, sinks_ref[h], F32) for h in heads], axis=t_ax)
        for sq in range(q_ref.shape[0]):
            (one_sequence_keys_major if keys_major else one_sequence)(sq, m, bias_p, bias_c, sink)


def _attention(proj, sinks, qb, prev_kv=None, seqs=1):
    nb, rows, _ = proj.shape
    wb = WINDOW
    cur = lambda col: pl.BlockSpec((seqs, qb, KV_DIM), lambda b, n: (b, n, col // KV_DIM))
    if prev_kv is None:
        prev = lambda col: pl.BlockSpec((seqs, wb, KV_DIM),
                                        lambda b, n: (b, jnp.maximum(n - 1, 0), col // KV_DIM))
        kp, vp = proj, proj
        prev_specs = [prev(COL_K), prev(COL_V)]
    else:
        kp, vp = prev_kv
        prev_specs = [pl.BlockSpec((seqs, wb, KV_DIM), lambda b, n: (b, 0, 0))] * 2
    return pl.pallas_call(
        functools.partial(_attention_kernel, qb, prev_kv is not None),
        grid=(nb // seqs, rows // qb),
        in_specs=[
            pl.BlockSpec(memory_space=pltpu.SMEM),
            pl.BlockSpec((seqs, qb, D_MODEL), lambda b, n: (b, n, COL_Q // D_MODEL)),
            *prev_specs,
            cur(COL_K), cur(COL_V),
        ],
        out_specs=pl.BlockSpec((seqs, qb, D_MODEL), lambda b, n: (b, n, 0)),
        out_shape=jax.ShapeDtypeStruct((nb, rows, D_MODEL), BF16),
        compiler_params=pltpu.CompilerParams(
            dimension_semantics=("parallel", "arbitrary"), vmem_limit_bytes=VMEM_LIMIT),
        name="attention",
    )(sinks, proj, kp, vp, proj, proj)


def _post_mix_kernel(valid_rows, u_ref, a_ref, gs_ref, ga_ref, x_ref, wos_ref, woa_ref, wout_ref, gpm_ref,
                     gpf_ref, g1_ref, sc2_ref, sh2_ref, wr_ref, br_ref, cnt_in_ref,
                     x1_ref, h2_ref, ti_ref, tg_ref, cnt_ref, cnt_scr):
    @pl.when(jnp.logical_and(pl.program_id(0) == 0, pl.program_id(1) == 0))
    def _():
        cnt_scr[...] = cnt_in_ref[...]

    y_ssd = _dot(u_ref[0], wos_ref[...])
    y_attn = _dot(a_ref[0], woa_ref[...])
    merged = jax.nn.sigmoid(gs_ref[0]) * y_ssd + jax.nn.sigmoid(ga_ref[0]) * y_attn
    mixed = _dot(merged.astype(BF16), wout_ref[...])
    x1 = x_ref[0] + g1_ref[0] * (_rms(mixed) * gpm_ref[...])
    x1_ref[0] = x1
    h2 = _rms(x1) * gpf_ref[...] * (1.0 + sc2_ref[0]) + sh2_ref[0]
    _store_token_tiles(h2_ref.at[0], h2)

    h_hi = h2.astype(BF16)
    h_lo = (h2 - h_hi.astype(F32)).astype(BF16)
    w = wr_ref[...]
    w_hi = w.astype(BF16)
    w_lo = (w - w_hi.astype(F32)).astype(BF16)
    logits = _dot(h_hi, w_hi) + (_dot(h_hi, w_lo) + _dot(h_lo, w_hi)) + br_ref[...]
    lane = lax.broadcasted_iota(jnp.int32, logits.shape, 1)
    idx_out = jnp.zeros(logits.shape, jnp.int32)
    val_out = jnp.zeros(logits.shape, F32)
    top = None
    denom = None
    idxs = []
    for k in range(TOP_K):
        m = jnp.max(logits, axis=-1, keepdims=True)
        idx = jnp.min(jnp.where(logits == m, lane, LANES), axis=-1, keepdims=True)
        if k == 0:
            top = m
            e = jnp.ones_like(m)
            denom = e
        else:
            e = jnp.exp(m - top)
            denom = denom + e
        idxs.append(idx)
        idx_out = jnp.where(lane == k, idx, idx_out)
        val_out = jnp.where(lane == k, e, val_out)
        logits = jnp.where(lane == idx, NEG_BIG * 2, logits)
    tg_ref[0] = val_out / denom
    ti_ref[0] = idx_out

    rowi = lax.broadcasted_iota(jnp.int32, (logits.shape[0], 1), 0)
    valid = jnp.bitwise_and(rowi, SAMPLE_ROWS - 1) < valid_rows
    picked = jnp.zeros(logits.shape, F32)
    for idx in idxs:
        picked = picked + jnp.where(jnp.logical_and(lane == idx, valid), 1.0, 0.0)
    cnt_scr[...] = cnt_scr[...] + jnp.sum(picked, axis=0, keepdims=True)
    cnt_ref[...] = cnt_scr[...]


def _post_mix(u, attn, proj, x, wos, woa, wout, gpm, gpf, gate1, scale2, shift2, wr, br, counts, *,
              tm, per_row, valid_rows):
    nb, rows, _ = x.shape
    row_spec = lambda w, col=0: pl.BlockSpec((1, tm, w), lambda b, i: (b, i, col // w))
    full = lambda shape: pl.BlockSpec(shape, lambda b, i: (0,) * len(shape))
    return pl.pallas_call(
        functools.partial(_post_mix_kernel, valid_rows),
        grid=(nb, rows // tm),
        in_specs=[
            row_spec(D_INNER), row_spec(D_MODEL), row_spec(D_MODEL, COL_GS), row_spec(D_MODEL, COL_GA),
            row_spec(D_MODEL),
            full((D_INNER, D_MODEL)), full((D_MODEL, D_MODEL)), full((D_MODEL, D_MODEL)),
            full((1, D_MODEL)), full((1, D_MODEL)),
            _mod_spec(per_row, tm), _mod_spec(per_row, tm), _mod_spec(per_row, tm),
            full((D_MODEL, LANES)), full((1, LANES)), full((1, LANES)),
        ],
        out_specs=[row_spec(D_MODEL), pl.BlockSpec((1, tm * TILE_ROWS, LANES), lambda b, i: (b, i, 0)),
                   row_spec(LANES), row_spec(LANES), full((1, LANES))],
        out_shape=[
            jax.ShapeDtypeStruct((nb, rows, D_MODEL), F32),
            jax.ShapeDtypeStruct((nb, rows * TILE_ROWS, LANES), F32),
            jax.ShapeDtypeStruct((nb, rows, LANES), jnp.int32),
            jax.ShapeDtypeStruct((nb, rows, LANES), F32),
            jax.ShapeDtypeStruct((1, LANES), F32),
        ],
        scratch_shapes=[pltpu.VMEM((1, LANES), F32)],
        compiler_params=pltpu.CompilerParams(
            dimension_semantics=("arbitrary", "arbitrary"), vmem_limit_bytes=MOE_VMEM_LIMIT),
        name="post_mix",
    )(u, attn, proj, proj, x, wos, woa, wout, gpm, gpf, gate1, scale2, shift2, wr, br, counts)


def _moe_kernel(n_tokens, be_ref, na_ref, nv_ref, idx_hbm, h_hbm, wu_ref, bu_ref, wd_ref, bd_ref, ys_hbm,
                idx_smem, xbuf, obuf, wu_bf, wd_bf, sem_idx, sem_in, sem_out):
    i = pl.program_id(0)
    na = na_ref[0]
    slot = lax.rem(i, 2)
    other = 1 - slot

    def idx_copy(block, s):
        return pltpu.make_async_copy(idx_hbm.at[block], idx_smem.at[s], sem_idx.at[s])

    block_rows = MOE_ROWS * TILE_ROWS

    def tile(ref, first_row):
        return ref.at[pl.ds(pl.multiple_of(first_row, TILE_ROWS), TILE_ROWS)]

    group_rows = MOE_GROUP * TILE_ROWS

    def per_group(block, fn):
        n_real = nv_ref[block]
        for g in range(MOE_ROWS // MOE_GROUP):
            pl.when(g * MOE_GROUP < n_real)(functools.partial(fn, g))

    def group_of(buf, s, g):
        return buf.at[s, pl.ds(g * group_rows, group_rows)]

    def start_gather(s, block):
        def group(g):
            for r in range(g * MOE_GROUP, (g + 1) * MOE_GROUP):
                pltpu.make_async_copy(tile(h_hbm, idx_smem[s, r]), xbuf.at[s, pl.ds(r * TILE_ROWS, TILE_ROWS)],
                                      sem_in.at[s]).start()
        per_group(block, group)

    def wait_gather(s, block):
        per_group(block, lambda g: pltpu.make_async_copy(
            h_hbm.at[pl.ds(0, group_rows)], group_of(xbuf, s, g), sem_in.at[s]).wait())

    def start_scatter(s, block):
        def group(g):
            for r in range(g * MOE_GROUP, (g + 1) * MOE_GROUP):
                pltpu.make_async_copy(obuf.at[s, pl.ds(r * TILE_ROWS, TILE_ROWS)],
                                      tile(ys_hbm, idx_smem[s, MOE_ROWS + r]), sem_out.at[s]).start()
        per_group(block, group)

    def wait_scatter(s, block):
        per_group(block, lambda g: pltpu.make_async_copy(
            group_of(obuf, s, g), ys_hbm.at[pl.ds(0, group_rows)], sem_out.at[s]).wait())

    @pl.when(i == 0)
    def _():
        obuf[0] = jnp.zeros((block_rows, LANES), F32)
        xbuf[...] = jnp.zeros(xbuf.shape, F32)
        plane_rows = ys_hbm.shape[0] // (TOP_K * TILE_ROWS)
        spare = [pltpu.make_async_copy(
            obuf.at[0], ys_hbm.at[pl.ds((k * plane_rows + n_tokens + hf * MOE_ROWS) * TILE_ROWS, block_rows)],
            sem_out.at[0]) for k in range(TOP_K) for hf in range(2)]
        for cp in spare:
            cp.start()
        for cp in spare:
            cp.wait()
        first = idx_copy(0, 0)
        first.start()
        first.wait()
        start_gather(0, 0)

        @pl.when(na > 1)
        def _():
            idx_copy(1, 1).start()

    @pl.when(i < na)
    def _():
        wait_gather(slot, i)

        @pl.when(i + 1 < na)
        def _():
            idx_copy(i + 1, other).wait()
            start_gather(other, i + 1)

        @pl.when(i >= 2)
        def _():
            wait_scatter(slot, i - 2)

        @pl.when(jnp.logical_or(i == 0, be_ref[i] != be_ref[jnp.maximum(i - 1, 0)]))
        def _():
            wu_bf[...] = wu_ref[0].astype(BF16)
            wd_bf[...] = wd_ref[0].astype(BF16)

        x = _load_token_tiles(xbuf.at[slot], MOE_ROWS).astype(BF16)
        up = _dot(x, wu_bf[...]) + bu_ref[0]
        glu = jnp.minimum(up[:, :D_FF], SWIGLU_LIMIT)
        lin = jnp.clip(up[:, D_FF:], -SWIGLU_LIMIT, SWIGLU_LIMIT)
        act = glu * jax.nn.sigmoid(SWIGLU_ALPHA * glu) * (lin + 1.0)
        _store_token_tiles(obuf.at[slot], _dot(act.astype(BF16), wd_bf[...]) + bd_ref[0])
        start_scatter(slot, i)

        @pl.when(i + 2 < na)
        def _():
            idx_copy(i + 2, slot).start()

        @pl.when(i == na - 1)
        def _():
            wait_scatter(slot, i)

            @pl.when(i >= 1)
            def _():
                wait_scatter(other, i - 1)


def _moe(idx, h_tiles, ys_rows, block_expert, n_active, n_real, wu, bu, wd, bd):
    n_blocks = idx.shape[0]
    grid_spec = pltpu.PrefetchScalarGridSpec(
        num_scalar_prefetch=3,
        grid=(n_blocks,),
        in_specs=[
            pl.BlockSpec(memory_space=pl.ANY),
            pl.BlockSpec(memory_space=pl.ANY),
            pl.BlockSpec((1, D_MODEL, 2 * D_FF), lambda i, be, na, nv: (be[i], 0, 0)),
            pl.BlockSpec((1, 1, 2 * D_FF), lambda i, be, na, nv: (be[i], 0, 0)),
            pl.BlockSpec((1, D_FF, D_MODEL), lambda i, be, na, nv: (be[i], 0, 0)),
            pl.BlockSpec((1, 1, D_MODEL), lambda i, be, na, nv: (be[i], 0, 0)),
        ],
        out_specs=pl.BlockSpec(memory_space=pl.ANY),
        scratch_shapes=[
            pltpu.SMEM((2, 2 * MOE_ROWS), jnp.int32),
            pltpu.VMEM((2, MOE_ROWS * TILE_ROWS, LANES), F32),
            pltpu.VMEM((2, MOE_ROWS * TILE_ROWS, LANES), F32),
            pltpu.VMEM((D_MODEL, 2 * D_FF), BF16),
            pltpu.VMEM((D_FF, D_MODEL), BF16),
            pltpu.SemaphoreType.DMA((2,)),
            pltpu.SemaphoreType.DMA((2,)),
            pltpu.SemaphoreType.DMA((2,)),
        ],
    )
    return pl.pallas_call(
        functools.partial(_moe_kernel, h_tiles.shape[0] // TILE_ROWS),
        grid_spec=grid_spec,
        out_shape=jax.ShapeDtypeStruct((ys_rows * TILE_ROWS, LANES), F32),
        compiler_params=pltpu.CompilerParams(
            dimension_semantics=("arbitrary",), vmem_limit_bytes=MOE_VMEM_LIMIT),
        name="moe",
    )(block_expert, n_active, n_real, idx, h_tiles, wu, bu, wd, bd)


def _combine_kernel(ys_ref, tg_ref, x1_ref, gpost_ref, g2_ref, o_ref):
    gates = tg_ref[0]
    tm = gates.shape[0]
    f = gates[:, 0:1] * _load_token_tiles(ys_ref.at[0], tm)
    for k in range(1, TOP_K):
        f = f + gates[:, k:k + 1] * _load_token_tiles(ys_ref.at[k], tm)
    o_ref[0] = x1_ref[0] + g2_ref[0] * (_rms(f) * gpost_ref[...])


def _combine(ys, row0, tg, x1, gpost, gate2, *, tm, per_row):
    nb, rows, _ = x1.shape
    per_b = rows // tm
    base = row0 // tm
    return pl.pallas_call(
        _combine_kernel,
        grid=(nb, per_b),
        in_specs=[
            pl.BlockSpec((TOP_K, tm * TILE_ROWS, LANES), lambda b, i: (0, base + b * per_b + i, 0)),
            pl.BlockSpec((1, tm, LANES), lambda b, i: (b, i, 0)),
            pl.BlockSpec((1, tm, D_MODEL), lambda b, i: (b, i, 0)),
            pl.BlockSpec((1, D_MODEL), lambda b, i: (0, 0)),
            _mod_spec(per_row, tm),
        ],
        out_specs=pl.BlockSpec((1, tm, D_MODEL), lambda b, i: (b, i, 0)),
        out_shape=jax.ShapeDtypeStruct((nb, rows, D_MODEL), F32),
        compiler_params=pltpu.CompilerParams(
            dimension_semantics=("parallel", "parallel"), vmem_limit_bytes=VMEM_LIMIT),
        name="combine",
    )(ys, tg, x1, gpost, gate2)


def _route(top_idx, counts, plane_rows):
    n_tokens = top_idx.shape[0]
    n_slots = n_tokens * TOP_K
    experts = jnp.arange(N_EXPERTS, dtype=jnp.int32)
    padded = (counts + MOE_ROWS - 1) // MOE_ROWS * MOE_ROWS
    group_start = jnp.cumsum(counts) - counts
    padded_end = jnp.cumsum(padded)
    padded_start = padded_end - padded
    n_blocks = -(-n_slots // MOE_ROWS) + N_EXPERTS
    n_active = (padded_end[-1] // MOE_ROWS).astype(jnp.int32)
    block_start = jnp.arange(n_blocks, dtype=jnp.int32) * MOE_ROWS
    block_expert = jnp.sum(block_start[:, None] >= padded_end[None, :], axis=1)
    last_expert = jnp.max(jnp.where(counts > 0, experts, 0))
    block_expert = jnp.where(block_start < padded_end[-1], block_expert, last_expert).astype(jnp.int32)
    slot_id = jnp.arange(n_slots, dtype=jnp.int32).reshape(n_tokens, TOP_K)
    keys = jnp.sort((top_idx * n_slots + slot_id).reshape(-1))
    row = jnp.arange(n_blocks * MOE_ROWS, dtype=jnp.int32).reshape(n_blocks, MOE_ROWS)
    src = row + (group_start - padded_start)[block_expert][:, None]
    real = row < (padded_start + counts)[block_expert][:, None]
    slot = keys[jnp.clip(src, 0, n_slots - 1).reshape(-1)].reshape(n_blocks, MOE_ROWS) % n_slots
    token = slot // TOP_K
    spare = n_tokens + (jnp.arange(n_blocks, dtype=jnp.int32) % 2)[:, None] * MOE_ROWS + row % MOE_ROWS
    dest = jnp.where(real, (slot % TOP_K) * plane_rows + token, spare)
    idx = (jnp.concatenate([token, dest], axis=1) * TILE_ROWS).astype(jnp.int32)
    n_real = jnp.clip((padded_start + counts)[block_expert] - block_start, 0, MOE_ROWS)
    n_real = jnp.where(block_start < padded_end[-1], n_real, 0).astype(jnp.int32)
    return idx, block_expert, n_active.reshape(1), n_real


def _pad_lanes(v, value=0.0):
    return jnp.pad(v, [(0, 0)] * (v.ndim - 1) + [(0, LANES - v.shape[-1])], constant_values=value)


def kernel(x_prompt, x_sample, c_prompt, c_sample, cache_swa_k, cache_swa_v, state_conv, state_ssm, w_ada, b_ada, g_pre_mix, g_post_mix, g_pre_ffn, g_post_ffn, w_in, conv_w, conv_b, dt_bias, a_log, d_skip, g_ssm_norm, sinks, w_o_ssd, w_o_attn, w_out, w_router, b_router, w_up, b_up, w_down, b_down):
    depth = w_ada.shape[0]
    n_prompt, seq, _ = x_prompt.shape
    n_sample, dec_seq, _ = x_sample.shape
    yp = x_prompt
    ys_pad = jnp.pad(x_sample, ((0, 0), (0, SAMPLE_ROWS - dec_seq), (0, 0)))
    outs = [[] for _ in range(8)]
    expand = jnp.repeat(jnp.eye(LANES, N_SSD_HEADS, dtype=BF16), SSD_HEAD_DIM, axis=1)
    rows_s = n_sample * SAMPLE_ROWS
    n_p = n_prompt * seq
    n_s = n_sample * dec_seq
    n_tok = n_p + n_s
    tm_p = min(512, seq)
    tm_s = min(512, rows_s)
    tm_mix_p = min(512, seq)
    tm_mix_s = min(256, rows_s)
    assert seq % CHUNK == 0 and seq % tm_p == 0 and rows_s % tm_s == 0 and n_p % n_s == 0
    c_all = jnp.concatenate([c_prompt, c_sample], axis=0)
    c_rows = -(-c_all.shape[0] // SUBLANES) * SUBLANES
    c_all = jnp.pad(c_all, ((0, c_rows - c_all.shape[0]), (0, 0)))

    for l in range(depth):
        wi = w_in[l]
        o_xbc = D_INNER
        o_dt = o_xbc + CONV_DIM
        o_q = o_dt + N_SSD_HEADS
        o_k = o_q + D_MODEL
        o_v = o_k + KV_DIM
        o_gs = o_v + KV_DIM
        o_ga = o_gs + D_MODEL
        head_order = jnp.array(ATTN_HEAD_ORDER)
        w_q = wi[:, o_q:o_k].reshape(D_MODEL, N_ATTN_HEADS, ATTN_HEAD_DIM)[:, head_order].reshape(D_MODEL, D_MODEL)
        w_proj = jnp.concatenate([
            wi[:, :o_xbc], wi[:, o_xbc:o_xbc + D_INNER], w_q, wi[:, o_gs:o_ga], wi[:, o_ga:],
            wi[:, o_xbc + D_INNER:o_dt], wi[:, o_k:o_v], wi[:, o_v:o_gs], _pad_lanes(wi[:, o_dt:o_q])],
            axis=1).astype(BF16)
        cwx, cwbc = conv_w[l][:, :D_INNER], conv_w[l][:, D_INNER:]
        cbx, cbbc = conv_b[l][None, :D_INNER], conv_b[l][None, D_INNER:]
        dtb = _pad_lanes(dt_bias[l][None])
        alog = _pad_lanes(a_log[l][None])
        dskip = jnp.repeat(d_skip[l], SSD_HEAD_DIM)[None]
        gn = g_ssm_norm[l][None]
        wos, wout = w_o_ssd[l].astype(BF16), w_out[l].astype(BF16)
        woa = w_o_attn[l].reshape(N_ATTN_HEADS, ATTN_HEAD_DIM, D_MODEL)[head_order].reshape(D_MODEL, D_MODEL)
        woa = woa.astype(BF16)
        zero_counts = jnp.zeros((1, LANES), F32)
        wr = _pad_lanes(w_router[l])
        br = _pad_lanes(b_router[l][None], NEG_BIG)
        wu, wd = w_up[l], w_down[l]
        bu, bd = b_up[l][:, None, :], b_down[l][:, None, :]
        g_pm, g_pom, g_pf, g_pof = (v[l][None] for v in (g_pre_mix, g_post_mix, g_pre_ffn, g_post_ffn))

        ada = _ada(c_all, w_ada[l], b_ada[l][None])
        ada_p = [ada[:n_prompt, k * D_MODEL:(k + 1) * D_MODEL] for k in range(6)]
        ada_s = [ada[n_prompt:n_prompt + n_sample, k * D_MODEL:(k + 1) * D_MODEL] for k in range(6)]

        mods = [m[:, None, :] for m in ada_p]
        proj = _in_proj(yp, g_pm, mods[1], mods[0], w_proj, tm=min(1024, seq), per_row=False)
        u, ssm_p = _ssd(proj, cwx, cbx, cwbc, cbbc, dtb, alog, dskip, gn)
        attn = _attention(proj, sinks[l], WINDOW)
        x1_p, h2_p, ti_p, tg_p, cnt_p = _post_mix(
            u, attn, proj, yp, wos, woa, wout, g_pom, g_pf, mods[2], mods[4], mods[3], wr, br, zero_counts,
            tm=tm_mix_p, per_row=False, valid_rows=SAMPLE_ROWS)
        gate2_p = mods[5]
        kp = proj[:, seq - WINDOW:, COL_K:COL_K + KV_DIM].reshape(n_prompt, WINDOW, N_KV_HEADS, ATTN_HEAD_DIM)
        vp = proj[:, seq - WINDOW:, COL_V:COL_V + KV_DIM].reshape(n_prompt, WINDOW, N_KV_HEADS, ATTN_HEAD_DIM)
        cp = jnp.concatenate([proj[:, seq - (CONV_W - 1):, COL_X:COL_X + D_INNER],
                              proj[:, seq - (CONV_W - 1):, COL_BC:COL_BC + BC_DIM]], axis=-1)

        mods_s = [jnp.repeat(m, SAMPLE_ROWS, axis=0)[None] for m in ada_s]
        xs_flat = ys_pad.reshape(1, rows_s, D_MODEL)
        proj_s = _in_proj(xs_flat, g_pm, mods_s[1], mods_s[0], w_proj, tm=tm_s, per_row=True)
        proj_sb = proj_s.reshape(n_sample, SAMPLE_ROWS, PROJ_DIM)
        u_s, ssm_s = _ssd_step(
            proj_sb, state_conv[l][:, :, :D_INNER], state_conv[l][:, :, D_INNER:],
            state_ssm[l].reshape(n_sample, D_INNER, D_STATE),
            cwx, cbx, cwbc, cbbc, dtb, alog, dskip, gn, expand, dec_seq)
        k_prev = cache_swa_k[l].reshape(n_sample, -1, KV_DIM)
        v_prev = cache_swa_v[l].reshape(n_sample, -1, KV_DIM)
        attn_s = _attention(proj_sb, sinks[l], SAMPLE_ROWS, prev_kv=(k_prev, v_prev),
                            seqs=math.gcd(n_sample, 8))
        x1_s, h2_s, ti_s, tg_s, cnt_all = _post_mix(
            u_s.reshape(1, rows_s, D_INNER), attn_s.reshape(1, rows_s, D_MODEL), proj_s, xs_flat,
            wos, woa, wout, g_pom, g_pf, mods_s[2], mods_s[4], mods_s[3], wr, br, cnt_p,
            tm=tm_mix_s, per_row=True, valid_rows=dec_seq)
        wb = k_prev.shape[1]
        k_new = proj_sb[:, :dec_seq, COL_K:COL_K + KV_DIM]
        v_new = proj_sb[:, :dec_seq, COL_V:COL_V + KV_DIM]
        ks = jnp.concatenate([k_prev, k_new], axis=1)[:, -wb:].reshape(n_sample, wb, N_KV_HEADS, ATTN_HEAD_DIM)
        vs = jnp.concatenate([v_prev, v_new], axis=1)[:, -wb:].reshape(n_sample, wb, N_KV_HEADS, ATTN_HEAD_DIM)
        raw_xbc = jnp.concatenate([proj_sb[:, :dec_seq, COL_X:COL_X + D_INNER],
                                   proj_sb[:, :dec_seq, COL_BC:COL_BC + BC_DIM]], axis=-1)
        cs = jnp.concatenate([state_conv[l], raw_xbc], axis=1)[:, -(CONV_W - 1):]

        valid = lambda v: v.reshape(n_sample, SAMPLE_ROWS, -1)[:, :dec_seq].reshape(n_s, -1)
        h2_sv = h2_s.reshape(n_sample, SAMPLE_ROWS * TILE_ROWS, LANES)[:, :dec_seq * TILE_ROWS]
        h2_all = jnp.concatenate([h2_p.reshape(n_p * TILE_ROWS, LANES),
                                  h2_sv.reshape(n_s * TILE_ROWS, LANES)], axis=0)
        ti_all = jnp.concatenate([ti_p.reshape(n_p, LANES), valid(ti_s)], axis=0)
        counts = cnt_all[0, :N_EXPERTS].astype(jnp.int32)
        plane_rows = n_tok + 2 * MOE_ROWS
        idx, block_expert, n_active, n_real = _route(ti_all[:, :TOP_K], counts, plane_rows)
        ysel = _moe(idx, h2_all, TOP_K * plane_rows, block_expert, n_active, n_real, wu, bu, wd, bd)
        ysel = ysel.reshape(TOP_K, plane_rows * TILE_ROWS, LANES)

        yp = _combine(ysel, 0, tg_p, x1_p, g_pof, gate2_p, tm=tm_p, per_row=False)
        x1_sv = valid(x1_s)[None]
        tg_sv = valid(tg_s)[None]
        gate2_s = jnp.repeat(ada_s[5], dec_seq, axis=0)[None]
        ys_new = _combine(ysel, n_p, tg_sv, x1_sv, g_pof, gate2_s, tm=n_s, per_row=True)
        ys_new = ys_new.reshape(n_sample, dec_seq, D_MODEL)
        ys_pad = jnp.pad(ys_new, ((0, 0), (0, SAMPLE_ROWS - dec_seq), (0, 0)))

        for lst, v in zip(outs, (kp, vp, cp, ssm_p, ks, vs, cs,
                                 ssm_s.reshape(n_sample, N_SSD_HEADS, SSD_HEAD_DIM, D_STATE))):
            lst.append(v)

    return (yp, ys_pad[:, :dec_seq], *[jnp.stack(v) for v in outs])
```

```python
import functools
import math

import jax
import jax.numpy as jnp
from jax import lax
from jax.experimental import pallas as pl
from jax.experimental.pallas import tpu as pltpu

F32 = jnp.float32
BF16 = jnp.bfloat16

D_MODEL = 1024
D_INNER = 2 * D_MODEL
SSD_HEAD_DIM = 64
N_SSD_HEADS = D_INNER // SSD_HEAD_DIM
N_SSD_GROUPS = 4
HEADS_PER_GROUP = N_SSD_HEADS // N_SSD_GROUPS
D_STATE = 128
CONV_W = 4
BC_DIM = 2 * N_SSD_GROUPS * D_STATE
CONV_DIM = D_INNER + BC_DIM
CHUNK = 128
ATTN_HEAD_DIM = 64
N_ATTN_HEADS = D_MODEL // ATTN_HEAD_DIM
N_KV_HEADS = 4
KV_REP = N_ATTN_HEADS // N_KV_HEADS
KV_DIM = N_KV_HEADS * ATTN_HEAD_DIM
WINDOW = 128
N_EXPERTS = 32
TOP_K = 4
D_FF = D_MODEL
SWIGLU_LIMIT = 7.0
SWIGLU_ALPHA = 1.702
EPS = 1e-6

LANES = 128
SUBLANES = 8
NEG_BIG = -1e30

COL_Z = 0
COL_X = COL_Z + D_INNER
COL_Q = COL_X + D_INNER
COL_GS = COL_Q + D_MODEL
COL_GA = COL_GS + D_MODEL
COL_BC = COL_GA + D_MODEL
COL_K = COL_BC + BC_DIM
COL_V = COL_K + KV_DIM
COL_DT = COL_V + KV_DIM
PROJ_DIM = COL_DT + LANES
PROJ_TILE_N = PROJ_DIM // 3

ATTN_HEAD_ORDER = tuple(2 * KV_REP * (j // KV_REP) + (j % KV_REP) + KV_REP * hf
                        for j in range(N_ATTN_HEADS // 2) for hf in (0, 1))

TILE_ROWS = D_MODEL // LANES
assert TILE_ROWS == SUBLANES

SAMPLE_ROWS = 8
MOE_ROWS = 512
MOE_GROUP = 32
VMEM_LIMIT = 48 * 1024 * 1024
BIG_VMEM_LIMIT = 56 * 1024 * 1024


def _silu(v):
    return v * jax.nn.sigmoid(v)


def _softplus(v):
    return jnp.maximum(v, 0.0) + jnp.log(1.0 + jnp.exp(-jnp.abs(v)))


def _rms(v):
    return v * lax.rsqrt(jnp.mean(v * v, axis=-1, keepdims=True) + EPS)


def _split3(v):
    hi = v.astype(BF16)
    r1 = v - hi.astype(F32)
    mid = r1.astype(BF16)
    lo = (r1 - mid.astype(F32)).astype(BF16)
    return hi, mid, lo


def _store_token_tiles(ref, rows):
    n = rows.shape[0]
    for j in range(TILE_ROWS):
        ref[pl.ds(j, n, stride=TILE_ROWS), :] = rows[:, j * LANES:(j + 1) * LANES]


def _load_token_tiles(ref, n):
    return jnp.concatenate([ref[pl.ds(j, n, stride=TILE_ROWS), :] for j in range(TILE_ROWS)], axis=1)


def _dot(a, b):
    return jnp.dot(a, b, preferred_element_type=F32)


def _dot_nt(a, b):
    return lax.dot_general(a, b, (((1,), (1,)), ((), ())), preferred_element_type=F32)


def _dot_tn(a, b):
    return lax.dot_general(a, b, (((0,), (0,)), ((), ())), preferred_element_type=F32)


def _dot_exact_rhs(a, b_bf16):
    hi, mid, lo = _split3(a)
    return _dot(hi, b_bf16) + _dot(mid, b_bf16) + _dot(lo, b_bf16)


def _ada_kernel(c_ref, w_ref, b_ref, o_ref):
    o_ref[...] = _dot(_silu(c_ref[...]).astype(BF16), w_ref[...].astype(BF16)) + b_ref[...]


def _ada(c, w, b):
    rows, n = c.shape[0], w.shape[1]
    tn = n // 4
    return pl.pallas_call(
        _ada_kernel,
        grid=(n // tn,),
        in_specs=[
            pl.BlockSpec((rows, D_MODEL), lambda j: (0, 0)),
            pl.BlockSpec((D_MODEL, tn), lambda j: (0, j)),
            pl.BlockSpec((1, tn), lambda j: (0, j)),
        ],
        out_specs=pl.BlockSpec((rows, tn), lambda j: (0, j)),
        out_shape=jax.ShapeDtypeStruct((rows, n), F32),
        compiler_params=pltpu.CompilerParams(
            dimension_semantics=("parallel",), vmem_limit_bytes=VMEM_LIMIT),
        name="ada",
    )(c, w, b)


def _in_proj_kernel(x_ref, g_ref, sc_ref, sh_ref, w_ref, o_ref, h_scr):
    @pl.when(pl.program_id(2) == 0)
    def _():
        h = _rms(x_ref[0]) * g_ref[...] * (1.0 + sc_ref[0]) + sh_ref[0]
        h_scr[...] = h.astype(BF16)

    o_ref[0] = _dot(h_scr[...], w_ref[...])


def _mod_spec(per_row, tm):
    if per_row:
        return pl.BlockSpec((1, tm, D_MODEL), lambda b, i, *_: (b, i, 0))
    return pl.BlockSpec((1, 1, D_MODEL), lambda b, i, *_: (b, 0, 0))


def _in_proj(x, g, scale, shift, w, *, tm, per_row):
    nb, rows, _ = x.shape
    return pl.pallas_call(
        _in_proj_kernel,
        grid=(nb, rows // tm, PROJ_DIM // PROJ_TILE_N),
        in_specs=[
            pl.BlockSpec((1, tm, D_MODEL), lambda b, i, j: (b, i, 0)),
            pl.BlockSpec((1, D_MODEL), lambda b, i, j: (0, 0)),
            _mod_spec(per_row, tm),
            _mod_spec(per_row, tm),
            pl.BlockSpec((D_MODEL, PROJ_TILE_N), lambda b, i, j: (0, j)),
        ],
        out_specs=pl.BlockSpec((1, tm, PROJ_TILE_N), lambda b, i, j: (b, i, j)),
        out_shape=jax.ShapeDtypeStruct((nb, rows, PROJ_DIM), F32),
        scratch_shapes=[pltpu.VMEM((tm, D_MODEL), BF16)],
        compiler_params=pltpu.CompilerParams(
            dimension_semantics=("parallel", "parallel", "arbitrary"),
            vmem_limit_bytes=BIG_VMEM_LIMIT),
        name="in_proj",
    )(x, g, scale, shift, w)


def _causal_conv(ext_ref, raw, prev, w_ref, b_ref, rows):
    base = SUBLANES
    if prev is not None:
        ext_ref[pl.ds(base - (CONV_W - 1), CONV_W - 1), :] = prev
    ext_ref[pl.ds(base, rows), :] = raw
    out = b_ref[...] + raw * w_ref[pl.ds(CONV_W - 1, 1), :]
    for k in range(CONV_W - 1):
        out = out + ext_ref[pl.ds(base - (CONV_W - 1) + k, rows), :] * w_ref[pl.ds(k, 1), :]
    return out


def _causal_conv_carried(carry_ref, raw, w_ref, b_ref):
    rows = raw.shape[0]
    first = lax.broadcasted_iota(jnp.int32, (rows, 1), 0) == 0
    acc = raw * w_ref[pl.ds(0, 1), :]
    for k in range(1, CONV_W):
        moved = jnp.where(first, carry_ref[pl.ds(k - 1, 1), :], pltpu.roll(acc, 1, 0))
        carry_ref[pl.ds(k - 1, 1), :] = acc[rows - 1:rows, :]
        acc = raw * w_ref[pl.ds(k, 1), :] + moved
    return acc + b_ref[...]


def _gated_group_norm(y, z, g_ref):
    u = y * _silu(z)
    gw = D_INNER // N_SSD_GROUPS
    parts = []
    for g in range(N_SSD_GROUPS):
        ug = u[:, g * gw:(g + 1) * gw]
        parts.append(ug * lax.rsqrt(jnp.mean(ug * ug, axis=-1, keepdims=True) + EPS))
    return jnp.concatenate(parts, axis=-1) * g_ref[...]


def _ssd_kernel(z_ref, x_ref, bc_ref, dt_ref, cwx_ref, cbx_ref, cwbc_ref, cbbc_ref, dtb_ref, alog_ref,
                dskip_ref, gn_ref, u_ref, fin_ref, extx_scr, extbc_scr, st_scr, y_scr):
    q = CHUNK
    c = pl.program_id(1)

    @pl.when(c == 0)
    def _():
        extx_scr[...] = jnp.zeros_like(extx_scr)
        extbc_scr[...] = jnp.zeros_like(extbc_scr)
        st_scr[...] = jnp.zeros_like(st_scr)

    xs = _silu(_causal_conv_carried(extx_scr, x_ref[0], cwx_ref, cbx_ref))
    bc = _silu(_causal_conv_carried(extbc_scr, bc_ref[0], cwbc_ref, cbbc_ref))

    dt = _softplus(dt_ref[0] + dtb_ref[...])
    adt = dt * (-jnp.exp(alog_ref[...]))
    row = lax.broadcasted_iota(jnp.int32, (q, q), 0)
    col = lax.broadcasted_iota(jnp.int32, (q, q), 1)
    causal = row >= col
    tri = jnp.where(causal, 1.0, 0.0).astype(BF16)
    h3, m3, l3 = _split3(adt)
    acs = _dot(tri, h3) + _dot(tri, m3) + _dot(tri, l3)
    eacs = jnp.exp(acs)
    acs_t = acs.T
    dt_t = dt.T
    lane = lax.broadcasted_iota(jnp.int32, (1, LANES), 1)
    lo_half = lane < SSD_HEAD_DIM

    for g in range(N_SSD_GROUPS):
        bm = bc[:, g * D_STATE:(g + 1) * D_STATE]
        cm = bc[:, (N_SSD_GROUPS + g) * D_STATE:(N_SSD_GROUPS + g + 1) * D_STATE]
        cb = jnp.where(causal, _dot_nt(cm.astype(BF16), bm.astype(BF16)), 0.0)
        bm_t = bm.T
        for pr in range(HEADS_PER_GROUP // 2):
            h0 = g * HEADS_PER_GROUP + 2 * pr
            lanes = pl.ds(h0 * SSD_HEAD_DIM, LANES)
            x_pair = xs[:, h0 * SSD_HEAD_DIM:h0 * SSD_HEAD_DIM + LANES]
            st_pair = st_scr[:, lanes]
            lhs_y, lhs_s, decs = [], [], []
            for h in (h0, h0 + 1):
                a_col = acs[:, h:h + 1]
                a_row = acs_t[h:h + 1, :]
                dt_row = dt_t[h:h + 1, :]
                a_last = acs_t[h:h + 1, q - 1:q]
                decay = jnp.exp(jnp.minimum(a_col - a_row, 0.0))
                lhs_y.append((cb * decay * dt_row).astype(BF16))
                lhs_s.append((bm_t * (dt_row * jnp.exp(a_last - a_row))).astype(BF16))
                decs.append(jnp.exp(a_last))
            for h in (h0, h0 + 1):
                lhs_y.append((cm * eacs[:, h:h + 1]).astype(BF16))
            x_top = jnp.where(lo_half, x_pair, 0.0).astype(BF16)
            x_bot = jnp.where(lo_half, 0.0, x_pair).astype(BF16)
            s_top = jnp.where(lo_half, st_pair, 0.0).astype(BF16)
            s_bot = jnp.where(lo_half, 0.0, st_pair).astype(BF16)
            x_bd = jnp.concatenate([x_top, x_bot], axis=0)
            rhs_y = jnp.concatenate([x_bd, s_top, s_bot], axis=0)
            y_pair = _dot(jnp.concatenate(lhs_y, axis=1), rhs_y)
            ds_pair = _dot(jnp.concatenate(lhs_s, axis=1), x_bd)
            dskip = dskip_ref[:, lanes]
            y_scr[:, lanes] = y_pair + dskip * x_pair
            st_scr[:, lanes] = st_pair * jnp.where(lo_half, decs[0], decs[1]) + ds_pair

    u_ref[0] = _gated_group_norm(y_scr[...], z_ref[0], gn_ref).astype(u_ref.dtype)

    @pl.when(c == pl.num_programs(1) - 1)
    def _():
        for pr in range(N_SSD_HEADS // 2):
            t = st_scr[:, pl.ds(pr * LANES, LANES)].T
            fin_ref[0, 2 * pr] = t[:SSD_HEAD_DIM]
            fin_ref[0, 2 * pr + 1] = t[SSD_HEAD_DIM:]


def _ssd(proj, cwx, cbx, cwbc, cbbc, dtb, alog, dskip, gn):
    nb, rows, _ = proj.shape
    q = CHUNK
    full = lambda shape: pl.BlockSpec(shape, lambda b, c: (0,) * len(shape))
    return pl.pallas_call(
        _ssd_kernel,
        grid=(nb, rows // q),
        in_specs=[
            pl.BlockSpec((1, q, D_INNER), lambda b, c: (b, c, COL_Z // D_INNER)),
            pl.BlockSpec((1, q, D_INNER), lambda b, c: (b, c, COL_X // D_INNER)),
            pl.BlockSpec((1, q, BC_DIM), lambda b, c: (b, c, COL_BC // BC_DIM)),
            pl.BlockSpec((1, q, LANES), lambda b, c: (b, c, COL_DT // LANES)),
            full((CONV_W, D_INNER)), full((1, D_INNER)), full((CONV_W, BC_DIM)), full((1, BC_DIM)),
            full((1, LANES)), full((1, LANES)), full((1, D_INNER)), full((1, D_INNER)),
        ],
        out_specs=[
            pl.BlockSpec((1, q, D_INNER), lambda b, c: (b, c, 0)),
            pl.BlockSpec((1, N_SSD_HEADS, SSD_HEAD_DIM, D_STATE), lambda b, c: (b, 0, 0, 0)),
        ],
        out_shape=[
            jax.ShapeDtypeStruct((nb, rows, D_INNER), BF16),
            jax.ShapeDtypeStruct((nb, N_SSD_HEADS, SSD_HEAD_DIM, D_STATE), F32),
        ],
        scratch_shapes=[
            pltpu.VMEM((SUBLANES, D_INNER), F32),
            pltpu.VMEM((SUBLANES, BC_DIM), F32),
            pltpu.VMEM((D_STATE, D_INNER), F32),
            pltpu.VMEM((q, D_INNER), F32),
        ],
        compiler_params=pltpu.CompilerParams(
            dimension_semantics=("parallel", "arbitrary"), vmem_limit_bytes=VMEM_LIMIT),
        name="ssd",
    )(proj, proj, proj, proj, cwx, cbx, cwbc, cbbc, dtb, alog, dskip, gn)


def _ssd_step_kernel(n_valid, z_ref, x_ref, bc_ref, dt_ref, px_ref, pbc_ref, st_ref, cwx_ref, cbx_ref,
                     cwbc_ref, cbbc_ref, dtb_ref, alog_ref, dskip_ref, gn_ref, exp_ref,
                     u_ref, nst_ref, extx_scr, extbc_scr):
    for sq in range(z_ref.shape[0]):
        _ssd_step_one(n_valid, z_ref.at[sq], x_ref.at[sq], bc_ref.at[sq], dt_ref.at[sq], px_ref.at[sq],
                      pbc_ref.at[sq], st_ref.at[sq], cwx_ref, cbx_ref, cwbc_ref, cbbc_ref, dtb_ref, alog_ref,
                      dskip_ref, gn_ref, exp_ref, u_ref.at[sq], nst_ref.at[sq], extx_scr.at[sq], extbc_scr.at[sq])


def _ssd_step_one(n_valid, z_ref, x_ref, bc_ref, dt_ref, px_ref, pbc_ref, st_ref, cwx_ref, cbx_ref,
                  cwbc_ref, cbbc_ref, dtb_ref, alog_ref, dskip_ref, gn_ref, exp_ref,
                  u_ref, nst_ref, extx_scr, extbc_scr):
    q = SAMPLE_ROWS
    gw = D_INNER // N_SSD_GROUPS
    xs = _silu(_causal_conv(extx_scr, x_ref[...], px_ref[...], cwx_ref, cbx_ref, q))
    bc = _silu(_causal_conv(extbc_scr, bc_ref[...], pbc_ref[...], cwbc_ref, cbbc_ref, q))

    rowi = lax.broadcasted_iota(jnp.int32, (q, 1), 0)
    dt = jnp.where(rowi < n_valid, _softplus(dt_ref[...] + dtb_ref[...]), 0.0)
    adt = dt * (-jnp.exp(alog_ref[...]))
    acs = jnp.zeros_like(adt)
    for s in range(n_valid):
        acs = acs + jnp.where(rowi >= s, adt[s:s + 1, :], 0.0)
    expand = exp_ref[...]
    dt_e = _dot_exact_rhs(dt, expand)
    acs_e = _dot_exact_rhs(acs, expand)
    last_e = acs_e[q - 1:q, :]

    x_end = xs * dt_e * jnp.exp(last_e - acs_e)
    dec3 = _split3(jnp.exp(last_e))
    zrow = jnp.zeros((q - 3, D_INNER), BF16)
    dec_rows = jnp.concatenate([dec3[0], dec3[1], dec3[2], zrow], axis=0)
    ones = jnp.ones((q, D_STATE), BF16)

    y_off, cbs = [], []
    for g in range(N_SSD_GROUPS):
        bm = bc[:, g * D_STATE:(g + 1) * D_STATE].astype(BF16)
        cm = bc[:, (N_SSD_GROUPS + g) * D_STATE:(N_SSD_GROUPS + g + 1) * D_STATE].astype(BF16)
        rows = pl.ds(g * gw, gw)
        st = st_ref[rows, :]
        y_off.append(_dot_nt(cm, st.astype(BF16)))
        cbs.append(_dot_nt(cm, bm))
        d_st = _dot_tn(x_end[:, g * gw:(g + 1) * gw].astype(BF16), bm)
        dec = _dot_tn(dec_rows[:, g * gw:(g + 1) * gw], ones)
        nst_ref[rows, :] = st * dec + d_st

    y = jnp.concatenate(y_off, axis=-1) * jnp.exp(acs_e) + dskip_ref[...] * xs
    for s in range(n_valid):
        keep = rowi >= s
        decay = jnp.where(keep, jnp.exp(jnp.where(keep, acs_e - acs_e[s:s + 1, :], 0.0)), 0.0)
        cb_e = jnp.concatenate([jnp.broadcast_to(cb[:, s:s + 1], (q, gw)) for cb in cbs], axis=-1)
        y = y + decay * cb_e * (dt_e[s:s + 1, :] * xs[s:s + 1, :])
    u_ref[...] = _gated_group_norm(y, z_ref[...], gn_ref).astype(u_ref.dtype)


def _ssd_step(proj, prev_x, prev_bc, state, cwx, cbx, cwbc, cbbc, dtb, alog, dskip, gn, expand, n_valid):
    nb = proj.shape[0]
    q = SAMPLE_ROWS
    seqs = math.gcd(nb, 4)
    full = lambda shape: pl.BlockSpec(shape, lambda b: (0,) * len(shape))
    return pl.pallas_call(
        functools.partial(_ssd_step_kernel, n_valid),
        grid=(nb // seqs,),
        in_specs=[
            pl.BlockSpec((seqs, q, D_INNER), lambda b: (b, 0, COL_Z // D_INNER)),
            pl.BlockSpec((seqs, q, D_INNER), lambda b: (b, 0, COL_X // D_INNER)),
            pl.BlockSpec((seqs, q, BC_DIM), lambda b: (b, 0, COL_BC // BC_DIM)),
            pl.BlockSpec((seqs, q, LANES), lambda b: (b, 0, COL_DT // LANES)),
            pl.BlockSpec((seqs, CONV_W - 1, D_INNER), lambda b: (b, 0, 0)),
            pl.BlockSpec((seqs, CONV_W - 1, BC_DIM), lambda b: (b, 0, 0)),
            pl.BlockSpec((seqs, D_INNER, D_STATE), lambda b: (b, 0, 0)),
            full((CONV_W, D_INNER)), full((1, D_INNER)), full((CONV_W, BC_DIM)), full((1, BC_DIM)),
            full((1, LANES)), full((1, LANES)), full((1, D_INNER)), full((1, D_INNER)),
            full((LANES, D_INNER)),
        ],
        out_specs=[
            pl.BlockSpec((seqs, q, D_INNER), lambda b: (b, 0, 0)),
            pl.BlockSpec((seqs, D_INNER, D_STATE), lambda b: (b, 0, 0)),
        ],
        out_shape=[
            jax.ShapeDtypeStruct((nb, q, D_INNER), BF16),
            jax.ShapeDtypeStruct((nb, D_INNER, D_STATE), F32),
        ],
        scratch_shapes=[
            pltpu.VMEM((seqs, 2 * SUBLANES, D_INNER), F32),
            pltpu.VMEM((seqs, 2 * SUBLANES, BC_DIM), F32),
        ],
        compiler_params=pltpu.CompilerParams(
            dimension_semantics=("parallel",), vmem_limit_bytes=VMEM_LIMIT),
        name="ssd_step",
    )(proj, proj, proj, proj, prev_x, prev_bc, state, cwx, cbx, cwbc, cbbc, dtb, alog, dskip, gn, expand)


def _attention_kernel(qb, prev_always_valid, sinks_ref, q_ref, kp_ref, vp_ref, kc_ref, vc_ref, o_ref):
    wb = WINDOW
    lane = lax.broadcasted_iota(jnp.int32, (1, LANES), 1)
    lo_half = lane < ATTN_HEAD_DIM
    keys_major = qb % LANES == 0
    t_ax, s_ax = (1, 0) if keys_major else (0, 1)
    shape_p = (wb, qb) if keys_major else (qb, wb)
    t_p = lax.broadcasted_iota(jnp.int32, shape_p, t_ax)
    s_p = lax.broadcasted_iota(jnp.int32, shape_p, s_ax)
    rel_p = wb + t_p - s_p
    ok_p = rel_p < WINDOW
    if not prev_always_valid:
        ok_p = jnp.logical_and(ok_p, pl.program_id(1) > 0)
    t_c = lax.broadcasted_iota(jnp.int32, (qb, qb), t_ax)
    s_c = lax.broadcasted_iota(jnp.int32, (qb, qb), s_ax)
    rel_c = t_c - s_c
    ok_c = rel_c >= 0
    rel_p = rel_p.astype(F32)
    rel_c = rel_c.astype(F32)
    scale = ATTN_HEAD_DIM ** -0.5

    pairs = KV_REP
    half = pairs * qb

    def one_sequence(sq, m, bias_p, bias_c, sink):
        kv_lanes = pl.ds(m * LANES, LANES)
        kp = kp_ref[sq, :, kv_lanes].astype(BF16)
        kc = kc_ref[sq, :, kv_lanes].astype(BF16)
        vp = vp_ref[sq, :, kv_lanes]
        vc = vc_ref[sq, :, kv_lanes]
        vp_lo, vp_hi = jnp.where(lo_half, vp, 1.0).astype(BF16), jnp.where(lo_half, 1.0, vp).astype(BF16)
        vc_lo, vc_hi = jnp.where(lo_half, vc, 1.0).astype(BF16), jnp.where(lo_half, 1.0, vc).astype(BF16)
        q_lo, q_hi = [], []
        for i in range(pairs):
            q_pair = q_ref[sq, :, pl.ds((pairs * m + i) * LANES, LANES)] * scale
            q_lo.append(jnp.where(lo_half, q_pair, 0.0).astype(BF16))
            q_hi.append(jnp.where(lo_half, 0.0, q_pair).astype(BF16))
        qs = jnp.concatenate(q_lo + q_hi, axis=0)
        s_p = _dot_nt(qs, kp) + bias_p
        s_c = _dot_nt(qs, kc) + bias_c
        if qb == wb:
            mx = jnp.max(jnp.maximum(s_p, s_c), axis=-1, keepdims=True)
        else:
            mx = jnp.maximum(jnp.max(s_p, axis=-1, keepdims=True), jnp.max(s_c, axis=-1, keepdims=True))
        mx = jnp.maximum(mx, sink)
        p_p = jnp.exp(s_p - mx).astype(BF16)
        p_c = jnp.exp(s_c - mx).astype(BF16)
        e_sink = jnp.exp(sink - mx)
        o_lo = _dot(p_p[:half], vp_lo) + _dot(p_c[:half], vc_lo)
        o_hi = _dot(p_p[half:], vp_hi) + _dot(p_c[half:], vc_hi)
        for i in range(pairs):
            a = o_lo[i * qb:(i + 1) * qb]
            b = o_hi[i * qb:(i + 1) * qb]
            num = jnp.where(lo_half, a, b)
            den = (pltpu.roll(jnp.where(lo_half, b, a), ATTN_HEAD_DIM, 1)
                   + jnp.where(lo_half, e_sink[i * qb:(i + 1) * qb], e_sink[half + i * qb:half + (i + 1) * qb]))
            o_ref[sq, :, pl.ds((pairs * m + i) * LANES, LANES)] = (num / den).astype(o_ref.dtype)

    def one_sequence_keys_major(sq, m, bias_p, bias_c, sink):
        kv_lanes = pl.ds(m * LANES, LANES)
        kp = kp_ref[sq, :, kv_lanes].astype(BF16)
        kc = kc_ref[sq, :, kv_lanes].astype(BF16)
        vp = vp_ref[sq, :, kv_lanes].astype(BF16)
        vc = vc_ref[sq, :, kv_lanes].astype(BF16)
        q_lo, q_hi = [], []
        for i in range(pairs):
            q_pair = q_ref[sq, :, pl.ds((pairs * m + i) * LANES, LANES)] * scale
            q_lo.append(jnp.where(lo_half, q_pair, 0.0).astype(BF16))
            q_hi.append(jnp.where(lo_half, 0.0, q_pair).astype(BF16))
        qs = jnp.concatenate(q_lo + q_hi, axis=0)
        s_p = _dot_nt(kp, qs) + bias_p
        s_c = _dot_nt(kc, qs) + bias_c
        mx = jnp.maximum(jnp.maximum(jnp.max(s_p, axis=0, keepdims=True), jnp.max(s_c, axis=0, keepdims=True)),
                         sink)
        p_p = jnp.exp(s_p - mx)
        p_c = jnp.exp(s_c - mx)
        den = jnp.sum(p_p, axis=0, keepdims=True) + jnp.sum(p_c, axis=0, keepdims=True) + jnp.exp(sink - mx)
        o_t = (_dot_tn(vp, p_p.astype(BF16)) + _dot_tn(vc, p_c.astype(BF16))) / den
        row_lo = lax.broadcasted_iota(jnp.int32, (LANES, 1), 0) < ATTN_HEAD_DIM
        for i in range(pairs):
            blk = jnp.where(row_lo, o_t[:, i * qb:(i + 1) * qb], o_t[:, (pairs + i) * qb:(pairs + i + 1) * qb])
            o_ref[sq, :, pl.ds((pairs * m + i) * LANES, LANES)] = blk.T.astype(o_ref.dtype)

    for m in range(KV_DIM // LANES):
        heads = [2 * pairs * m + r for r in range(2 * pairs)]
        slopes = [2.0 ** (-8.0 * (h + 1) / N_ATTN_HEADS) for h in heads]
        bias_p = jnp.concatenate([jnp.where(ok_p, -sl * rel_p, NEG_BIG) for sl in slopes], axis=t_ax)
        bias_c = jnp.concatenate([jnp.where(ok_c, -sl * rel_c, NEG_BIG) for sl in slopes], axis=t_ax)
        one = (1, qb) if keys_major else (qb, 1)
        sink = jnp.concatenate([jnp.full(one, sinks_ref[h], F32) for h in heads], axis=t_ax)
        for sq in range(q_ref.shape[0]):
            (one_sequence_keys_major if keys_major else one_sequence)(sq, m, bias_p, bias_c, sink)


def _attention(proj, sinks, qb, prev_kv=None, seqs=1):
    nb, rows, _ = proj.shape
    wb = WINDOW
    cur = lambda col: pl.BlockSpec((seqs, qb, KV_DIM), lambda b, n: (b, n, col // KV_DIM))
    if prev_kv is None:
        prev = lambda col: pl.BlockSpec((seqs, wb, KV_DIM),
                                        lambda b, n: (b, jnp.maximum(n - 1, 0), col // KV_DIM))
        kp, vp = proj, proj
        prev_specs = [prev(COL_K), prev(COL_V)]
    else:
        kp, vp = prev_kv
        prev_specs = [pl.BlockSpec((seqs, wb, KV_DIM), lambda b, n: (b, 0, 0))] * 2
    return pl.pallas_call(
        functools.partial(_attention_kernel, qb, prev_kv is not None),
        grid=(nb // seqs, rows // qb),
        in_specs=[
            pl.BlockSpec(memory_space=pltpu.SMEM),
            pl.BlockSpec((seqs, qb, D_MODEL), lambda b, n: (b, n, COL_Q // D_MODEL)),
            *prev_specs,
            cur(COL_K), cur(COL_V),
        ],
        out_specs=pl.BlockSpec((seqs, qb, D_MODEL), lambda b, n: (b, n, 0)),
        out_shape=jax.ShapeDtypeStruct((nb, rows, D_MODEL), BF16),
        compiler_params=pltpu.CompilerParams(
            dimension_semantics=("parallel", "arbitrary"), vmem_limit_bytes=VMEM_LIMIT),
        name="attention",
    )(sinks, proj, kp, vp, proj, proj)


def _post_mix_kernel(valid_rows, u_ref, a_ref, gs_ref, ga_ref, x_ref, wos_ref, woa_ref, wout_ref, gpm_ref,
                     gpf_ref, g1_ref, sc2_ref, sh2_ref, wr_ref, br_ref, cnt_in_ref,
                     x1_ref, h2_ref, ti_ref, tg_ref, cnt_ref, cnt_scr):
    @pl.when(jnp.logical_and(pl.program_id(0) == 0, pl.program_id(1) == 0))
    def _():
        cnt_scr[...] = cnt_in_ref[...]

    y_ssd = _dot(u_ref[0], wos_ref[...])
    y_attn = _dot(a_ref[0], woa_ref[...])
    merged = jax.nn.sigmoid(gs_ref[0]) * y_ssd + jax.nn.sigmoid(ga_ref[0]) * y_attn
    mixed = _dot(merged.astype(BF16), wout_ref[...])
    x1 = x_ref[0] + g1_ref[0] * (_rms(mixed) * gpm_ref[...])
    x1_ref[0] = x1
    h2 = _rms(x1) * gpf_ref[...] * (1.0 + sc2_ref[0]) + sh2_ref[0]
    _store_token_tiles(h2_ref.at[0], h2)

    h_hi = h2.astype(BF16)
    h_lo = (h2 - h_hi.astype(F32)).astype(BF16)
    w = wr_ref[...]
    w_hi = w.astype(BF16)
    w_lo = (w - w_hi.astype(F32)).astype(BF16)
    logits = _dot(h_hi, w_hi) + (_dot(h_hi, w_lo) + _dot(h_lo, w_hi)) + br_ref[...]
    lane = lax.broadcasted_iota(jnp.int32, logits.shape, 1)
    idx_out = jnp.zeros(logits.shape, jnp.int32)
    val_out = jnp.zeros(logits.shape, F32)
    top = None
    denom = None
    idxs = []
    for k in range(TOP_K):
        m = jnp.max(logits, axis=-1, keepdims=True)
        idx = jnp.min(jnp.where(logits == m, lane, LANES), axis=-1, keepdims=True)
        if k == 0:
            top = m
            e = jnp.ones_like(m)
            denom = e
        else:
            e = jnp.exp(m - top)
            denom = denom + e
        idxs.append(idx)
        idx_out = jnp.where(lane == k, idx, idx_out)
        val_out = jnp.where(lane == k, e, val_out)
        logits = jnp.where(lane == idx, NEG_BIG * 2, logits)
    tg_ref[0] = val_out / denom
    ti_ref[0] = idx_out

    rowi = lax.broadcasted_iota(jnp.int32, (logits.shape[0], 1), 0)
    valid = jnp.bitwise_and(rowi, SAMPLE_ROWS - 1) < valid_rows
    picked = jnp.zeros(logits.shape, F32)
    for idx in idxs:
        picked = picked + jnp.where(jnp.logical_and(lane == idx, valid), 1.0, 0.0)
    cnt_scr[...] = cnt_scr[...] + jnp.sum(picked, axis=0, keepdims=True)
    cnt_ref[...] = cnt_scr[...]


def _post_mix(u, attn, proj, x, wos, woa, wout, gpm, gpf, gate1, scale2, shift2, wr, br, counts, *,
              tm, per_row, valid_rows):
    nb, rows, _ = x.shape
    row_spec = lambda w, col=0: pl.BlockSpec((1, tm, w), lambda b, i: (b, i, col // w))
    full = lambda shape: pl.BlockSpec(shape, lambda b, i: (0,) * len(shape))
    return pl.pallas_call(
        functools.partial(_post_mix_kernel, valid_rows),
        grid=(nb, rows // tm),
        in_specs=[
            row_spec(D_INNER), row_spec(D_MODEL), row_spec(D_MODEL, COL_GS), row_spec(D_MODEL, COL_GA),
            row_spec(D_MODEL),
            full((D_INNER, D_MODEL)), full((D_MODEL, D_MODEL)), full((D_MODEL, D_MODEL)),
            full((1, D_MODEL)), full((1, D_MODEL)),
            _mod_spec(per_row, tm), _mod_spec(per_row, tm), _mod_spec(per_row, tm),
            full((D_MODEL, LANES)), full((1, LANES)), full((1, LANES)),
        ],
        out_specs=[row_spec(D_MODEL), pl.BlockSpec((1, tm * TILE_ROWS, LANES), lambda b, i: (b, i, 0)),
                   row_spec(LANES), row_spec(LANES), full((1, LANES))],
        out_shape=[
            jax.ShapeDtypeStruct((nb, rows, D_MODEL), F32),
            jax.ShapeDtypeStruct((nb, rows * TILE_ROWS, LANES), F32),
            jax.ShapeDtypeStruct((nb, rows, LANES), jnp.int32),
            jax.ShapeDtypeStruct((nb, rows, LANES), F32),
            jax.ShapeDtypeStruct((1, LANES), F32),
        ],
        scratch_shapes=[pltpu.VMEM((1, LANES), F32)],
        compiler_params=pltpu.CompilerParams(
            dimension_semantics=("arbitrary", "arbitrary"), vmem_limit_bytes=BIG_VMEM_LIMIT),
        name="post_mix",
    )(u, attn, proj, proj, x, wos, woa, wout, gpm, gpf, gate1, scale2, shift2, wr, br, counts)


def _moe_kernel(n_tokens, be_ref, na_ref, nv_ref, idx_hbm, h_hbm, wu_ref, bu_ref, wd_ref, bd_ref, ys_hbm,
                idx_smem, xbuf, obuf, wu_bf, wd_bf, sem_idx, sem_in, sem_out):
    i = pl.program_id(0)
    na = na_ref[0]
    slot = lax.rem(i, 2)
    other = 1 - slot

    def idx_copy(block, s):
        return pltpu.make_async_copy(idx_hbm.at[block], idx_smem.at[s], sem_idx.at[s])

    block_rows = MOE_ROWS * TILE_ROWS

    def tile(ref, first_row):
        return ref.at[pl.ds(pl.multiple_of(first_row, TILE_ROWS), TILE_ROWS)]

    group_rows = MOE_GROUP * TILE_ROWS

    def per_group(block, fn):
        n_real = nv_ref[block]
        for g in range(MOE_ROWS // MOE_GROUP):
            pl.when(g * MOE_GROUP < n_real)(functools.partial(fn, g))

    def group_of(buf, s, g):
        return buf.at[s, pl.ds(g * group_rows, group_rows)]

    def start_gather(s, block):
        def group(g):
            for r in range(g * MOE_GROUP, (g + 1) * MOE_GROUP):
                pltpu.make_async_copy(tile(h_hbm, idx_smem[s, r]), xbuf.at[s, pl.ds(r * TILE_ROWS, TILE_ROWS)],
                                      sem_in.at[s]).start()
        per_group(block, group)

    def wait_gather(s, block):
        per_group(block, lambda g: pltpu.make_async_copy(
            h_hbm.at[pl.ds(0, group_rows)], group_of(xbuf, s, g), sem_in.at[s]).wait())

    def start_scatter(s, block):
        def group(g):
            for r in range(g * MOE_GROUP, (g + 1) * MOE_GROUP):
                pltpu.make_async_copy(obuf.at[s, pl.ds(r * TILE_ROWS, TILE_ROWS)],
                                      tile(ys_hbm, idx_smem[s, MOE_ROWS + r]), sem_out.at[s]).start()
        per_group(block, group)

    def wait_scatter(s, block):
        per_group(block, lambda g: pltpu.make_async_copy(
            group_of(obuf, s, g), ys_hbm.at[pl.ds(0, group_rows)], sem_out.at[s]).wait())

    @pl.when(i == 0)
    def _():
        obuf[0] = jnp.zeros((block_rows, LANES), F32)
        xbuf[...] = jnp.zeros(xbuf.shape, F32)
        plane_rows = ys_hbm.shape[0] // (TOP_K * TILE_ROWS)
        spare = [pltpu.make_async_copy(
            obuf.at[0], ys_hbm.at[pl.ds((k * plane_rows + n_tokens + hf * MOE_ROWS) * TILE_ROWS, block_rows)],
            sem_out.at[0]) for k in range(TOP_K) for hf in range(2)]
        for cp in spare:
            cp.start()
        for cp in spare:
            cp.wait()
        first = idx_copy(0, 0)
        first.start()
        first.wait()
        start_gather(0, 0)

        @pl.when(na > 1)
        def _():
            idx_copy(1, 1).start()

    @pl.when(i < na)
    def _():
        wait_gather(slot, i)

        @pl.when(i + 1 < na)
        def _():
            idx_copy(i + 1, other).wait()
            start_gather(other, i + 1)

        @pl.when(i >= 2)
        def _():
            wait_scatter(slot, i - 2)

        @pl.when(jnp.logical_or(i == 0, be_ref[i] != be_ref[jnp.maximum(i - 1, 0)]))
        def _():
            wu_bf[...] = wu_ref[0].astype(BF16)
            wd_bf[...] = wd_ref[0].astype(BF16)

        x = _load_token_tiles(xbuf.at[slot], MOE_ROWS).astype(BF16)
        up = _dot(x, wu_bf[...]) + bu_ref[0]
        glu = jnp.minimum(up[:, :D_FF], SWIGLU_LIMIT)
        lin = jnp.clip(up[:, D_FF:], -SWIGLU_LIMIT, SWIGLU_LIMIT)
        act = glu * jax.nn.sigmoid(SWIGLU_ALPHA * glu) * (lin + 1.0)
        _store_token_tiles(obuf.at[slot], _dot(act.astype(BF16), wd_bf[...]) + bd_ref[0])
        start_scatter(slot, i)

        @pl.when(i + 2 < na)
        def _():
            idx_copy(i + 2, slot).start()

        @pl.when(i == na - 1)
        def _():
            wait_scatter(slot, i)

            @pl.when(i >= 1)
            def _():
                wait_scatter(other, i - 1)


def _moe(idx, h_tiles, ys_rows, block_expert, n_active, n_real, wu, bu, wd, bd):
    n_blocks = idx.shape[0]
    grid_spec = pltpu.PrefetchScalarGridSpec(
        num_scalar_prefetch=3,
        grid=(n_blocks,),
        in_specs=[
            pl.BlockSpec(memory_space=pl.ANY),
            pl.BlockSpec(memory_space=pl.ANY),
            pl.BlockSpec((1, D_MODEL, 2 * D_FF), lambda i, be, na, nv: (be[i], 0, 0)),
            pl.BlockSpec((1, 1, 2 * D_FF), lambda i, be, na, nv: (be[i], 0, 0)),
            pl.BlockSpec((1, D_FF, D_MODEL), lambda i, be, na, nv: (be[i], 0, 0)),
            pl.BlockSpec((1, 1, D_MODEL), lambda i, be, na, nv: (be[i], 0, 0)),
        ],
        out_specs=pl.BlockSpec(memory_space=pl.ANY),
        scratch_shapes=[
            pltpu.SMEM((2, 2 * MOE_ROWS), jnp.int32),
            pltpu.VMEM((2, MOE_ROWS * TILE_ROWS, LANES), F32),
            pltpu.VMEM((2, MOE_ROWS * TILE_ROWS, LANES), F32),
            pltpu.VMEM((D_MODEL, 2 * D_FF), BF16),
            pltpu.VMEM((D_FF, D_MODEL), BF16),
            pltpu.SemaphoreType.DMA((2,)),
            pltpu.SemaphoreType.DMA((2,)),
            pltpu.SemaphoreType.DMA((2,)),
        ],
    )
    return pl.pallas_call(
        functools.partial(_moe_kernel, h_tiles.shape[0] // TILE_ROWS),
        grid_spec=grid_spec,
        out_shape=jax.ShapeDtypeStruct((ys_rows * TILE_ROWS, LANES), F32),
        compiler_params=pltpu.CompilerParams(
            dimension_semantics=("arbitrary",), vmem_limit_bytes=BIG_VMEM_LIMIT),
        name="moe",
    )(block_expert, n_active, n_real, idx, h_tiles, wu, bu, wd, bd)


def _combine_kernel(ys_ref, tg_ref, x1_ref, gpost_ref, g2_ref, o_ref):
    gates = tg_ref[0]
    tm = gates.shape[0]
    f = gates[:, 0:1] * _load_token_tiles(ys_ref.at[0], tm)
    for k in range(1, TOP_K):
        f = f + gates[:, k:k + 1] * _load_token_tiles(ys_ref.at[k], tm)
    o_ref[0] = x1_ref[0] + g2_ref[0] * (_rms(f) * gpost_ref[...])


def _combine(ys, row0, tg, x1, gpost, gate2, *, tm, per_row):
    nb, rows, _ = x1.shape
    per_b = rows // tm
    base = row0 // tm
    return pl.pallas_call(
        _combine_kernel,
        grid=(nb, per_b),
        in_specs=[
            pl.BlockSpec((TOP_K, tm * TILE_ROWS, LANES), lambda b, i: (0, base + b * per_b + i, 0)),
            pl.BlockSpec((1, tm, LANES), lambda b, i: (b, i, 0)),
            pl.BlockSpec((1, tm, D_MODEL), lambda b, i: (b, i, 0)),
            pl.BlockSpec((1, D_MODEL), lambda b, i: (0, 0)),
            _mod_spec(per_row, tm),
        ],
        out_specs=pl.BlockSpec((1, tm, D_MODEL), lambda b, i: (b, i, 0)),
        out_shape=jax.ShapeDtypeStruct((nb, rows, D_MODEL), F32),
        compiler_params=pltpu.CompilerParams(
            dimension_semantics=("parallel", "parallel"), vmem_limit_bytes=VMEM_LIMIT),
        name="combine",
    )(ys, tg, x1, gpost, gate2)


def _route(top_idx, counts, plane_rows):
    n_tokens = top_idx.shape[0]
    n_slots = n_tokens * TOP_K
    experts = jnp.arange(N_EXPERTS, dtype=jnp.int32)
    padded = (counts + MOE_ROWS - 1) // MOE_ROWS * MOE_ROWS
    group_start = jnp.cumsum(counts) - counts
    padded_end = jnp.cumsum(padded)
    padded_start = padded_end - padded
    n_blocks = -(-n_slots // MOE_ROWS) + N_EXPERTS
    n_active = (padded_end[-1] // MOE_ROWS).astype(jnp.int32)
    block_start = jnp.arange(n_blocks, dtype=jnp.int32) * MOE_ROWS
    block_expert = jnp.sum(block_start[:, None] >= padded_end[None, :], axis=1)
    last_expert = jnp.max(jnp.where(counts > 0, experts, 0))
    block_expert = jnp.where(block_start < padded_end[-1], block_expert, last_expert).astype(jnp.int32)
    slot_id = jnp.arange(n_slots, dtype=jnp.int32).reshape(n_tokens, TOP_K)
    keys = jnp.sort((top_idx * n_slots + slot_id).reshape(-1))
    row = jnp.arange(n_blocks * MOE_ROWS, dtype=jnp.int32).reshape(n_blocks, MOE_ROWS)
    src = row + (group_start - padded_start)[block_expert][:, None]
    real = row < (padded_start + counts)[block_expert][:, None]
    slot = keys[jnp.clip(src, 0, n_slots - 1).reshape(-1)].reshape(n_blocks, MOE_ROWS) % n_slots
    token = slot // TOP_K
    spare = n_tokens + (jnp.arange(n_blocks, dtype=jnp.int32) % 2)[:, None] * MOE_ROWS + row % MOE_ROWS
    dest = jnp.where(real, (slot % TOP_K) * plane_rows + token, spare)
    idx = (jnp.concatenate([token, dest], axis=1) * TILE_ROWS).astype(jnp.int32)
    n_real = jnp.clip((padded_start + counts)[block_expert] - block_start, 0, MOE_ROWS)
    n_real = jnp.where(block_start < padded_end[-1], n_real, 0).astype(jnp.int32)
    return idx, block_expert, n_active.reshape(1), n_real


def _pad_lanes(v, value=0.0):
    return jnp.pad(v, [(0, 0)] * (v.ndim - 1) + [(0, LANES - v.shape[-1])], constant_values=value)


def kernel(x_prompt, x_sample, c_prompt, c_sample, cache_swa_k, cache_swa_v, state_conv, state_ssm, w_ada, b_ada, g_pre_mix, g_post_mix, g_pre_ffn, g_post_ffn, w_in, conv_w, conv_b, dt_bias, a_log, d_skip, g_ssm_norm, sinks, w_o_ssd, w_o_attn, w_out, w_router, b_router, w_up, b_up, w_down, b_down):
    depth = w_ada.shape[0]
    n_prompt, seq, _ = x_prompt.shape
    n_sample, dec_seq, _ = x_sample.shape
    yp = x_prompt
    ys_pad = jnp.pad(x_sample, ((0, 0), (0, SAMPLE_ROWS - dec_seq), (0, 0)))
    outs = [[] for _ in range(8)]
    expand = jnp.repeat(jnp.eye(LANES, N_SSD_HEADS, dtype=BF16), SSD_HEAD_DIM, axis=1)
    rows_s = n_sample * SAMPLE_ROWS
    n_p = n_prompt * seq
    n_s = n_sample * dec_seq
    n_tok = n_p + n_s
    tm_p = min(512, seq)
    tm_s = min(512, rows_s)
    tm_mix_p = min(512, seq)
    tm_mix_s = min(256, rows_s)
    assert seq % CHUNK == 0 and seq % tm_p == 0 and rows_s % tm_s == 0 and n_p % n_s == 0
    c_all = jnp.concatenate([c_prompt, c_sample], axis=0)
    c_rows = -(-c_all.shape[0] // SUBLANES) * SUBLANES
    c_all = jnp.pad(c_all, ((0, c_rows - c_all.shape[0]), (0, 0)))

    for l in range(depth):
        wi = w_in[l]
        o_xbc = D_INNER
        o_dt = o_xbc + CONV_DIM
        o_q = o_dt + N_SSD_HEADS
        o_k = o_q + D_MODEL
        o_v = o_k + KV_DIM
        o_gs = o_v + KV_DIM
        o_ga = o_gs + D_MODEL
        head_order = jnp.array(ATTN_HEAD_ORDER)
        w_q = wi[:, o_q:o_k].reshape(D_MODEL, N_ATTN_HEADS, ATTN_HEAD_DIM)[:, head_order].reshape(D_MODEL, D_MODEL)
        w_proj = jnp.concatenate([
            wi[:, :o_xbc], wi[:, o_xbc:o_xbc + D_INNER], w_q, wi[:, o_gs:o_ga], wi[:, o_ga:],
            wi[:, o_xbc + D_INNER:o_dt], wi[:, o_k:o_v], wi[:, o_v:o_gs], _pad_lanes(wi[:, o_dt:o_q])],
            axis=1).astype(BF16)
        cwx, cwbc = conv_w[l][:, :D_INNER], conv_w[l][:, D_INNER:]
        cbx, cbbc = conv_b[l][None, :D_INNER], conv_b[l][None, D_INNER:]
        dtb = _pad_lanes(dt_bias[l][None])
        alog = _pad_lanes(a_log[l][None])
        dskip = jnp.repeat(d_skip[l], SSD_HEAD_DIM)[None]
        gn = g_ssm_norm[l][None]
        wos, wout = w_o_ssd[l].astype(BF16), w_out[l].astype(BF16)
        woa = w_o_attn[l].reshape(N_ATTN_HEADS, ATTN_HEAD_DIM, D_MODEL)[head_order].reshape(D_MODEL, D_MODEL)
        woa = woa.astype(BF16)
        zero_counts = jnp.zeros((1, LANES), F32)
        wr = _pad_lanes(w_router[l])
        br = _pad_lanes(b_router[l][None], NEG_BIG)
        wu, wd = w_up[l], w_down[l]
        bu, bd = b_up[l][:, None, :], b_down[l][:, None, :]
        g_pm, g_pom, g_pf, g_pof = (v[l][None] for v in (g_pre_mix, g_post_mix, g_pre_ffn, g_post_ffn))

        ada = _ada(c_all, w_ada[l], b_ada[l][None])
        ada_p = [ada[:n_prompt, k * D_MODEL:(k + 1) * D_MODEL] for k in range(6)]
        ada_s = [ada[n_prompt:n_prompt + n_sample, k * D_MODEL:(k + 1) * D_MODEL] for k in range(6)]

        mods = [m[:, None, :] for m in ada_p]
        proj = _in_proj(yp, g_pm, mods[1], mods[0], w_proj, tm=min(1024, seq), per_row=False)
        u, ssm_p = _ssd(proj, cwx, cbx, cwbc, cbbc, dtb, alog, dskip, gn)
        attn = _attention(proj, sinks[l], WINDOW)
        x1_p, h2_p, ti_p, tg_p, cnt_p = _post_mix(
            u, attn, proj, yp, wos, woa, wout, g_pom, g_pf, mods[2], mods[4], mods[3], wr, br, zero_counts,
            tm=tm_mix_p, per_row=False, valid_rows=SAMPLE_ROWS)
        gate2_p = mods[5]
        kp = proj[:, seq - WINDOW:, COL_K:COL_K + KV_DIM].reshape(n_prompt, WINDOW, N_KV_HEADS, ATTN_HEAD_DIM)
        vp = proj[:, seq - WINDOW:, COL_V:COL_V + KV_DIM].reshape(n_prompt, WINDOW, N_KV_HEADS, ATTN_HEAD_DIM)
        cp = jnp.concatenate([proj[:, seq - (CONV_W - 1):, COL_X:COL_X + D_INNER],
                              proj[:, seq - (CONV_W - 1):, COL_BC:COL_BC + BC_DIM]], axis=-1)

        mods_s = [jnp.repeat(m, SAMPLE_ROWS, axis=0)[None] for m in ada_s]
        xs_flat = ys_pad.reshape(1, rows_s, D_MODEL)
        proj_s = _in_proj(xs_flat, g_pm, mods_s[1], mods_s[0], w_proj, tm=tm_s, per_row=True)
        proj_sb = proj_s.reshape(n_sample, SAMPLE_ROWS, PROJ_DIM)
        u_s, ssm_s = _ssd_step(
            proj_sb, state_conv[l][:, :, :D_INNER], state_conv[l][:, :, D_INNER:],
            state_ssm[l].reshape(n_sample, D_INNER, D_STATE),
            cwx, cbx, cwbc, cbbc, dtb, alog, dskip, gn, expand, dec_seq)
        k_prev = cache_swa_k[l].reshape(n_sample, -1, KV_DIM)
        v_prev = cache_swa_v[l].reshape(n_sample, -1, KV_DIM)
        attn_s = _attention(proj_sb, sinks[l], SAMPLE_ROWS, prev_kv=(k_prev, v_prev),
                            seqs=math.gcd(n_sample, 8))
        x1_s, h2_s, ti_s, tg_s, cnt_all = _post_mix(
            u_s.reshape(1, rows_s, D_INNER), attn_s.reshape(1, rows_s, D_MODEL), proj_s, xs_flat,
            wos, woa, wout, g_pom, g_pf, mods_s[2], mods_s[4], mods_s[3], wr, br, cnt_p,
            tm=tm_mix_s, per_row=True, valid_rows=dec_seq)
        wb = k_prev.shape[1]
        k_new = proj_sb[:, :dec_seq, COL_K:COL_K + KV_DIM]
        v_new = proj_sb[:, :dec_seq, COL_V:COL_V + KV_DIM]
        ks = jnp.concatenate([k_prev, k_new], axis=1)[:, -wb:].reshape(n_sample, wb, N_KV_HEADS, ATTN_HEAD_DIM)
        vs = jnp.concatenate([v_prev, v_new], axis=1)[:, -wb:].reshape(n_sample, wb, N_KV_HEADS, ATTN_HEAD_DIM)
        raw_xbc = jnp.concatenate([proj_sb[:, :dec_seq, COL_X:COL_X + D_INNER],
                                   proj_sb[:, :dec_seq, COL_BC:COL_BC + BC_DIM]], axis=-1)
        cs = jnp.concatenate([state_conv[l], raw_xbc], axis=1)[:, -(CONV_W - 1):]

        valid = lambda v: v.reshape(n_sample, SAMPLE_ROWS, -1)[:, :dec_seq].reshape(n_s, -1)
        h2_sv = h2_s.reshape(n_sample, SAMPLE_ROWS * TILE_ROWS, LANES)[:, :dec_seq * TILE_ROWS]
        h2_all = jnp.concatenate([h2_p.reshape(n_p * TILE_ROWS, LANES),
                                  h2_sv.reshape(n_s * TILE_ROWS, LANES)], axis=0)
        ti_all = jnp.concatenate([ti_p.reshape(n_p, LANES), valid(ti_s)], axis=0)
        counts = cnt_all[0, :N_EXPERTS].astype(jnp.int32)
        plane_rows = n_tok + 2 * MOE_ROWS
        idx, block_expert, n_active, n_real = _route(ti_all[:, :TOP_K], counts, plane_rows)
        ysel = _moe(idx, h2_all, TOP_K * plane_rows, block_expert, n_active, n_real, wu, bu, wd, bd)
        ysel = ysel.reshape(TOP_K, plane_rows * TILE_ROWS, LANES)

        yp = _combine(ysel, 0, tg_p, x1_p, g_pof, gate2_p, tm=tm_p, per_row=False)
        x1_sv = valid(x1_s)[None]
        tg_sv = valid(tg_s)[None]
        gate2_s = jnp.repeat(ada_s[5], dec_seq, axis=0)[None]
        ys_new = _combine(ysel, n_p, tg_sv, x1_sv, g_pof, gate2_s, tm=n_s, per_row=True)
        ys_new = ys_new.reshape(n_sample, dec_seq, D_MODEL)
        ys_pad = jnp.pad(ys_new, ((0, 0), (0, SAMPLE_ROWS - dec_seq), (0, 0)))

        for lst, v in zip(outs, (kp, vp, cp, ssm_p, ks, vs, cs,
                                 ssm_s.reshape(n_sample, N_SSD_HEADS, SSD_HEAD_DIM, D_STATE))):
            lst.append(v)

    return (yp, ys_pad[:, :dec_seq], *[jnp.stack(v) for v in outs])
```

```python
import functools
import math

import jax
import jax.numpy as jnp
from jax import lax
from jax.experimental import pallas as pl
from jax.experimental.pallas import tpu as pltpu

F32 = jnp.float32
BF16 = jnp.bfloat16

D_MODEL = 1024
D_INNER = 2 * D_MODEL
SSD_HEAD_DIM = 64
N_SSD_HEADS = D_INNER // SSD_HEAD_DIM
N_SSD_GROUPS = 4
HEADS_PER_GROUP = N_SSD_HEADS // N_SSD_GROUPS
D_STATE = 128
CONV_W = 4
BC_DIM = 2 * N_SSD_GROUPS * D_STATE
CONV_DIM = D_INNER + BC_DIM
CHUNK = 128
ATTN_HEAD_DIM = 64
N_ATTN_HEADS = D_MODEL // ATTN_HEAD_DIM
N_KV_HEADS = 4
KV_REP = N_ATTN_HEADS // N_KV_HEADS
KV_DIM = N_KV_HEADS * ATTN_HEAD_DIM
WINDOW = 128
N_EXPERTS = 32
TOP_K = 4
D_FF = D_MODEL
SWIGLU_LIMIT = 7.0
SWIGLU_ALPHA = 1.702
EPS = 1e-6

LANES = 128
SUBLANES = 8
NEG_BIG = -1e30

COL_Z = 0
COL_X = COL_Z + D_INNER
COL_Q = COL_X + D_INNER
COL_GS = COL_Q + D_MODEL
COL_GA = COL_GS + D_MODEL
COL_BC = COL_GA + D_MODEL
COL_K = COL_BC + BC_DIM
COL_V = COL_K + KV_DIM
COL_DT = COL_V + KV_DIM
PROJ_DIM = COL_DT + LANES
PROJ_TILE_N = PROJ_DIM // 3

ATTN_HEAD_ORDER = tuple(2 * KV_REP * (j // KV_REP) + (j % KV_REP) + KV_REP * hf
                        for j in range(N_ATTN_HEADS // 2) for hf in (0, 1))

TILE_ROWS = D_MODEL // LANES
assert TILE_ROWS == SUBLANES

SAMPLE_ROWS = 8
MOE_ROWS = 512
MOE_GROUP = 64
VMEM_LIMIT = 48 * 1024 * 1024
BIG_VMEM_LIMIT = 56 * 1024 * 1024


def _silu(v):
    return v * jax.nn.sigmoid(v)


def _softplus(v):
    return jnp.maximum(v, 0.0) + jnp.log(1.0 + jnp.exp(-jnp.abs(v)))


def _rms(v):
    return v * lax.rsqrt(jnp.mean(v * v, axis=-1, keepdims=True) + EPS)


def _split3(v):
    hi = v.astype(BF16)
    r1 = v - hi.astype(F32)
    mid = r1.astype(BF16)
    lo = (r1 - mid.astype(F32)).astype(BF16)
    return hi, mid, lo


def _store_token_tiles(ref, rows):
    n = rows.shape[0]
    for j in range(TILE_ROWS):
        ref[pl.ds(j, n, stride=TILE_ROWS), :] = rows[:, j * LANES:(j + 1) * LANES]


def _load_token_tiles(ref, n):
    return jnp.concatenate([ref[pl.ds(j, n, stride=TILE_ROWS), :] for j in range(TILE_ROWS)], axis=1)


def _dot(a, b):
    return jnp.dot(a, b, preferred_element_type=F32)


def _dot_nt(a, b):
    return lax.dot_general(a, b, (((1,), (1,)), ((), ())), preferred_element_type=F32)


def _dot_tn(a, b):
    return lax.dot_general(a, b, (((0,), (0,)), ((), ())), preferred_element_type=F32)


def _dot_exact_rhs(a, b_bf16):
    hi, mid, lo = _split3(a)
    return _dot(hi, b_bf16) + _dot(mid, b_bf16) + _dot(lo, b_bf16)


def _ada_kernel(c_ref, w_ref, b_ref, o_ref):
    o_ref[...] = _dot(_silu(c_ref[...]).astype(BF16), w_ref[...].astype(BF16)) + b_ref[...]


def _ada(c, w, b):
    rows, n = c.shape[0], w.shape[1]
    tn = n // 4
    return pl.pallas_call(
        _ada_kernel,
        grid=(n // tn,),
        in_specs=[
            pl.BlockSpec((rows, D_MODEL), lambda j: (0, 0)),
            pl.BlockSpec((D_MODEL, tn), lambda j: (0, j)),
            pl.BlockSpec((1, tn), lambda j: (0, j)),
        ],
        out_specs=pl.BlockSpec((rows, tn), lambda j: (0, j)),
        out_shape=jax.ShapeDtypeStruct((rows, n), F32),
        compiler_params=pltpu.CompilerParams(
            dimension_semantics=("parallel",), vmem_limit_bytes=VMEM_LIMIT),
        name="ada",
    )(c, w, b)


def _in_proj_kernel(x_ref, g_ref, sc_ref, sh_ref, w_ref, o_ref, h_scr):
    @pl.when(pl.program_id(2) == 0)
    def _():
        h = _rms(x_ref[0]) * g_ref[...] * (1.0 + sc_ref[0]) + sh_ref[0]
        h_scr[...] = h.astype(BF16)

    o_ref[0] = _dot(h_scr[...], w_ref[...])


def _mod_spec(per_row, tm):
    if per_row:
        return pl.BlockSpec((1, tm, D_MODEL), lambda b, i, *_: (b, i, 0))
    return pl.BlockSpec((1, 1, D_MODEL), lambda b, i, *_: (b, 0, 0))


def _in_proj(x, g, scale, shift, w, *, tm, per_row):
    nb, rows, _ = x.shape
    return pl.pallas_call(
        _in_proj_kernel,
        grid=(nb, rows // tm, PROJ_DIM // PROJ_TILE_N),
        in_specs=[
            pl.BlockSpec((1, tm, D_MODEL), lambda b, i, j: (b, i, 0)),
            pl.BlockSpec((1, D_MODEL), lambda b, i, j: (0, 0)),
            _mod_spec(per_row, tm),
            _mod_spec(per_row, tm),
            pl.BlockSpec((D_MODEL, PROJ_TILE_N), lambda b, i, j: (0, j)),
        ],
        out_specs=pl.BlockSpec((1, tm, PROJ_TILE_N), lambda b, i, j: (b, i, j)),
        out_shape=jax.ShapeDtypeStruct((nb, rows, PROJ_DIM), F32),
        scratch_shapes=[pltpu.VMEM((tm, D_MODEL), BF16)],
        compiler_params=pltpu.CompilerParams(
            dimension_semantics=("parallel", "parallel", "arbitrary"),
            vmem_limit_bytes=BIG_VMEM_LIMIT),
        name="in_proj",
    )(x, g, scale, shift, w)


def _causal_conv(ext_ref, raw, prev, w_ref, b_ref, rows):
    base = SUBLANES
    if prev is not None:
        ext_ref[pl.ds(base - (CONV_W - 1), CONV_W - 1), :] = prev
    ext_ref[pl.ds(base, rows), :] = raw
    out = b_ref[...] + raw * w_ref[pl.ds(CONV_W - 1, 1), :]
    for k in range(CONV_W - 1):
        out = out + ext_ref[pl.ds(base - (CONV_W - 1) + k, rows), :] * w_ref[pl.ds(k, 1), :]
    return out


def _causal_conv_carried(carry_ref, raw, w_ref, b_ref):
    rows = raw.shape[0]
    first = lax.broadcasted_iota(jnp.int32, (rows, 1), 0) == 0
    acc = raw * w_ref[pl.ds(0, 1), :]
    for k in range(1, CONV_W):
        moved = jnp.where(first, carry_ref[pl.ds(k - 1, 1), :], pltpu.roll(acc, 1, 0))
        carry_ref[pl.ds(k - 1, 1), :] = acc[rows - 1:rows, :]
        acc = raw * w_ref[pl.ds(k, 1), :] + moved
    return acc + b_ref[...]


def _gated_group_norm(y, z, g_ref):
    u = y * _silu(z)
    gw = D_INNER // N_SSD_GROUPS
    parts = []
    for g in range(N_SSD_GROUPS):
        ug = u[:, g * gw:(g + 1) * gw]
        parts.append(ug * lax.rsqrt(jnp.mean(ug * ug, axis=-1, keepdims=True) + EPS))
    return jnp.concatenate(parts, axis=-1) * g_ref[...]


def _ssd_kernel(z_ref, x_ref, bc_ref, dt_ref, cwx_ref, cbx_ref, cwbc_ref, cbbc_ref, dtb_ref, alog_ref,
                dskip_ref, gn_ref, u_ref, fin_ref, extx_scr, extbc_scr, st_scr, y_scr):
    q = CHUNK
    c = pl.program_id(1)

    @pl.when(c == 0)
    def _():
        extx_scr[...] = jnp.zeros_like(extx_scr)
        extbc_scr[...] = jnp.zeros_like(extbc_scr)
        st_scr[...] = jnp.zeros_like(st_scr)

    xs = _silu(_causal_conv_carried(extx_scr, x_ref[0], cwx_ref, cbx_ref))
    bc = _silu(_causal_conv_carried(extbc_scr, bc_ref[0], cwbc_ref, cbbc_ref))

    dt = _softplus(dt_ref[0] + dtb_ref[...])
    adt = dt * (-jnp.exp(alog_ref[...]))
    row = lax.broadcasted_iota(jnp.int32, (q, q), 0)
    col = lax.broadcasted_iota(jnp.int32, (q, q), 1)
    causal = row >= col
    tri = jnp.where(causal, 1.0, 0.0).astype(BF16)
    h3, m3, l3 = _split3(adt)
    acs = _dot(tri, h3) + _dot(tri, m3) + _dot(tri, l3)
    eacs = jnp.exp(acs)
    acs_t = acs.T
    dt_t = dt.T
    lane = lax.broadcasted_iota(jnp.int32, (1, LANES), 1)
    lo_half = lane < SSD_HEAD_DIM

    for g in range(N_SSD_GROUPS):
        bm = bc[:, g * D_STATE:(g + 1) * D_STATE]
        cm = bc[:, (N_SSD_GROUPS + g) * D_STATE:(N_SSD_GROUPS + g + 1) * D_STATE]
        cb = jnp.where(causal, _dot_nt(cm.astype(BF16), bm.astype(BF16)), 0.0)
        bm_t = bm.T
        for pr in range(HEADS_PER_GROUP // 2):
            h0 = g * HEADS_PER_GROUP + 2 * pr
            lanes = pl.ds(h0 * SSD_HEAD_DIM, LANES)
            x_pair = xs[:, h0 * SSD_HEAD_DIM:h0 * SSD_HEAD_DIM + LANES]
            st_pair = st_scr[:, lanes]
            lhs_y, lhs_s, decs = [], [], []
            for h in (h0, h0 + 1):
                a_col = acs[:, h:h + 1]
                a_row = acs_t[h:h + 1, :]
                dt_row = dt_t[h:h + 1, :]
                a_last = acs_t[h:h + 1, q - 1:q]
                decay = jnp.exp(jnp.minimum(a_col - a_row, 0.0))
                lhs_y.append((cb * decay * dt_row).astype(BF16))
                lhs_s.append((bm_t * (dt_row * jnp.exp(a_last - a_row))).astype(BF16))
                decs.append(jnp.exp(a_last))
            for h in (h0, h0 + 1):
                lhs_y.append((cm * eacs[:, h:h + 1]).astype(BF16))
            x_top = jnp.where(lo_half, x_pair, 0.0).astype(BF16)
            x_bot = jnp.where(lo_half, 0.0, x_pair).astype(BF16)
            s_top = jnp.where(lo_half, st_pair, 0.0).astype(BF16)
            s_bot = jnp.where(lo_half, 0.0, st_pair).astype(BF16)
            x_bd = jnp.concatenate([x_top, x_bot], axis=0)
            rhs_y = jnp.concatenate([x_bd, s_top, s_bot], axis=0)
            y_pair = _dot(jnp.concatenate(lhs_y, axis=1), rhs_y)
            ds_pair = _dot(jnp.concatenate(lhs_s, axis=1), x_bd)
            dskip = dskip_ref[:, lanes]
            y_scr[:, lanes] = y_pair + dskip * x_pair
            st_scr[:, lanes] = st_pair * jnp.where(lo_half, decs[0], decs[1]) + ds_pair

    u_ref[0] = _gated_group_norm(y_scr[...], z_ref[0], gn_ref).astype(u_ref.dtype)

    @pl.when(c == pl.num_programs(1) - 1)
    def _():
        for pr in range(N_SSD_HEADS // 2):
            t = st_scr[:, pl.ds(pr * LANES, LANES)].T
            fin_ref[0, 2 * pr] = t[:SSD_HEAD_DIM]
            fin_ref[0, 2 * pr + 1] = t[SSD_HEAD_DIM:]


def _ssd(proj, cwx, cbx, cwbc, cbbc, dtb, alog, dskip, gn):
    nb, rows, _ = proj.shape
    q = CHUNK
    full = lambda shape: pl.BlockSpec(shape, lambda b, c: (0,) * len(shape))
    return pl.pallas_call(
        _ssd_kernel,
        grid=(nb, rows // q),
        in_specs=[
            pl.BlockSpec((1, q, D_INNER), lambda b, c: (b, c, COL_Z // D_INNER)),
            pl.BlockSpec((1, q, D_INNER), lambda b, c: (b, c, COL_X // D_INNER)),
            pl.BlockSpec((1, q, BC_DIM), lambda b, c: (b, c, COL_BC // BC_DIM)),
            pl.BlockSpec((1, q, LANES), lambda b, c: (b, c, COL_DT // LANES)),
            full((CONV_W, D_INNER)), full((1, D_INNER)), full((CONV_W, BC_DIM)), full((1, BC_DIM)),
            full((1, LANES)), full((1, LANES)), full((1, D_INNER)), full((1, D_INNER)),
        ],
        out_specs=[
            pl.BlockSpec((1, q, D_INNER), lambda b, c: (b, c, 0)),
            pl.BlockSpec((1, N_SSD_HEADS, SSD_HEAD_DIM, D_STATE), lambda b, c: (b, 0, 0, 0)),
        ],
        out_shape=[
            jax.ShapeDtypeStruct((nb, rows, D_INNER), BF16),
            jax.ShapeDtypeStruct((nb, N_SSD_HEADS, SSD_HEAD_DIM, D_STATE), F32),
        ],
        scratch_shapes=[
            pltpu.VMEM((SUBLANES, D_INNER), F32),
            pltpu.VMEM((SUBLANES, BC_DIM), F32),
            pltpu.VMEM((D_STATE, D_INNER), F32),
            pltpu.VMEM((q, D_INNER), F32),
        ],
        compiler_params=pltpu.CompilerParams(
            dimension_semantics=("parallel", "arbitrary"), vmem_limit_bytes=VMEM_LIMIT),
        name="ssd",
    )(proj, proj, proj, proj, cwx, cbx, cwbc, cbbc, dtb, alog, dskip, gn)


def _ssd_step_kernel(n_valid, z_ref, x_ref, bc_ref, dt_ref, px_ref, pbc_ref, st_ref, cwx_ref, cbx_ref,
                     cwbc_ref, cbbc_ref, dtb_ref, alog_ref, dskip_ref, gn_ref, exp_ref,
                     u_ref, nst_ref, extx_scr, extbc_scr):
    for sq in range(z_ref.shape[0]):
        _ssd_step_one(n_valid, z_ref.at[sq], x_ref.at[sq], bc_ref.at[sq], dt_ref.at[sq], px_ref.at[sq],
                      pbc_ref.at[sq], st_ref.at[sq], cwx_ref, cbx_ref, cwbc_ref, cbbc_ref, dtb_ref, alog_ref,
                      dskip_ref, gn_ref, exp_ref, u_ref.at[sq], nst_ref.at[sq], extx_scr.at[sq], extbc_scr.at[sq])


def _ssd_step_one(n_valid, z_ref, x_ref, bc_ref, dt_ref, px_ref, pbc_ref, st_ref, cwx_ref, cbx_ref,
                  cwbc_ref, cbbc_ref, dtb_ref, alog_ref, dskip_ref, gn_ref, exp_ref,
                  u_ref, nst_ref, extx_scr, extbc_scr):
    q = SAMPLE_ROWS
    gw = D_INNER // N_SSD_GROUPS
    xs = _silu(_causal_conv(extx_scr, x_ref[...], px_ref[...], cwx_ref, cbx_ref, q))
    bc = _silu(_causal_conv(extbc_scr, bc_ref[...], pbc_ref[...], cwbc_ref, cbbc_ref, q))

    rowi = lax.broadcasted_iota(jnp.int32, (q, 1), 0)
    dt = jnp.where(rowi < n_valid, _softplus(dt_ref[...] + dtb_ref[...]), 0.0)
    adt = dt * (-jnp.exp(alog_ref[...]))
    acs = jnp.zeros_like(adt)
    for s in range(n_valid):
        acs = acs + jnp.where(rowi >= s, adt[s:s + 1, :], 0.0)
    expand = exp_ref[...]
    dt_e = _dot_exact_rhs(dt, expand)
    acs_e = _dot_exact_rhs(acs, expand)
    last_e = acs_e[q - 1:q, :]

    x_end = xs * dt_e * jnp.exp(last_e - acs_e)
    dec3 = _split3(jnp.exp(last_e))
    zrow = jnp.zeros((q - 3, D_INNER), BF16)
    dec_rows = jnp.concatenate([dec3[0], dec3[1], dec3[2], zrow], axis=0)
    ones = jnp.ones((q, D_STATE), BF16)

    y_off, cbs = [], []
    for g in range(N_SSD_GROUPS):
        bm = bc[:, g * D_STATE:(g + 1) * D_STATE].astype(BF16)
        cm = bc[:, (N_SSD_GROUPS + g) * D_STATE:(N_SSD_GROUPS + g + 1) * D_STATE].astype(BF16)
        rows = pl.ds(g * gw, gw)
        st = st_ref[rows, :]
        y_off.append(_dot_nt(cm, st.astype(BF16)))
        cbs.append(_dot_nt(cm, bm))
        d_st = _dot_tn(x_end[:, g * gw:(g + 1) * gw].astype(BF16), bm)
        dec = _dot_tn(dec_rows[:, g * gw:(g + 1) * gw], ones)
        nst_ref[rows, :] = st * dec + d_st

    y = jnp.concatenate(y_off, axis=-1) * jnp.exp(acs_e) + dskip_ref[...] * xs
    for s in range(n_valid):
        keep = rowi >= s
        decay = jnp.where(keep, jnp.exp(jnp.where(keep, acs_e - acs_e[s:s + 1, :], 0.0)), 0.0)
        cb_e = jnp.concatenate([jnp.broadcast_to(cb[:, s:s + 1], (q, gw)) for cb in cbs], axis=-1)
        y = y + decay * cb_e * (dt_e[s:s + 1, :] * xs[s:s + 1, :])
    u_ref[...] = _gated_group_norm(y, z_ref[...], gn_ref).astype(u_ref.dtype)


def _ssd_step(proj, prev_x, prev_bc, state, cwx, cbx, cwbc, cbbc, dtb, alog, dskip, gn, expand, n_valid):
    nb = proj.shape[0]
    q = SAMPLE_ROWS
    seqs = math.gcd(nb, 4)
    full = lambda shape: pl.BlockSpec(shape, lambda b: (0,) * len(shape))
    return pl.pallas_call(
        functools.partial(_ssd_step_kernel, n_valid),
        grid=(nb // seqs,),
        in_specs=[
            pl.BlockSpec((seqs, q, D_INNER), lambda b: (b, 0, COL_Z // D_INNER)),
            pl.BlockSpec((seqs, q, D_INNER), lambda b: (b, 0, COL_X // D_INNER)),
            pl.BlockSpec((seqs, q, BC_DIM), lambda b: (b, 0, COL_BC // BC_DIM)),
            pl.BlockSpec((seqs, q, LANES), lambda b: (b, 0, COL_DT // LANES)),
            pl.BlockSpec((seqs, CONV_W - 1, D_INNER), lambda b: (b, 0, 0)),
            pl.BlockSpec((seqs, CONV_W - 1, BC_DIM), lambda b: (b, 0, 0)),
            pl.BlockSpec((seqs, D_INNER, D_STATE), lambda b: (b, 0, 0)),
            full((CONV_W, D_INNER)), full((1, D_INNER)), full((CONV_W, BC_DIM)), full((1, BC_DIM)),
            full((1, LANES)), full((1, LANES)), full((1, D_INNER)), full((1, D_INNER)),
            full((LANES, D_INNER)),
        ],
        out_specs=[
            pl.BlockSpec((seqs, q, D_INNER), lambda b: (b, 0, 0)),
            pl.BlockSpec((seqs, D_INNER, D_STATE), lambda b: (b, 0, 0)),
        ],
        out_shape=[
            jax.ShapeDtypeStruct((nb, q, D_INNER), BF16),
            jax.ShapeDtypeStruct((nb, D_INNER, D_STATE), F32),
        ],
        scratch_shapes=[
            pltpu.VMEM((seqs, 2 * SUBLANES, D_INNER), F32),
            pltpu.VMEM((seqs, 2 * SUBLANES, BC_DIM), F32),
        ],
        compiler_params=pltpu.CompilerParams(
            dimension_semantics=("parallel",), vmem_limit_bytes=VMEM_LIMIT),
        name="ssd_step",
    )(proj, proj, proj, proj, prev_x, prev_bc, state, cwx, cbx, cwbc, cbbc, dtb, alog, dskip, gn, expand)


def _attention_kernel(qb, prev_always_valid, sinks_ref, q_ref, kp_ref, vp_ref, kc_ref, vc_ref, o_ref):
    wb = WINDOW
    lane = lax.broadcasted_iota(jnp.int32, (1, LANES), 1)
    lo_half = lane < ATTN_HEAD_DIM
    keys_major = qb % LANES == 0
    t_ax, s_ax = (1, 0) if keys_major else (0, 1)
    shape_p = (wb, qb) if keys_major else (qb, wb)
    t_p = lax.broadcasted_iota(jnp.int32, shape_p, t_ax)
    s_p = lax.broadcasted_iota(jnp.int32, shape_p, s_ax)
    rel_p = wb + t_p - s_p
    ok_p = rel_p < WINDOW
    if not prev_always_valid:
        ok_p = jnp.logical_and(ok_p, pl.program_id(1) > 0)
    t_c = lax.broadcasted_iota(jnp.int32, (qb, qb), t_ax)
    s_c = lax.broadcasted_iota(jnp.int32, (qb, qb), s_ax)
    rel_c = t_c - s_c
    ok_c = rel_c >= 0
    rel_p = rel_p.astype(F32)
    rel_c = rel_c.astype(F32)
    scale = ATTN_HEAD_DIM ** -0.5

    pairs = KV_REP
    half = pairs * qb

    def one_sequence(sq, m, bias_p, bias_c, sink):
        kv_lanes = pl.ds(m * LANES, LANES)
        kp = kp_ref[sq, :, kv_lanes].astype(BF16)
        kc = kc_ref[sq, :, kv_lanes].astype(BF16)
        vp = vp_ref[sq, :, kv_lanes]
        vc = vc_ref[sq, :, kv_lanes]
        vp_lo, vp_hi = jnp.where(lo_half, vp, 1.0).astype(BF16), jnp.where(lo_half, 1.0, vp).astype(BF16)
        vc_lo, vc_hi = jnp.where(lo_half, vc, 1.0).astype(BF16), jnp.where(lo_half, 1.0, vc).astype(BF16)
        q_lo, q_hi = [], []
        for i in range(pairs):
            q_pair = q_ref[sq, :, pl.ds((pairs * m + i) * LANES, LANES)] * scale
            q_lo.append(jnp.where(lo_half, q_pair, 0.0).astype(BF16))
            q_hi.append(jnp.where(lo_half, 0.0, q_pair).astype(BF16))
        qs = jnp.concatenate(q_lo + q_hi, axis=0)
        s_p = _dot_nt(qs, kp) + bias_p
        s_c = _dot_nt(qs, kc) + bias_c
        if qb == wb:
            mx = jnp.max(jnp.maximum(s_p, s_c), axis=-1, keepdims=True)
        else:
            mx = jnp.maximum(jnp.max(s_p, axis=-1, keepdims=True), jnp.max(s_c, axis=-1, keepdims=True))
        mx = jnp.maximum(mx, sink)
        p_p = jnp.exp(s_p - mx).astype(BF16)
        p_c = jnp.exp(s_c - mx).astype(BF16)
        e_sink = jnp.exp(sink - mx)
        o_lo = _dot(p_p[:half], vp_lo) + _dot(p_c[:half], vc_lo)
        o_hi = _dot(p_p[half:], vp_hi) + _dot(p_c[half:], vc_hi)
        for i in range(pairs):
            a = o_lo[i * qb:(i + 1) * qb]
            b = o_hi[i * qb:(i + 1) * qb]
            num = jnp.where(lo_half, a, b)
            den = (pltpu.roll(jnp.where(lo_half, b, a), ATTN_HEAD_DIM, 1)
                   + jnp.where(lo_half, e_sink[i * qb:(i + 1) * qb], e_sink[half + i * qb:half + (i + 1) * qb]))
            o_ref[sq, :, pl.ds((pairs * m + i) * LANES, LANES)] = (num / den).astype(o_ref.dtype)

    def one_sequence_keys_major(sq, m, bias_p, bias_c, sink):
        kv_lanes = pl.ds(m * LANES, LANES)
        kp = kp_ref[sq, :, kv_lanes].astype(BF16)
        kc = kc_ref[sq, :, kv_lanes].astype(BF16)
        vp = vp_ref[sq, :, kv_lanes].astype(BF16)
        vc = vc_ref[sq, :, kv_lanes].astype(BF16)
        q_lo, q_hi = [], []
        for i in range(pairs):
            q_pair = q_ref[sq, :, pl.ds((pairs * m + i) * LANES, LANES)] * scale
            q_lo.append(jnp.where(lo_half, q_pair, 0.0).astype(BF16))
            q_hi.append(jnp.where(lo_half, 0.0, q_pair).astype(BF16))
        qs = jnp.concatenate(q_lo + q_hi, axis=0)
        s_p = _dot_nt(kp, qs) + bias_p
        s_c = _dot_nt(kc, qs) + bias_c
        mx = jnp.maximum(jnp.maximum(jnp.max(s_p, axis=0, keepdims=True), jnp.max(s_c, axis=0, keepdims=True)),
                         sink)
        p_p = jnp.exp(s_p - mx)
        p_c = jnp.exp(s_c - mx)
        den = jnp.sum(p_p, axis=0, keepdims=True) + jnp.sum(p_c, axis=0, keepdims=True) + jnp.exp(sink - mx)
        o_t = (_dot_tn(vp, p_p.astype(BF16)) + _dot_tn(vc, p_c.astype(BF16))) / den
        row_lo = lax.broadcasted_iota(jnp.int32, (LANES, 1), 0) < ATTN_HEAD_DIM
        for i in range(pairs):
            blk = jnp.where(row_lo, o_t[:, i * qb:(i + 1) * qb], o_t[:, (pairs + i) * qb:(pairs + i + 1) * qb])
            o_ref[sq, :, pl.ds((pairs * m + i) * LANES, LANES)] = blk.T.astype(o_ref.dtype)

    for m in range(KV_DIM // LANES):
        heads = [2 * pairs * m + r for r in range(2 * pairs)]
        slopes = [2.0 ** (-8.0 * (h + 1) / N_ATTN_HEADS) for h in heads]
        bias_p = jnp.concatenate([jnp.where(ok_p, -sl * rel_p, NEG_BIG) for sl in slopes], axis=t_ax)
        bias_c = jnp.concatenate([jnp.where(ok_c, -sl * rel_c, NEG_BIG) for sl in slopes], axis=t_ax)
        one = (1, qb) if keys_major else (qb, 1)
        sink = jnp.concatenate([jnp.full(one, sinks_ref[h], F32) for h in heads], axis=t_ax)
        for sq in range(q_ref.shape[0]):
            (one_sequence_keys_major if keys_major else one_sequence)(sq, m, bias_p, bias_c, sink)


def _attention(proj, sinks, qb, prev_kv=None, seqs=1):
    nb, rows, _ = proj.shape
    wb = WINDOW
    cur = lambda col: pl.BlockSpec((seqs, qb, KV_DIM), lambda b, n: (b, n, col // KV_DIM))
    if prev_kv is None:
        prev = lambda col: pl.BlockSpec((seqs, wb, KV_DIM),
                                        lambda b, n: (b, jnp.maximum(n - 1, 0), col // KV_DIM))
        kp, vp = proj, proj
        prev_specs = [prev(COL_K), prev(COL_V)]
    else:
        kp, vp = prev_kv
        prev_specs = [pl.BlockSpec((seqs, wb, KV_DIM), lambda b, n: (b, 0, 0))] * 2
    return pl.pallas_call(
        functools.partial(_attention_kernel, qb, prev_kv is not None),
        grid=(nb // seqs, rows // qb),
        in_specs=[
            pl.BlockSpec(memory_space=pltpu.SMEM),
            pl.BlockSpec((seqs, qb, D_MODEL), lambda b, n: (b, n, COL_Q // D_MODEL)),
            *prev_specs,
            cur(COL_K), cur(COL_V),
        ],
        out_specs=pl.BlockSpec((seqs, qb, D_MODEL), lambda b, n: (b, n, 0)),
        out_shape=jax.ShapeDtypeStruct((nb, rows, D_MODEL), BF16),
        compiler_params=pltpu.CompilerParams(
            dimension_semantics=("parallel", "arbitrary"), vmem_limit_bytes=VMEM_LIMIT),
        name="attention",
    )(sinks, proj, kp, vp, proj, proj)


def _post_mix_kernel(valid_rows, u_ref, a_ref, gs_ref, ga_ref, x_ref, wos_ref, woa_ref, wout_ref, gpm_ref,
                     gpf_ref, g1_ref, sc2_ref, sh2_ref, wr_ref, br_ref, cnt_in_ref,
                     x1_ref, h2_ref, ti_ref, tg_ref, cnt_ref, cnt_scr):
    @pl.when(jnp.logical_and(pl.program_id(0) == 0, pl.program_id(1) == 0))
    def _():
        cnt_scr[...] = cnt_in_ref[...]

    y_ssd = _dot(u_ref[0], wos_ref[...])
    y_attn = _dot(a_ref[0], woa_ref[...])
    merged = jax.nn.sigmoid(gs_ref[0]) * y_ssd + jax.nn.sigmoid(ga_ref[0]) * y_attn
    mixed = _dot(merged.astype(BF16), wout_ref[...])
    x1 = x_ref[0] + g1_ref[0] * (_rms(mixed) * gpm_ref[...])
    x1_ref[0] = x1
    h2 = _rms(x1) * gpf_ref[...] * (1.0 + sc2_ref[0]) + sh2_ref[0]
    _store_token_tiles(h2_ref.at[0], h2)

    h_hi = h2.astype(BF16)
    h_lo = (h2 - h_hi.astype(F32)).astype(BF16)
    w = wr_ref[...]
    w_hi = w.astype(BF16)
    w_lo = (w - w_hi.astype(F32)).astype(BF16)
    logits = _dot(h_hi, w_hi) + (_dot(h_hi, w_lo) + _dot(h_lo, w_hi)) + br_ref[...]
    lane = lax.broadcasted_iota(jnp.int32, logits.shape, 1)
    idx_out = jnp.zeros(logits.shape, jnp.int32)
    val_out = jnp.zeros(logits.shape, F32)
    top = None
    denom = None
    idxs = []
    for k in range(TOP_K):
        m = jnp.max(logits, axis=-1, keepdims=True)
        idx = jnp.min(jnp.where(logits == m, lane, LANES), axis=-1, keepdims=True)
        if k == 0:
            top = m
            e = jnp.ones_like(m)
            denom = e
        else:
            e = jnp.exp(m - top)
            denom = denom + e
        idxs.append(idx)
        idx_out = jnp.where(lane == k, idx, idx_out)
        val_out = jnp.where(lane == k, e, val_out)
        logits = jnp.where(lane == idx, NEG_BIG * 2, logits)
    tg_ref[0] = val_out / denom
    ti_ref[0] = idx_out

    rowi = lax.broadcasted_iota(jnp.int32, (logits.shape[0], 1), 0)
    valid = jnp.bitwise_and(rowi, SAMPLE_ROWS - 1) < valid_rows
    picked = jnp.zeros(logits.shape, F32)
    for idx in idxs:
        picked = picked + jnp.where(jnp.logical_and(lane == idx, valid), 1.0, 0.0)
    cnt_scr[...] = cnt_scr[...] + jnp.sum(picked, axis=0, keepdims=True)
    cnt_ref[...] = cnt_scr[...]


def _post_mix(u, attn, proj, x, wos, woa, wout, gpm, gpf, gate1, scale2, shift2, wr, br, counts, *,
              tm, per_row, valid_rows):
    nb, rows, _ = x.shape
    row_spec = lambda w, col=0: pl.BlockSpec((1, tm, w), lambda b, i: (b, i, col // w))
    full = lambda shape: pl.BlockSpec(shape, lambda b, i: (0,) * len(shape))
    return pl.pallas_call(
        functools.partial(_post_mix_kernel, valid_rows),
        grid=(nb, rows // tm),
        in_specs=[
            row_spec(D_INNER), row_spec(D_MODEL), row_spec(D_MODEL, COL_GS), row_spec(D_MODEL, COL_GA),
            row_spec(D_MODEL),
            full((D_INNER, D_MODEL)), full((D_MODEL, D_MODEL)), full((D_MODEL, D_MODEL)),
            full((1, D_MODEL)), full((1, D_MODEL)),
            _mod_spec(per_row, tm), _mod_spec(per_row, tm), _mod_spec(per_row, tm),
            full((D_MODEL, LANES)), full((1, LANES)), full((1, LANES)),
        ],
        out_specs=[row_spec(D_MODEL), pl.BlockSpec((1, tm * TILE_ROWS, LANES), lambda b, i: (b, i, 0)),
                   row_spec(LANES), row_spec(LANES), full((1, LANES))],
        out_shape=[
            jax.ShapeDtypeStruct((nb, rows, D_MODEL), F32),
            jax.ShapeDtypeStruct((nb, rows * TILE_ROWS, LANES), F32),
            jax.ShapeDtypeStruct((nb, rows, LANES), jnp.int32),
            jax.ShapeDtypeStruct((nb, rows, LANES), F32),
            jax.ShapeDtypeStruct((1, LANES), F32),
        ],
        scratch_shapes=[pltpu.VMEM((1, LANES), F32)],
        compiler_params=pltpu.CompilerParams(
            dimension_semantics=("arbitrary", "arbitrary"), vmem_limit_bytes=BIG_VMEM_LIMIT),
        name="post_mix",
    )(u, attn, proj, proj, x, wos, woa, wout, gpm, gpf, gate1, scale2, shift2, wr, br, counts)


def _moe_kernel(n_tokens, be_ref, na_ref, nv_ref, idx_hbm, h_hbm, wu_ref, bu_ref, wd_ref, bd_ref, ys_hbm,
                idx_smem, xbuf, obuf, wu_bf, wd_bf, sem_idx, sem_in, sem_out):
    i = pl.program_id(0)
    na = na_ref[0]
    slot = lax.rem(i, 2)
    other = 1 - slot

    def idx_copy(block, s):
        return pltpu.make_async_copy(idx_hbm.at[block], idx_smem.at[s], sem_idx.at[s])

    block_rows = MOE_ROWS * TILE_ROWS

    def tile(ref, first_row):
        return ref.at[pl.ds(pl.multiple_of(first_row, TILE_ROWS), TILE_ROWS)]

    group_rows = MOE_GROUP * TILE_ROWS

    def per_group(block, fn):
        n_real = nv_ref[block]
        for g in range(MOE_ROWS // MOE_GROUP):
            pl.when(g * MOE_GROUP < n_real)(functools.partial(fn, g))

    def group_of(buf, s, g):
        return buf.at[s, pl.ds(g * group_rows, group_rows)]

    def start_gather(s, block):
        def group(g):
            for r in range(g * MOE_GROUP, (g + 1) * MOE_GROUP):
                pltpu.make_async_copy(tile(h_hbm, idx_smem[s, r]), xbuf.at[s, pl.ds(r * TILE_ROWS, TILE_ROWS)],
                                      sem_in.at[s]).start()
        per_group(block, group)

    def wait_gather(s, block):
        per_group(block, lambda g: pltpu.make_async_copy(
            h_hbm.at[pl.ds(0, group_rows)], group_of(xbuf, s, g), sem_in.at[s]).wait())

    def start_scatter(s, block):
        def group(g):
            for r in range(g * MOE_GROUP, (g + 1) * MOE_GROUP):
                pltpu.make_async_copy(obuf.at[s, pl.ds(r * TILE_ROWS, TILE_ROWS)],
                                      tile(ys_hbm, idx_smem[s, MOE_ROWS + r]), sem_out.at[s]).start()
        per_group(block, group)

    def wait_scatter(s, block):
        per_group(block, lambda g: pltpu.make_async_copy(
            group_of(obuf, s, g), ys_hbm.at[pl.ds(0, group_rows)], sem_out.at[s]).wait())

    @pl.when(i == 0)
    def _():
        obuf[0] = jnp.zeros((block_rows, LANES), F32)
        xbuf[...] = jnp.zeros(xbuf.shape, F32)
        plane_rows = ys_hbm.shape[0] // (TOP_K * TILE_ROWS)
        spare = [pltpu.make_async_copy(
            obuf.at[0], ys_hbm.at[pl.ds((k * plane_rows + n_tokens + hf * MOE_ROWS) * TILE_ROWS, block_rows)],
            sem_out.at[0]) for k in range(TOP_K) for hf in range(2)]
        for cp in spare:
            cp.start()
        for cp in spare:
            cp.wait()
        first = idx_copy(0, 0)
        first.start()
        first.wait()
        start_gather(0, 0)

        @pl.when(na > 1)
        def _():
            idx_copy(1, 1).start()

    @pl.when(i < na)
    def _():
        wait_gather(slot, i)

        @pl.when(i + 1 < na)
        def _():
            idx_copy(i + 1, other).wait()
            start_gather(other, i + 1)

        @pl.when(i >= 2)
        def _():
            wait_scatter(slot, i - 2)

        @pl.when(jnp.logical_or(i == 0, be_ref[i] != be_ref[jnp.maximum(i - 1, 0)]))
        def _():
            wu_bf[...] = wu_ref[0].astype(BF16)
            wd_bf[...] = wd_ref[0].astype(BF16)

        x = _load_token_tiles(xbuf.at[slot], MOE_ROWS).astype(BF16)
        up = _dot(x, wu_bf[...]) + bu_ref[0]
        glu = jnp.minimum(up[:, :D_FF], SWIGLU_LIMIT)
        lin = jnp.clip(up[:, D_FF:], -SWIGLU_LIMIT, SWIGLU_LIMIT)
        act = glu * jax.nn.sigmoid(SWIGLU_ALPHA * glu) * (lin + 1.0)
        _store_token_tiles(obuf.at[slot], _dot(act.astype(BF16), wd_bf[...]) + bd_ref[0])
        start_scatter(slot, i)

        @pl.when(i + 2 < na)
        def _():
            idx_copy(i + 2, slot).start()

        @pl.when(i == na - 1)
        def _():
            wait_scatter(slot, i)

            @pl.when(i >= 1)
            def _():
                wait_scatter(other, i - 1)


def _moe(idx, h_tiles, ys_rows, block_expert, n_active, n_real, wu, bu, wd, bd):
    n_blocks = idx.shape[0]
    grid_spec = pltpu.PrefetchScalarGridSpec(
        num_scalar_prefetch=3,
        grid=(n_blocks,),
        in_specs=[
            pl.BlockSpec(memory_space=pl.ANY),
            pl.BlockSpec(memory_space=pl.ANY),
            pl.BlockSpec((1, D_MODEL, 2 * D_FF), lambda i, be, na, nv: (be[i], 0, 0)),
            pl.BlockSpec((1, 1, 2 * D_FF), lambda i, be, na, nv: (be[i], 0, 0)),
            pl.BlockSpec((1, D_FF, D_MODEL), lambda i, be, na, nv: (be[i], 0, 0)),
            pl.BlockSpec((1, 1, D_MODEL), lambda i, be, na, nv: (be[i], 0, 0)),
        ],
        out_specs=pl.BlockSpec(memory_space=pl.ANY),
        scratch_shapes=[
            pltpu.SMEM((2, 2 * MOE_ROWS), jnp.int32),
            pltpu.VMEM((2, MOE_ROWS * TILE_ROWS, LANES), F32),
            pltpu.VMEM((2, MOE_ROWS * TILE_ROWS, LANES), F32),
            pltpu.VMEM((D_MODEL, 2 * D_FF), BF16),
            pltpu.VMEM((D_FF, D_MODEL), BF16),
            pltpu.SemaphoreType.DMA((2,)),
            pltpu.SemaphoreType.DMA((2,)),
            pltpu.SemaphoreType.DMA((2,)),
        ],
    )
    return pl.pallas_call(
        functools.partial(_moe_kernel, h_tiles.shape[0] // TILE_ROWS),
        grid_spec=grid_spec,
        out_shape=jax.ShapeDtypeStruct((ys_rows * TILE_ROWS, LANES), F32),
        compiler_params=pltpu.CompilerParams(
            dimension_semantics=("arbitrary",), vmem_limit_bytes=BIG_VMEM_LIMIT),
        name="moe",
    )(block_expert, n_active, n_real, idx, h_tiles, wu, bu, wd, bd)


def _combine_kernel(ys_ref, tg_ref, x1_ref, gpost_ref, g2_ref, o_ref):
    gates = tg_ref[0]
    tm = gates.shape[0]
    f = gates[:, 0:1] * _load_token_tiles(ys_ref.at[0], tm)
    for k in range(1, TOP_K):
        f = f + gates[:, k:k + 1] * _load_token_tiles(ys_ref.at[k], tm)
    o_ref[0] = x1_ref[0] + g2_ref[0] * (_rms(f) * gpost_ref[...])


def _combine(ys, row0, tg, x1, gpost, gate2, *, tm, per_row):
    nb, rows, _ = x1.shape
    per_b = rows // tm
    base = row0 // tm
    return pl.pallas_call(
        _combine_kernel,
        grid=(nb, per_b),
        in_specs=[
            pl.BlockSpec((TOP_K, tm * TILE_ROWS, LANES), lambda b, i: (0, base + b * per_b + i, 0)),
            pl.BlockSpec((1, tm, LANES), lambda b, i: (b, i, 0)),
            pl.BlockSpec((1, tm, D_MODEL), lambda b, i: (b, i, 0)),
            pl.BlockSpec((1, D_MODEL), lambda b, i: (0, 0)),
            _mod_spec(per_row, tm),
        ],
        out_specs=pl.BlockSpec((1, tm, D_MODEL), lambda b, i: (b, i, 0)),
        out_shape=jax.ShapeDtypeStruct((nb, rows, D_MODEL), F32),
        compiler_params=pltpu.CompilerParams(
            dimension_semantics=("parallel", "parallel"), vmem_limit_bytes=VMEM_LIMIT),
        name="combine",
    )(ys, tg, x1, gpost, gate2)


def _route(top_idx, counts, plane_rows):
    n_tokens = top_idx.shape[0]
    n_slots = n_tokens * TOP_K
    experts = jnp.arange(N_EXPERTS, dtype=jnp.int32)
    padded = (counts + MOE_ROWS - 1) // MOE_ROWS * MOE_ROWS
    group_start = jnp.cumsum(counts) - counts
    padded_end = jnp.cumsum(padded)
    padded_start = padded_end - padded
    n_blocks = -(-n_slots // MOE_ROWS) + N_EXPERTS
    n_active = (padded_end[-1] // MOE_ROWS).astype(jnp.int32)
    block_start = jnp.arange(n_blocks, dtype=jnp.int32) * MOE_ROWS
    block_expert = jnp.sum(block_start[:, None] >= padded_end[None, :], axis=1)
    last_expert = jnp.max(jnp.where(counts > 0, experts, 0))
    block_expert = jnp.where(block_start < padded_end[-1], block_expert, last_expert).astype(jnp.int32)
    slot_id = jnp.arange(n_slots, dtype=jnp.int32).reshape(n_tokens, TOP_K)
    keys = jnp.sort((top_idx * n_slots + slot_id).reshape(-1))
    row = jnp.arange(n_blocks * MOE_ROWS, dtype=jnp.int32).reshape(n_blocks, MOE_ROWS)
    src = row + (group_start - padded_start)[block_expert][:, None]
    real = row < (padded_start + counts)[block_expert][:, None]
    slot = keys[jnp.clip(src, 0, n_slots - 1).reshape(-1)].reshape(n_blocks, MOE_ROWS) % n_slots
    token = slot // TOP_K
    spare = n_tokens + (jnp.arange(n_blocks, dtype=jnp.int32) % 2)[:, None] * MOE_ROWS + row % MOE_ROWS
    dest = jnp.where(real, (slot % TOP_K) * plane_rows + token, spare)
    idx = (jnp.concatenate([token, dest], axis=1) * TILE_ROWS).astype(jnp.int32)
    n_real = jnp.clip((padded_start + counts)[block_expert] - block_start, 0, MOE_ROWS)
    n_real = jnp.where(block_start < padded_end[-1], n_real, 0).astype(jnp.int32)
    return idx, block_expert, n_active.reshape(1), n_real


def _pad_lanes(v, value=0.0):
    return jnp.pad(v, [(0, 0)] * (v.ndim - 1) + [(0, LANES - v.shape[-1])], constant_values=value)


def kernel(x_prompt, x_sample, c_prompt, c_sample, cache_swa_k, cache_swa_v, state_conv, state_ssm, w_ada, b_ada, g_pre_mix, g_post_mix, g_pre_ffn, g_post_ffn, w_in, conv_w, conv_b, dt_bias, a_log, d_skip, g_ssm_norm, sinks, w_o_ssd, w_o_attn, w_out, w_router, b_router, w_up, b_up, w_down, b_down):
    depth = w_ada.shape[0]
    n_prompt, seq, _ = x_prompt.shape
    n_sample, dec_seq, _ = x_sample.shape
    yp = x_prompt
    ys_pad = jnp.pad(x_sample, ((0, 0), (0, SAMPLE_ROWS - dec_seq), (0, 0)))
    outs = [[] for _ in range(8)]
    expand = jnp.repeat(jnp.eye(LANES, N_SSD_HEADS, dtype=BF16), SSD_HEAD_DIM, axis=1)
    rows_s = n_sample * SAMPLE_ROWS
    n_p = n_prompt * seq
    n_s = n_sample * dec_seq
    n_tok = n_p + n_s
    tm_p = min(512, seq)
    tm_s = min(512, rows_s)
    tm_mix_p = min(512, seq)
    tm_mix_s = min(256, rows_s)
    assert seq % CHUNK == 0 and seq % tm_p == 0 and rows_s % tm_s == 0 and n_p % n_s == 0
    c_all = jnp.concatenate([c_prompt, c_sample], axis=0)
    c_rows = -(-c_all.shape[0] // SUBLANES) * SUBLANES
    c_all = jnp.pad(c_all, ((0, c_rows - c_all.shape[0]), (0, 0)))

    for l in range(depth):
        wi = w_in[l]
        o_xbc = D_INNER
        o_dt = o_xbc + CONV_DIM
        o_q = o_dt + N_SSD_HEADS
        o_k = o_q + D_MODEL
        o_v = o_k + KV_DIM
        o_gs = o_v + KV_DIM
        o_ga = o_gs + D_MODEL
        head_order = jnp.array(ATTN_HEAD_ORDER)
        w_q = wi[:, o_q:o_k].reshape(D_MODEL, N_ATTN_HEADS, ATTN_HEAD_DIM)[:, head_order].reshape(D_MODEL, D_MODEL)
        w_proj = jnp.concatenate([
            wi[:, :o_xbc], wi[:, o_xbc:o_xbc + D_INNER], w_q, wi[:, o_gs:o_ga], wi[:, o_ga:],
            wi[:, o_xbc + D_INNER:o_dt], wi[:, o_k:o_v], wi[:, o_v:o_gs], _pad_lanes(wi[:, o_dt:o_q])],
            axis=1).astype(BF16)
        cwx, cwbc = conv_w[l][:, :D_INNER], conv_w[l][:, D_INNER:]
        cbx, cbbc = conv_b[l][None, :D_INNER], conv_b[l][None, D_INNER:]
        dtb = _pad_lanes(dt_bias[l][None])
        alog = _pad_lanes(a_log[l][None])
        dskip = jnp.repeat(d_skip[l], SSD_HEAD_DIM)[None]
        gn = g_ssm_norm[l][None]
        wos, wout = w_o_ssd[l].astype(BF16), w_out[l].astype(BF16)
        woa = w_o_attn[l].reshape(N_ATTN_HEADS, ATTN_HEAD_DIM, D_MODEL)[head_order].reshape(D_MODEL, D_MODEL)
        woa = woa.astype(BF16)
        zero_counts = jnp.zeros((1, LANES), F32)
        wr = _pad_lanes(w_router[l])
        br = _pad_lanes(b_router[l][None], NEG_BIG)
        wu, wd = w_up[l], w_down[l]
        bu, bd = b_up[l][:, None, :], b_down[l][:, None, :]
        g_pm, g_pom, g_pf, g_pof = (v[l][None] for v in (g_pre_mix, g_post_mix, g_pre_ffn, g_post_ffn))

        ada = _ada(c_all, w_ada[l], b_ada[l][None])
        ada_p = [ada[:n_prompt, k * D_MODEL:(k + 1) * D_MODEL] for k in range(6)]
        ada_s = [ada[n_prompt:n_prompt + n_sample, k * D_MODEL:(k + 1) * D_MODEL] for k in range(6)]

        mods = [m[:, None, :] for m in ada_p]
        proj = _in_proj(yp, g_pm, mods[1], mods[0], w_proj, tm=min(1024, seq), per_row=False)
        u, ssm_p = _ssd(proj, cwx, cbx, cwbc, cbbc, dtb, alog, dskip, gn)
        attn = _attention(proj, sinks[l], WINDOW)
        x1_p, h2_p, ti_p, tg_p, cnt_p = _post_mix(
            u, attn, proj, yp, wos, woa, wout, g_pom, g_pf, mods[2], mods[4], mods[3], wr, br, zero_counts,
            tm=tm_mix_p, per_row=False, valid_rows=SAMPLE_ROWS)
        gate2_p = mods[5]
        kp = proj[:, seq - WINDOW:, COL_K:COL_K + KV_DIM].reshape(n_prompt, WINDOW, N_KV_HEADS, ATTN_HEAD_DIM)
        vp = proj[:, seq - WINDOW:, COL_V:COL_V + KV_DIM].reshape(n_prompt, WINDOW, N_KV_HEADS, ATTN_HEAD_DIM)
        cp = jnp.concatenate([proj[:, seq - (CONV_W - 1):, COL_X:COL_X + D_INNER],
                              proj[:, seq - (CONV_W - 1):, COL_BC:COL_BC + BC_DIM]], axis=-1)

        mods_s = [jnp.repeat(m, SAMPLE_ROWS, axis=0)[None] for m in ada_s]
        xs_flat = ys_pad.reshape(1, rows_s, D_MODEL)
        proj_s = _in_proj(xs_flat, g_pm, mods_s[1], mods_s[0], w_proj, tm=tm_s, per_row=True)
        proj_sb = proj_s.reshape(n_sample, SAMPLE_ROWS, PROJ_DIM)
        u_s, ssm_s = _ssd_step(
            proj_sb, state_conv[l][:, :, :D_INNER], state_conv[l][:, :, D_INNER:],
            state_ssm[l].reshape(n_sample, D_INNER, D_STATE),
            cwx, cbx, cwbc, cbbc, dtb, alog, dskip, gn, expand, dec_seq)
        k_prev = cache_swa_k[l].reshape(n_sample, -1, KV_DIM)
        v_prev = cache_swa_v[l].reshape(n_sample, -1, KV_DIM)
        attn_s = _attention(proj_sb, sinks[l], SAMPLE_ROWS, prev_kv=(k_prev, v_prev),
                            seqs=math.gcd(n_sample, 8))
        x1_s, h2_s, ti_s, tg_s, cnt_all = _post_mix(
            u_s.reshape(1, rows_s, D_INNER), attn_s.reshape(1, rows_s, D_MODEL), proj_s, xs_flat,
            wos, woa, wout, g_pom, g_pf, mods_s[2], mods_s[4], mods_s[3], wr, br, cnt_p,
            tm=tm_mix_s, per_row=True, valid_rows=dec_seq)
        wb = k_prev.shape[1]
        k_new = proj_sb[:, :dec_seq, COL_K:COL_K + KV_DIM]
        v_new = proj_sb[:, :dec_seq, COL_V:COL_V + KV_DIM]
        ks = jnp.concatenate([k_prev, k_new], axis=1)[:, -wb:].reshape(n_sample, wb, N_KV_HEADS, ATTN_HEAD_DIM)
        vs = jnp.concatenate([v_prev, v_new], axis=1)[:, -wb:].reshape(n_sample, wb, N_KV_HEADS, ATTN_HEAD_DIM)
        raw_xbc = jnp.concatenate([proj_sb[:, :dec_seq, COL_X:COL_X + D_INNER],
                                   proj_sb[:, :dec_seq, COL_BC:COL_BC + BC_DIM]], axis=-1)
        cs = jnp.concatenate([state_conv[l], raw_xbc], axis=1)[:, -(CONV_W - 1):]

        valid = lambda v: v.reshape(n_sample, SAMPLE_ROWS, -1)[:, :dec_seq].reshape(n_s, -1)
        h2_sv = h2_s.reshape(n_sample, SAMPLE_ROWS * TILE_ROWS, LANES)[:, :dec_seq * TILE_ROWS]
        h2_all = jnp.concatenate([h2_p.reshape(n_p * TILE_ROWS, LANES),
                                  h2_sv.reshape(n_s * TILE_ROWS, LANES)], axis=0)
        ti_all = jnp.concatenate([ti_p.reshape(n_p, LANES), valid(ti_s)], axis=0)
        counts = cnt_all[0, :N_EXPERTS].astype(jnp.int32)
        plane_rows = n_tok + 2 * MOE_ROWS
        idx, block_expert, n_active, n_real = _route(ti_all[:, :TOP_K], counts, plane_rows)
        ysel = _moe(idx, h2_all, TOP_K * plane_rows, block_expert, n_active, n_real, wu, bu, wd, bd)
        ysel = ysel.reshape(TOP_K, plane_rows * TILE_ROWS, LANES)

        yp = _combine(ysel, 0, tg_p, x1_p, g_pof, gate2_p, tm=tm_p, per_row=False)
        x1_sv = valid(x1_s)[None]
        tg_sv = valid(tg_s)[None]
        gate2_s = jnp.repeat(ada_s[5], dec_seq, axis=0)[None]
        ys_new = _combine(ysel, n_p, tg_sv, x1_sv, g_pof, gate2_s, tm=n_s, per_row=True)
        ys_new = ys_new.reshape(n_sample, dec_seq, D_MODEL)
        ys_pad = jnp.pad(ys_new, ((0, 0), (0, SAMPLE_ROWS - dec_seq), (0, 0)))

        for lst, v in zip(outs, (kp, vp, cp, ssm_p, ks, vs, cs,
                                 ssm_s.reshape(n_sample, N_SSD_HEADS, SSD_HEAD_DIM, D_STATE))):
            lst.append(v)

    return (yp, ys_pad[:, :dec_seq], *[jnp.stack(v) for v in outs])
```

```python
import functools
import math

import jax
import jax.numpy as jnp
from jax import lax
from jax.experimental import pallas as pl
from jax.experimental.pallas import tpu as pltpu

F32 = jnp.float32
BF16 = jnp.bfloat16

D_MODEL = 1024
D_INNER = 2 * D_MODEL
SSD_HEAD_DIM = 64
N_SSD_HEADS = D_INNER // SSD_HEAD_DIM
N_SSD_GROUPS = 4
HEADS_PER_GROUP = N_SSD_HEADS // N_SSD_GROUPS
D_STATE = 128
CONV_W = 4
BC_DIM = 2 * N_SSD_GROUPS * D_STATE
CONV_DIM = D_INNER + BC_DIM
CHUNK = 128
ATTN_HEAD_DIM = 64
N_ATTN_HEADS = D_MODEL // ATTN_HEAD_DIM
N_KV_HEADS = 4
KV_REP = N_ATTN_HEADS // N_KV_HEADS
KV_DIM = N_KV_HEADS * ATTN_HEAD_DIM
WINDOW = 128
N_EXPERTS = 32
TOP_K = 4
D_FF = D_MODEL
SWIGLU_LIMIT = 7.0
SWIGLU_ALPHA = 1.702
EPS = 1e-6

LANES = 128
SUBLANES = 8
NEG_BIG = -1e30

COL_Z = 0
COL_X = COL_Z + D_INNER
COL_Q = COL_X + D_INNER
COL_GS = COL_Q + D_MODEL
COL_GA = COL_GS + D_MODEL
COL_BC = COL_GA + D_MODEL
COL_K = COL_BC + BC_DIM
COL_V = COL_K + KV_DIM
COL_DT = COL_V + KV_DIM
PROJ_DIM = COL_DT + LANES
PROJ_TILE_N = PROJ_DIM // 3

ATTN_HEAD_ORDER = tuple(2 * KV_REP * (j // KV_REP) + (j % KV_REP) + KV_REP * hf
                        for j in range(N_ATTN_HEADS // 2) for hf in (0, 1))

TILE_ROWS = D_MODEL // LANES
assert TILE_ROWS == SUBLANES

SAMPLE_ROWS = 8
MOE_ROWS = 512
MOE_GROUP = 64
VMEM_LIMIT = 48 * 1024 * 1024
BIG_VMEM_LIMIT = 56 * 1024 * 1024


def _silu(v):
    return v * jax.nn.sigmoid(v)


def _softplus(v):
    return jnp.maximum(v, 0.0) + jnp.log(1.0 + jnp.exp(-jnp.abs(v)))


def _rms(v):
    return v * lax.rsqrt(jnp.mean(v * v, axis=-1, keepdims=True) + EPS)


def _split3(v):
    hi = v.astype(BF16)
    r1 = v - hi.astype(F32)
    mid = r1.astype(BF16)
    lo = (r1 - mid.astype(F32)).astype(BF16)
    return hi, mid, lo


def _store_token_tiles(ref, rows):
    n = rows.shape[0]
    for j in range(TILE_ROWS):
        ref[pl.ds(j, n, stride=TILE_ROWS), :] = rows[:, j * LANES:(j + 1) * LANES]


def _load_token_tiles(ref, n):
    return jnp.concatenate([ref[pl.ds(j, n, stride=TILE_ROWS), :] for j in range(TILE_ROWS)], axis=1)


def _dot(a, b):
    return jnp.dot(a, b, preferred_element_type=F32)


def _dot_nt(a, b):
    return lax.dot_general(a, b, (((1,), (1,)), ((), ())), preferred_element_type=F32)


def _dot_tn(a, b):
    return lax.dot_general(a, b, (((0,), (0,)), ((), ())), preferred_element_type=F32)


def _dot_exact_rhs(a, b_bf16):
    hi, mid, lo = _split3(a)
    return _dot(hi, b_bf16) + _dot(mid, b_bf16) + _dot(lo, b_bf16)


def _ada_kernel(c_ref, w_ref, b_ref, o_ref):
    o_ref[...] = _dot(_silu(c_ref[...]).astype(BF16), w_ref[...].astype(BF16)) + b_ref[...]


def _ada(c, w, b):
    rows, n = c.shape[0], w.shape[1]
    tn = n // 4
    return pl.pallas_call(
        _ada_kernel,
        grid=(n // tn,),
        in_specs=[
            pl.BlockSpec((rows, D_MODEL), lambda j: (0, 0)),
            pl.BlockSpec((D_MODEL, tn), lambda j: (0, j)),
            pl.BlockSpec((1, tn), lambda j: (0, j)),
        ],
        out_specs=pl.BlockSpec((rows, tn), lambda j: (0, j)),
        out_shape=jax.ShapeDtypeStruct((rows, n), F32),
        compiler_params=pltpu.CompilerParams(
            dimension_semantics=("parallel",), vmem_limit_bytes=VMEM_LIMIT),
        name="ada",
    )(c, w, b)


def _in_proj_kernel(x_ref, g_ref, sc_ref, sh_ref, w_ref, o_ref, h_scr):
    @pl.when(pl.program_id(2) == 0)
    def _():
        h = _rms(x_ref[0]) * g_ref[...] * (1.0 + sc_ref[0]) + sh_ref[0]
        h_scr[...] = h.astype(BF16)

    o_ref[0] = _dot(h_scr[...], w_ref[...])


def _mod_spec(per_row, tm):
    if per_row:
        return pl.BlockSpec((1, tm, D_MODEL), lambda b, i, *_: (b, i, 0))
    return pl.BlockSpec((1, 1, D_MODEL), lambda b, i, *_: (b, 0, 0))


def _in_proj(x, g, scale, shift, w, *, tm, per_row):
    nb, rows, _ = x.shape
    return pl.pallas_call(
        _in_proj_kernel,
        grid=(nb, rows // tm, PROJ_DIM // PROJ_TILE_N),
        in_specs=[
            pl.BlockSpec((1, tm, D_MODEL), lambda b, i, j: (b, i, 0)),
            pl.BlockSpec((1, D_MODEL), lambda b, i, j: (0, 0)),
            _mod_spec(per_row, tm),
            _mod_spec(per_row, tm),
            pl.BlockSpec((D_MODEL, PROJ_TILE_N), lambda b, i, j: (0, j)),
        ],
        out_specs=pl.BlockSpec((1, tm, PROJ_TILE_N), lambda b, i, j: (b, i, j)),
        out_shape=jax.ShapeDtypeStruct((nb, rows, PROJ_DIM), F32),
        scratch_shapes=[pltpu.VMEM((tm, D_MODEL), BF16)],
        compiler_params=pltpu.CompilerParams(
            dimension_semantics=("parallel", "parallel", "arbitrary"),
            vmem_limit_bytes=BIG_VMEM_LIMIT),
        name="in_proj",
    )(x, g, scale, shift, w)


def _causal_conv(ext_ref, raw, prev, w_ref, b_ref, rows):
    base = SUBLANES
    if prev is not None:
        ext_ref[pl.ds(base - (CONV_W - 1), CONV_W - 1), :] = prev
    ext_ref[pl.ds(base, rows), :] = raw
    out = b_ref[...] + raw * w_ref[pl.ds(CONV_W - 1, 1), :]
    for k in range(CONV_W - 1):
        out = out + ext_ref[pl.ds(base - (CONV_W - 1) + k, rows), :] * w_ref[pl.ds(k, 1), :]
    return out


def _causal_conv_carried(carry_ref, raw, w_ref, b_ref):
    rows = raw.shape[0]
    first = lax.broadcasted_iota(jnp.int32, (rows, 1), 0) == 0
    acc = raw * w_ref[pl.ds(0, 1), :]
    for k in range(1, CONV_W):
        moved = jnp.where(first, carry_ref[pl.ds(k - 1, 1), :], pltpu.roll(acc, 1, 0))
        carry_ref[pl.ds(k - 1, 1), :] = acc[rows - 1:rows, :]
        acc = raw * w_ref[pl.ds(k, 1), :] + moved
    return acc + b_ref[...]


def _gated_group_norm(y, z, g_ref):
    u = y * _silu(z)
    gw = D_INNER // N_SSD_GROUPS
    parts = []
    for g in range(N_SSD_GROUPS):
        ug = u[:, g * gw:(g + 1) * gw]
        parts.append(ug * lax.rsqrt(jnp.mean(ug * ug, axis=-1, keepdims=True) + EPS))
    return jnp.concatenate(parts, axis=-1) * g_ref[...]


def _ssd_kernel(z_ref, x_ref, bc_ref, dt_ref, cwx_ref, cbx_ref, cwbc_ref, cbbc_ref, dtb_ref, alog_ref,
                dskip_ref, gn_ref, u_ref, fin_ref, extx_scr, extbc_scr, st_scr, y_scr):
    q = CHUNK
    c = pl.program_id(1)

    @pl.when(c == 0)
    def _():
        extx_scr[...] = jnp.zeros_like(extx_scr)
        extbc_scr[...] = jnp.zeros_like(extbc_scr)
        st_scr[...] = jnp.zeros_like(st_scr)

    xs = _silu(_causal_conv_carried(extx_scr, x_ref[0], cwx_ref, cbx_ref))
    bc = _silu(_causal_conv_carried(extbc_scr, bc_ref[0], cwbc_ref, cbbc_ref))

    dt = _softplus(dt_ref[0] + dtb_ref[...])
    adt = dt * (-jnp.exp(alog_ref[...]))
    row = lax.broadcasted_iota(jnp.int32, (q, q), 0)
    col = lax.broadcasted_iota(jnp.int32, (q, q), 1)
    causal = row >= col
    tri = jnp.where(causal, 1.0, 0.0).astype(BF16)
    h3, m3, l3 = _split3(adt)
    acs = _dot(tri, h3) + _dot(tri, m3) + _dot(tri, l3)
    eacs = jnp.exp(acs)
    acs_t = acs.T
    dt_t = dt.T
    lane = lax.broadcasted_iota(jnp.int32, (1, LANES), 1)
    lo_half = lane < SSD_HEAD_DIM

    for g in range(N_SSD_GROUPS):
        bm = bc[:, g * D_STATE:(g + 1) * D_STATE]
        cm = bc[:, (N_SSD_GROUPS + g) * D_STATE:(N_SSD_GROUPS + g + 1) * D_STATE]
        cb = jnp.where(causal, _dot_nt(cm.astype(BF16), bm.astype(BF16)), 0.0)
        bm_t = bm.T
        for pr in range(HEADS_PER_GROUP // 2):
            h0 = g * HEADS_PER_GROUP + 2 * pr
            lanes = pl.ds(h0 * SSD_HEAD_DIM, LANES)
            x_pair = xs[:, h0 * SSD_HEAD_DIM:h0 * SSD_HEAD_DIM + LANES]
            st_pair = st_scr[:, lanes]
            lhs_y, lhs_s, decs = [], [], []
            for h in (h0, h0 + 1):
                a_col = acs[:, h:h + 1]
                a_row = acs_t[h:h + 1, :]
                dt_row = dt_t[h:h + 1, :]
                a_last = acs_t[h:h + 1, q - 1:q]
                decay = jnp.exp(jnp.minimum(a_col - a_row, 0.0))
                lhs_y.append((cb * decay * dt_row).astype(BF16))
                lhs_s.append((bm_t * (dt_row * jnp.exp(a_last - a_row))).astype(BF16))
                decs.append(jnp.exp(a_last))
            for h in (h0, h0 + 1):
                lhs_y.append((cm * eacs[:, h:h + 1]).astype(BF16))
            x_top = jnp.where(lo_half, x_pair, 0.0).astype(BF16)
            x_bot = jnp.where(lo_half, 0.0, x_pair).astype(BF16)
            s_top = jnp.where(lo_half, st_pair, 0.0).astype(BF16)
            s_bot = jnp.where(lo_half, 0.0, st_pair).astype(BF16)
            x_bd = jnp.concatenate([x_top, x_bot], axis=0)
            rhs_y = jnp.concatenate([x_bd, s_top, s_bot], axis=0)
            y_pair = _dot(jnp.concatenate(lhs_y, axis=1), rhs_y)
            ds_pair = _dot(jnp.concatenate(lhs_s, axis=1), x_bd)
            dskip = dskip_ref[:, lanes]
            y_scr[:, lanes] = y_pair + dskip * x_pair
            st_scr[:, lanes] = st_pair * jnp.where(lo_half, decs[0], decs[1]) + ds_pair

    u_ref[0] = _gated_group_norm(y_scr[...], z_ref[0], gn_ref).astype(u_ref.dtype)

    @pl.when(c == pl.num_programs(1) - 1)
    def _():
        for pr in range(N_SSD_HEADS // 2):
            t = st_scr[:, pl.ds(pr * LANES, LANES)].T
            fin_ref[0, 2 * pr] = t[:SSD_HEAD_DIM]
            fin_ref[0, 2 * pr + 1] = t[SSD_HEAD_DIM:]


def _ssd(proj, cwx, cbx, cwbc, cbbc, dtb, alog, dskip, gn):
    nb, rows, _ = proj.shape
    q = CHUNK
    full = lambda shape: pl.BlockSpec(shape, lambda b, c: (0,) * len(shape))
    return pl.pallas_call(
        _ssd_kernel,
        grid=(nb, rows // q),
        in_specs=[
            pl.BlockSpec((1, q, D_INNER), lambda b, c: (b, c, COL_Z // D_INNER)),
            pl.BlockSpec((1, q, D_INNER), lambda b, c: (b, c, COL_X // D_INNER)),
            pl.BlockSpec((1, q, BC_DIM), lambda b, c: (b, c, COL_BC // BC_DIM)),
            pl.BlockSpec((1, q, LANES), lambda b, c: (b, c, COL_DT // LANES)),
            full((CONV_W, D_INNER)), full((1, D_INNER)), full((CONV_W, BC_DIM)), full((1, BC_DIM)),
            full((1, LANES)), full((1, LANES)), full((1, D_INNER)), full((1, D_INNER)),
        ],
        out_specs=[
            pl.BlockSpec((1, q, D_INNER), lambda b, c: (b, c, 0)),
            pl.BlockSpec((1, N_SSD_HEADS, SSD_HEAD_DIM, D_STATE), lambda b, c: (b, 0, 0, 0)),
        ],
        out_shape=[
            jax.ShapeDtypeStruct((nb, rows, D_INNER), BF16),
            jax.ShapeDtypeStruct((nb, N_SSD_HEADS, SSD_HEAD_DIM, D_STATE), F32),
        ],
        scratch_shapes=[
            pltpu.VMEM((SUBLANES, D_INNER), F32),
            pltpu.VMEM((SUBLANES, BC_DIM), F32),
            pltpu.VMEM((D_STATE, D_INNER), F32),
            pltpu.VMEM((q, D_INNER), F32),
        ],
        compiler_params=pltpu.CompilerParams(
            dimension_semantics=("parallel", "arbitrary"), vmem_limit_bytes=VMEM_LIMIT),
        name="ssd",
    )(proj, proj, proj, proj, cwx, cbx, cwbc, cbbc, dtb, alog, dskip, gn)


def _ssd_step_kernel(n_valid, z_ref, x_ref, bc_ref, dt_ref, px_ref, pbc_ref, st_ref, cwx_ref, cbx_ref,
                     cwbc_ref, cbbc_ref, dtb_ref, alog_ref, dskip_ref, gn_ref, exp_ref,
                     u_ref, nst_ref, extx_scr, extbc_scr):
    for sq in range(z_ref.shape[0]):
        _ssd_step_one(n_valid, z_ref.at[sq], x_ref.at[sq], bc_ref.at[sq], dt_ref.at[sq], px_ref.at[sq],
                      pbc_ref.at[sq], st_ref.at[sq], cwx_ref, cbx_ref, cwbc_ref, cbbc_ref, dtb_ref, alog_ref,
                      dskip_ref, gn_ref, exp_ref, u_ref.at[sq], nst_ref.at[sq], extx_scr.at[sq], extbc_scr.at[sq])


def _ssd_step_one(n_valid, z_ref, x_ref, bc_ref, dt_ref, px_ref, pbc_ref, st_ref, cwx_ref, cbx_ref,
                  cwbc_ref, cbbc_ref, dtb_ref, alog_ref, dskip_ref, gn_ref, exp_ref,
                  u_ref, nst_ref, extx_scr, extbc_scr):
    q = SAMPLE_ROWS
    gw = D_INNER // N_SSD_GROUPS
    xs = _silu(_causal_conv(extx_scr, x_ref[...], px_ref[...], cwx_ref, cbx_ref, q))
    bc = _silu(_causal_conv(extbc_scr, bc_ref[...], pbc_ref[...], cwbc_ref, cbbc_ref, q))

    rowi = lax.broadcasted_iota(jnp.int32, (q, 1), 0)
    dt = jnp.where(rowi < n_valid, _softplus(dt_ref[...] + dtb_ref[...]), 0.0)
    adt = dt * (-jnp.exp(alog_ref[...]))
    acs = jnp.zeros_like(adt)
    for s in range(n_valid):
        acs = acs + jnp.where(rowi >= s, adt[s:s + 1, :], 0.0)
    expand = exp_ref[...]
    dt_e = _dot_exact_rhs(dt, expand)
    acs_e = _dot_exact_rhs(acs, expand)
    last_e = acs_e[q - 1:q, :]

    x_end = xs * dt_e * jnp.exp(last_e - acs_e)
    dec3 = _split3(jnp.exp(last_e))
    zrow = jnp.zeros((q - 3, D_INNER), BF16)
    dec_rows = jnp.concatenate([dec3[0], dec3[1], dec3[2], zrow], axis=0)
    ones = jnp.ones((q, D_STATE), BF16)

    y_off, cbs = [], []
    for g in range(N_SSD_GROUPS):
        bm = bc[:, g * D_STATE:(g + 1) * D_STATE].astype(BF16)
        cm = bc[:, (N_SSD_GROUPS + g) * D_STATE:(N_SSD_GROUPS + g + 1) * D_STATE].astype(BF16)
        rows = pl.ds(g * gw, gw)
        st = st_ref[rows, :]
        y_off.append(_dot_nt(cm, st.astype(BF16)))
        cbs.append(_dot_nt(cm, bm))
        d_st = _dot_tn(x_end[:, g * gw:(g + 1) * gw].astype(BF16), bm)
        dec = _dot_tn(dec_rows[:, g * gw:(g + 1) * gw], ones)
        nst_ref[rows, :] = st * dec + d_st

    y = jnp.concatenate(y_off, axis=-1) * jnp.exp(acs_e) + dskip_ref[...] * xs
    for s in range(n_valid):
        keep = rowi >= s
        decay = jnp.where(keep, jnp.exp(jnp.where(keep, acs_e - acs_e[s:s + 1, :], 0.0)), 0.0)
        cb_e = jnp.concatenate([jnp.broadcast_to(cb[:, s:s + 1], (q, gw)) for cb in cbs], axis=-1)
        y = y + decay * cb_e * (dt_e[s:s + 1, :] * xs[s:s + 1, :])
    u_ref[...] = _gated_group_norm(y, z_ref[...], gn_ref).astype(u_ref.dtype)


def _ssd_step(proj, prev_x, prev_bc, state, cwx, cbx, cwbc, cbbc, dtb, alog, dskip, gn, expand, n_valid):
    nb = proj.shape[0]
    q = SAMPLE_ROWS
    seqs = math.gcd(nb, 4)
    full = lambda shape: pl.BlockSpec(shape, lambda b: (0,) * len(shape))
    return pl.pallas_call(
        functools.partial(_ssd_step_kernel, n_valid),
        grid=(nb // seqs,),
        in_specs=[
            pl.BlockSpec((seqs, q, D_INNER), lambda b: (b, 0, COL_Z // D_INNER)),
            pl.BlockSpec((seqs, q, D_INNER), lambda b: (b, 0, COL_X // D_INNER)),
            pl.BlockSpec((seqs, q, BC_DIM), lambda b: (b, 0, COL_BC // BC_DIM)),
            pl.BlockSpec((seqs, q, LANES), lambda b: (b, 0, COL_DT // LANES)),
            pl.BlockSpec((seqs, CONV_W - 1, D_INNER), lambda b: (b, 0, 0)),
            pl.BlockSpec((seqs, CONV_W - 1, BC_DIM), lambda b: (b, 0, 0)),
            pl.BlockSpec((seqs, D_INNER, D_STATE), lambda b: (b, 0, 0)),
            full((CONV_W, D_INNER)), full((1, D_INNER)), full((CONV_W, BC_DIM)), full((1, BC_DIM)),
            full((1, LANES)), full((1, LANES)), full((1, D_INNER)), full((1, D_INNER)),
            full((LANES, D_INNER)),
        ],
        out_specs=[
            pl.BlockSpec((seqs, q, D_INNER), lambda b: (b, 0, 0)),
            pl.BlockSpec((seqs, D_INNER, D_STATE), lambda b: (b, 0, 0)),
        ],
        out_shape=[
            jax.ShapeDtypeStruct((nb, q, D_INNER), BF16),
            jax.ShapeDtypeStruct((nb, D_INNER, D_STATE), F32),
        ],
        scratch_shapes=[
            pltpu.VMEM((seqs, 2 * SUBLANES, D_INNER), F32),
            pltpu.VMEM((seqs, 2 * SUBLANES, BC_DIM), F32),
        ],
        compiler_params=pltpu.CompilerParams(
            dimension_semantics=("parallel",), vmem_limit_bytes=VMEM_LIMIT),
        name="ssd_step",
    )(proj, proj, proj, proj, prev_x, prev_bc, state, cwx, cbx, cwbc, cbbc, dtb, alog, dskip, gn, expand)


def _attention_kernel(qb, prev_always_valid, sinks_ref, q_ref, kp_ref, vp_ref, kc_ref, vc_ref, o_ref,
                      *bias_scratch):
    wb = WINDOW
    lane = lax.broadcasted_iota(jnp.int32, (1, LANES), 1)
    lo_half = lane < ATTN_HEAD_DIM
    keys_major = qb % LANES == 0
    t_ax, s_ax = (1, 0) if keys_major else (0, 1)
    shape_p = (wb, qb) if keys_major else (qb, wb)
    t_p = lax.broadcasted_iota(jnp.int32, shape_p, t_ax)
    s_p = lax.broadcasted_iota(jnp.int32, shape_p, s_ax)
    rel_p = wb + t_p - s_p
    ok_p = rel_p < WINDOW
    t_c = lax.broadcasted_iota(jnp.int32, (qb, qb), t_ax)
    s_c = lax.broadcasted_iota(jnp.int32, (qb, qb), s_ax)
    rel_c = t_c - s_c
    ok_c = rel_c >= 0
    rel_p = rel_p.astype(F32)
    rel_c = rel_c.astype(F32)
    scale = ATTN_HEAD_DIM ** -0.5

    pairs = KV_REP
    half = pairs * qb

    def one_sequence(sq, m, bias_p, bias_c, sink):
        kv_lanes = pl.ds(m * LANES, LANES)
        kp = kp_ref[sq, :, kv_lanes].astype(BF16)
        kc = kc_ref[sq, :, kv_lanes].astype(BF16)
        vp = vp_ref[sq, :, kv_lanes]
        vc = vc_ref[sq, :, kv_lanes]
        vp_lo, vp_hi = jnp.where(lo_half, vp, 1.0).astype(BF16), jnp.where(lo_half, 1.0, vp).astype(BF16)
        vc_lo, vc_hi = jnp.where(lo_half, vc, 1.0).astype(BF16), jnp.where(lo_half, 1.0, vc).astype(BF16)
        q_lo, q_hi = [], []
        for i in range(pairs):
            q_pair = q_ref[sq, :, pl.ds((pairs * m + i) * LANES, LANES)] * scale
            q_lo.append(jnp.where(lo_half, q_pair, 0.0).astype(BF16))
            q_hi.append(jnp.where(lo_half, 0.0, q_pair).astype(BF16))
        qs = jnp.concatenate(q_lo + q_hi, axis=0)
        s_p = _dot_nt(qs, kp) + bias_p
        s_c = _dot_nt(qs, kc) + bias_c
        if qb == wb:
            mx = jnp.max(jnp.maximum(s_p, s_c), axis=-1, keepdims=True)
        else:
            mx = jnp.maximum(jnp.max(s_p, axis=-1, keepdims=True), jnp.max(s_c, axis=-1, keepdims=True))
        mx = jnp.maximum(mx, sink)
        p_p = jnp.exp(s_p - mx).astype(BF16)
        p_c = jnp.exp(s_c - mx).astype(BF16)
        e_sink = jnp.exp(sink - mx)
        o_lo = _dot(p_p[:half], vp_lo) + _dot(p_c[:half], vc_lo)
        o_hi = _dot(p_p[half:], vp_hi) + _dot(p_c[half:], vc_hi)
        for i in range(pairs):
            a = o_lo[i * qb:(i + 1) * qb]
            b = o_hi[i * qb:(i + 1) * qb]
            num = jnp.where(lo_half, a, b)
            den = (pltpu.roll(jnp.where(lo_half, b, a), ATTN_HEAD_DIM, 1)
                   + jnp.where(lo_half, e_sink[i * qb:(i + 1) * qb], e_sink[half + i * qb:half + (i + 1) * qb]))
            o_ref[sq, :, pl.ds((pairs * m + i) * LANES, LANES)] = (num / den).astype(o_ref.dtype)

    def one_sequence_keys_major(sq, m, bias_p, bias_c, sink):
        kv_lanes = pl.ds(m * LANES, LANES)
        kp = kp_ref[sq, :, kv_lanes].astype(BF16)
        kc = kc_ref[sq, :, kv_lanes].astype(BF16)
        vp = vp_ref[sq, :, kv_lanes].astype(BF16)
        vc = vc_ref[sq, :, kv_lanes].astype(BF16)
        q_lo, q_hi = [], []
        for i in range(pairs):
            q_pair = q_ref[sq, :, pl.ds((pairs * m + i) * LANES, LANES)] * scale
            q_lo.append(jnp.where(lo_half, q_pair, 0.0).astype(BF16))
            q_hi.append(jnp.where(lo_half, 0.0, q_pair).astype(BF16))
        qs = jnp.concatenate(q_lo + q_hi, axis=0)
        s_p = _dot_nt(kp, qs) + bias_p
        s_c = _dot_nt(kc, qs) + bias_c
        mx = jnp.maximum(jnp.maximum(jnp.max(s_p, axis=0, keepdims=True), jnp.max(s_c, axis=0, keepdims=True)),
                         sink)
        p_p = jnp.exp(s_p - mx)
        p_c = jnp.exp(s_c - mx)
        den = jnp.sum(p_p, axis=0, keepdims=True) + jnp.sum(p_c, axis=0, keepdims=True) + jnp.exp(sink - mx)
        o_t = (_dot_tn(vp, p_p.astype(BF16)) + _dot_tn(vc, p_c.astype(BF16))) / den
        row_lo = lax.broadcasted_iota(jnp.int32, (LANES, 1), 0) < ATTN_HEAD_DIM
        for i in range(pairs):
            blk = jnp.where(row_lo, o_t[:, i * qb:(i + 1) * qb], o_t[:, (pairs + i) * qb:(pairs + i + 1) * qb])
            o_ref[sq, :, pl.ds((pairs * m + i) * LANES, LANES)] = blk.T.astype(o_ref.dtype)

    def bias_tables(m):
        slopes = [2.0 ** (-8.0 * (2 * pairs * m + r + 1) / N_ATTN_HEADS) for r in range(2 * pairs)]
        return (jnp.concatenate([jnp.where(ok_p, -sl * rel_p, NEG_BIG) for sl in slopes], axis=t_ax),
                jnp.concatenate([jnp.where(ok_c, -sl * rel_c, NEG_BIG) for sl in slopes], axis=t_ax))

    if bias_scratch:
        bias_scr, = bias_scratch

        @pl.when(pl.program_id(1) == 0)
        def _():
            for m in range(KV_DIM // LANES):
                table_p, table_c = bias_tables(m)
                bias_scr[m, pl.ds(0, wb), :] = table_p
                bias_scr[m, pl.ds(wb, qb), :] = table_c

    for m in range(KV_DIM // LANES):
        heads = [2 * pairs * m + r for r in range(2 * pairs)]
        if bias_scratch:
            bias_p, bias_c = bias_scr[m, pl.ds(0, wb), :], bias_scr[m, pl.ds(wb, qb), :]
        else:
            bias_p, bias_c = bias_tables(m)
        if not prev_always_valid:
            bias_p = jnp.where(pl.program_id(1) > 0, bias_p, NEG_BIG)
        one = (1, qb) if keys_major else (qb, 1)
        sink = jnp.concatenate([jnp.full(one, sinks_ref[h], F32) for h in heads], axis=t_ax)
        for sq in range(q_ref.shape[0]):
            (one_sequence_keys_major if keys_major else one_sequence)(sq, m, bias_p, bias_c, sink)


def _attention(proj, sinks, qb, prev_kv=None, seqs=1):
    nb, rows, _ = proj.shape
    wb = WINDOW
    cur = lambda col: pl.BlockSpec((seqs, qb, KV_DIM), lambda b, n: (b, n, col // KV_DIM))
    if prev_kv is None:
        prev = lambda col: pl.BlockSpec((seqs, wb, KV_DIM),
                                        lambda b, n: (b, jnp.maximum(n - 1, 0), col // KV_DIM))
        kp, vp = proj, proj
        prev_specs = [prev(COL_K), prev(COL_V)]
    else:
        kp, vp = prev_kv
        prev_specs = [pl.BlockSpec((seqs, wb, KV_DIM), lambda b, n: (b, 0, 0))] * 2
    return pl.pallas_call(
        functools.partial(_attention_kernel, qb, prev_kv is not None),
        grid=(nb // seqs, rows // qb),
        in_specs=[
            pl.BlockSpec(memory_space=pltpu.SMEM),
            pl.BlockSpec((seqs, qb, D_MODEL), lambda b, n: (b, n, COL_Q // D_MODEL)),
            *prev_specs,
            cur(COL_K), cur(COL_V),
        ],
        out_specs=pl.BlockSpec((seqs, qb, D_MODEL), lambda b, n: (b, n, 0)),
        out_shape=jax.ShapeDtypeStruct((nb, rows, D_MODEL), BF16),
        scratch_shapes=([pltpu.VMEM((KV_DIM // LANES, wb + qb, 2 * KV_REP * qb), F32)]
                        if qb % LANES == 0 and rows > qb else []),
        compiler_params=pltpu.CompilerParams(
            dimension_semantics=("parallel", "arbitrary"), vmem_limit_bytes=VMEM_LIMIT),
        name="attention",
    )(sinks, proj, kp, vp, proj, proj)


def _post_mix_kernel(valid_rows, u_ref, a_ref, gs_ref, ga_ref, x_ref, wos_ref, woa_ref, wout_ref, gpm_ref,
                     gpf_ref, g1_ref, sc2_ref, sh2_ref, wr_ref, br_ref, cnt_in_ref,
                     x1_ref, h2_ref, ti_ref, tg_ref, cnt_ref, cnt_scr):
    @pl.when(jnp.logical_and(pl.program_id(0) == 0, pl.program_id(1) == 0))
    def _():
        cnt_scr[...] = cnt_in_ref[...]

    y_ssd = _dot(u_ref[0], wos_ref[...])
    y_attn = _dot(a_ref[0], woa_ref[...])
    merged = jax.nn.sigmoid(gs_ref[0]) * y_ssd + jax.nn.sigmoid(ga_ref[0]) * y_attn
    mixed = _dot(merged.astype(BF16), wout_ref[...])
    x1 = x_ref[0] + g1_ref[0] * (_rms(mixed) * gpm_ref[...])
    x1_ref[0] = x1
    h2 = _rms(x1) * gpf_ref[...] * (1.0 + sc2_ref[0]) + sh2_ref[0]
    _store_token_tiles(h2_ref.at[0], h2)

    h_hi = h2.astype(BF16)
    h_lo = (h2 - h_hi.astype(F32)).astype(BF16)
    w = wr_ref[...]
    w_hi = w.astype(BF16)
    w_lo = (w - w_hi.astype(F32)).astype(BF16)
    logits = _dot(h_hi, w_hi) + (_dot(h_hi, w_lo) + _dot(h_lo, w_hi)) + br_ref[...]
    lane = lax.broadcasted_iota(jnp.int32, logits.shape, 1)
    idx_out = jnp.zeros(logits.shape, jnp.int32)
    val_out = jnp.zeros(logits.shape, F32)
    top = None
    denom = None
    idxs = []
    for k in range(TOP_K):
        m = jnp.max(logits, axis=-1, keepdims=True)
        idx = jnp.min(jnp.where(logits == m, lane, LANES), axis=-1, keepdims=True)
        if k == 0:
            top = m
            e = jnp.ones_like(m)
            denom = e
        else:
            e = jnp.exp(m - top)
            denom = denom + e
        idxs.append(idx)
        idx_out = jnp.where(lane == k, idx, idx_out)
        val_out = jnp.where(lane == k, e, val_out)
        logits = jnp.where(lane == idx, NEG_BIG * 2, logits)
    tg_ref[0] = val_out / denom
    ti_ref[0] = idx_out

    rowi = lax.broadcasted_iota(jnp.int32, (logits.shape[0], 1), 0)
    valid = jnp.bitwise_and(rowi, SAMPLE_ROWS - 1) < valid_rows
    picked = jnp.zeros(logits.shape, F32)
    for idx in idxs:
        picked = picked + jnp.where(jnp.logical_and(lane == idx, valid), 1.0, 0.0)
    cnt_scr[...] = cnt_scr[...] + jnp.sum(picked, axis=0, keepdims=True)
    cnt_ref[...] = cnt_scr[...]


def _post_mix(u, attn, proj, x, wos, woa, wout, gpm, gpf, gate1, scale2, shift2, wr, br, counts, *,
              tm, per_row, valid_rows):
    nb, rows, _ = x.shape
    row_spec = lambda w, col=0: pl.BlockSpec((1, tm, w), lambda b, i: (b, i, col // w))
    full = lambda shape: pl.BlockSpec(shape, lambda b, i: (0,) * len(shape))
    return pl.pallas_call(
        functools.partial(_post_mix_kernel, valid_rows),
        grid=(nb, rows // tm),
        in_specs=[
            row_spec(D_INNER), row_spec(D_MODEL), row_spec(D_MODEL, COL_GS), row_spec(D_MODEL, COL_GA),
            row_spec(D_MODEL),
            full((D_INNER, D_MODEL)), full((D_MODEL, D_MODEL)), full((D_MODEL, D_MODEL)),
            full((1, D_MODEL)), full((1, D_MODEL)),
            _mod_spec(per_row, tm), _mod_spec(per_row, tm), _mod_spec(per_row, tm),
            full((D_MODEL, LANES)), full((1, LANES)), full((1, LANES)),
        ],
        out_specs=[row_spec(D_MODEL), pl.BlockSpec((1, tm * TILE_ROWS, LANES), lambda b, i: (b, i, 0)),
                   row_spec(LANES), row_spec(LANES), full((1, LANES))],
        out_shape=[
            jax.ShapeDtypeStruct((nb, rows, D_MODEL), F32),
            jax.ShapeDtypeStruct((nb, rows * TILE_ROWS, LANES), F32),
            jax.ShapeDtypeStruct((nb, rows, LANES), jnp.int32),
            jax.ShapeDtypeStruct((nb, rows, LANES), F32),
            jax.ShapeDtypeStruct((1, LANES), F32),
        ],
        scratch_shapes=[pltpu.VMEM((1, LANES), F32)],
        compiler_params=pltpu.CompilerParams(
            dimension_semantics=("arbitrary", "arbitrary"), vmem_limit_bytes=BIG_VMEM_LIMIT),
        name="post_mix",
    )(u, attn, proj, proj, x, wos, woa, wout, gpm, gpf, gate1, scale2, shift2, wr, br, counts)


def _moe_kernel(n_tokens, be_ref, na_ref, nv_ref, idx_hbm, h_hbm, wu_ref, bu_ref, wd_ref, bd_ref, ys_hbm,
                idx_smem, xbuf, obuf, wu_bf, wd_bf, sem_idx, sem_in, sem_out):
    i = pl.program_id(0)
    na = na_ref[0]
    slot = lax.rem(i, 2)
    other = 1 - slot

    def idx_copy(block, s):
        return pltpu.make_async_copy(idx_hbm.at[block], idx_smem.at[s], sem_idx.at[s])

    block_rows = MOE_ROWS * TILE_ROWS

    def tile(ref, first_row):
        return ref.at[pl.ds(pl.multiple_of(first_row, TILE_ROWS), TILE_ROWS)]

    group_rows = MOE_GROUP * TILE_ROWS

    def per_group(block, fn):
        n_real = nv_ref[block]
        for g in range(MOE_ROWS // MOE_GROUP):
            pl.when(g * MOE_GROUP < n_real)(functools.partial(fn, g))

    def group_of(buf, s, g):
        return buf.at[s, pl.ds(g * group_rows, group_rows)]

    def start_gather(s, block):
        def group(g):
            for r in range(g * MOE_GROUP, (g + 1) * MOE_GROUP):
                pltpu.make_async_copy(tile(h_hbm, idx_smem[s, r]), xbuf.at[s, pl.ds(r * TILE_ROWS, TILE_ROWS)],
                                      sem_in.at[s]).start()
        per_group(block, group)

    def wait_gather(s, block):
        per_group(block, lambda g: pltpu.make_async_copy(
            h_hbm.at[pl.ds(0, group_rows)], group_of(xbuf, s, g), sem_in.at[s]).wait())

    def start_scatter(s, block):
        def group(g):
            for r in range(g * MOE_GROUP, (g + 1) * MOE_GROUP):
                pltpu.make_async_copy(obuf.at[s, pl.ds(r * TILE_ROWS, TILE_ROWS)],
                                      tile(ys_hbm, idx_smem[s, MOE_ROWS + r]), sem_out.at[s]).start()
        per_group(block, group)

    def wait_scatter(s, block):
        per_group(block, lambda g: pltpu.make_async_copy(
            group_of(obuf, s, g), ys_hbm.at[pl.ds(0, group_rows)], sem_out.at[s]).wait())

    @pl.when(i == 0)
    def _():
        obuf[0] = jnp.zeros((block_rows, LANES), F32)
        xbuf[...] = jnp.zeros(xbuf.shape, F32)
        plane_rows = ys_hbm.shape[0] // (TOP_K * TILE_ROWS)
        spare = [pltpu.make_async_copy(
            obuf.at[0], ys_hbm.at[pl.ds((k * plane_rows + n_tokens + hf * MOE_ROWS) * TILE_ROWS, block_rows)],
            sem_out.at[0]) for k in range(TOP_K) for hf in range(2)]
        for cp in spare:
            cp.start()
        for cp in spare:
            cp.wait()
        first = idx_copy(0, 0)
        first.start()
        first.wait()
        start_gather(0, 0)

        @pl.when(na > 1)
        def _():
            idx_copy(1, 1).start()

    @pl.when(i < na)
    def _():
        wait_gather(slot, i)

        @pl.when(i + 1 < na)
        def _():
            idx_copy(i + 1, other).wait()
            start_gather(other, i + 1)

        @pl.when(i >= 2)
        def _():
            wait_scatter(slot, i - 2)

        @pl.when(jnp.logical_or(i == 0, be_ref[i] != be_ref[jnp.maximum(i - 1, 0)]))
        def _():
            wu_bf[...] = wu_ref[0].astype(BF16)
            wd_bf[...] = wd_ref[0].astype(BF16)

        x = _load_token_tiles(xbuf.at[slot], MOE_ROWS).astype(BF16)
        up = _dot(x, wu_bf[...]) + bu_ref[0]
        glu = jnp.minimum(up[:, :D_FF], SWIGLU_LIMIT)
        lin = jnp.clip(up[:, D_FF:], -SWIGLU_LIMIT, SWIGLU_LIMIT)
        act = glu * jax.nn.sigmoid(SWIGLU_ALPHA * glu) * (lin + 1.0)
        _store_token_tiles(obuf.at[slot], _dot(act.astype(BF16), wd_bf[...]) + bd_ref[0])
        start_scatter(slot, i)

        @pl.when(i + 2 < na)
        def _():
            idx_copy(i + 2, slot).start()

        @pl.when(i == na - 1)
        def _():
            wait_scatter(slot, i)

            @pl.when(i >= 1)
            def _():
                wait_scatter(other, i - 1)


def _moe(idx, h_tiles, ys_rows, block_expert, n_active, n_real, wu, bu, wd, bd):
    n_blocks = idx.shape[0]
    grid_spec = pltpu.PrefetchScalarGridSpec(
        num_scalar_prefetch=3,
        grid=(n_blocks,),
        in_specs=[
            pl.BlockSpec(memory_space=pl.ANY),
            pl.BlockSpec(memory_space=pl.ANY),
            pl.BlockSpec((1, D_MODEL, 2 * D_FF), lambda i, be, na, nv: (be[i], 0, 0)),
            pl.BlockSpec((1, 1, 2 * D_FF), lambda i, be, na, nv: (be[i], 0, 0)),
            pl.BlockSpec((1, D_FF, D_MODEL), lambda i, be, na, nv: (be[i], 0, 0)),
            pl.BlockSpec((1, 1, D_MODEL), lambda i, be, na, nv: (be[i], 0, 0)),
        ],
        out_specs=pl.BlockSpec(memory_space=pl.ANY),
        scratch_shapes=[
            pltpu.SMEM((2, 2 * MOE_ROWS), jnp.int32),
            pltpu.VMEM((2, MOE_ROWS * TILE_ROWS, LANES), F32),
            pltpu.VMEM((2, MOE_ROWS * TILE_ROWS, LANES), F32),
            pltpu.VMEM((D_MODEL, 2 * D_FF), BF16),
            pltpu.VMEM((D_FF, D_MODEL), BF16),
            pltpu.SemaphoreType.DMA((2,)),
            pltpu.SemaphoreType.DMA((2,)),
            pltpu.SemaphoreType.DMA((2,)),
        ],
    )
    return pl.pallas_call(
        functools.partial(_moe_kernel, h_tiles.shape[0] // TILE_ROWS),
        grid_spec=grid_spec,
        out_shape=jax.ShapeDtypeStruct((ys_rows * TILE_ROWS, LANES), F32),
        compiler_params=pltpu.CompilerParams(
            dimension_semantics=("arbitrary",), vmem_limit_bytes=BIG_VMEM_LIMIT),
        name="moe",
    )(block_expert, n_active, n_real, idx, h_tiles, wu, bu, wd, bd)


def _combine_kernel(ys_ref, tg_ref, x1_ref, gpost_ref, g2_ref, o_ref):
    gates = tg_ref[0]
    tm = gates.shape[0]
    f = gates[:, 0:1] * _load_token_tiles(ys_ref.at[0], tm)
    for k in range(1, TOP_K):
        f = f + gates[:, k:k + 1] * _load_token_tiles(ys_ref.at[k], tm)
    o_ref[0] = x1_ref[0] + g2_ref[0] * (_rms(f) * gpost_ref[...])


def _combine(ys, row0, tg, x1, gpost, gate2, *, tm, per_row):
    nb, rows, _ = x1.shape
    per_b = rows // tm
    base = row0 // tm
    return pl.pallas_call(
        _combine_kernel,
        grid=(nb, per_b),
        in_specs=[
            pl.BlockSpec((TOP_K, tm * TILE_ROWS, LANES), lambda b, i: (0, base + b * per_b + i, 0)),
            pl.BlockSpec((1, tm, LANES), lambda b, i: (b, i, 0)),
            pl.BlockSpec((1, tm, D_MODEL), lambda b, i: (b, i, 0)),
            pl.BlockSpec((1, D_MODEL), lambda b, i: (0, 0)),
            _mod_spec(per_row, tm),
        ],
        out_specs=pl.BlockSpec((1, tm, D_MODEL), lambda b, i: (b, i, 0)),
        out_shape=jax.ShapeDtypeStruct((nb, rows, D_MODEL), F32),
        compiler_params=pltpu.CompilerParams(
            dimension_semantics=("parallel", "parallel"), vmem_limit_bytes=VMEM_LIMIT),
        name="combine",
    )(ys, tg, x1, gpost, gate2)


def _route(top_idx, counts, plane_rows):
    n_tokens = top_idx.shape[0]
    n_slots = n_tokens * TOP_K
    experts = jnp.arange(N_EXPERTS, dtype=jnp.int32)
    padded = (counts + MOE_ROWS - 1) // MOE_ROWS * MOE_ROWS
    group_start = jnp.cumsum(counts) - counts
    padded_end = jnp.cumsum(padded)
    padded_start = padded_end - padded
    n_blocks = -(-n_slots // MOE_ROWS) + N_EXPERTS
    n_active = (padded_end[-1] // MOE_ROWS).astype(jnp.int32)
    block_start = jnp.arange(n_blocks, dtype=jnp.int32) * MOE_ROWS
    block_expert = jnp.sum(block_start[:, None] >= padded_end[None, :], axis=1)
    last_expert = jnp.max(jnp.where(counts > 0, experts, 0))
    block_expert = jnp.where(block_start < padded_end[-1], block_expert, last_expert).astype(jnp.int32)
    slot_id = jnp.arange(n_slots, dtype=jnp.int32).reshape(n_tokens, TOP_K)
    keys = jnp.sort((top_idx * n_slots + slot_id).reshape(-1))
    row = jnp.arange(n_blocks * MOE_ROWS, dtype=jnp.int32).reshape(n_blocks, MOE_ROWS)
    src = row + (group_start - padded_start)[block_expert][:, None]
    real = row < (padded_start + counts)[block_expert][:, None]
    slot = keys[jnp.clip(src, 0, n_slots - 1).reshape(-1)].reshape(n_blocks, MOE_ROWS) % n_slots
    token = slot // TOP_K
    spare = n_tokens + (jnp.arange(n_blocks, dtype=jnp.int32) % 2)[:, None] * MOE_ROWS + row % MOE_ROWS
    dest = jnp.where(real, (slot % TOP_K) * plane_rows + token, spare)
    idx = (jnp.concatenate([token, dest], axis=1) * TILE_ROWS).astype(jnp.int32)
    n_real = jnp.clip((padded_start + counts)[block_expert] - block_start, 0, MOE_ROWS)
    n_real = jnp.where(block_start < padded_end[-1], n_real, 0).astype(jnp.int32)
    return idx, block_expert, n_active.reshape(1), n_real


def _pad_lanes(v, value=0.0):
    return jnp.pad(v, [(0, 0)] * (v.ndim - 1) + [(0, LANES - v.shape[-1])], constant_values=value)


def kernel(x_prompt, x_sample, c_prompt, c_sample, cache_swa_k, cache_swa_v, state_conv, state_ssm, w_ada, b_ada, g_pre_mix, g_post_mix, g_pre_ffn, g_post_ffn, w_in, conv_w, conv_b, dt_bias, a_log, d_skip, g_ssm_norm, sinks, w_o_ssd, w_o_attn, w_out, w_router, b_router, w_up, b_up, w_down, b_down):
    depth = w_ada.shape[0]
    n_prompt, seq, _ = x_prompt.shape
    n_sample, dec_seq, _ = x_sample.shape
    yp = x_prompt
    ys_pad = jnp.pad(x_sample, ((0, 0), (0, SAMPLE_ROWS - dec_seq), (0, 0)))
    outs = [[] for _ in range(8)]
    expand = jnp.repeat(jnp.eye(LANES, N_SSD_HEADS, dtype=BF16), SSD_HEAD_DIM, axis=1)
    rows_s = n_sample * SAMPLE_ROWS
    n_p = n_prompt * seq
    n_s = n_sample * dec_seq
    n_tok = n_p + n_s
    tm_p = min(512, seq)
    tm_s = min(512, rows_s)
    tm_mix_p = min(512, seq)
    tm_mix_s = min(256, rows_s)
    assert seq % CHUNK == 0 and seq % tm_p == 0 and rows_s % tm_s == 0 and n_p % n_s == 0
    c_all = jnp.concatenate([c_prompt, c_sample], axis=0)
    c_rows = -(-c_all.shape[0] // SUBLANES) * SUBLANES
    c_all = jnp.pad(c_all, ((0, c_rows - c_all.shape[0]), (0, 0)))

    for l in range(depth):
        wi = w_in[l]
        o_xbc = D_INNER
        o_dt = o_xbc + CONV_DIM
        o_q = o_dt + N_SSD_HEADS
        o_k = o_q + D_MODEL
        o_v = o_k + KV_DIM
        o_gs = o_v + KV_DIM
        o_ga = o_gs + D_MODEL
        head_order = jnp.array(ATTN_HEAD_ORDER)
        w_q = wi[:, o_q:o_k].reshape(D_MODEL, N_ATTN_HEADS, ATTN_HEAD_DIM)[:, head_order].reshape(D_MODEL, D_MODEL)
        w_proj = jnp.concatenate([
            wi[:, :o_xbc], wi[:, o_xbc:o_xbc + D_INNER], w_q, wi[:, o_gs:o_ga], wi[:, o_ga:],
            wi[:, o_xbc + D_INNER:o_dt], wi[:, o_k:o_v], wi[:, o_v:o_gs], _pad_lanes(wi[:, o_dt:o_q])],
            axis=1).astype(BF16)
        cwx, cwbc = conv_w[l][:, :D_INNER], conv_w[l][:, D_INNER:]
        cbx, cbbc = conv_b[l][None, :D_INNER], conv_b[l][None, D_INNER:]
        dtb = _pad_lanes(dt_bias[l][None])
        alog = _pad_lanes(a_log[l][None])
        dskip = jnp.repeat(d_skip[l], SSD_HEAD_DIM)[None]
        gn = g_ssm_norm[l][None]
        wos, wout = w_o_ssd[l].astype(BF16), w_out[l].astype(BF16)
        woa = w_o_attn[l].reshape(N_ATTN_HEADS, ATTN_HEAD_DIM, D_MODEL)[head_order].reshape(D_MODEL, D_MODEL)
        woa = woa.astype(BF16)
        zero_counts = jnp.zeros((1, LANES), F32)
        wr = _pad_lanes(w_router[l])
        br = _pad_lanes(b_router[l][None], NEG_BIG)
        wu, wd = w_up[l], w_down[l]
        bu, bd = b_up[l][:, None, :], b_down[l][:, None, :]
        g_pm, g_pom, g_pf, g_pof = (v[l][None] for v in (g_pre_mix, g_post_mix, g_pre_ffn, g_post_ffn))

        ada = _ada(c_all, w_ada[l], b_ada[l][None])
        ada_p = [ada[:n_prompt, k * D_MODEL:(k + 1) * D_MODEL] for k in range(6)]
        ada_s = [ada[n_prompt:n_prompt + n_sample, k * D_MODEL:(k + 1) * D_MODEL] for k in range(6)]

        mods = [m[:, None, :] for m in ada_p]
        proj = _in_proj(yp, g_pm, mods[1], mods[0], w_proj, tm=min(1024, seq), per_row=False)
        u, ssm_p = _ssd(proj, cwx, cbx, cwbc, cbbc, dtb, alog, dskip, gn)
        attn = _attention(proj, sinks[l], WINDOW)
        x1_p, h2_p, ti_p, tg_p, cnt_p = _post_mix(
            u, attn, proj, yp, wos, woa, wout, g_pom, g_pf, mods[2], mods[4], mods[3], wr, br, zero_counts,
            tm=tm_mix_p, per_row=False, valid_rows=SAMPLE_ROWS)
        gate2_p = mods[5]
        kp = proj[:, seq - WINDOW:, COL_K:COL_K + KV_DIM].reshape(n_prompt, WINDOW, N_KV_HEADS, ATTN_HEAD_DIM)
        vp = proj[:, seq - WINDOW:, COL_V:COL_V + KV_DIM].reshape(n_prompt, WINDOW, N_KV_HEADS, ATTN_HEAD_DIM)
        cp = jnp.concatenate([proj[:, seq - (CONV_W - 1):, COL_X:COL_X + D_INNER],
                              proj[:, seq - (CONV_W - 1):, COL_BC:COL_BC + BC_DIM]], axis=-1)

        mods_s = [jnp.repeat(m, SAMPLE_ROWS, axis=0)[None] for m in ada_s]
        xs_flat = ys_pad.reshape(1, rows_s, D_MODEL)
        proj_s = _in_proj(xs_flat, g_pm, mods_s[1], mods_s[0], w_proj, tm=tm_s, per_row=True)
        proj_sb = proj_s.reshape(n_sample, SAMPLE_ROWS, PROJ_DIM)
        u_s, ssm_s = _ssd_step(
            proj_sb, state_conv[l][:, :, :D_INNER], state_conv[l][:, :, D_INNER:],
            state_ssm[l].reshape(n_sample, D_INNER, D_STATE),
            cwx, cbx, cwbc, cbbc, dtb, alog, dskip, gn, expand, dec_seq)
        k_prev = cache_swa_k[l].reshape(n_sample, -1, KV_DIM)
        v_prev = cache_swa_v[l].reshape(n_sample, -1, KV_DIM)
        attn_s = _attention(proj_sb, sinks[l], SAMPLE_ROWS, prev_kv=(k_prev, v_prev),
                            seqs=math.gcd(n_sample, 8))
        x1_s, h2_s, ti_s, tg_s, cnt_all = _post_mix(
            u_s.reshape(1, rows_s, D_INNER), attn_s.reshape(1, rows_s, D_MODEL), proj_s, xs_flat,
            wos, woa, wout, g_pom, g_pf, mods_s[2], mods_s[4], mods_s[3], wr, br, cnt_p,
            tm=tm_mix_s, per_row=True, valid_rows=dec_seq)
        wb = k_prev.shape[1]
        k_new = proj_sb[:, :dec_seq, COL_K:COL_K + KV_DIM]
        v_new = proj_sb[:, :dec_seq, COL_V:COL_V + KV_DIM]
        ks = jnp.concatenate([k_prev, k_new], axis=1)[:, -wb:].reshape(n_sample, wb, N_KV_HEADS, ATTN_HEAD_DIM)
        vs = jnp.concatenate([v_prev, v_new], axis=1)[:, -wb:].reshape(n_sample, wb, N_KV_HEADS, ATTN_HEAD_DIM)
        raw_xbc = jnp.concatenate([proj_sb[:, :dec_seq, COL_X:COL_X + D_INNER],
                                   proj_sb[:, :dec_seq, COL_BC:COL_BC + BC_DIM]], axis=-1)
        cs = jnp.concatenate([state_conv[l], raw_xbc], axis=1)[:, -(CONV_W - 1):]

        valid = lambda v: v.reshape(n_sample, SAMPLE_ROWS, -1)[:, :dec_seq].reshape(n_s, -1)
        h2_sv = h2_s.reshape(n_sample, SAMPLE_ROWS * TILE_ROWS, LANES)[:, :dec_seq * TILE_ROWS]
        h2_all = jnp.concatenate([h2_p.reshape(n_p * TILE_ROWS, LANES),
                                  h2_sv.reshape(n_s * TILE_ROWS, LANES)], axis=0)
        ti_all = jnp.concatenate([ti_p.reshape(n_p, LANES), valid(ti_s)], axis=0)
        counts = cnt_all[0, :N_EXPERTS].astype(jnp.int32)
        plane_rows = n_tok + 2 * MOE_ROWS
        idx, block_expert, n_active, n_real = _route(ti_all[:, :TOP_K], counts, plane_rows)
        ysel = _moe(idx, h2_all, TOP_K * plane_rows, block_expert, n_active, n_real, wu, bu, wd, bd)
        ysel = ysel.reshape(TOP_K, plane_rows * TILE_ROWS, LANES)

        yp = _combine(ysel, 0, tg_p, x1_p, g_pof, gate2_p, tm=tm_p, per_row=False)
        x1_sv = valid(x1_s)[None]
        tg_sv = valid(tg_s)[None]
        gate2_s = jnp.repeat(ada_s[5], dec_seq, axis=0)[None]
        ys_new = _combine(ysel, n_p, tg_sv, x1_sv, g_pof, gate2_s, tm=n_s, per_row=True)
        ys_new = ys_new.reshape(n_sample, dec_seq, D_MODEL)
        ys_pad = jnp.pad(ys_new, ((0, 0), (0, SAMPLE_ROWS - dec_seq), (0, 0)))

        for lst, v in zip(outs, (kp, vp, cp, ssm_p, ks, vs, cs,
                                 ssm_s.reshape(n_sample, N_SSD_HEADS, SSD_HEAD_DIM, D_STATE))):
            lst.append(v)

    return (yp, ys_pad[:, :dec_seq], *[jnp.stack(v) for v in outs])
```
